```python
import math
import jax, jax.numpy as jnp
from jax import lax
import numpy as np

D_MODEL = 2048
BATCH = 2
SEQ = 16384
DEPTH = 1

HEAD_DIM = 128
NSA_HEADS = 8
NSA_KV_GROUPS = 2
NSA_HPG = NSA_HEADS // NSA_KV_GROUPS
FOX_HEADS = 8
CMP_BLOCK = 32
CMP_STRIDE = 16
SLC_BLOCK = 64
SLC_TOPK = 16
WINDOW = 512
Q_BLOCK = 128
REL_BUCKETS = 32
REL_MAX_DIST = 128
D_FF = 5632
CONV_WIDTH = 3
EPS = 1e-6
NEG_INF = -1e30
FORCED_SCORE = 1e4

NSA_Q_DIM = NSA_HEADS * HEAD_DIM
NSA_KV_DIM = NSA_KV_GROUPS * HEAD_DIM
FOX_DIM = FOX_HEADS * HEAD_DIM
IN_SPLITS = (NSA_Q_DIM, NSA_KV_DIM, NSA_KV_DIM, NSA_KV_DIM, NSA_KV_DIM, NSA_KV_DIM, NSA_KV_DIM,
             3 * NSA_HEADS, FOX_DIM, FOX_DIM, FOX_DIM, FOX_HEADS, D_MODEL, D_MODEL)
D_IN = NSA_Q_DIM + 6 * NSA_KV_DIM + 3 * NSA_HEADS + 3 * FOX_DIM + FOX_HEADS + 2 * D_MODEL

kernel_name = "nsa_fox_gated_hybrid_block"


def rms_norm(x, g):
    xf = x.astype(jnp.float32)
    y = xf * lax.rsqrt(jnp.mean(xf * xf, axis=-1, keepdims=True) + EPS)
    return y.astype(x.dtype) * g


def t5_bucket(dist):
    n = jnp.maximum(dist, 0)
    max_exact = REL_BUCKETS // 2
    nf = jnp.maximum(n, 1).astype(jnp.float32)
    large = max_exact + (jnp.log(nf / max_exact) / math.log(REL_MAX_DIST / max_exact)
                         * (REL_BUCKETS - max_exact)).astype(jnp.int32)
    return jnp.where(n < max_exact, n, jnp.minimum(large, REL_BUCKETS - 1))


def rel_bias_dense(dist, table):
    bias = table[t5_bucket(dist)]
    return bias.transpose(2, 0, 1).reshape(NSA_KV_GROUPS, NSA_HPG, *dist.shape)


def masked_softmax(s, mask):
    p = jax.nn.softmax(jnp.where(mask, s, NEG_INF), axis=-1)
    return jnp.where(mask, p, 0.0)


def compress_tokens(kv, pos, w1, w2):
    bsz, seq, grp, dh = kv.shape
    ch = kv.reshape(bsz, seq // CMP_STRIDE, CMP_STRIDE, grp, dh)
    blocks = jnp.concatenate([ch[:, :-1], ch[:, 1:]], axis=2) + pos[:, None, :]
    flat = blocks.transpose(0, 1, 3, 2, 4).reshape(bsz, -1, grp, CMP_BLOCK * dh)
    return jax.nn.gelu(flat @ w1) @ w2


def causal_dwconv(a, w, bias):
    seq = a.shape[1]
    ap = jnp.pad(a, ((0, 0), (CONV_WIDTH - 1, 0), (0, 0)))
    out = bias
    for k in range(CONV_WIDTH):
        out = out + w[k] * ap[:, k:k + seq]
    return out


def nsa_attention(q, k_cmp, v_cmp, k_slc, v_slc, k_win, v_win, gate_logits,
                  pos_k, w1_k, w2_k, pos_v, w1_v, w2_v, rel_table):
    bsz, seq = q.shape[:2]
    scale = HEAD_DIM ** -0.5
    q = q.reshape(bsz, seq, NSA_KV_GROUPS, NSA_HPG, HEAD_DIM)
    kc = compress_tokens(k_cmp, pos_k, w1_k, w2_k)
    vc = compress_tokens(v_cmp, pos_v, w1_v, w2_v)
    n_cmp = kc.shape[1]
    n_slc = seq // SLC_BLOCK
    top_n = min(SLC_TOPK, n_slc)
    c_start = jnp.arange(n_cmp) * CMP_STRIDE
    c_end = c_start + CMP_BLOCK - 1
    s_start = jnp.arange(n_slc) * SLC_BLOCK
    overlap = ((c_start[:, None] < s_start[None, :] + SLC_BLOCK)
               & (c_end[:, None] >= s_start[None, :])).astype(jnp.float32)
    ks_blk = k_slc.reshape(bsz, n_slc, SLC_BLOCK, NSA_KV_GROUPS, HEAD_DIM).transpose(0, 3, 1, 2, 4)
    vs_blk = v_slc.reshape(bsz, n_slc, SLC_BLOCK, NSA_KV_GROUPS, HEAD_DIM).transpose(0, 3, 1, 2, 4)
    pad = ((0, 0), (WINDOW, 0), (0, 0), (0, 0))
    kw_pad = jnp.pad(k_win, pad)
    vw_pad = jnp.pad(v_win, pad)
    gates = jax.nn.sigmoid(gate_logits).reshape(bsz, seq, NSA_KV_GROUPS, NSA_HPG, 3)
    table_g = rel_table.reshape(REL_BUCKETS, NSA_KV_GROUPS, NSA_HPG)
    b_idx = jnp.arange(bsz)[:, None, None, None]
    g_idx = jnp.arange(NSA_KV_GROUPS)[None, :, None, None]
    blk_ids = jnp.arange(n_slc)

    def query_block(qb):
        t0 = qb * Q_BLOCK
        qblk = lax.dynamic_slice_in_dim(q, t0, Q_BLOCK, axis=1)
        tpos = t0 + jnp.arange(Q_BLOCK)
        sc = jnp.einsum('bqghd,bcgd->bghqc', qblk, kc).astype(jnp.float32) * scale
        dist_c = tpos[:, None] - c_end[None, :]
        pc = masked_softmax(sc + rel_bias_dense(dist_c, rel_table), dist_c >= 0)
        o_c = jnp.einsum('bghqc,bcgd->bqghd', pc.astype(vc.dtype), vc)
        imp = jnp.einsum('bghqc,cn->bgqn', pc, overlap)
        cur = tpos // SLC_BLOCK
        valid = s_start[None, :] <= tpos[:, None]
        forced = ((blk_ids[None, :] == 0) | (blk_ids[None, :] == cur[:, None])
                  | (blk_ids[None, :] == cur[:, None] - 1))
        score = jnp.where(valid, jnp.where(forced, FORCED_SCORE, imp), -1.0)
        _, idx = lax.top_k(score, top_n)
        ksel = ks_blk[b_idx, g_idx, idx].reshape(bsz, NSA_KV_GROUPS, Q_BLOCK, top_n * SLC_BLOCK, HEAD_DIM)
        vsel = vs_blk[b_idx, g_idx, idx].reshape(bsz, NSA_KV_GROUPS, Q_BLOCK, top_n * SLC_BLOCK, HEAD_DIM)
        spos = (idx[..., None] * SLC_BLOCK + jnp.arange(SLC_BLOCK)).reshape(bsz, NSA_KV_GROUPS, Q_BLOCK, -1)
        dist_s = tpos[None, None, :, None] - spos
        bias_s = table_g[t5_bucket(dist_s), g_idx].transpose(0, 1, 4, 2, 3)
        ss = jnp.einsum('bqghd,bgqkd->bghqk', qblk, ksel).astype(jnp.float32) * scale
        ps = masked_softmax(ss + bias_s, (dist_s >= 0)[:, :, None])
        o_s = jnp.einsum('bghqk,bgqkd->bqghd', ps.astype(vsel.dtype), vsel)
        kw = lax.dynamic_slice_in_dim(kw_pad, t0, Q_BLOCK + WINDOW, axis=1)
        vw = lax.dynamic_slice_in_dim(vw_pad, t0, Q_BLOCK + WINDOW, axis=1)
        wpos = t0 - WINDOW + jnp.arange(Q_BLOCK + WINDOW)
        dist_w = tpos[:, None] - wpos[None, :]
        mask_w = (dist_w >= 0) & (dist_w < WINDOW) & (wpos[None, :] >= 0)
        sw = jnp.einsum('bqghd,bkgd->bghqk', qblk, kw).astype(jnp.float32) * scale
        pw = masked_softmax(sw + rel_bias_dense(dist_w, rel_table), mask_w)
        o_w = jnp.einsum('bghqk,bkgd->bqghd', pw.astype(vw.dtype), vw)
        g = lax.dynamic_slice_in_dim(gates, t0, Q_BLOCK, axis=1)
        o = g[..., 0:1] * o_c + g[..., 1:2] * o_s + g[..., 2:3] * o_w
        return o.reshape(bsz, Q_BLOCK, NSA_Q_DIM)

    out = lax.map(query_block, jnp.arange(seq // Q_BLOCK))
    return out.transpose(1, 0, 2, 3).reshape(bsz, seq, NSA_Q_DIM)


def fox_attention(q, k, v, f_logit, f_bias):
    bsz, seq = q.shape[:2]
    scale = HEAD_DIM ** -0.5
    q = q.reshape(bsz, seq, FOX_HEADS, HEAD_DIM)
    k = k.reshape(bsz, seq, FOX_HEADS, HEAD_DIM)
    v = v.reshape(bsz, seq, FOX_HEADS, HEAD_DIM)
    log_f = jax.nn.log_sigmoid(f_logit.astype(jnp.float32) + f_bias.astype(jnp.float32))
    cum = lax.cumsum(log_f, axis=1).transpose(0, 2, 1)
    kpos = jnp.arange(seq)

    def query_block(qb):
        t0 = qb * Q_BLOCK
        qblk = lax.dynamic_slice_in_dim(q, t0, Q_BLOCK, axis=1)
        cq = lax.dynamic_slice_in_dim(cum, t0, Q_BLOCK, axis=2)
        tpos = t0 + jnp.arange(Q_BLOCK)
        logits = (jnp.einsum('bqhd,bkhd->bhqk', qblk, k).astype(jnp.float32) * scale
                  + (cq[..., None] - cum[:, :, None, :]))
        p = masked_softmax(logits, kpos[None, :] <= tpos[:, None])
        o = jnp.einsum('bhqk,bkhd->bqhd', p.astype(v.dtype), v)
        return o.reshape(bsz, Q_BLOCK, FOX_DIM)

    out = lax.map(query_block, jnp.arange(seq // Q_BLOCK))
    return out.transpose(1, 0, 2, 3).reshape(bsz, seq, FOX_DIM)


def setup_inputs(seed: int = 0) -> dict:
    key = jax.random.key(seed)
    ks = jax.random.split(key, 24)
    f32 = jnp.float32

    def w(k, shape, fan_in):
        return jax.random.normal(k, shape, f32) * fan_in ** -0.5

    def gain(k, shape):
        return 1.0 + 0.02 * jax.random.normal(k, shape, f32)

    return {
        "x": jax.random.normal(ks[0], (BATCH, SEQ, D_MODEL), f32),
        "attn_norm_g": gain(ks[1], (DEPTH, D_MODEL)),
        "w_in": w(ks[2], (DEPTH, D_MODEL, D_IN), D_MODEL),
        "cmp_pos_k": 0.1 * jax.random.normal(ks[3], (DEPTH, CMP_BLOCK, HEAD_DIM), f32),
        "cmp_w1_k": w(ks[4], (DEPTH, CMP_BLOCK * HEAD_DIM, HEAD_DIM), CMP_BLOCK * HEAD_DIM),
        "cmp_w2_k": w(ks[5], (DEPTH, HEAD_DIM, HEAD_DIM), HEAD_DIM),
        "cmp_pos_v": 0.1 * jax.random.normal(ks[6], (DEPTH, CMP_BLOCK, HEAD_DIM), f32),
        "cmp_w1_v": w(ks[7], (DEPTH, CMP_BLOCK * HEAD_DIM, HEAD_DIM), CMP_BLOCK * HEAD_DIM),
        "cmp_w2_v": w(ks[8], (DEPTH, HEAD_DIM, HEAD_DIM), HEAD_DIM),
        "rel_bias_table": 0.5 * jax.random.normal(ks[9], (REL_BUCKETS, NSA_HEADS), f32),
        "fox_forget_bias": 3.0 + 0.5 * jax.random.normal(ks[10], (DEPTH, FOX_HEADS), f32),
        "w_branch_nsa": w(ks[11], (DEPTH, NSA_Q_DIM, D_MODEL), NSA_Q_DIM),
        "w_branch_fox": w(ks[12], (DEPTH, FOX_DIM, D_MODEL), FOX_DIM),
        "w_out": w(ks[13], (DEPTH, D_MODEL, D_MODEL), D_MODEL),
        "ffn_norm_g": gain(ks[14], (DEPTH, D_MODEL)),
        "w_up": w(ks[15], (DEPTH, D_MODEL, 2 * D_FF), D_MODEL),
        "conv_w": w(ks[16], (DEPTH, CONV_WIDTH, D_FF), CONV_WIDTH),
        "conv_b": 0.02 * jax.random.normal(ks[17], (DEPTH, D_FF), f32),
        "w_down": w(ks[18], (DEPTH, D_FF, D_MODEL), D_FF),
        "final_norm_g": gain(ks[19], (D_MODEL,)),
    }


def reference(x, attn_norm_g, w_in, cmp_pos_k, cmp_w1_k, cmp_w2_k, cmp_pos_v, cmp_w1_v, cmp_w2_v,
              rel_bias_table, fox_forget_bias, w_branch_nsa, w_branch_fox, w_out,
              ffn_norm_g, w_up, conv_w, conv_b, w_down, final_norm_g):
    bsz, seq, _ = x.shape
    offsets = np.cumsum(np.array(IN_SPLITS))[:-1].tolist()
    kv_shape = (bsz, seq, NSA_KV_GROUPS, HEAD_DIM)
    for layer in range(DEPTH):
        h = rms_norm(x, attn_norm_g[layer])
        (q_n, kc_n, vc_n, ks_n, vs_n, kw_n, vw_n, g_n,
         q_f, k_f, v_f, f_l, m_a, m_b) = jnp.split(h @ w_in[layer], offsets, axis=-1)
        y_nsa = nsa_attention(q_n, kc_n.reshape(kv_shape), vc_n.reshape(kv_shape),
                              ks_n.reshape(kv_shape), vs_n.reshape(kv_shape),
                              kw_n.reshape(kv_shape), vw_n.reshape(kv_shape), g_n,
                              cmp_pos_k[layer], cmp_w1_k[layer], cmp_w2_k[layer],
                              cmp_pos_v[layer], cmp_w1_v[layer], cmp_w2_v[layer], rel_bias_table)
        y_fox = fox_attention(q_f, k_f, v_f, f_l, fox_forget_bias[layer])
        merged = (jax.nn.sigmoid(m_a) * (y_nsa @ w_branch_nsa[layer])
                  + jax.nn.sigmoid(m_b) * (y_fox @ w_branch_fox[layer]))
        x = x + merged @ w_out[layer]
        h = rms_norm(x, ffn_norm_g[layer])
        u, v = jnp.split(h @ w_up[layer], 2, axis=-1)
        u = causal_dwconv(u, conv_w[layer], conv_b[layer])
        x = x + (jax.nn.gelu(u) * v) @ w_down[layer]
    return rms_norm(x, final_norm_g)
```

```python
import functools
import math

import jax
import jax.numpy as jnp
import numpy as np
from jax import lax
from jax.experimental import pallas as pl
from jax.experimental.pallas import tpu as pltpu

HEAD_DIM = 128
NSA_HEADS = 8
NSA_GROUPS = 2
NSA_HPG = NSA_HEADS // NSA_GROUPS
FOX_HEADS = 8
CMP_BLOCK = 32
CMP_STRIDE = 16
SLC_BLOCK = 64
SLC_TOPK = 16
WINDOW = 512
REL_BUCKETS = 32
REL_MAX_DIST = 128
CONV_WIDTH = 3
EPS = 1e-6
NEG = -1e30
FORCED_SCORE = 1e4

LANES = 128
BF16_SUBLANES = 16
VMEM_LIMIT = 56 * 1024 * 1024

NSA_Q_DIM = NSA_HEADS * HEAD_DIM
NSA_KV_DIM = NSA_GROUPS * HEAD_DIM
FOX_DIM = FOX_HEADS * HEAD_DIM
N_QKV = NSA_Q_DIM + 6 * NSA_KV_DIM + 3 * FOX_DIM
GATES_PER_GROUP = 3 * NSA_HPG
N_GATE = (NSA_GROUPS + 1) * LANES

CB_KC = 8
CB_VC = 10
CB_KS = 12
CB_VS = 14
CB_KW = 16
CB_VW = 18
CB_QF = 20
CB_KF = 28
CB_VF = 36

Q_TILE = 128
ROWS = NSA_HPG * Q_TILE
KEY_CHUNK = 512
FOX_Q_TILE = 512
BAND_ROWS = 24
WIN_KEYS = WINDOW + Q_TILE

_NT = (((1,), (1,)), ((), ()))


def _params(sem):
    return pltpu.CompilerParams(dimension_semantics=sem, vmem_limit_bytes=VMEM_LIMIT)


def _norm_rows(x, g):
    return (x * lax.rsqrt(jnp.mean(x * x, axis=-1, keepdims=True) + EPS)) * g


def _norm_matmul_kernel(x_ref, g_ref, w_ref, o_ref, h_ref):
    @pl.when(pl.program_id(1) == 0)
    def _():
        h_ref[...] = _norm_rows(x_ref[...], g_ref[...]).astype(h_ref.dtype)

    o_ref[...] = jnp.dot(h_ref[...], w_ref[...], preferred_element_type=jnp.float32).astype(o_ref.dtype)


def _norm_matmul(x2d, g, w, out_dtype, tm, tn):
    t, d = x2d.shape
    n = w.shape[1]
    return pl.pallas_call(
        _norm_matmul_kernel,
        grid=(t // tm, n // tn),
        in_specs=[
            pl.BlockSpec((tm, d), lambda i, j: (i, 0)),
            pl.BlockSpec((1, d), lambda i, j: (0, 0)),
            pl.BlockSpec((d, tn), lambda i, j: (0, j)),
        ],
        out_specs=pl.BlockSpec((tm, tn), lambda i, j: (i, j)),
        out_shape=jax.ShapeDtypeStruct((t, n), out_dtype),
        scratch_shapes=[pltpu.VMEM((tm, d), jnp.bfloat16)],
        compiler_params=_params(("parallel", "arbitrary")),
        name="norm_in_proj",
    )(x2d, g.reshape(1, d), w)


def _compress_kernel(ch_ref, w1_ref, posf_ref, w2_ref, o_ref):
    half = ch_ref.shape[1]
    ch = ch_ref[...]
    pa = jnp.dot(ch, w1_ref[:half, :], preferred_element_type=jnp.float32)
    pb = jnp.dot(ch, w1_ref[half:, :], preferred_element_type=jnp.float32)
    pos = jnp.dot(posf_ref[...], w1_ref[...], preferred_element_type=jnp.float32)[0:1, :]
    n = pa.shape[0]
    pre = pa + pltpu.roll(pb, n - 1, 0) + pos
    act = jax.nn.gelu(pre)
    o_ref[...] = jnp.dot(act.astype(jnp.bfloat16), w2_ref[...],
                         preferred_element_type=jnp.float32).astype(o_ref.dtype)


def _compress(chunks, w1, posf, w2):
    b, g, n, k = chunks.shape
    return pl.pallas_call(
        _compress_kernel,
        grid=(b, g),
        in_specs=[
            pl.BlockSpec((None, None, n, k), lambda i, j: (i, j, 0, 0)),
            pl.BlockSpec(w1.shape, lambda i, j: (0, 0)),
            pl.BlockSpec(posf.shape, lambda i, j: (0, 0)),
            pl.BlockSpec(w2.shape, lambda i, j: (0, 0)),
        ],
        out_specs=pl.BlockSpec((None, None, n, HEAD_DIM), lambda i, j: (i, j, 0, 0)),
        out_shape=jax.ShapeDtypeStruct((b, g, n, HEAD_DIM), jnp.bfloat16),
        compiler_params=_params(("parallel", "parallel")),
        name="compress_tokens",
    )(chunks, w1, posf, w2)


def _decay_kernel(f_ref, b_ref, tri_ref, o_ref, carry_ref):
    @pl.when(pl.program_id(0) == 0)
    def _():
        carry_ref[...] = jnp.zeros_like(carry_ref)

    x = f_ref[...] + b_ref[...]
    logf = jnp.minimum(x, 0.0) - jnp.log1p(jnp.exp(-jnp.abs(x)))
    carry = carry_ref[...]
    for seg in range(f_ref.shape[1] // LANES):
        part = jnp.dot(logf[:, seg * LANES:(seg + 1) * LANES], tri_ref[...],
                       preferred_element_type=jnp.float32, precision=lax.Precision.HIGHEST) + carry
        o_ref[:, seg * LANES:(seg + 1) * LANES] = part
        carry = part[:, LANES - 1:LANES]
    carry_ref[...] = carry


def _decay_cumsum(f_t, bias_col, width):
    rows, s = f_t.shape
    tri = jnp.asarray(np.triu(np.ones((LANES, LANES), np.float32)))
    return pl.pallas_call(
        _decay_kernel,
        grid=(s // width,),
        in_specs=[
            pl.BlockSpec((rows, width), lambda i: (0, i)),
            pl.BlockSpec((rows, 1), lambda i: (0, 0)),
            pl.BlockSpec((LANES, LANES), lambda i: (0, 0)),
        ],
        out_specs=pl.BlockSpec((rows, width), lambda i: (0, i)),
        out_shape=jax.ShapeDtypeStruct((rows, s), jnp.float32),
        scratch_shapes=[pltpu.VMEM((rows, 1), jnp.float32)],
        compiler_params=_params(("arbitrary",)),
        name="decay_cumsum",
    )(f_t, bias_col, tri)


def _flash_init(m_ref, l_ref, acc_ref):
    m_ref[...] = jnp.full(m_ref.shape, NEG, jnp.float32)
    l_ref[...] = jnp.zeros(l_ref.shape, jnp.float32)
    acc_ref[...] = jnp.zeros(acc_ref.shape, jnp.float32)


def _flash_update(s, v, m_ref, l_ref, acc_ref):
    m_prev = m_ref[...]
    m_new = jnp.maximum(m_prev, jnp.max(s, axis=1, keepdims=True))
    alpha = jnp.exp(m_prev - m_new)
    p = jnp.exp(s - m_new)
    l_ref[...] = alpha * l_ref[...] + jnp.sum(p, axis=1, keepdims=True)
    acc_ref[...] = alpha * acc_ref[...] + jnp.dot(p.astype(v.dtype), v, preferred_element_type=jnp.float32)
    m_ref[...] = m_new


def _stack_heads(q):
    return jnp.concatenate([q[:, h * HEAD_DIM:(h + 1) * HEAD_DIM] for h in range(NSA_HPG)], axis=0)


def _branch_gate(gn, head, branch):
    col = head * 3 + branch
    return jax.nn.sigmoid(gn[:, col:col + 1])


def _cmp_win_kernel(q_ref, kc_ref, vct_ref, band_ref, ovt_ref, kw_ref, vw_ref, wb_ref, gn_ref,
                    ycw_ref, pen_ref, sc_ref):
    qb = pl.program_id(2)
    n_cmp = kc_ref.shape[0]
    n_slc = ovt_ref.shape[0]
    q4 = _stack_heads(q_ref[...])

    sc_ref[...] = lax.dot_general(kc_ref[...], q4, _NT, preferred_element_type=jnp.float32)
    i0 = pl.multiple_of(jnp.maximum(8 * qb - 16, 0), 8)
    sc_ref[pl.ds(i0, BAND_ROWS), :] = sc_ref[pl.ds(i0, BAND_ROWS), :] + band_ref[...]
    row = lax.broadcasted_iota(jnp.int32, (n_cmp, ROWS), 0)
    sc = jnp.where(row < i0 + BAND_ROWS, sc_ref[...], NEG)
    m = jnp.max(sc, axis=0, keepdims=True)
    p = jnp.exp(sc - m)
    l = jnp.sum(p, axis=0, keepdims=True)
    pn = p * jnp.where(m > 0.5 * NEG, 1.0 / l, 0.0)
    oct_ = jnp.dot(vct_ref[...], pn.astype(jnp.bfloat16), preferred_element_type=jnp.float32)
    psum = pn[:, 0:Q_TILE]
    for h in range(1, NSA_HPG):
        psum = psum + pn[:, h * Q_TILE:(h + 1) * Q_TILE]
    imp = jnp.dot(ovt_ref[...], psum.astype(jnp.bfloat16), preferred_element_type=jnp.float32)

    ji = lax.broadcasted_iota(jnp.int32, (n_slc, Q_TILE), 0)
    jf = ji.astype(jnp.float32)
    t = qb * Q_TILE + lax.broadcasted_iota(jnp.int32, (n_slc, Q_TILE), 1)
    cur = t // SLC_BLOCK
    forced = (ji == 0) | (ji == cur) | (ji == cur - 1)
    score = jnp.where(ji <= cur, jnp.where(forced, FORCED_SCORE, imp), -1.0)
    pen_t = jnp.full((n_slc, Q_TILE), NEG, jnp.float32)
    for _ in range(min(SLC_TOPK, n_slc)):
        mx = jnp.max(score, axis=0, keepdims=True)
        idx = jnp.min(jnp.where(score == mx, jf, float(n_slc)), axis=0, keepdims=True)
        pick = jf == idx
        pen_t = jnp.where(pick, 0.0, pen_t)
        score = jnp.where(pick, -2.0, score)
    pen = pen_t.T
    pad = pen_ref.shape[1] - n_slc
    if pad:
        pen = jnp.concatenate([pen, jnp.full((Q_TILE, pad), NEG, jnp.float32)], axis=1)
    pen_ref[...] = pen.astype(pen_ref.dtype)

    ts = pl.multiple_of(jnp.maximum(qb * Q_TILE - WINDOW, 0), Q_TILE)
    sw = lax.dot_general(q4, kw_ref[pl.ds(ts, WIN_KEYS), :], _NT,
                         preferred_element_type=jnp.float32) + wb_ref[...]
    mw = jnp.max(sw, axis=1, keepdims=True)
    pw = jnp.exp(sw - mw)
    lw = jnp.sum(pw, axis=1, keepdims=True)
    ow = jnp.dot(pw.astype(jnp.bfloat16), vw_ref[pl.ds(ts, WIN_KEYS), :],
                 preferred_element_type=jnp.float32) / lw

    gn = gn_ref[...]
    for h in range(NSA_HPG):
        oc_h = oct_[:, h * Q_TILE:(h + 1) * Q_TILE].T
        ow_h = ow[h * Q_TILE:(h + 1) * Q_TILE, :]
        ycw_ref[:, h * HEAD_DIM:(h + 1) * HEAD_DIM] = (
            _branch_gate(gn, h, 0) * oc_h + _branch_gate(gn, h, 2) * ow_h)


def _cmp_win(qkv, kc, vct, band, ovt, wbias, gates):
    b, s, _ = qkv.shape
    n_cmp = kc.shape[2]
    n_slc = ovt.shape[0]
    nq = s // Q_TILE
    pen_w = -(-n_slc // LANES) * LANES
    n_band = band.shape[0] - 1
    n_wb = wbias.shape[0] - 1
    return pl.pallas_call(
        _cmp_win_kernel,
        grid=(b, NSA_GROUPS, nq),
        in_specs=[
            pl.BlockSpec((None, Q_TILE, NSA_HPG * HEAD_DIM), lambda i, g, q: (i, q, g)),
            pl.BlockSpec((None, None, n_cmp, HEAD_DIM), lambda i, g, q: (i, g, 0, 0)),
            pl.BlockSpec((None, None, HEAD_DIM, n_cmp), lambda i, g, q: (i, g, 0, 0)),
            pl.BlockSpec((None, None, BAND_ROWS, ROWS), lambda i, g, q: (jnp.minimum(q, n_band), g, 0, 0)),
            pl.BlockSpec((n_slc, n_cmp), lambda i, g, q: (0, 0)),
            pl.BlockSpec((None, s, HEAD_DIM), lambda i, g, q: (i, 0, CB_KW + g)),
            pl.BlockSpec((None, s, HEAD_DIM), lambda i, g, q: (i, 0, CB_VW + g)),
            pl.BlockSpec((None, None, ROWS, WIN_KEYS), lambda i, g, q: (jnp.minimum(q, n_wb), g, 0, 0)),
            pl.BlockSpec((None, Q_TILE, LANES), lambda i, g, q: (i, q, g)),
        ],
        out_specs=[
            pl.BlockSpec((None, Q_TILE, NSA_HPG * HEAD_DIM), lambda i, g, q: (i, q, g)),
            pl.BlockSpec((None, None, Q_TILE, pen_w), lambda i, g, q: (i, g, q, 0)),
        ],
        out_shape=[
            jax.ShapeDtypeStruct((b, s, NSA_Q_DIM), jnp.float32),
            jax.ShapeDtypeStruct((b, NSA_GROUPS, s, pen_w), jnp.bfloat16),
        ],
        scratch_shapes=[pltpu.VMEM((n_cmp, ROWS), jnp.float32)],
        compiler_params=_params(("parallel", "parallel", "arbitrary")),
        name="nsa_cmp_topk_win",
    )(qkv, kc, vct, band, ovt, qkv, qkv, wbias, gates)


def _sel_kernel(q_ref, pen_ref, kaug_ref, vs_ref, tb_ref, ycw_ref, gn_ref, y_ref,
                qaug_ref, m_ref, l_ref, acc_ref):
    qb = pl.program_id(2)
    n_half = pen_ref.shape[1] // LANES
    tail_w = tb_ref.shape[1]
    chunks_per_half = LANES * SLC_BLOCK // KEY_CHUNK
    blocks_per_chunk = KEY_CHUNK // SLC_BLOCK

    te = (qb + 1) * Q_TILE
    nf = jnp.maximum((te - tail_w + KEY_CHUNK - 1) // KEY_CHUNK, 0)
    ts = pl.multiple_of(jnp.maximum(te - tail_w, 0), Q_TILE)
    jb0 = ts // SLC_BLOCK

    q4 = _stack_heads(q_ref[...])
    pen = pen_ref[...]
    for hf in range(n_half):
        pen_h = pen[:, hf * LANES:(hf + 1) * LANES]
        qaug_ref[hf] = jnp.concatenate([q4, jnp.concatenate([pen_h] * NSA_HPG, axis=0)], axis=1)
    c = lax.broadcasted_iota(jnp.int32, (Q_TILE, LANES), 1)
    pen_tail = pen[:, 0:LANES]
    blk = c
    for hf in range(1, n_half):
        in_lower = (c + (hf - 1) * LANES >= jb0) & (jb0 < hf * LANES)
        pen_tail = jnp.where(in_lower, pen_tail, pen[:, hf * LANES:(hf + 1) * LANES])
        blk = jnp.where(in_lower, blk, c + hf * LANES)
    pen_tail = jnp.where(blk < nf * blocks_per_chunk, jnp.asarray(NEG, pen_tail.dtype), pen_tail)
    qaug_ref[n_half] = jnp.concatenate([q4, jnp.concatenate([pen_tail] * NSA_HPG, axis=0)], axis=1)

    _flash_init(m_ref, l_ref, acc_ref)

    def far(cidx, carry):
        k0 = pl.multiple_of(cidx * KEY_CHUNK, KEY_CHUNK)
        s = lax.dot_general(qaug_ref[cidx // chunks_per_half], kaug_ref[pl.ds(k0, KEY_CHUNK), :], _NT,
                            preferred_element_type=jnp.float32)
        _flash_update(s, vs_ref[pl.ds(k0, KEY_CHUNK), :], m_ref, l_ref, acc_ref)
        return carry

    lax.fori_loop(0, nf, far, 0)

    s = lax.dot_general(qaug_ref[n_half], kaug_ref[pl.ds(ts, tail_w), :], _NT,
                        preferred_element_type=jnp.float32) + tb_ref[...]
    _flash_update(s, vs_ref[pl.ds(ts, tail_w), :], m_ref, l_ref, acc_ref)

    o = acc_ref[...] / l_ref[...]
    gn = gn_ref[...]
    for h in range(NSA_HPG):
        y_ref[:, h * HEAD_DIM:(h + 1) * HEAD_DIM] = (
            ycw_ref[:, h * HEAD_DIM:(h + 1) * HEAD_DIM]
            + _branch_gate(gn, h, 1) * o[h * Q_TILE:(h + 1) * Q_TILE, :]).astype(y_ref.dtype)


def _sel(qkv, pen, kaug, tbias, ycw, gates):
    b, s, _ = qkv.shape
    nq = s // Q_TILE
    pen_w = pen.shape[3]
    tail_w = tbias.shape[3]
    n_tb = tbias.shape[0] - 1
    return pl.pallas_call(
        _sel_kernel,
        grid=(b, NSA_GROUPS, nq),
        in_specs=[
            pl.BlockSpec((None, Q_TILE, NSA_HPG * HEAD_DIM), lambda i, g, q: (i, q, g)),
            pl.BlockSpec((None, None, Q_TILE, pen_w), lambda i, g, q: (i, g, q, 0)),
            pl.BlockSpec((None, None, s, 2 * HEAD_DIM), lambda i, g, q: (i, g, 0, 0)),
            pl.BlockSpec((None, s, HEAD_DIM), lambda i, g, q: (i, 0, CB_VS + g)),
            pl.BlockSpec((None, None, ROWS, tail_w), lambda i, g, q: (jnp.minimum(q, n_tb), g, 0, 0)),
            pl.BlockSpec((None, Q_TILE, NSA_HPG * HEAD_DIM), lambda i, g, q: (i, q, g)),
            pl.BlockSpec((None, Q_TILE, LANES), lambda i, g, q: (i, q, g)),
        ],
        out_specs=pl.BlockSpec((None, Q_TILE, NSA_HPG * HEAD_DIM), lambda i, g, q: (i, q, g)),
        out_shape=jax.ShapeDtypeStruct((b, s, NSA_Q_DIM), jnp.bfloat16),
        scratch_shapes=[
            pltpu.VMEM((pen_w // LANES + 1, ROWS, 2 * HEAD_DIM), jnp.bfloat16),
            pltpu.VMEM((ROWS, 1), jnp.float32),
            pltpu.VMEM((ROWS, 1), jnp.float32),
            pltpu.VMEM((ROWS, HEAD_DIM), jnp.float32),
        ],
        compiler_params=_params(("parallel", "parallel", "arbitrary")),
        name="nsa_selected",
    )(qkv, pen, kaug, qkv, tbias, ycw, gates)


def _fox_kernel(q_ref, k_ref, v_ref, ck_ref, o_ref, m_ref, l_ref, acc_ref):
    qi = pl.program_id(2)
    tq = q_ref.shape[0]
    q = q_ref[...]
    _flash_init(m_ref, l_ref, acc_ref)

    def scores(cidx):
        k0 = pl.multiple_of(cidx * tq, tq)
        s = lax.dot_general(q, k_ref[pl.ds(k0, tq), :], _NT, preferred_element_type=jnp.float32)
        return s - ck_ref[cidx], v_ref[pl.ds(k0, tq), :]

    def far(cidx, carry):
        s, v = scores(cidx)
        _flash_update(s, v, m_ref, l_ref, acc_ref)
        return carry

    lax.fori_loop(0, qi, far, 0)

    s, v = scores(qi)
    r = lax.broadcasted_iota(jnp.int32, s.shape, 0)
    cc = lax.broadcasted_iota(jnp.int32, s.shape, 1)
    _flash_update(jnp.where(r >= cc, s, NEG), v, m_ref, l_ref, acc_ref)
    o_ref[...] = (acc_ref[...] / l_ref[...]).astype(o_ref.dtype)


def _fox(qkv, ck, tq):
    b, s, _ = qkv.shape
    return pl.pallas_call(
        _fox_kernel,
        grid=(b, FOX_HEADS, s // tq),
        in_specs=[
            pl.BlockSpec((None, tq, HEAD_DIM), lambda i, h, q: (i, q, CB_QF + h)),
            pl.BlockSpec((None, s, HEAD_DIM), lambda i, h, q: (i, 0, CB_KF + h)),
            pl.BlockSpec((None, s, HEAD_DIM), lambda i, h, q: (i, 0, CB_VF + h)),
            pl.BlockSpec((None, None, s // tq, 1, tq), lambda i, h, q: (i, h, 0, 0, 0)),
        ],
        out_specs=pl.BlockSpec((None, tq, HEAD_DIM), lambda i, h, q: (i, q, h)),
        out_shape=jax.ShapeDtypeStruct((b, s, FOX_DIM), jnp.bfloat16),
        scratch_shapes=[
            pltpu.VMEM((tq, 1), jnp.float32),
            pltpu.VMEM((tq, 1), jnp.float32),
            pltpu.VMEM((tq, HEAD_DIM), jnp.float32),
        ],
        compiler_params=_params(("parallel", "parallel", "arbitrary")),
        name="fox_attention",
    )(qkv, qkv, qkv, ck)


def _merge_kernel(x_ref, g_ref, wa_ref, wb_ref, pa_ref, pb_ref, wo_ref, ya_ref, yb_ref, o_ref,
                  h_ref, acc_ref):
    j = pl.program_id(1)

    @pl.when(j == 0)
    def _():
        h_ref[...] = _norm_rows(x_ref[...], g_ref[...]).astype(h_ref.dtype)
        acc_ref[...] = jnp.zeros_like(acc_ref)

    h = h_ref[...]
    ga = jax.nn.sigmoid(jnp.dot(h, wa_ref[...], preferred_element_type=jnp.float32))
    gb = jax.nn.sigmoid(jnp.dot(h, wb_ref[...], preferred_element_type=jnp.float32))
    a = jnp.dot(ya_ref[...], pa_ref[...], preferred_element_type=jnp.float32)
    bb = jnp.dot(yb_ref[...], pb_ref[...], preferred_element_type=jnp.float32)
    merged = (ga * a + gb * bb).astype(jnp.bfloat16)
    acc_ref[...] += jnp.dot(merged, wo_ref[...], preferred_element_type=jnp.float32)

    @pl.when(j == pl.num_programs(1) - 1)
    def _():
        o_ref[...] = x_ref[...] + acc_ref[...]


def _merge(x2d, g, w_ma, w_mb, p_a, p_b, w_out, y_a, y_b, tm, tn):
    t, d = x2d.shape
    ka = p_a.shape[0]
    kb = p_b.shape[0]
    return pl.pallas_call(
        _merge_kernel,
        grid=(t // tm, d // tn),
        in_specs=[
            pl.BlockSpec((tm, d), lambda i, j: (i, 0)),
            pl.BlockSpec((1, d), lambda i, j: (0, 0)),
            pl.BlockSpec((d, tn), lambda i, j: (0, j)),
            pl.BlockSpec((d, tn), lambda i, j: (0, j)),
            pl.BlockSpec((ka, tn), lambda i, j: (0, j)),
            pl.BlockSpec((kb, tn), lambda i, j: (0, j)),
            pl.BlockSpec((tn, d), lambda i, j: (j, 0)),
            pl.BlockSpec((tm, ka), lambda i, j: (i, 0)),
            pl.BlockSpec((tm, kb), lambda i, j: (i, 0)),
        ],
        out_specs=pl.BlockSpec((tm, d), lambda i, j: (i, 0)),
        out_shape=jax.ShapeDtypeStruct((t, d), jnp.float32),
        scratch_shapes=[pltpu.VMEM((tm, d), jnp.bfloat16), pltpu.VMEM((tm, d), jnp.float32)],
        compiler_params=_params(("parallel", "arbitrary")),
        name="merge_out_proj",
    )(x2d, g.reshape(1, d), w_ma, w_mb, p_a, p_b, w_out, y_a, y_b)


def _ffn_kernel(x_ref, xh_ref, g_ref, wu_ref, wv_ref, cw_ref, cb_ref, wd_ref, gf_ref, o_ref,
                h_ref, u_ref, acc_ref, *, seq, tm):
    i = pl.program_id(0)
    j = pl.program_id(1)
    halo = BF16_SUBLANES

    @pl.when(j == 0)
    def _():
        g = g_ref[...]
        keep = jnp.where((i * tm) % seq == 0, 0.0, 1.0)
        h_ref[0:halo, :] = (_norm_rows(xh_ref[...], g) * keep).astype(h_ref.dtype)
        h_ref[halo:, :] = _norm_rows(x_ref[...], g).astype(h_ref.dtype)
        acc_ref[...] = jnp.zeros_like(acc_ref)

    u_ref[...] = jnp.dot(h_ref[...], wu_ref[...], preferred_element_type=jnp.float32)
    v = jnp.dot(h_ref[halo:, :], wv_ref[...], preferred_element_type=jnp.float32)
    conv = cb_ref[...]
    for k in range(CONV_WIDTH):
        conv = conv + cw_ref[k:k + 1, :] * u_ref[pl.ds(halo - (CONV_WIDTH - 1) + k, tm), :]
    act = (jax.nn.gelu(conv) * v).astype(jnp.bfloat16)
    acc_ref[...] += jnp.dot(act, wd_ref[...], preferred_element_type=jnp.float32)

    @pl.when(j == pl.num_programs(1) - 1)
    def _():
        o_ref[...] = _norm_rows(x_ref[...] + acc_ref[...], gf_ref[...])


def _ffn(x2d, g, w_up, conv_w, conv_b, w_down, g_final, seq, tm, tn):
    t, d = x2d.shape
    d_ff = w_down.shape[0]
    nt = d_ff // tn
    halo = BF16_SUBLANES
    per = tm // halo
    return pl.pallas_call(
        functools.partial(_ffn_kernel, seq=seq, tm=tm),
        grid=(t // tm, nt),
        in_specs=[
            pl.BlockSpec((tm, d), lambda i, j: (i, 0)),
            pl.BlockSpec((halo, d), lambda i, j: (jnp.maximum(i * per - 1, 0), 0)),
            pl.BlockSpec((1, d), lambda i, j: (0, 0)),
            pl.BlockSpec((d, tn), lambda i, j: (0, j)),
            pl.BlockSpec((d, tn), lambda i, j: (0, nt + j)),
            pl.BlockSpec((CONV_WIDTH, tn), lambda i, j: (0, j)),
            pl.BlockSpec((1, tn), lambda i, j: (0, j)),
            pl.BlockSpec((tn, d), lambda i, j: (j, 0)),
            pl.BlockSpec((1, d), lambda i, j: (0, 0)),
        ],
        out_specs=pl.BlockSpec((tm, d), lambda i, j: (i, 0)),
        out_shape=jax.ShapeDtypeStruct((t, d), jnp.float32),
        scratch_shapes=[
            pltpu.VMEM((tm + halo, d), jnp.bfloat16),
            pltpu.VMEM((tm + halo, tn), jnp.float32),
            pltpu.VMEM((tm, d), jnp.float32),
        ],
        compiler_params=_params(("parallel", "arbitrary")),
        name="ffn_final_norm",
    )(x2d, x2d, g.reshape(1, d), w_up, w_up, conv_w, conv_b.reshape(1, d_ff), w_down, g_final.reshape(1, d))


def _t5_bucket_np(dist):
    n = np.maximum(dist, 0)
    max_exact = REL_BUCKETS // 2
    nf = np.maximum(n, 1).astype(np.float32)
    large = max_exact + (np.log(nf / np.float32(max_exact)) / np.float32(math.log(REL_MAX_DIST / max_exact))
                         * np.float32(REL_BUCKETS - max_exact)).astype(np.int32)
    return np.where(n < max_exact, n, np.minimum(large, REL_BUCKETS - 1)).astype(np.int32)


def _bias_by_distance(dist, rel_table, far_shift):
    vals = rel_table[jnp.asarray(_t5_bucket_np(np.clip(dist, 0, REL_MAX_DIST)))]
    if far_shift:
        vals = vals - rel_table[REL_BUCKETS - 1]
    return jnp.where(jnp.asarray(dist >= 0)[..., None], vals, NEG)


def _rows_by_head(vals):
    v, q, c, _ = vals.shape
    return vals.transpose(0, 3, 1, 2).reshape(v, NSA_GROUPS, NSA_HPG * q, c)


def _window_bias(rel_table):
    tl = np.arange(Q_TILE)[:, None]
    col = np.arange(WIN_KEYS)[None, :]
    dist = np.stack([v * Q_TILE + tl - col for v in range(WINDOW // Q_TILE)] + [tl + WINDOW - col])
    dist = np.where(dist < WINDOW, dist, -1)
    return _rows_by_head(_bias_by_distance(dist, rel_table, False))


def _tail_bias(rel_table, tail_w):
    tl = np.arange(Q_TILE)[:, None]
    col = np.arange(tail_w)[None, :]
    dist = np.stack([v * Q_TILE + tl - col for v in range(tail_w // Q_TILE - 1)]
                    + [tl + tail_w - Q_TILE - col])
    return _rows_by_head(_bias_by_distance(dist, rel_table, True))


def _band_bias(rel_table):
    tl = np.arange(Q_TILE)[None, :]
    r = np.arange(BAND_ROWS)[:, None]
    dist = np.stack([tl - CMP_STRIDE * (r - off) - (CMP_BLOCK - 1) for off in (0, 8, 16)])
    vals = _bias_by_distance(dist, rel_table, True)
    v, rr, q, _ = vals.shape
    return vals.transpose(0, 1, 3, 2).reshape(v, rr, NSA_GROUPS, NSA_HPG * q).transpose(0, 2, 1, 3)


def _overlap_t(n_cmp_pad, n_slc):
    i = np.arange(n_cmp_pad)[None, :]
    jj = np.arange(n_slc)[:, None]
    c_start = i * CMP_STRIDE
    ov = (c_start < jj * SLC_BLOCK + SLC_BLOCK) & (c_start + CMP_BLOCK - 1 >= jj * SLC_BLOCK)
    ov = ov & (i < n_cmp_pad - 1)
    return jnp.asarray(ov.astype(np.float32), jnp.bfloat16)


def _block_onehot(seq):
    blk = (np.arange(seq) // SLC_BLOCK) % LANES
    return jnp.asarray((blk[:, None] == np.arange(LANES)[None, :]).astype(np.float32), jnp.bfloat16)


def _pick_tile(n, pref):
    return pref if n % pref == 0 else n


def kernel(x, attn_norm_g, w_in, cmp_pos_k, cmp_w1_k, cmp_w2_k, cmp_pos_v, cmp_w1_v, cmp_w2_v,
           rel_bias_table, fox_forget_bias, w_branch_nsa, w_branch_fox, w_out,
           ffn_norm_g, w_up, conv_w, conv_b, w_down, final_norm_g):
    assert w_in.shape[0] == 1, "the final norm is fused into the single layer's FFN kernel"
    bsz, seq, d = x.shape
    t = bsz * seq
    bf = jnp.bfloat16
    scale = HEAD_DIM ** -0.5
    x2d = x.reshape(t, d)
    w_in = w_in[0]

    o = np.cumsum([0, NSA_Q_DIM] + [NSA_KV_DIM] * 6 + [3 * NSA_HEADS, FOX_DIM, FOX_DIM, FOX_DIM, FOX_HEADS, d, d])
    w_qkv = jnp.concatenate([w_in[:, o[0]:o[1]] * scale, w_in[:, o[1]:o[7]],
                             w_in[:, o[8]:o[9]] * scale, w_in[:, o[9]:o[11]]], axis=1).astype(bf)
    gate_cols = []
    for grp in range(NSA_GROUPS):
        gate_cols += [w_in[:, o[7] + grp * GATES_PER_GROUP:o[7] + (grp + 1) * GATES_PER_GROUP],
                      jnp.zeros((d, LANES - GATES_PER_GROUP), w_in.dtype)]
    gate_cols += [w_in[:, o[11]:o[12]], jnp.zeros((d, LANES - FOX_HEADS), w_in.dtype)]
    w_gate = jnp.concatenate(gate_cols, axis=1).astype(bf)
    w_ma = w_in[:, o[12]:o[13]].astype(bf)
    w_mb = w_in[:, o[13]:o[14]].astype(bf)

    tm = _pick_tile(t, 1024)
    qkv = _norm_matmul(x2d, attn_norm_g[0], w_qkv, bf, tm, 512).reshape(bsz, seq, N_QKV)
    gates = _norm_matmul(x2d, attn_norm_g[0], w_gate, jnp.float32, tm, LANES).reshape(bsz, seq, N_GATE)

    n_ch = seq // CMP_STRIDE

    def chunked(cb):
        sl = qkv[:, :, cb * HEAD_DIM:(cb + NSA_GROUPS) * HEAD_DIM]
        sl = sl.reshape(bsz, n_ch, CMP_STRIDE, NSA_GROUPS, HEAD_DIM).transpose(0, 3, 1, 2, 4)
        return sl.reshape(bsz, NSA_GROUPS, n_ch, CMP_STRIDE * HEAD_DIM)

    def posflat(pos):
        return jnp.broadcast_to(pos.reshape(1, CMP_BLOCK * HEAD_DIM), (8, CMP_BLOCK * HEAD_DIM)).astype(bf)

    kc = _compress(chunked(CB_KC), cmp_w1_k[0].astype(bf), posflat(cmp_pos_k[0]), cmp_w2_k[0].astype(bf))
    vc = _compress(chunked(CB_VC), cmp_w1_v[0].astype(bf), posflat(cmp_pos_v[0]), cmp_w2_v[0].astype(bf))
    vct = vc.transpose(0, 1, 3, 2)

    f_t = gates[:, :, NSA_GROUPS * LANES:NSA_GROUPS * LANES + FOX_HEADS]
    f_t = f_t.transpose(0, 2, 1).reshape(bsz * FOX_HEADS, seq)
    bias_col = jnp.tile(fox_forget_bias[0].astype(jnp.float32), bsz).reshape(bsz * FOX_HEADS, 1)
    cum = _decay_cumsum(f_t, bias_col, _pick_tile(seq, 2048))
    fox_tq = _pick_tile(seq, FOX_Q_TILE)
    ck = cum.reshape(bsz, FOX_HEADS, seq // fox_tq, 1, fox_tq)

    n_slc = seq // SLC_BLOCK
    tail_w = min(1024, seq)
    ycw, pen = _cmp_win(qkv, kc, vct, _band_bias(rel_bias_table), _overlap_t(n_ch, n_slc),
                        _window_bias(rel_bias_table), gates)
    ks = qkv[:, :, CB_KS * HEAD_DIM:(CB_KS + NSA_GROUPS) * HEAD_DIM]
    ks = ks.reshape(bsz, seq, NSA_GROUPS, HEAD_DIM).transpose(0, 2, 1, 3)
    kaug = jnp.concatenate([ks, jnp.broadcast_to(_block_onehot(seq), ks.shape)], axis=-1)
    y_nsa = _sel(qkv, pen, kaug, _tail_bias(rel_bias_table, tail_w), ycw, gates)

    y_fox = _fox(qkv, ck, fox_tq)

    tm2 = _pick_tile(t, 512)
    x_mid = _merge(x2d, attn_norm_g[0], w_ma, w_mb, w_branch_nsa[0].astype(bf), w_branch_fox[0].astype(bf),
                   w_out[0].astype(bf), y_nsa.reshape(t, NSA_Q_DIM), y_fox.reshape(t, FOX_DIM),
                   tm2, _pick_tile(d, 512))
    d_ff = w_down.shape[1]
    out = _ffn(x_mid, ffn_norm_g[0], w_up[0].astype(bf), conv_w[0], conv_b[0], w_down[0].astype(bf),
               final_norm_g, seq, tm2, _pick_tile(d_ff, 512))
    return out.reshape(bsz, seq, d)
```

```python
import functools
import math

import jax
import jax.numpy as jnp
import numpy as np
from jax import lax
from jax.experimental import pallas as pl
from jax.experimental.pallas import tpu as pltpu

HEAD_DIM = 128
NSA_HEADS = 8
NSA_GROUPS = 2
NSA_HPG = NSA_HEADS // NSA_GROUPS
FOX_HEADS = 8
CMP_BLOCK = 32
CMP_STRIDE = 16
SLC_BLOCK = 64
SLC_TOPK = 16
WINDOW = 512
REL_BUCKETS = 32
REL_MAX_DIST = 128
CONV_WIDTH = 3
EPS = 1e-6
NEG = -1e30
FORCED_SCORE = 1e4
LOG2E = math.log2(math.e)

LANES = 128
BF16_SUBLANES = 16
VMEM_LIMIT = 56 * 1024 * 1024

NSA_Q_DIM = NSA_HEADS * HEAD_DIM
NSA_KV_DIM = NSA_GROUPS * HEAD_DIM
FOX_DIM = FOX_HEADS * HEAD_DIM
N_QKV = NSA_Q_DIM + 6 * NSA_KV_DIM + 3 * FOX_DIM
GATES_PER_GROUP = 3 * NSA_HPG
N_GATE = (NSA_GROUPS + 1) * LANES

CB_KC = 8
CB_VC = 10
CB_KS = 12
CB_VS = 14
CB_KW = 16
CB_VW = 18
CB_QF = 20
CB_KF = 28
CB_VF = 36

Q_TILE = 128
ROWS = NSA_HPG * Q_TILE
KEY_CHUNK = 512
KEY_PIECES = KEY_CHUNK // LANES
FOX_Q_TILE = 512
N_SPLIT = 2
DECAY_TERMS = 3
BAND_ROWS = 24
WIN_KEYS = WINDOW + Q_TILE

_NT = (((1,), (1,)), ((), ()))


def _params(sem):
    return pltpu.CompilerParams(dimension_semantics=sem, vmem_limit_bytes=VMEM_LIMIT)


def _norm_rows(x, g):
    return (x * lax.rsqrt(jnp.mean(x * x, axis=-1, keepdims=True) + EPS)) * g


def _norm_matmul_kernel(x_ref, g_ref, w_ref, o_ref, h_ref):
    @pl.when(pl.program_id(1) == 0)
    def _():
        h_ref[...] = _norm_rows(x_ref[...], g_ref[...]).astype(h_ref.dtype)

    o_ref[...] = jnp.dot(h_ref[...], w_ref[...], preferred_element_type=jnp.float32).astype(o_ref.dtype)


def _norm_matmul(x2d, g, w, out_dtype, tm, tn):
    t, d = x2d.shape
    n = w.shape[1]
    return pl.pallas_call(
        _norm_matmul_kernel,
        grid=(t // tm, n // tn),
        in_specs=[
            pl.BlockSpec((tm, d), lambda i, j: (i, 0)),
            pl.BlockSpec((1, d), lambda i, j: (0, 0)),
            pl.BlockSpec((d, tn), lambda i, j: (0, j)),
        ],
        out_specs=pl.BlockSpec((tm, tn), lambda i, j: (i, j)),
        out_shape=jax.ShapeDtypeStruct((t, n), out_dtype),
        scratch_shapes=[pltpu.VMEM((tm, d), jnp.bfloat16)],
        compiler_params=_params(("parallel", "arbitrary")),
        name="norm_in_proj",
    )(x2d, g.reshape(1, d), w)


def _compress_kernel(ch_ref, w1_ref, posf_ref, w2_ref, o_ref):
    half = ch_ref.shape[1]
    ch = ch_ref[...]
    pa = jnp.dot(ch, w1_ref[:half, :], preferred_element_type=jnp.float32)
    pb = jnp.dot(ch, w1_ref[half:, :], preferred_element_type=jnp.float32)
    pos = jnp.dot(posf_ref[...], w1_ref[...], preferred_element_type=jnp.float32)[0:1, :]
    n = pa.shape[0]
    pre = pa + pltpu.roll(pb, n - 1, 0) + pos
    act = jax.nn.gelu(pre)
    o_ref[...] = jnp.dot(act.astype(jnp.bfloat16), w2_ref[...],
                         preferred_element_type=jnp.float32).astype(o_ref.dtype)


def _compress(chunks, w1, posf, w2):
    b, g, n, k = chunks.shape
    return pl.pallas_call(
        _compress_kernel,
        grid=(b, g),
        in_specs=[
            pl.BlockSpec((None, None, n, k), lambda i, j: (i, j, 0, 0)),
            pl.BlockSpec(w1.shape, lambda i, j: (0, 0)),
            pl.BlockSpec(posf.shape, lambda i, j: (0, 0)),
            pl.BlockSpec(w2.shape, lambda i, j: (0, 0)),
        ],
        out_specs=pl.BlockSpec((None, None, n, HEAD_DIM), lambda i, j: (i, j, 0, 0)),
        out_shape=jax.ShapeDtypeStruct((b, g, n, HEAD_DIM), jnp.bfloat16),
        compiler_params=_params(("parallel", "parallel")),
        name="compress_tokens",
    )(chunks, w1, posf, w2)


def _decay_kernel(f_ref, b_ref, tri_ref, o_ref, carry_ref):
    @pl.when(pl.program_id(0) == 0)
    def _():
        carry_ref[...] = jnp.zeros_like(carry_ref)

    x = f_ref[...] + b_ref[...]
    logf = (jnp.minimum(x, 0.0) - jnp.log1p(jnp.exp(-jnp.abs(x)))) * LOG2E
    carry = carry_ref[...]
    for seg in range(f_ref.shape[1] // LANES):
        part = jnp.dot(logf[:, seg * LANES:(seg + 1) * LANES], tri_ref[...],
                       preferred_element_type=jnp.float32, precision=lax.Precision.HIGHEST) + carry
        carry = part[:, LANES - 1:LANES]
        rest = part
        for term in range(DECAY_TERMS):
            piece = rest.astype(o_ref.dtype)
            o_ref[term, :, seg * LANES:(seg + 1) * LANES] = piece
            rest = rest - piece.astype(jnp.float32)
    carry_ref[...] = carry


def _decay_cumsum(f_t, bias_col, width):
    rows, s = f_t.shape
    tri = jnp.asarray(np.triu(np.ones((LANES, LANES), np.float32)))
    return pl.pallas_call(
        _decay_kernel,
        grid=(s // width,),
        in_specs=[
            pl.BlockSpec((rows, width), lambda i: (0, i)),
            pl.BlockSpec((rows, 1), lambda i: (0, 0)),
            pl.BlockSpec((LANES, LANES), lambda i: (0, 0)),
        ],
        out_specs=pl.BlockSpec((DECAY_TERMS, rows, width), lambda i: (0, 0, i)),
        out_shape=jax.ShapeDtypeStruct((DECAY_TERMS, rows, s), jnp.bfloat16),
        scratch_shapes=[pltpu.VMEM((rows, 1), jnp.float32)],
        compiler_params=_params(("arbitrary",)),
        name="decay_cumsum",
    )(f_t, bias_col, tri)


def _flash_init(m_ref, l_ref, acc_ref):
    m_ref[...] = jnp.full(m_ref.shape, NEG, jnp.float32)
    l_ref[...] = jnp.zeros(l_ref.shape, jnp.float32)
    acc_ref[...] = jnp.zeros(acc_ref.shape, jnp.float32)


def _flash_step(kaug, vt, get_q, width, m_ref, l_ref, acc_ref, extra=None):
    w = width // N_SPLIT
    strips = [(i * w, (i + 1) * w) for i in range(N_SPLIT)]
    scores = [jnp.dot(kaug, get_q(lo, hi), preferred_element_type=jnp.float32) for lo, hi in strips]
    for (lo, hi), s in zip(strips, scores):
        if extra is not None:
            s = extra(s, lo, hi)
        m_prev = m_ref[:, lo:hi]
        m_new = jnp.maximum(m_prev, jnp.max(s, axis=0, keepdims=True))
        alpha = jnp.exp2(m_prev - m_new)
        p = jnp.exp2(s - m_new)
        l_ref[:, lo:hi] = alpha * l_ref[:, lo:hi] + jnp.sum(p, axis=0, keepdims=True)
        acc_ref[:, lo:hi] = alpha * acc_ref[:, lo:hi] + jnp.dot(
            vt, p.astype(vt.dtype), preferred_element_type=jnp.float32)
        m_ref[:, lo:hi] = m_new


def _key_chunk(k_ref, aug_ref, vt_ref, k0, n_keys):
    kaug = jnp.concatenate([k_ref[pl.ds(k0, n_keys), :], aug_ref[pl.ds(k0, n_keys), :]], axis=1)
    p0 = k0 // LANES
    vt = jnp.concatenate([vt_ref[p0 + j] for j in range(n_keys // LANES)], axis=1)
    return kaug, vt


def _stack_heads(q):
    return jnp.concatenate([q[:, h * HEAD_DIM:(h + 1) * HEAD_DIM] for h in range(NSA_HPG)], axis=0)


def _branch_gate(gn, head, branch):
    col = head * 3 + branch
    return jax.nn.sigmoid(gn[:, col:col + 1])


def _cmp_win_kernel(q_ref, kc_ref, vct_ref, band_ref, ovt_ref, kw_ref, vw_ref, wb_ref, gn_ref,
                    ycw_ref, pen_ref, sc_ref):
    qb = pl.program_id(2)
    n_cmp = kc_ref.shape[0]
    n_slc = ovt_ref.shape[0]
    q4 = _stack_heads(q_ref[...])

    sc_ref[...] = lax.dot_general(kc_ref[...], q4, _NT, preferred_element_type=jnp.float32)
    i0 = pl.multiple_of(jnp.maximum(8 * qb - 16, 0), 8)
    sc_ref[pl.ds(i0, BAND_ROWS), :] = sc_ref[pl.ds(i0, BAND_ROWS), :] + band_ref[...]
    row = lax.broadcasted_iota(jnp.int32, (n_cmp, ROWS), 0)
    sc = jnp.where(row < i0 + BAND_ROWS, sc_ref[...], NEG)
    m = jnp.max(sc, axis=0, keepdims=True)
    p = jnp.exp2(sc - m)
    l = jnp.sum(p, axis=0, keepdims=True)
    pn = p * jnp.where(m > 0.5 * NEG, 1.0 / l, 0.0)
    oct_ = jnp.dot(vct_ref[...], pn.astype(jnp.bfloat16), preferred_element_type=jnp.float32)
    psum = pn[:, 0:Q_TILE]
    for h in range(1, NSA_HPG):
        psum = psum + pn[:, h * Q_TILE:(h + 1) * Q_TILE]
    imp = jnp.dot(ovt_ref[...], psum.astype(jnp.bfloat16), preferred_element_type=jnp.float32)

    ji = lax.broadcasted_iota(jnp.int32, (n_slc, Q_TILE), 0)
    jf = ji.astype(jnp.float32)
    t = qb * Q_TILE + lax.broadcasted_iota(jnp.int32, (n_slc, Q_TILE), 1)
    cur = t // SLC_BLOCK
    forced = (ji == 0) | (ji == cur) | (ji == cur - 1)
    score = jnp.where(ji <= cur, jnp.where(forced, FORCED_SCORE, imp), -1.0)
    pen_t = jnp.full((n_slc, Q_TILE), NEG, jnp.float32)
    for _ in range(min(SLC_TOPK, n_slc)):
        mx = jnp.max(score, axis=0, keepdims=True)
        idx = jnp.min(jnp.where(score == mx, jf, float(n_slc)), axis=0, keepdims=True)
        pick = jf == idx
        pen_t = jnp.where(pick, 0.0, pen_t)
        score = jnp.where(pick, -2.0, score)
    pad = pen_ref.shape[0] - n_slc
    if pad:
        pen_t = jnp.concatenate([pen_t, jnp.full((pad, Q_TILE), NEG, jnp.float32)], axis=0)
    pen_ref[...] = pen_t.astype(pen_ref.dtype)

    ts = pl.multiple_of(jnp.maximum(qb * Q_TILE - WINDOW, 0), Q_TILE)
    sw = lax.dot_general(q4, kw_ref[pl.ds(ts, WIN_KEYS), :], _NT,
                         preferred_element_type=jnp.float32) + wb_ref[...]
    mw = jnp.max(sw, axis=1, keepdims=True)
    pw = jnp.exp2(sw - mw)
    lw = jnp.sum(pw, axis=1, keepdims=True)
    ow = jnp.dot(pw.astype(jnp.bfloat16), vw_ref[pl.ds(ts, WIN_KEYS), :],
                 preferred_element_type=jnp.float32) / lw

    gn = gn_ref[...]
    for h in range(NSA_HPG):
        oc_h = oct_[:, h * Q_TILE:(h + 1) * Q_TILE].T
        ow_h = ow[h * Q_TILE:(h + 1) * Q_TILE, :]
        ycw_ref[:, h * HEAD_DIM:(h + 1) * HEAD_DIM] = (
            _branch_gate(gn, h, 0) * oc_h + _branch_gate(gn, h, 2) * ow_h)


def _cmp_win(qkv, kc, vct, band, ovt, wbias, gates):
    b, s, _ = qkv.shape
    n_cmp = kc.shape[2]
    n_slc = ovt.shape[0]
    nq = s // Q_TILE
    pen_w = -(-n_slc // LANES) * LANES
    n_band = band.shape[0] - 1
    n_wb = wbias.shape[0] - 1
    return pl.pallas_call(
        _cmp_win_kernel,
        grid=(b, NSA_GROUPS, nq),
        in_specs=[
            pl.BlockSpec((None, Q_TILE, NSA_HPG * HEAD_DIM), lambda i, g, q: (i, q, g)),
            pl.BlockSpec((None, None, n_cmp, HEAD_DIM), lambda i, g, q: (i, g, 0, 0)),
            pl.BlockSpec((None, None, HEAD_DIM, n_cmp), lambda i, g, q: (i, g, 0, 0)),
            pl.BlockSpec((None, None, BAND_ROWS, ROWS), lambda i, g, q: (jnp.minimum(q, n_band), g, 0, 0)),
            pl.BlockSpec((n_slc, n_cmp), lambda i, g, q: (0, 0)),
            pl.BlockSpec((None, s, HEAD_DIM), lambda i, g, q: (i, 0, CB_KW + g)),
            pl.BlockSpec((None, s, HEAD_DIM), lambda i, g, q: (i, 0, CB_VW + g)),
            pl.BlockSpec((None, None, ROWS, WIN_KEYS), lambda i, g, q: (jnp.minimum(q, n_wb), g, 0, 0)),
            pl.BlockSpec((None, Q_TILE, LANES), lambda i, g, q: (i, q, g)),
        ],
        out_specs=[
            pl.BlockSpec((None, Q_TILE, NSA_HPG * HEAD_DIM), lambda i, g, q: (i, q, g)),
            pl.BlockSpec((None, None, pen_w, Q_TILE), lambda i, g, q: (i, g, 0, q)),
        ],
        out_shape=[
            jax.ShapeDtypeStruct((b, s, NSA_Q_DIM), jnp.float32),
            jax.ShapeDtypeStruct((b, NSA_GROUPS, pen_w, s), jnp.bfloat16),
        ],
        scratch_shapes=[pltpu.VMEM((n_cmp, ROWS), jnp.float32)],
        compiler_params=_params(("parallel", "parallel", "arbitrary")),
        name="nsa_cmp_topk_win",
    )(qkv, kc, vct, band, ovt, qkv, qkv, wbias, gates)


def _sel_kernel(qt_ref, pen_ref, ks_ref, oh_ref, vt_ref, tb_ref, ycw_ref, gn_ref, y_ref,
                qaug_ref, m_ref, l_ref, acc_ref):
    qb = pl.program_id(2)
    n_half = pen_ref.shape[0] // LANES
    tail_w = tb_ref.shape[0]
    chunks_per_half = LANES * SLC_BLOCK // KEY_CHUNK
    blocks_per_chunk = KEY_CHUNK // SLC_BLOCK

    te = (qb + 1) * Q_TILE
    nf = jnp.maximum((te - tail_w + KEY_CHUNK - 1) // KEY_CHUNK, 0)
    ts = pl.multiple_of(jnp.maximum(te - tail_w, 0), Q_TILE)
    jb0 = ts // SLC_BLOCK

    q4t = jnp.concatenate([qt_ref[h] for h in range(NSA_HPG)], axis=1)
    pen = pen_ref[...]
    for hf in range(n_half):
        qaug_ref[hf, 0:HEAD_DIM, :] = q4t
        qaug_ref[hf, HEAD_DIM:, :] = jnp.concatenate([pen[hf * LANES:(hf + 1) * LANES, :]] * NSA_HPG, axis=1)
    c = lax.broadcasted_iota(jnp.int32, (LANES, Q_TILE), 0)
    pen_tail = pen[0:LANES, :]
    blk = c
    for hf in range(1, n_half):
        in_lower = (c + (hf - 1) * LANES >= jb0) & (jb0 < hf * LANES)
        pen_tail = jnp.where(in_lower, pen_tail, pen[hf * LANES:(hf + 1) * LANES, :])
        blk = jnp.where(in_lower, blk, c + hf * LANES)
    pen_tail = jnp.where(blk < nf * blocks_per_chunk, jnp.asarray(NEG, pen_tail.dtype), pen_tail)
    qaug_ref[n_half, 0:HEAD_DIM, :] = q4t
    qaug_ref[n_half, HEAD_DIM:, :] = jnp.concatenate([pen_tail] * NSA_HPG, axis=1)

    _flash_init(m_ref, l_ref, acc_ref)

    def far(cidx, carry):
        k0 = pl.multiple_of(cidx * KEY_CHUNK, KEY_CHUNK)
        kaug, vt = _key_chunk(ks_ref, oh_ref, vt_ref, k0, KEY_CHUNK)
        half = cidx // chunks_per_half
        _flash_step(kaug, vt, lambda lo, hi: qaug_ref[half, :, lo:hi], ROWS, m_ref, l_ref, acc_ref)
        return carry

    lax.fori_loop(0, nf, far, 0)

    for j in range(tail_w // KEY_CHUNK):
        k0 = pl.multiple_of(ts + j * KEY_CHUNK, Q_TILE)
        kaug, vt = _key_chunk(ks_ref, oh_ref, vt_ref, k0, KEY_CHUNK)
        _flash_step(kaug, vt, lambda lo, hi: qaug_ref[n_half, :, lo:hi], ROWS, m_ref, l_ref, acc_ref,
                    extra=lambda s, lo, hi, j=j: s + tb_ref[j * KEY_CHUNK:(j + 1) * KEY_CHUNK, lo:hi])

    o = acc_ref[...] / l_ref[...]
    gn = gn_ref[...]
    for h in range(NSA_HPG):
        y_ref[:, h * HEAD_DIM:(h + 1) * HEAD_DIM] = (
            ycw_ref[:, h * HEAD_DIM:(h + 1) * HEAD_DIM]
            + _branch_gate(gn, h, 1) * o[:, h * Q_TILE:(h + 1) * Q_TILE].T).astype(y_ref.dtype)


def _sel(qkv, qt, pen, onehot, vt, tbias, ycw, gates):
    b, s, _ = qkv.shape
    nq = s // Q_TILE
    pen_w = pen.shape[2]
    tail_w = tbias.shape[2]
    n_tb = tbias.shape[0] - 1
    return pl.pallas_call(
        _sel_kernel,
        grid=(b, NSA_GROUPS, nq),
        in_specs=[
            pl.BlockSpec((None, NSA_HPG, HEAD_DIM, Q_TILE), lambda i, g, q: (i, g, 0, q)),
            pl.BlockSpec((None, None, pen_w, Q_TILE), lambda i, g, q: (i, g, 0, q)),
            pl.BlockSpec((None, s, HEAD_DIM), lambda i, g, q: (i, 0, CB_KS + g)),
            pl.BlockSpec((s, LANES), lambda i, g, q: (0, 0)),
            pl.BlockSpec((None, None, s // LANES, HEAD_DIM, LANES), lambda i, g, q: (i, g, 0, 0, 0)),
            pl.BlockSpec((None, None, tail_w, ROWS), lambda i, g, q: (jnp.minimum(q, n_tb), g, 0, 0)),
            pl.BlockSpec((None, Q_TILE, NSA_HPG * HEAD_DIM), lambda i, g, q: (i, q, g)),
            pl.BlockSpec((None, Q_TILE, LANES), lambda i, g, q: (i, q, g)),
        ],
        out_specs=pl.BlockSpec((None, Q_TILE, NSA_HPG * HEAD_DIM), lambda i, g, q: (i, q, g)),
        out_shape=jax.ShapeDtypeStruct((b, s, NSA_Q_DIM), jnp.bfloat16),
        scratch_shapes=[
            pltpu.VMEM((pen_w // LANES + 1, 2 * HEAD_DIM, ROWS), jnp.bfloat16),
            pltpu.VMEM((1, ROWS), jnp.float32),
            pltpu.VMEM((1, ROWS), jnp.float32),
            pltpu.VMEM((HEAD_DIM, ROWS), jnp.float32),
        ],
        compiler_params=_params(("parallel", "parallel", "arbitrary")),
        name="nsa_selected",
    )(qt, pen, qkv, onehot, vt, tbias, ycw, gates)


def _fox_kernel(qt_ref, k_ref, dec_ref, vt_ref, o_ref, qaug_ref, m_ref, l_ref, acc_ref):
    qi = pl.program_id(2)
    tq = qt_ref.shape[1]
    row = lax.broadcasted_iota(jnp.int32, (HEAD_DIM, tq), 0)
    qaug_ref[0:HEAD_DIM, :] = qt_ref[...]
    qaug_ref[HEAD_DIM:, :] = jnp.where(row < DECAY_TERMS, -1.0, 0.0).astype(qaug_ref.dtype)
    _flash_init(m_ref, l_ref, acc_ref)

    def get_q(lo, hi):
        return qaug_ref[:, lo:hi]

    def far(cidx, carry):
        kaug, vt = _key_chunk(k_ref, dec_ref, vt_ref, pl.multiple_of(cidx * tq, tq), tq)
        _flash_step(kaug, vt, get_q, tq, m_ref, l_ref, acc_ref)
        return carry

    lax.fori_loop(0, qi, far, 0)

    def causal(s, lo, hi):
        key = lax.broadcasted_iota(jnp.int32, s.shape, 0)
        qry = lo + lax.broadcasted_iota(jnp.int32, s.shape, 1)
        return jnp.where(key <= qry, s, NEG)

    kaug, vt = _key_chunk(k_ref, dec_ref, vt_ref, pl.multiple_of(qi * tq, tq), tq)
    _flash_step(kaug, vt, get_q, tq, m_ref, l_ref, acc_ref, extra=causal)
    o_ref[...] = (acc_ref[...] / l_ref[...]).T.astype(o_ref.dtype)


def _fox(qkv, qt, dec, vt, tq):
    b, s, _ = qkv.shape
    return pl.pallas_call(
        _fox_kernel,
        grid=(b, FOX_HEADS, s // tq),
        in_specs=[
            pl.BlockSpec((None, None, HEAD_DIM, tq), lambda i, h, q: (i, NSA_HEADS + h, 0, q)),
            pl.BlockSpec((None, s, HEAD_DIM), lambda i, h, q: (i, 0, CB_KF + h)),
            pl.BlockSpec((None, None, s, LANES), lambda i, h, q: (i, h, 0, 0)),
            pl.BlockSpec((None, None, s // LANES, HEAD_DIM, LANES), lambda i, h, q: (i, h, 0, 0, 0)),
        ],
        out_specs=pl.BlockSpec((None, tq, HEAD_DIM), lambda i, h, q: (i, q, h)),
        out_shape=jax.ShapeDtypeStruct((b, s, FOX_DIM), jnp.bfloat16),
        scratch_shapes=[
            pltpu.VMEM((2 * HEAD_DIM, tq), jnp.bfloat16),
            pltpu.VMEM((1, tq), jnp.float32),
            pltpu.VMEM((1, tq), jnp.float32),
            pltpu.VMEM((HEAD_DIM, tq), jnp.float32),
        ],
        compiler_params=_params(("parallel", "parallel", "arbitrary")),
        name="fox_attention",
    )(qt, qkv, dec, vt)


def _merge_kernel(x_ref, g_ref, wa_ref, wb_ref, pa_ref, pb_ref, wo_ref, ya_ref, yb_ref, o_ref,
                  h_ref, acc_ref):
    j = pl.program_id(1)

    @pl.when(j == 0)
    def _():
        h_ref[...] = _norm_rows(x_ref[...], g_ref[...]).astype(h_ref.dtype)
        acc_ref[...] = jnp.zeros_like(acc_ref)

    h = h_ref[...]
    ga = jax.nn.sigmoid(jnp.dot(h, wa_ref[...], preferred_element_type=jnp.float32))
    gb = jax.nn.sigmoid(jnp.dot(h, wb_ref[...], preferred_element_type=jnp.float32))
    a = jnp.dot(ya_ref[...], pa_ref[...], preferred_element_type=jnp.float32)
    bb = jnp.dot(yb_ref[...], pb_ref[...], preferred_element_type=jnp.float32)
    merged = (ga * a + gb * bb).astype(jnp.bfloat16)
    acc_ref[...] += jnp.dot(merged, wo_ref[...], preferred_element_type=jnp.float32)

    @pl.when(j == pl.num_programs(1) - 1)
    def _():
        o_ref[...] = x_ref[...] + acc_ref[...]


def _merge(x2d, g, w_ma, w_mb, p_a, p_b, w_out, y_a, y_b, tm, tn):
    t, d = x2d.shape
    ka = p_a.shape[0]
    kb = p_b.shape[0]
    return pl.pallas_call(
        _merge_kernel,
        grid=(t // tm, d // tn),
        in_specs=[
            pl.BlockSpec((tm, d), lambda i, j: (i, 0)),
            pl.BlockSpec((1, d), lambda i, j: (0, 0)),
            pl.BlockSpec((d, tn), lambda i, j: (0, j)),
            pl.BlockSpec((d, tn), lambda i, j: (0, j)),
            pl.BlockSpec((ka, tn), lambda i, j: (0, j)),
            pl.BlockSpec((kb, tn), lambda i, j: (0, j)),
            pl.BlockSpec((tn, d), lambda i, j: (j, 0)),
            pl.BlockSpec((tm, ka), lambda i, j: (i, 0)),
            pl.BlockSpec((tm, kb), lambda i, j: (i, 0)),
        ],
        out_specs=pl.BlockSpec((tm, d), lambda i, j: (i, 0)),
        out_shape=jax.ShapeDtypeStruct((t, d), jnp.float32),
        scratch_shapes=[pltpu.VMEM((tm, d), jnp.bfloat16), pltpu.VMEM((tm, d), jnp.float32)],
        compiler_params=_params(("parallel", "arbitrary")),
        name="merge_out_proj",
    )(x2d, g.reshape(1, d), w_ma, w_mb, p_a, p_b, w_out, y_a, y_b)


def _ffn_kernel(x_ref, xh_ref, g_ref, wu_ref, wv_ref, cw_ref, cb_ref, wd_ref, gf_ref, o_ref,
                h_ref, u_ref, acc_ref, *, seq, tm):
    i = pl.program_id(0)
    j = pl.program_id(1)
    halo = BF16_SUBLANES

    @pl.when(j == 0)
    def _():
        g = g_ref[...]
        keep = jnp.where((i * tm) % seq == 0, 0.0, 1.0)
        h_ref[0:halo, :] = (_norm_rows(xh_ref[...], g) * keep).astype(h_ref.dtype)
        h_ref[halo:, :] = _norm_rows(x_ref[...], g).astype(h_ref.dtype)
        acc_ref[...] = jnp.zeros_like(acc_ref)

    u_ref[...] = jnp.dot(h_ref[...], wu_ref[...], preferred_element_type=jnp.float32)
    v = jnp.dot(h_ref[halo:, :], wv_ref[...], preferred_element_type=jnp.float32)
    conv = cb_ref[...]
    for k in range(CONV_WIDTH):
        conv = conv + cw_ref[k:k + 1, :] * u_ref[pl.ds(halo - (CONV_WIDTH - 1) + k, tm), :]
    act = (jax.nn.gelu(conv) * v).astype(jnp.bfloat16)
    acc_ref[...] += jnp.dot(act, wd_ref[...], preferred_element_type=jnp.float32)

    @pl.when(j == pl.num_programs(1) - 1)
    def _():
        o_ref[...] = _norm_rows(x_ref[...] + acc_ref[...], gf_ref[...])


def _ffn(x2d, g, w_up, conv_w, conv_b, w_down, g_final, seq, tm, tn):
    t, d = x2d.shape
    d_ff = w_down.shape[0]
    nt = d_ff // tn
    halo = BF16_SUBLANES
    per = tm // halo
    return pl.pallas_call(
        functools.partial(_ffn_kernel, seq=seq, tm=tm),
        grid=(t // tm, nt),
        in_specs=[
            pl.BlockSpec((tm, d), lambda i, j: (i, 0)),
            pl.BlockSpec((halo, d), lambda i, j: (jnp.maximum(i * per - 1, 0), 0)),
            pl.BlockSpec((1, d), lambda i, j: (0, 0)),
            pl.BlockSpec((d, tn), lambda i, j: (0, j)),
            pl.BlockSpec((d, tn), lambda i, j: (0, nt + j)),
            pl.BlockSpec((CONV_WIDTH, tn), lambda i, j: (0, j)),
            pl.BlockSpec((1, tn), lambda i, j: (0, j)),
            pl.BlockSpec((tn, d), lambda i, j: (j, 0)),
            pl.BlockSpec((1, d), lambda i, j: (0, 0)),
        ],
        out_specs=pl.BlockSpec((tm, d), lambda i, j: (i, 0)),
        out_shape=jax.ShapeDtypeStruct((t, d), jnp.float32),
        scratch_shapes=[
            pltpu.VMEM((tm + halo, d), jnp.bfloat16),
            pltpu.VMEM((tm + halo, tn), jnp.float32),
            pltpu.VMEM((tm, d), jnp.float32),
        ],
        compiler_params=_params(("parallel", "arbitrary")),
        name="ffn_final_norm",
    )(x2d, x2d, g.reshape(1, d), w_up, w_up, conv_w, conv_b.reshape(1, d_ff), w_down, g_final.reshape(1, d))


def _t5_bucket_np(dist):
    n = np.maximum(dist, 0)
    max_exact = REL_BUCKETS // 2
    nf = np.maximum(n, 1).astype(np.float32)
    large = max_exact + (np.log(nf / np.float32(max_exact)) / np.float32(math.log(REL_MAX_DIST / max_exact))
                         * np.float32(REL_BUCKETS - max_exact)).astype(np.int32)
    return np.where(n < max_exact, n, np.minimum(large, REL_BUCKETS - 1)).astype(np.int32)


def _bias_by_distance(rel_table, far_shift):
    vals = rel_table[jnp.asarray(_t5_bucket_np(np.arange(REL_MAX_DIST + 1)))]
    if far_shift:
        vals = vals - rel_table[REL_BUCKETS - 1]
    return vals * LOG2E


def _distance_vector(fd, n_neg, n_far, n_beyond):
    h = fd.shape[1]
    return jnp.concatenate([jnp.full((n_neg, h), NEG, fd.dtype), fd[:REL_MAX_DIST],
                            jnp.broadcast_to(fd[REL_MAX_DIST:], (n_far, h)),
                            jnp.full((n_beyond, h), NEG, fd.dtype)], axis=0)


def _toeplitz(vec, rows, cols, start):
    period = cols + rows - 1
    v = jnp.concatenate([vec[start:start + cols], vec[start - (rows - 1):start]], axis=0)
    flat = jnp.tile(v, (rows, 1))[:rows * (period - 1)]
    return flat.reshape(rows, period - 1, vec.shape[1])[:, :cols]


def _key_query_tiles(vec, n_keys, n_neg, offsets):
    return jnp.stack([_toeplitz(vec, n_keys, Q_TILE, a + n_neg) for a in offsets])


def _window_bias(rel_table):
    fd = _bias_by_distance(rel_table, False)
    vec = _distance_vector(fd, WIN_KEYS - 1, WINDOW - REL_MAX_DIST, Q_TILE)
    offsets = [v * Q_TILE for v in range(WINDOW // Q_TILE)] + [WINDOW]
    tiles = _key_query_tiles(vec, WIN_KEYS, WIN_KEYS - 1, offsets)
    v = tiles.shape[0]
    tiles = tiles.reshape(v, WIN_KEYS, Q_TILE, NSA_GROUPS, NSA_HPG).transpose(0, 3, 4, 2, 1)
    return tiles.reshape(v, NSA_GROUPS, ROWS, WIN_KEYS)


def _tail_bias(rel_table, tail_w):
    fd = _bias_by_distance(rel_table, True)
    vec = _distance_vector(fd, tail_w - 1, tail_w - REL_MAX_DIST, 0)
    offsets = [v * Q_TILE for v in range(tail_w // Q_TILE)]
    tiles = _key_query_tiles(vec, tail_w, tail_w - 1, offsets)
    v = tiles.shape[0]
    tiles = tiles.reshape(v, tail_w, Q_TILE, NSA_GROUPS, NSA_HPG).transpose(0, 3, 1, 4, 2)
    return tiles.reshape(v, NSA_GROUPS, tail_w, ROWS)


def _band_bias(rel_table):
    fd = _bias_by_distance(rel_table, True)
    tl = np.arange(Q_TILE)[None, :]
    r = np.arange(BAND_ROWS)[:, None]
    dist = np.stack([tl - CMP_STRIDE * (r - off) - (CMP_BLOCK - 1) for off in (0, 8, 16)])
    vals = fd[jnp.asarray(np.clip(dist, 0, REL_MAX_DIST))]
    vals = jnp.where(jnp.asarray(dist >= 0)[..., None], vals, NEG)
    v, rr, q, _ = vals.shape
    return vals.transpose(0, 1, 3, 2).reshape(v, rr, NSA_GROUPS, NSA_HPG * q).transpose(0, 2, 1, 3)


def _overlap_t(n_cmp_pad, n_slc):
    i = np.arange(n_cmp_pad)[None, :]
    jj = np.arange(n_slc)[:, None]
    c_start = i * CMP_STRIDE
    ov = (c_start < jj * SLC_BLOCK + SLC_BLOCK) & (c_start + CMP_BLOCK - 1 >= jj * SLC_BLOCK)
    ov = ov & (i < n_cmp_pad - 1)
    return jnp.asarray(ov.astype(np.float32), jnp.bfloat16)


def _block_onehot(seq):
    blk = (np.arange(seq) // SLC_BLOCK) % LANES
    return jnp.asarray((blk[:, None] == np.arange(LANES)[None, :]).astype(np.float32), jnp.bfloat16)


def _pick_tile(n, pref):
    return pref if n % pref == 0 else n


def kernel(x, attn_norm_g, w_in, cmp_pos_k, cmp_w1_k, cmp_w2_k, cmp_pos_v, cmp_w1_v, cmp_w2_v,
           rel_bias_table, fox_forget_bias, w_branch_nsa, w_branch_fox, w_out,
           ffn_norm_g, w_up, conv_w, conv_b, w_down, final_norm_g):
    assert w_in.shape[0] == 1, "the final norm is fused into the single layer's FFN kernel"
    bsz, seq, d = x.shape
    t = bsz * seq
    bf = jnp.bfloat16
    scale = HEAD_DIM ** -0.5 * LOG2E
    x2d = x.reshape(t, d)
    w_in = w_in[0]

    o = np.cumsum([0, NSA_Q_DIM] + [NSA_KV_DIM] * 6 + [3 * NSA_HEADS, FOX_DIM, FOX_DIM, FOX_DIM, FOX_HEADS, d, d])
    w_qkv = jnp.concatenate([w_in[:, o[0]:o[1]] * scale, w_in[:, o[1]:o[7]],
                             w_in[:, o[8]:o[9]] * scale, w_in[:, o[9]:o[11]]], axis=1).astype(bf)
    gate_cols = []
    for grp in range(NSA_GROUPS):
        gate_cols += [w_in[:, o[7] + grp * GATES_PER_GROUP:o[7] + (grp + 1) * GATES_PER_GROUP],
                      jnp.zeros((d, LANES - GATES_PER_GROUP), w_in.dtype)]
    gate_cols += [w_in[:, o[11]:o[12]], jnp.zeros((d, LANES - FOX_HEADS), w_in.dtype)]
    w_gate = jnp.concatenate(gate_cols, axis=1).astype(bf)
    w_ma = w_in[:, o[12]:o[13]].astype(bf)
    w_mb = w_in[:, o[13]:o[14]].astype(bf)

    tm = _pick_tile(t, 1024)
    qkv = _norm_matmul(x2d, attn_norm_g[0], w_qkv, bf, tm, 512).reshape(bsz, seq, N_QKV)
    gates = _norm_matmul(x2d, attn_norm_g[0], w_gate, jnp.float32, tm, LANES).reshape(bsz, seq, N_GATE)

    q_all = jnp.concatenate([qkv[:, :, :NSA_Q_DIM], qkv[:, :, CB_QF * HEAD_DIM:(CB_QF + FOX_HEADS) * HEAD_DIM]],
                            axis=-1)
    qt = q_all.reshape(bsz, seq, NSA_HEADS + FOX_HEADS, HEAD_DIM).transpose(0, 2, 3, 1)

    def values_t(cb, heads):
        v = qkv[:, :, cb * HEAD_DIM:(cb + heads) * HEAD_DIM]
        return v.reshape(bsz, seq // LANES, LANES, heads, HEAD_DIM).transpose(0, 3, 1, 4, 2)

    n_ch = seq // CMP_STRIDE

    def chunked(cb):
        sl = qkv[:, :, cb * HEAD_DIM:(cb + NSA_GROUPS) * HEAD_DIM]
        sl = sl.reshape(bsz, n_ch, CMP_STRIDE, NSA_GROUPS, HEAD_DIM).transpose(0, 3, 1, 2, 4)
        return sl.reshape(bsz, NSA_GROUPS, n_ch, CMP_STRIDE * HEAD_DIM)

    def posflat(pos):
        return jnp.broadcast_to(pos.reshape(1, CMP_BLOCK * HEAD_DIM), (8, CMP_BLOCK * HEAD_DIM)).astype(bf)

    kc = _compress(chunked(CB_KC), cmp_w1_k[0].astype(bf), posflat(cmp_pos_k[0]), cmp_w2_k[0].astype(bf))
    vc = _compress(chunked(CB_VC), cmp_w1_v[0].astype(bf), posflat(cmp_pos_v[0]), cmp_w2_v[0].astype(bf))
    vct = vc.transpose(0, 1, 3, 2)

    f_t = gates[:, :, NSA_GROUPS * LANES:NSA_GROUPS * LANES + FOX_HEADS]
    f_t = f_t.transpose(0, 2, 1).reshape(bsz * FOX_HEADS, seq)
    bias_col = jnp.tile(fox_forget_bias[0].astype(jnp.float32), bsz).reshape(bsz * FOX_HEADS, 1)
    terms = _decay_cumsum(f_t, bias_col, _pick_tile(seq, 2048))
    dec = jnp.pad(terms.transpose(1, 2, 0), ((0, 0), (0, 0), (0, LANES - DECAY_TERMS)))
    dec = dec.reshape(bsz, FOX_HEADS, seq, LANES)

    n_slc = seq // SLC_BLOCK
    tail_w = min(1024, seq)
    ycw, pen = _cmp_win(qkv, kc, vct, _band_bias(rel_bias_table), _overlap_t(n_ch, n_slc),
                        _window_bias(rel_bias_table), gates)
    y_nsa = _sel(qkv, qt, pen, _block_onehot(seq), values_t(CB_VS, NSA_GROUPS),
                 _tail_bias(rel_bias_table, tail_w), ycw, gates)

    y_fox = _fox(qkv, qt, dec, values_t(CB_VF, FOX_HEADS), _pick_tile(seq, FOX_Q_TILE))

    tm2 = _pick_tile(t, 512)
    x_mid = _merge(x2d, attn_norm_g[0], w_ma, w_mb, w_branch_nsa[0].astype(bf), w_branch_fox[0].astype(bf),
                   w_out[0].astype(bf), y_nsa.reshape(t, NSA_Q_DIM), y_fox.reshape(t, FOX_DIM),
                   tm2, _pick_tile(d, 512))
    d_ff = w_down.shape[1]
    out = _ffn(x_mid, ffn_norm_g[0], w_up[0].astype(bf), conv_w[0], conv_b[0], w_down[0].astype(bf),
               final_norm_g, seq, tm2, _pick_tile(d_ff, 512))
    return out.reshape(bsz, seq, d)
```

```python
import functools
import math

import jax
import jax.numpy as jnp
import numpy as np
from jax import lax
from jax.experimental import pallas as pl
from jax.experimental.pallas import tpu as pltpu

HEAD_DIM = 128
NSA_HEADS = 8
NSA_GROUPS = 2
NSA_HPG = NSA_HEADS // NSA_GROUPS
FOX_HEADS = 8
CMP_BLOCK = 32
CMP_STRIDE = 16
SLC_BLOCK = 64
SLC_TOPK = 16
WINDOW = 512
REL_BUCKETS = 32
REL_MAX_DIST = 128
CONV_WIDTH = 3
EPS = 1e-6
NEG = -1e30
FORCED_SCORE = 1e4
LOG2E = math.log2(math.e)

LANES = 128
BF16_SUBLANES = 16
VMEM_LIMIT = 56 * 1024 * 1024

NSA_Q_DIM = NSA_HEADS * HEAD_DIM
NSA_KV_DIM = NSA_GROUPS * HEAD_DIM
FOX_DIM = FOX_HEADS * HEAD_DIM
N_QKV = NSA_Q_DIM + 6 * NSA_KV_DIM + 3 * FOX_DIM
GATES_PER_GROUP = 3 * NSA_HPG
N_GATE = (NSA_GROUPS + 1) * LANES

CB_KC = 8
CB_VC = 10
CB_KS = 12
CB_VS = 14
CB_KW = 16
CB_VW = 18
CB_QF = 20
CB_KF = 28
CB_VF = 36

Q_TILE = 128
ROWS = NSA_HPG * Q_TILE
KEY_CHUNK = 512
KEY_PIECES = KEY_CHUNK // LANES
FOX_Q_TILE = 512
N_SPLIT = 2
DECAY_TERMS = 3
BAND_ROWS = 24
WIN_KEYS = WINDOW + Q_TILE
NEAR_KEYS = 2 * Q_TILE
GATE_ROWS = 16


def _params(sem):
    return pltpu.CompilerParams(dimension_semantics=sem, vmem_limit_bytes=VMEM_LIMIT)


def _norm_rows(x, g):
    return (x * lax.rsqrt(jnp.mean(x * x, axis=-1, keepdims=True) + EPS)) * g


def _norm_matmul_kernel(x_ref, g_ref, w_ref, o_ref, h_ref):
    @pl.when(pl.program_id(1) == 0)
    def _():
        h_ref[...] = _norm_rows(x_ref[...], g_ref[...]).astype(h_ref.dtype)

    o_ref[...] = jnp.dot(h_ref[...], w_ref[...], preferred_element_type=jnp.float32).astype(o_ref.dtype)


def _norm_matmul(x2d, g, w, out_dtype, tm, tn):
    t, d = x2d.shape
    n = w.shape[1]
    return pl.pallas_call(
        _norm_matmul_kernel,
        grid=(t // tm, n // tn),
        in_specs=[
            pl.BlockSpec((tm, d), lambda i, j: (i, 0)),
            pl.BlockSpec((1, d), lambda i, j: (0, 0)),
            pl.BlockSpec((d, tn), lambda i, j: (0, j)),
        ],
        out_specs=pl.BlockSpec((tm, tn), lambda i, j: (i, j)),
        out_shape=jax.ShapeDtypeStruct((t, n), out_dtype),
        scratch_shapes=[pltpu.VMEM((tm, d), jnp.bfloat16)],
        compiler_params=_params(("parallel", "arbitrary")),
        name="norm_in_proj",
    )(x2d, g.reshape(1, d), w)


def _compress_kernel(ch_ref, w1_ref, posf_ref, w2_ref, o_ref):
    half = ch_ref.shape[1]
    ch = ch_ref[...]
    pa = jnp.dot(ch, w1_ref[:half, :], preferred_element_type=jnp.float32)
    pb = jnp.dot(ch, w1_ref[half:, :], preferred_element_type=jnp.float32)
    pos = jnp.dot(posf_ref[...], w1_ref[...], preferred_element_type=jnp.float32)[0:1, :]
    n = pa.shape[0]
    pre = pa + pltpu.roll(pb, n - 1, 0) + pos
    act = jax.nn.gelu(pre)
    o_ref[...] = jnp.dot(act.astype(jnp.bfloat16), w2_ref[...],
                         preferred_element_type=jnp.float32).astype(o_ref.dtype)


def _compress(chunks, w1, posf, w2):
    b, g, n, k = chunks.shape
    return pl.pallas_call(
        _compress_kernel,
        grid=(b, g),
        in_specs=[
            pl.BlockSpec((None, None, n, k), lambda i, j: (i, j, 0, 0)),
            pl.BlockSpec(w1.shape, lambda i, j: (0, 0)),
            pl.BlockSpec(posf.shape, lambda i, j: (0, 0)),
            pl.BlockSpec(w2.shape, lambda i, j: (0, 0)),
        ],
        out_specs=pl.BlockSpec((None, None, n, HEAD_DIM), lambda i, j: (i, j, 0, 0)),
        out_shape=jax.ShapeDtypeStruct((b, g, n, HEAD_DIM), jnp.bfloat16),
        compiler_params=_params(("parallel", "parallel")),
        name="compress_tokens",
    )(chunks, w1, posf, w2)


def _decay_kernel(f_ref, b_ref, tri_ref, o_ref, carry_ref):
    @pl.when(pl.program_id(0) == 0)
    def _():
        carry_ref[...] = jnp.zeros_like(carry_ref)

    x = f_ref[...] + b_ref[...]
    logf = (jnp.minimum(x, 0.0) - jnp.log1p(jnp.exp(-jnp.abs(x)))) * LOG2E
    carry = carry_ref[...]
    for seg in range(f_ref.shape[1] // LANES):
        part = jnp.dot(logf[:, seg * LANES:(seg + 1) * LANES], tri_ref[...],
                       preferred_element_type=jnp.float32, precision=lax.Precision.HIGHEST) + carry
        carry = part[:, LANES - 1:LANES]
        rest = part
        for term in range(DECAY_TERMS):
            piece = rest.astype(o_ref.dtype)
            o_ref[term, :, seg * LANES:(seg + 1) * LANES] = piece
            rest = rest - piece.astype(jnp.float32)
    carry_ref[...] = carry


def _decay_cumsum(f_t, bias_col, width):
    rows, s = f_t.shape
    tri = jnp.asarray(np.triu(np.ones((LANES, LANES), np.float32)))
    return pl.pallas_call(
        _decay_kernel,
        grid=(s // width,),
        in_specs=[
            pl.BlockSpec((rows, width), lambda i: (0, i)),
            pl.BlockSpec((rows, 1), lambda i: (0, 0)),
            pl.BlockSpec((LANES, LANES), lambda i: (0, 0)),
        ],
        out_specs=pl.BlockSpec((DECAY_TERMS, rows, width), lambda i: (0, 0, i)),
        out_shape=jax.ShapeDtypeStruct((DECAY_TERMS, rows, s), jnp.bfloat16),
        scratch_shapes=[pltpu.VMEM((rows, 1), jnp.float32)],
        compiler_params=_params(("arbitrary",)),
        name="decay_cumsum",
    )(f_t, bias_col, tri)


def _flash_init(m_ref, l_ref, acc_ref):
    m_ref[...] = jnp.full(m_ref.shape, NEG, jnp.float32)
    l_ref[...] = jnp.zeros(l_ref.shape, jnp.float32)
    acc_ref[...] = jnp.zeros(acc_ref.shape, jnp.float32)


def _flash_step(kaug, vt, get_q, width, m_ref, l_ref, acc_ref, extra=None):
    w = width // N_SPLIT
    strips = [(i * w, (i + 1) * w) for i in range(N_SPLIT)]
    scores = [jnp.dot(kaug, get_q(lo, hi), preferred_element_type=jnp.float32) for lo, hi in strips]
    for (lo, hi), s in zip(strips, scores):
        if extra is not None:
            s = extra(s, lo, hi)
        m_prev = m_ref[:, lo:hi]
        m_new = jnp.maximum(m_prev, jnp.max(s, axis=0, keepdims=True))
        alpha = jnp.exp2(m_prev - m_new)
        p = jnp.exp2(s - m_new)
        l_ref[:, lo:hi] = alpha * l_ref[:, lo:hi] + jnp.sum(p, axis=0, keepdims=True)
        acc_ref[:, lo:hi] = alpha * acc_ref[:, lo:hi] + jnp.dot(
            vt, p.astype(vt.dtype), preferred_element_type=jnp.float32)
        m_ref[:, lo:hi] = m_new


def _key_chunk(k_ref, aug_ref, vt_ref, k0, n_keys):
    kaug = jnp.concatenate([k_ref[pl.ds(k0, n_keys), :], aug_ref[pl.ds(k0, n_keys), :]], axis=1)
    return kaug, _values_t(vt_ref, k0, n_keys)


def _group_queries_t(qt_ref):
    return jnp.concatenate([qt_ref[h] for h in range(NSA_HPG)], axis=1)


def _branch_gate(gates_t, head, branch):
    row = head * 3 + branch
    return gates_t[row:row + 1, :]


def _values_t(vt_ref, k0, n_keys):
    p0 = k0 // LANES
    return jnp.concatenate([vt_ref[p0 + j] for j in range(n_keys // LANES)], axis=1)


def _cmp_win_kernel(qt_ref, kc_ref, vct_ref, band_ref, ovt_ref, kw_ref, vwt_ref, near_ref, gt_ref,
                    ycw_ref, pen_ref, sc_ref, sw_ref):
    qb = pl.program_id(2)
    n_cmp = kc_ref.shape[0]
    n_slc = ovt_ref.shape[0]
    q4t = _group_queries_t(qt_ref)

    sc_ref[...] = jnp.dot(kc_ref[...], q4t, preferred_element_type=jnp.float32)
    i0 = pl.multiple_of(jnp.maximum(8 * qb - 16, 0), 8)
    sc_ref[pl.ds(i0, BAND_ROWS), :] = sc_ref[pl.ds(i0, BAND_ROWS), :] + band_ref[...]
    row = lax.broadcasted_iota(jnp.int32, (n_cmp, ROWS), 0)
    sc = jnp.where(row < i0 + BAND_ROWS, sc_ref[...], NEG)
    m = jnp.max(sc, axis=0, keepdims=True)
    p = jnp.exp2(sc - m)
    l = jnp.sum(p, axis=0, keepdims=True)
    pn = p * jnp.where(m > 0.5 * NEG, 1.0 / l, 0.0)
    oct_ = jnp.dot(vct_ref[...], pn.astype(jnp.bfloat16), preferred_element_type=jnp.float32)
    psum = pn[:, 0:Q_TILE]
    for h in range(1, NSA_HPG):
        psum = psum + pn[:, h * Q_TILE:(h + 1) * Q_TILE]
    imp = jnp.dot(ovt_ref[...], psum.astype(jnp.bfloat16), preferred_element_type=jnp.float32)

    ji = lax.broadcasted_iota(jnp.int32, (n_slc, Q_TILE), 0)
    jf = ji.astype(jnp.float32)
    t = qb * Q_TILE + lax.broadcasted_iota(jnp.int32, (n_slc, Q_TILE), 1)
    cur = t // SLC_BLOCK
    forced = (ji == 0) | (ji == cur) | (ji == cur - 1)
    score = jnp.where(ji <= cur, jnp.where(forced, FORCED_SCORE, imp), -1.0)
    pen_t = jnp.full((n_slc, Q_TILE), NEG, jnp.float32)
    for _ in range(min(SLC_TOPK, n_slc)):
        mx = jnp.max(score, axis=0, keepdims=True)
        idx = jnp.min(jnp.where(score == mx, jf, float(n_slc)), axis=0, keepdims=True)
        pick = jf == idx
        pen_t = jnp.where(pick, 0.0, pen_t)
        score = jnp.where(pick, -2.0, score)
    pad = pen_ref.shape[0] - n_slc
    if pad:
        pen_t = jnp.concatenate([pen_t, jnp.full((pad, Q_TILE), NEG, jnp.float32)], axis=0)
    pen_ref[...] = pen_t.astype(pen_ref.dtype)

    t0 = qb * Q_TILE
    ws = pl.multiple_of(jnp.maximum(t0 - WINDOW, 0), Q_TILE)
    ns = pl.multiple_of(jnp.maximum(t0 - Q_TILE, 0), Q_TILE)
    sw_ref[...] = jnp.dot(kw_ref[pl.ds(ws, WIN_KEYS), :], q4t, preferred_element_type=jnp.float32)
    off = pl.multiple_of(ns - ws, Q_TILE)
    sw_ref[pl.ds(off, NEAR_KEYS), :] = sw_ref[pl.ds(off, NEAR_KEYS), :] + near_ref[...]
    key = ws + lax.broadcasted_iota(jnp.int32, (WIN_KEYS, ROWS), 0)
    qry = t0 + (lax.broadcasted_iota(jnp.int32, (WIN_KEYS, ROWS), 1) & (Q_TILE - 1))
    sw = jnp.where((key <= qry) & (key > qry - WINDOW), sw_ref[...], NEG)
    mw = jnp.max(sw, axis=0, keepdims=True)
    pw = jnp.exp2(sw - mw)
    lw = jnp.sum(pw, axis=0, keepdims=True)
    owt = jnp.dot(_values_t(vwt_ref, ws, WIN_KEYS), pw.astype(jnp.bfloat16),
                  preferred_element_type=jnp.float32) / lw

    gates_t = jax.nn.sigmoid(gt_ref[...])
    for h in range(NSA_HPG):
        cols = slice(h * Q_TILE, (h + 1) * Q_TILE)
        mixed = _branch_gate(gates_t, h, 0) * oct_[:, cols] + _branch_gate(gates_t, h, 2) * owt[:, cols]
        ycw_ref[:, h * HEAD_DIM:(h + 1) * HEAD_DIM] = mixed.T


def _cmp_win(qt, kc, vct, band, ovt, qkv, vwt, near, gates_t):
    b, s, _ = qkv.shape
    n_cmp = kc.shape[2]
    n_slc = ovt.shape[0]
    nq = s // Q_TILE
    pen_w = -(-n_slc // LANES) * LANES
    n_band = band.shape[0] - 1
    return pl.pallas_call(
        _cmp_win_kernel,
        grid=(b, NSA_GROUPS, nq),
        in_specs=[
            pl.BlockSpec((None, NSA_HPG, HEAD_DIM, Q_TILE), lambda i, g, q: (i, g, 0, q)),
            pl.BlockSpec((None, None, n_cmp, HEAD_DIM), lambda i, g, q: (i, g, 0, 0)),
            pl.BlockSpec((None, None, HEAD_DIM, n_cmp), lambda i, g, q: (i, g, 0, 0)),
            pl.BlockSpec((None, None, BAND_ROWS, ROWS), lambda i, g, q: (jnp.minimum(q, n_band), g, 0, 0)),
            pl.BlockSpec((n_slc, n_cmp), lambda i, g, q: (0, 0)),
            pl.BlockSpec((None, s, HEAD_DIM), lambda i, g, q: (i, 0, CB_KW + g)),
            pl.BlockSpec((None, None, s // LANES, HEAD_DIM, LANES), lambda i, g, q: (i, g, 0, 0, 0)),
            pl.BlockSpec((None, None, NEAR_KEYS, ROWS), lambda i, g, q: (jnp.minimum(q, 1), g, 0, 0)),
            pl.BlockSpec((None, None, GATE_ROWS, Q_TILE), lambda i, g, q: (i, g, 0, q)),
        ],
        out_specs=[
            pl.BlockSpec((None, Q_TILE, NSA_HPG * HEAD_DIM), lambda i, g, q: (i, q, g)),
            pl.BlockSpec((None, None, pen_w, Q_TILE), lambda i, g, q: (i, g, 0, q)),
        ],
        out_shape=[
            jax.ShapeDtypeStruct((b, s, NSA_Q_DIM), jnp.float32),
            jax.ShapeDtypeStruct((b, NSA_GROUPS, pen_w, s), jnp.bfloat16),
        ],
        scratch_shapes=[pltpu.VMEM((n_cmp, ROWS), jnp.float32), pltpu.VMEM((WIN_KEYS, ROWS), jnp.float32)],
        compiler_params=_params(("parallel", "parallel", "arbitrary")),
        name="nsa_cmp_topk_win",
    )(qt, kc, vct, band, ovt, qkv, vwt, near, gates_t)


def _range_penalty(pen, first_blk, lo_ok, hi_ok):
    n_half = pen.shape[0] // LANES
    c = lax.broadcasted_iota(jnp.int32, (LANES, pen.shape[1]), 0)
    out = pen[0:LANES, :]
    blk = c
    for hf in range(1, n_half):
        in_lower = (c + (hf - 1) * LANES >= first_blk) & (first_blk < hf * LANES)
        out = jnp.where(in_lower, out, pen[hf * LANES:(hf + 1) * LANES, :])
        blk = jnp.where(in_lower, blk, c + hf * LANES)
    return jnp.where((blk >= lo_ok) & (blk < hi_ok), out, jnp.asarray(NEG, out.dtype))


def _sel_kernel(qt_ref, pen_ref, ks_ref, oh_ref, vt_ref, near_ref, ycw_ref, gt_ref, y_ref,
                qaug_ref, m_ref, l_ref, acc_ref):
    qb = pl.program_id(2)
    n_half = pen_ref.shape[0] // LANES
    n_blocks = pen_ref.shape[0]
    chunks_per_half = LANES * SLC_BLOCK // KEY_CHUNK
    blocks_per_chunk = KEY_CHUNK // SLC_BLOCK

    t0 = qb * Q_TILE
    near_end = t0 - Q_TILE
    nf = jnp.maximum(near_end // KEY_CHUNK, 0)
    ms = pl.multiple_of(jnp.maximum(near_end - KEY_CHUNK, 0), Q_TILE)
    ns = pl.multiple_of(jnp.maximum(near_end, 0), Q_TILE)

    q4t = _group_queries_t(qt_ref)
    pen = pen_ref[...]
    operands = [pen[hf * LANES:(hf + 1) * LANES, :] for hf in range(n_half)]
    operands.append(_range_penalty(pen, ms // SLC_BLOCK, nf * blocks_per_chunk, near_end // SLC_BLOCK))
    operands.append(_range_penalty(pen, ns // SLC_BLOCK, 0, n_blocks))
    for idx, channels in enumerate(operands):
        qaug_ref[idx, 0:HEAD_DIM, :] = q4t
        qaug_ref[idx, HEAD_DIM:, :] = jnp.concatenate([channels] * NSA_HPG, axis=1)

    _flash_init(m_ref, l_ref, acc_ref)

    def far(cidx, carry):
        k0 = pl.multiple_of(cidx * KEY_CHUNK, KEY_CHUNK)
        kaug, vt = _key_chunk(ks_ref, oh_ref, vt_ref, k0, KEY_CHUNK)
        half = cidx // chunks_per_half
        _flash_step(kaug, vt, lambda lo, hi: qaug_ref[half, :, lo:hi], ROWS, m_ref, l_ref, acc_ref)
        return carry

    lax.fori_loop(0, nf, far, 0)

    kaug, vt = _key_chunk(ks_ref, oh_ref, vt_ref, ms, KEY_CHUNK)
    _flash_step(kaug, vt, lambda lo, hi: qaug_ref[n_half, :, lo:hi], ROWS, m_ref, l_ref, acc_ref)
    kaug, vt = _key_chunk(ks_ref, oh_ref, vt_ref, ns, NEAR_KEYS)
    _flash_step(kaug, vt, lambda lo, hi: qaug_ref[n_half + 1, :, lo:hi], ROWS, m_ref, l_ref, acc_ref,
                extra=lambda s, lo, hi: s + near_ref[:, lo:hi])

    o = acc_ref[...] / l_ref[...]
    gates_t = jax.nn.sigmoid(gt_ref[...])
    for h in range(NSA_HPG):
        sel_h = (_branch_gate(gates_t, h, 1) * o[:, h * Q_TILE:(h + 1) * Q_TILE]).T
        y_ref[:, h * HEAD_DIM:(h + 1) * HEAD_DIM] = (
            ycw_ref[:, h * HEAD_DIM:(h + 1) * HEAD_DIM] + sel_h).astype(y_ref.dtype)


def _sel(qkv, qt, pen, onehot, vt, near, ycw, gates_t):
    b, s, _ = qkv.shape
    nq = s // Q_TILE
    pen_w = pen.shape[2]
    return pl.pallas_call(
        _sel_kernel,
        grid=(b, NSA_GROUPS, nq),
        in_specs=[
            pl.BlockSpec((None, NSA_HPG, HEAD_DIM, Q_TILE), lambda i, g, q: (i, g, 0, q)),
            pl.BlockSpec((None, None, pen_w, Q_TILE), lambda i, g, q: (i, g, 0, q)),
            pl.BlockSpec((None, s, HEAD_DIM), lambda i, g, q: (i, 0, CB_KS + g)),
            pl.BlockSpec((s, LANES), lambda i, g, q: (0, 0)),
            pl.BlockSpec((None, None, s // LANES, HEAD_DIM, LANES), lambda i, g, q: (i, g, 0, 0, 0)),
            pl.BlockSpec((None, None, NEAR_KEYS, ROWS), lambda i, g, q: (jnp.minimum(q, 1), g, 0, 0)),
            pl.BlockSpec((None, Q_TILE, NSA_HPG * HEAD_DIM), lambda i, g, q: (i, q, g)),
            pl.BlockSpec((None, None, GATE_ROWS, Q_TILE), lambda i, g, q: (i, g, 0, q)),
        ],
        out_specs=pl.BlockSpec((None, Q_TILE, NSA_HPG * HEAD_DIM), lambda i, g, q: (i, q, g)),
        out_shape=jax.ShapeDtypeStruct((b, s, NSA_Q_DIM), jnp.bfloat16),
        scratch_shapes=[
            pltpu.VMEM((pen_w // LANES + 2, 2 * HEAD_DIM, ROWS), jnp.bfloat16),
            pltpu.VMEM((1, ROWS), jnp.float32),
            pltpu.VMEM((1, ROWS), jnp.float32),
            pltpu.VMEM((HEAD_DIM, ROWS), jnp.float32),
        ],
        compiler_params=_params(("parallel", "parallel", "arbitrary")),
        name="nsa_selected",
    )(qt, pen, qkv, onehot, vt, near, ycw, gates_t)


def _fox_kernel(qt_ref, k_ref, dec_ref, vt_ref, o_ref, qaug_ref, m_ref, l_ref, acc_ref):
    qi = pl.program_id(2)
    tq = qt_ref.shape[1]
    row = lax.broadcasted_iota(jnp.int32, (HEAD_DIM, tq), 0)
    qaug_ref[0:HEAD_DIM, :] = qt_ref[...]
    qaug_ref[HEAD_DIM:, :] = jnp.where(row < DECAY_TERMS, -1.0, 0.0).astype(qaug_ref.dtype)
    _flash_init(m_ref, l_ref, acc_ref)

    def get_q(lo, hi):
        return qaug_ref[:, lo:hi]

    def far(cidx, carry):
        kaug, vt = _key_chunk(k_ref, dec_ref, vt_ref, pl.multiple_of(cidx * tq, tq), tq)
        _flash_step(kaug, vt, get_q, tq, m_ref, l_ref, acc_ref)
        return carry

    lax.fori_loop(0, qi, far, 0)

    def causal(s, lo, hi):
        key = lax.broadcasted_iota(jnp.int32, s.shape, 0)
        qry = lo + lax.broadcasted_iota(jnp.int32, s.shape, 1)
        return jnp.where(key <= qry, s, NEG)

    kaug, vt = _key_chunk(k_ref, dec_ref, vt_ref, pl.multiple_of(qi * tq, tq), tq)
    _flash_step(kaug, vt, get_q, tq, m_ref, l_ref, acc_ref, extra=causal)
    o_ref[...] = (acc_ref[...] / l_ref[...]).T.astype(o_ref.dtype)


def _fox(qkv, qt, dec, vt, tq):
    b, s, _ = qkv.shape
    return pl.pallas_call(
        _fox_kernel,
        grid=(b, FOX_HEADS, s // tq),
        in_specs=[
            pl.BlockSpec((None, None, HEAD_DIM, tq), lambda i, h, q: (i, NSA_HEADS + h, 0, q)),
            pl.BlockSpec((None, s, HEAD_DIM), lambda i, h, q: (i, 0, CB_KF + h)),
            pl.BlockSpec((None, None, s, LANES), lambda i, h, q: (i, h, 0, 0)),
            pl.BlockSpec((None, None, s // LANES, HEAD_DIM, LANES), lambda i, h, q: (i, h, 0, 0, 0)),
        ],
        out_specs=pl.BlockSpec((None, tq, HEAD_DIM), lambda i, h, q: (i, q, h)),
        out_shape=jax.ShapeDtypeStruct((b, s, FOX_DIM), jnp.bfloat16),
        scratch_shapes=[
            pltpu.VMEM((2 * HEAD_DIM, tq), jnp.bfloat16),
            pltpu.VMEM((1, tq), jnp.float32),
            pltpu.VMEM((1, tq), jnp.float32),
            pltpu.VMEM((HEAD_DIM, tq), jnp.float32),
        ],
        compiler_params=_params(("parallel", "parallel", "arbitrary")),
        name="fox_attention",
    )(qt, qkv, dec, vt)


def _merge_kernel(x_ref, g_ref, wa_ref, wb_ref, pa_ref, pb_ref, wo_ref, ya_ref, yb_ref, o_ref,
                  h_ref, acc_ref):
    j = pl.program_id(1)

    @pl.when(j == 0)
    def _():
        h_ref[...] = _norm_rows(x_ref[...], g_ref[...]).astype(h_ref.dtype)
        acc_ref[...] = jnp.zeros_like(acc_ref)

    h = h_ref[...]
    ga = jax.nn.sigmoid(jnp.dot(h, wa_ref[...], preferred_element_type=jnp.float32))
    gb = jax.nn.sigmoid(jnp.dot(h, wb_ref[...], preferred_element_type=jnp.float32))
    a = jnp.dot(ya_ref[...], pa_ref[...], preferred_element_type=jnp.float32)
    bb = jnp.dot(yb_ref[...], pb_ref[...], preferred_element_type=jnp.float32)
    merged = (ga * a + gb * bb).astype(jnp.bfloat16)
    acc_ref[...] += jnp.dot(merged, wo_ref[...], preferred_element_type=jnp.float32)

    @pl.when(j == pl.num_programs(1) - 1)
    def _():
        o_ref[...] = x_ref[...] + acc_ref[...]


def _merge(x2d, g, w_ma, w_mb, p_a, p_b, w_out, y_a, y_b, tm, tn):
    t, d = x2d.shape
    ka = p_a.shape[0]
    kb = p_b.shape[0]
    return pl.pallas_call(
        _merge_kernel,
        grid=(t // tm, d // tn),
        in_specs=[
            pl.BlockSpec((tm, d), lambda i, j: (i, 0)),
            pl.BlockSpec((1, d), lambda i, j: (0, 0)),
            pl.BlockSpec((d, tn), lambda i, j: (0, j)),
            pl.BlockSpec((d, tn), lambda i, j: (0, j)),
            pl.BlockSpec((ka, tn), lambda i, j: (0, j)),
            pl.BlockSpec((kb, tn), lambda i, j: (0, j)),
            pl.BlockSpec((tn, d), lambda i, j: (j, 0)),
            pl.BlockSpec((tm, ka), lambda i, j: (i, 0)),
            pl.BlockSpec((tm, kb), lambda i, j: (i, 0)),
        ],
        out_specs=pl.BlockSpec((tm, d), lambda i, j: (i, 0)),
        out_shape=jax.ShapeDtypeStruct((t, d), jnp.float32),
        scratch_shapes=[pltpu.VMEM((tm, d), jnp.bfloat16), pltpu.VMEM((tm, d), jnp.float32)],
        compiler_params=_params(("parallel", "arbitrary")),
        name="merge_out_proj",
    )(x2d, g.reshape(1, d), w_ma, w_mb, p_a, p_b, w_out, y_a, y_b)


def _ffn_kernel(x_ref, xh_ref, g_ref, wu_ref, wv_ref, cw_ref, cb_ref, wd_ref, gf_ref, o_ref,
                h_ref, u_ref, acc_ref, *, seq, tm):
    i = pl.program_id(0)
    j = pl.program_id(1)
    halo = BF16_SUBLANES

    @pl.when(j == 0)
    def _():
        g = g_ref[...]
        keep = jnp.where((i * tm) % seq == 0, 0.0, 1.0)
        h_ref[0:halo, :] = (_norm_rows(xh_ref[...], g) * keep).astype(h_ref.dtype)
        h_ref[halo:, :] = _norm_rows(x_ref[...], g).astype(h_ref.dtype)
        acc_ref[...] = jnp.zeros_like(acc_ref)

    u_ref[...] = jnp.dot(h_ref[...], wu_ref[...], preferred_element_type=jnp.float32)
    v = jnp.dot(h_ref[halo:, :], wv_ref[...], preferred_element_type=jnp.float32)
    conv = cb_ref[...]
    for k in range(CONV_WIDTH):
        conv = conv + cw_ref[k:k + 1, :] * u_ref[pl.ds(halo - (CONV_WIDTH - 1) + k, tm), :]
    act = (jax.nn.gelu(conv) * v).astype(jnp.bfloat16)
    acc_ref[...] += jnp.dot(act, wd_ref[...], preferred_element_type=jnp.float32)

    @pl.when(j == pl.num_programs(1) - 1)
    def _():
        o_ref[...] = _norm_rows(x_ref[...] + acc_ref[...], gf_ref[...])


def _ffn(x2d, g, w_up, conv_w, conv_b, w_down, g_final, seq, tm, tn):
    t, d = x2d.shape
    d_ff = w_down.shape[0]
    nt = d_ff // tn
    halo = BF16_SUBLANES
    per = tm // halo
    return pl.pallas_call(
        functools.partial(_ffn_kernel, seq=seq, tm=tm),
        grid=(t // tm, nt),
        in_specs=[
            pl.BlockSpec((tm, d), lambda i, j: (i, 0)),
            pl.BlockSpec((halo, d), lambda i, j: (jnp.maximum(i * per - 1, 0), 0)),
            pl.BlockSpec((1, d), lambda i, j: (0, 0)),
            pl.BlockSpec((d, tn), lambda i, j: (0, j)),
            pl.BlockSpec((d, tn), lambda i, j: (0, nt + j)),
            pl.BlockSpec((CONV_WIDTH, tn), lambda i, j: (0, j)),
            pl.BlockSpec((1, tn), lambda i, j: (0, j)),
            pl.BlockSpec((tn, d), lambda i, j: (j, 0)),
            pl.BlockSpec((1, d), lambda i, j: (0, 0)),
        ],
        out_specs=pl.BlockSpec((tm, d), lambda i, j: (i, 0)),
        out_shape=jax.ShapeDtypeStruct((t, d), jnp.float32),
        scratch_shapes=[
            pltpu.VMEM((tm + halo, d), jnp.bfloat16),
            pltpu.VMEM((tm + halo, tn), jnp.float32),
            pltpu.VMEM((tm, d), jnp.float32),
        ],
        compiler_params=_params(("parallel", "arbitrary")),
        name="ffn_final_norm",
    )(x2d, x2d, g.reshape(1, d), w_up, w_up, conv_w, conv_b.reshape(1, d_ff), w_down, g_final.reshape(1, d))


def _t5_bucket_np(dist):
    n = np.maximum(dist, 0)
    max_exact = REL_BUCKETS // 2
    nf = np.maximum(n, 1).astype(np.float32)
    large = max_exact + (np.log(nf / np.float32(max_exact)) / np.float32(math.log(REL_MAX_DIST / max_exact))
                         * np.float32(REL_BUCKETS - max_exact)).astype(np.int32)
    return np.where(n < max_exact, n, np.minimum(large, REL_BUCKETS - 1)).astype(np.int32)


def _bias_by_distance(rel_table, far_shift):
    vals = rel_table[jnp.asarray(_t5_bucket_np(np.arange(REL_MAX_DIST + 1)))]
    if far_shift:
        vals = vals - rel_table[REL_BUCKETS - 1]
    return vals * LOG2E


def _near_bias(rel_table):
    fd = _bias_by_distance(rel_table, True)
    h = fd.shape[1]
    lo = NEAR_KEYS - 1
    vec = jnp.concatenate([jnp.full((lo, h), NEG, fd.dtype), fd[:REL_MAX_DIST],
                           jnp.zeros((NEAR_KEYS - REL_MAX_DIST, h), fd.dtype)], axis=0)
    tiles = jnp.stack([jnp.stack([vec[lo + a - r:lo + a - r + Q_TILE] for r in range(NEAR_KEYS)])
                       for a in (0, Q_TILE)])
    tiles = tiles.reshape(2, NEAR_KEYS, Q_TILE, NSA_GROUPS, NSA_HPG).transpose(0, 3, 1, 4, 2)
    return tiles.reshape(2, NSA_GROUPS, NEAR_KEYS, ROWS)


def _band_bias(rel_table):
    fd = _bias_by_distance(rel_table, True)
    tl = np.arange(Q_TILE)[None, :]
    r = np.arange(BAND_ROWS)[:, None]
    dist = np.stack([tl - CMP_STRIDE * (r - off) - (CMP_BLOCK - 1) for off in (0, 8, 16)])
    vals = fd[jnp.asarray(np.clip(dist, 0, REL_MAX_DIST))]
    vals = jnp.where(jnp.asarray(dist >= 0)[..., None], vals, NEG)
    v, rr, q, _ = vals.shape
    return vals.transpose(0, 1, 3, 2).reshape(v, rr, NSA_GROUPS, NSA_HPG * q).transpose(0, 2, 1, 3)


def _overlap_t(n_cmp_pad, n_slc):
    i = np.arange(n_cmp_pad)[None, :]
    jj = np.arange(n_slc)[:, None]
    c_start = i * CMP_STRIDE
    ov = (c_start < jj * SLC_BLOCK + SLC_BLOCK) & (c_start + CMP_BLOCK - 1 >= jj * SLC_BLOCK)
    ov = ov & (i < n_cmp_pad - 1)
    return jnp.asarray(ov.astype(np.float32), jnp.bfloat16)


def _block_onehot(seq):
    blk = (np.arange(seq) // SLC_BLOCK) % LANES
    return jnp.asarray((blk[:, None] == np.arange(LANES)[None, :]).astype(np.float32), jnp.bfloat16)


def _pick_tile(n, pref):
    return pref if n % pref == 0 else n


def kernel(x, attn_norm_g, w_in, cmp_pos_k, cmp_w1_k, cmp_w2_k, cmp_pos_v, cmp_w1_v, cmp_w2_v,
           rel_bias_table, fox_forget_bias, w_branch_nsa, w_branch_fox, w_out,
           ffn_norm_g, w_up, conv_w, conv_b, w_down, final_norm_g):
    assert w_in.shape[0] == 1, "the final norm is fused into the single layer's FFN kernel"
    bsz, seq, d = x.shape
    t = bsz * seq
    bf = jnp.bfloat16
    scale = HEAD_DIM ** -0.5 * LOG2E
    x2d = x.reshape(t, d)
    w_in = w_in[0]

    o = np.cumsum([0, NSA_Q_DIM] + [NSA_KV_DIM] * 6 + [3 * NSA_HEADS, FOX_DIM, FOX_DIM, FOX_DIM, FOX_HEADS, d, d])
    w_qkv = jnp.concatenate([w_in[:, o[0]:o[1]] * scale, w_in[:, o[1]:o[7]],
                             w_in[:, o[8]:o[9]] * scale, w_in[:, o[9]:o[11]]], axis=1).astype(bf)
    gate_cols = []
    for grp in range(NSA_GROUPS):
        gate_cols += [w_in[:, o[7] + grp * GATES_PER_GROUP:o[7] + (grp + 1) * GATES_PER_GROUP],
                      jnp.zeros((d, LANES - GATES_PER_GROUP), w_in.dtype)]
    gate_cols += [w_in[:, o[11]:o[12]], jnp.zeros((d, LANES - FOX_HEADS), w_in.dtype)]
    w_gate = jnp.concatenate(gate_cols, axis=1).astype(bf)
    w_ma = w_in[:, o[12]:o[13]].astype(bf)
    w_mb = w_in[:, o[13]:o[14]].astype(bf)

    tm = _pick_tile(t, 1024)
    qkv = _norm_matmul(x2d, attn_norm_g[0], w_qkv, bf, tm, 512).reshape(bsz, seq, N_QKV)
    gates = _norm_matmul(x2d, attn_norm_g[0], w_gate, jnp.float32, tm, LANES).reshape(bsz, seq, N_GATE)

    q_all = jnp.concatenate([qkv[:, :, :NSA_Q_DIM], qkv[:, :, CB_QF * HEAD_DIM:(CB_QF + FOX_HEADS) * HEAD_DIM]],
                            axis=-1)
    qt = q_all.reshape(bsz, seq, NSA_HEADS + FOX_HEADS, HEAD_DIM).transpose(0, 2, 3, 1)

    def values_t(cb, heads):
        v = qkv[:, :, cb * HEAD_DIM:(cb + heads) * HEAD_DIM]
        return v.reshape(bsz, seq // LANES, LANES, heads, HEAD_DIM).transpose(0, 3, 1, 4, 2)

    n_ch = seq // CMP_STRIDE

    def chunked(cb):
        sl = qkv[:, :, cb * HEAD_DIM:(cb + NSA_GROUPS) * HEAD_DIM]
        sl = sl.reshape(bsz, n_ch, CMP_STRIDE, NSA_GROUPS, HEAD_DIM).transpose(0, 3, 1, 2, 4)
        return sl.reshape(bsz, NSA_GROUPS, n_ch, CMP_STRIDE * HEAD_DIM)

    def posflat(pos):
        return jnp.broadcast_to(pos.reshape(1, CMP_BLOCK * HEAD_DIM), (8, CMP_BLOCK * HEAD_DIM)).astype(bf)

    kc = _compress(chunked(CB_KC), cmp_w1_k[0].astype(bf), posflat(cmp_pos_k[0]), cmp_w2_k[0].astype(bf))
    vc = _compress(chunked(CB_VC), cmp_w1_v[0].astype(bf), posflat(cmp_pos_v[0]), cmp_w2_v[0].astype(bf))
    vct = vc.transpose(0, 1, 3, 2)

    f_t = gates[:, :, NSA_GROUPS * LANES:NSA_GROUPS * LANES + FOX_HEADS]
    f_t = f_t.transpose(0, 2, 1).reshape(bsz * FOX_HEADS, seq)
    bias_col = jnp.tile(fox_forget_bias[0].astype(jnp.float32), bsz).reshape(bsz * FOX_HEADS, 1)
    terms = _decay_cumsum(f_t, bias_col, _pick_tile(seq, 2048))
    dec = jnp.pad(terms.transpose(1, 2, 0), ((0, 0), (0, 0), (0, LANES - DECAY_TERMS)))
    dec = dec.reshape(bsz, FOX_HEADS, seq, LANES)

    n_slc = seq // SLC_BLOCK
    near = _near_bias(rel_bias_table)
    gates_t = gates[:, :, :NSA_GROUPS * LANES].reshape(bsz, seq, NSA_GROUPS, LANES)[..., :GATE_ROWS]
    gates_t = gates_t.transpose(0, 2, 3, 1)
    ycw, pen = _cmp_win(qt, kc, vct, _band_bias(rel_bias_table), _overlap_t(n_ch, n_slc), qkv,
                        values_t(CB_VW, NSA_GROUPS), near, gates_t)
    y_nsa = _sel(qkv, qt, pen, _block_onehot(seq), values_t(CB_VS, NSA_GROUPS), near, ycw, gates_t)

    y_fox = _fox(qkv, qt, dec, values_t(CB_VF, FOX_HEADS), _pick_tile(seq, FOX_Q_TILE))

    tm2 = _pick_tile(t, 512)
    x_mid = _merge(x2d, attn_norm_g[0], w_ma, w_mb, w_branch_nsa[0].astype(bf), w_branch_fox[0].astype(bf),
                   w_out[0].astype(bf), y_nsa.reshape(t, NSA_Q_DIM), y_fox.reshape(t, FOX_DIM),
                   tm2, _pick_tile(d, 512))
    d_ff = w_down.shape[1]
    out = _ffn(x_mid, ffn_norm_g[0], w_up[0].astype(bf), conv_w[0], conv_b[0], w_down[0].astype(bf),
               final_norm_g, seq, tm2, _pick_tile(d_ff, 512))
    return out.reshape(bsz, seq, d)
```

```python
import functools
import math

import jax
import jax.numpy as jnp
import numpy as np
from jax import lax
from jax.experimental import pallas as pl
from jax.experimental.pallas import tpu as pltpu

HEAD_DIM = 128
NSA_HEADS = 8
NSA_GROUPS = 2
NSA_HPG = NSA_HEADS // NSA_GROUPS
FOX_HEADS = 8
CMP_BLOCK = 32
CMP_STRIDE = 16
SLC_BLOCK = 64
SLC_TOPK = 16
WINDOW = 512
REL_BUCKETS = 32
REL_MAX_DIST = 128
CONV_WIDTH = 3
EPS = 1e-6
NEG = -1e30
FORCED_SCORE = 1e4
LOG2E = math.log2(math.e)

LANES = 128
BF16_SUBLANES = 16
VMEM_LIMIT = 56 * 1024 * 1024

NSA_Q_DIM = NSA_HEADS * HEAD_DIM
NSA_KV_DIM = NSA_GROUPS * HEAD_DIM
FOX_DIM = FOX_HEADS * HEAD_DIM
N_QKV = NSA_Q_DIM + 6 * NSA_KV_DIM + 3 * FOX_DIM
GATES_PER_GROUP = 3 * NSA_HPG
N_GATE = (NSA_GROUPS + 1) * LANES

CB_KC = 8
CB_VC = 10
CB_KS = 12
CB_VS = 14
CB_KW = 16
CB_VW = 18
CB_QF = 20
CB_KF = 28
CB_VF = 36

Q_TILE = 128
ROWS = NSA_HPG * Q_TILE
KEY_CHUNK = 512
KEY_PIECES = KEY_CHUNK // LANES
FOX_Q_TILE = 512
N_SPLIT = 2
DECAY_TERMS = 3
BAND_ROWS = 24
WIN_KEYS = WINDOW + Q_TILE
NEAR_KEYS = 2 * Q_TILE
GATE_ROWS = 16


def _params(sem):
    return pltpu.CompilerParams(dimension_semantics=sem, vmem_limit_bytes=VMEM_LIMIT)


def _norm_rows(x, g):
    return (x * lax.rsqrt(jnp.mean(x * x, axis=-1, keepdims=True) + EPS)) * g


def _norm_matmul_kernel(x_ref, g_ref, w_ref, o_ref, h_ref):
    @pl.when(pl.program_id(1) == 0)
    def _():
        h_ref[...] = _norm_rows(x_ref[...], g_ref[...]).astype(h_ref.dtype)

    o_ref[...] = jnp.dot(h_ref[...], w_ref[...], preferred_element_type=jnp.float32).astype(o_ref.dtype)


def _norm_matmul(x2d, g, w, out_dtype, tm, tn):
    t, d = x2d.shape
    n = w.shape[1]
    return pl.pallas_call(
        _norm_matmul_kernel,
        grid=(t // tm, n // tn),
        in_specs=[
            pl.BlockSpec((tm, d), lambda i, j: (i, 0)),
            pl.BlockSpec((1, d), lambda i, j: (0, 0)),
            pl.BlockSpec((d, tn), lambda i, j: (0, j)),
        ],
        out_specs=pl.BlockSpec((tm, tn), lambda i, j: (i, j)),
        out_shape=jax.ShapeDtypeStruct((t, n), out_dtype),
        scratch_shapes=[pltpu.VMEM((tm, d), jnp.bfloat16)],
        compiler_params=_params(("parallel", "arbitrary")),
        name="norm_in_proj",
    )(x2d, g.reshape(1, d), w)


def _compress_kernel(ch_ref, w1_ref, posf_ref, w2_ref, o_ref):
    half = ch_ref.shape[1]
    ch = ch_ref[...]
    pa = jnp.dot(ch, w1_ref[:half, :], preferred_element_type=jnp.float32)
    pb = jnp.dot(ch, w1_ref[half:, :], preferred_element_type=jnp.float32)
    pos = jnp.dot(posf_ref[...], w1_ref[...], preferred_element_type=jnp.float32)[0:1, :]
    n = pa.shape[0]
    pre = pa + pltpu.roll(pb, n - 1, 0) + pos
    act = jax.nn.gelu(pre)
    o_ref[...] = jnp.dot(act.astype(jnp.bfloat16), w2_ref[...],
                         preferred_element_type=jnp.float32).astype(o_ref.dtype)


def _compress(chunks, w1, posf, w2):
    b, g, n, k = chunks.shape
    return pl.pallas_call(
        _compress_kernel,
        grid=(b, g),
        in_specs=[
            pl.BlockSpec((None, None, n, k), lambda i, j: (i, j, 0, 0)),
            pl.BlockSpec(w1.shape, lambda i, j: (0, 0)),
            pl.BlockSpec(posf.shape, lambda i, j: (0, 0)),
            pl.BlockSpec(w2.shape, lambda i, j: (0, 0)),
        ],
        out_specs=pl.BlockSpec((None, None, n, HEAD_DIM), lambda i, j: (i, j, 0, 0)),
        out_shape=jax.ShapeDtypeStruct((b, g, n, HEAD_DIM), jnp.bfloat16),
        compiler_params=_params(("parallel", "parallel")),
        name="compress_tokens",
    )(chunks, w1, posf, w2)


def _decay_kernel(f_ref, b_ref, tri_ref, o_ref, carry_ref):
    @pl.when(pl.program_id(0) == 0)
    def _():
        carry_ref[...] = jnp.zeros_like(carry_ref)

    x = f_ref[...] + b_ref[...]
    logf = (jnp.minimum(x, 0.0) - jnp.log1p(jnp.exp(-jnp.abs(x)))) * LOG2E
    carry = carry_ref[...]
    for seg in range(f_ref.shape[1] // LANES):
        part = jnp.dot(logf[:, seg * LANES:(seg + 1) * LANES], tri_ref[...],
                       preferred_element_type=jnp.float32, precision=lax.Precision.HIGHEST) + carry
        carry = part[:, LANES - 1:LANES]
        rest = part
        for term in range(DECAY_TERMS):
            piece = rest.astype(o_ref.dtype)
            o_ref[term, :, seg * LANES:(seg + 1) * LANES] = piece
            rest = rest - piece.astype(jnp.float32)
    carry_ref[...] = carry


def _decay_cumsum(f_t, bias_col, width):
    rows, s = f_t.shape
    tri = jnp.asarray(np.triu(np.ones((LANES, LANES), np.float32)))
    return pl.pallas_call(
        _decay_kernel,
        grid=(s // width,),
        in_specs=[
            pl.BlockSpec((rows, width), lambda i: (0, i)),
            pl.BlockSpec((rows, 1), lambda i: (0, 0)),
            pl.BlockSpec((LANES, LANES), lambda i: (0, 0)),
        ],
        out_specs=pl.BlockSpec((DECAY_TERMS, rows, width), lambda i: (0, 0, i)),
        out_shape=jax.ShapeDtypeStruct((DECAY_TERMS, rows, s), jnp.bfloat16),
        scratch_shapes=[pltpu.VMEM((rows, 1), jnp.float32)],
        compiler_params=_params(("arbitrary",)),
        name="decay_cumsum",
    )(f_t, bias_col, tri)


def _col_reduce(op, x):
    reduce = {jnp.maximum: jnp.max, jnp.minimum: jnp.min, jnp.add: jnp.sum}[op]
    return reduce(x, axis=0, keepdims=True)


def _flash_init(m_ref, l_ref, acc_ref):
    m_ref[...] = jnp.full(m_ref.shape, NEG, jnp.float32)
    l_ref[...] = jnp.zeros(l_ref.shape, jnp.float32)
    acc_ref[...] = jnp.zeros(acc_ref.shape, jnp.float32)


def _flash_step(kaug, vt, get_q, width, m_ref, l_ref, acc_ref, extra=None):
    w = width // N_SPLIT
    strips = [(i * w, (i + 1) * w) for i in range(N_SPLIT)]
    scores = [jnp.dot(kaug, get_q(lo, hi), preferred_element_type=jnp.float32) for lo, hi in strips]
    for (lo, hi), s in zip(strips, scores):
        if extra is not None:
            s = extra(s, lo, hi)
        m_prev = m_ref[:, lo:hi]
        m_new = jnp.maximum(m_prev, _col_reduce(jnp.maximum, s))
        alpha = jnp.exp2(m_prev - m_new)
        p = jnp.exp2(s - m_new)
        l_ref[:, lo:hi] = alpha * l_ref[:, lo:hi] + _col_reduce(jnp.add, p)
        acc_ref[:, lo:hi] = alpha * acc_ref[:, lo:hi] + jnp.dot(
            vt, p.astype(vt.dtype), preferred_element_type=jnp.float32)
        m_ref[:, lo:hi] = m_new


def _flash_far_chunks(n, load_keys, load_values, get_q_for, width, m_ref, l_ref, acc_ref, s_ref, mx_ref, p_ref):
    w = width // N_SPLIT
    strips = [(i * w, (i + 1) * w) for i in range(N_SPLIT)]
    base = n % 2
    pairs = n // 2

    def scores_into(c, slot):
        kaug = load_keys(c)
        get_q = get_q_for(c)
        for lo, hi in strips:
            s = jnp.dot(kaug, get_q(lo, hi), preferred_element_type=jnp.float32)
            s_ref[slot, :, lo:hi] = s
            mx_ref[slot, :, lo:hi] = _col_reduce(jnp.maximum, s)

    def value_product(c, slot):
        vt = load_values(c)
        return jnp.concatenate([jnp.dot(vt, p_ref[slot, :, lo:hi], preferred_element_type=jnp.float32)
                                for lo, hi in strips], axis=1)

    @pl.when(base == 1)
    def _():
        _flash_step(load_keys(0), load_values(0), get_q_for(0), width, m_ref, l_ref, acc_ref)

    @pl.when(pairs > 0)
    def _():
        scores_into(base, 0)
        p_ref[1] = jnp.zeros(p_ref.shape[1:], p_ref.dtype)

        def pair(j, carry):
            for cur in (0, 1):
                c = base + 2 * j + cur
                nxt = 1 - cur
                scores_into(jnp.minimum(c + 1, n - 1), nxt)
                pv = value_product(jnp.maximum(c - 1, base), nxt)
                alphas = []
                for lo, hi in strips:
                    s = s_ref[cur, :, lo:hi]
                    m_prev = m_ref[:, lo:hi]
                    m_new = jnp.maximum(m_prev, mx_ref[cur, :, lo:hi])
                    alpha = jnp.exp2(m_prev - m_new)
                    p = jnp.exp2(s - m_new)
                    l_ref[:, lo:hi] = alpha * l_ref[:, lo:hi] + _col_reduce(jnp.add, p)
                    m_ref[:, lo:hi] = m_new
                    p_ref[cur, :, lo:hi] = p.astype(p_ref.dtype)
                    alphas.append(alpha)
                acc_ref[...] = jnp.concatenate(alphas, axis=1) * (acc_ref[...] + pv)
            return carry

        lax.fori_loop(0, pairs, pair, 0)
        acc_ref[...] = acc_ref[...] + value_product(n - 1, 1)


def _key_chunk(k_ref, aug_ref, vt_ref, k0, n_keys):
    kaug = jnp.concatenate([k_ref[pl.ds(k0, n_keys), :], aug_ref[pl.ds(k0, n_keys), :]], axis=1)
    return kaug, _values_t(vt_ref, k0, n_keys)


def _group_queries_t(qt_ref):
    return jnp.concatenate([qt_ref[h] for h in range(NSA_HPG)], axis=1)


def _branch_gate(gates_t, head, branch):
    row = head * 3 + branch
    return gates_t[row:row + 1, :]


def _values_t(vt_ref, k0, n_keys):
    p0 = k0 // LANES
    return jnp.concatenate([vt_ref[p0 + j] for j in range(n_keys // LANES)], axis=1)


def _cmp_win_kernel(qt_ref, kc_ref, vct_ref, band_ref, ovt_ref, kw_ref, vwt_ref, near_ref, gt_ref,
                    ycw_ref, pen_ref, sc_ref, sw_ref):
    qb = pl.program_id(2)
    n_cmp = kc_ref.shape[0]
    n_slc = ovt_ref.shape[0]
    q4t = _group_queries_t(qt_ref)

    sc_ref[...] = jnp.dot(kc_ref[...], q4t, preferred_element_type=jnp.float32)
    i0 = pl.multiple_of(jnp.maximum(8 * qb - 16, 0), 8)
    sc_ref[pl.ds(i0, BAND_ROWS), :] = sc_ref[pl.ds(i0, BAND_ROWS), :] + band_ref[...]
    row = lax.broadcasted_iota(jnp.int32, (n_cmp, ROWS), 0)
    sc = jnp.where(row < i0 + BAND_ROWS, sc_ref[...], NEG)
    m = _col_reduce(jnp.maximum, sc)
    p = jnp.exp2(sc - m)
    l = _col_reduce(jnp.add, p)
    pn = p * jnp.where(m > 0.5 * NEG, 1.0 / l, 0.0)
    oct_ = jnp.dot(vct_ref[...], pn.astype(jnp.bfloat16), preferred_element_type=jnp.float32)
    psum = pn[:, 0:Q_TILE]
    for h in range(1, NSA_HPG):
        psum = psum + pn[:, h * Q_TILE:(h + 1) * Q_TILE]
    imp = jnp.dot(ovt_ref[...], psum.astype(jnp.bfloat16), preferred_element_type=jnp.float32)

    ji = lax.broadcasted_iota(jnp.int32, (n_slc, Q_TILE), 0)
    jf = ji.astype(jnp.float32)
    t = qb * Q_TILE + lax.broadcasted_iota(jnp.int32, (n_slc, Q_TILE), 1)
    cur = t // SLC_BLOCK
    forced = (ji == 0) | (ji == cur) | (ji == cur - 1)
    score = jnp.where(ji <= cur, jnp.where(forced, FORCED_SCORE, imp), -1.0)
    pen_t = jnp.full((n_slc, Q_TILE), NEG, jnp.float32)
    for _ in range(min(SLC_TOPK, n_slc)):
        mx = _col_reduce(jnp.maximum, score)
        idx = _col_reduce(jnp.minimum, jnp.where(score == mx, jf, float(n_slc)))
        pick = jf == idx
        pen_t = jnp.where(pick, 0.0, pen_t)
        score = jnp.where(pick, -2.0, score)
    pad = pen_ref.shape[0] - n_slc
    if pad:
        pen_t = jnp.concatenate([pen_t, jnp.full((pad, Q_TILE), NEG, jnp.float32)], axis=0)
    pen_ref[...] = pen_t.astype(pen_ref.dtype)

    t0 = qb * Q_TILE
    ws = pl.multiple_of(jnp.maximum(t0 - WINDOW, 0), Q_TILE)
    ns = pl.multiple_of(jnp.maximum(t0 - Q_TILE, 0), Q_TILE)
    sw_ref[...] = jnp.dot(kw_ref[pl.ds(ws, WIN_KEYS), :], q4t, preferred_element_type=jnp.float32)
    off = pl.multiple_of(ns - ws, Q_TILE)
    sw_ref[pl.ds(off, NEAR_KEYS), :] = sw_ref[pl.ds(off, NEAR_KEYS), :] + near_ref[...]
    key = ws + lax.broadcasted_iota(jnp.int32, (WIN_KEYS, ROWS), 0)
    qry = t0 + (lax.broadcasted_iota(jnp.int32, (WIN_KEYS, ROWS), 1) & (Q_TILE - 1))
    sw = jnp.where((key <= qry) & (key > qry - WINDOW), sw_ref[...], NEG)
    mw = _col_reduce(jnp.maximum, sw)
    pw = jnp.exp2(sw - mw)
    lw = _col_reduce(jnp.add, pw)
    owt = jnp.dot(_values_t(vwt_ref, ws, WIN_KEYS), pw.astype(jnp.bfloat16),
                  preferred_element_type=jnp.float32) / lw

    gates_t = jax.nn.sigmoid(gt_ref[...])
    for h in range(NSA_HPG):
        cols = slice(h * Q_TILE, (h + 1) * Q_TILE)
        mixed = _branch_gate(gates_t, h, 0) * oct_[:, cols] + _branch_gate(gates_t, h, 2) * owt[:, cols]
        ycw_ref[:, h * HEAD_DIM:(h + 1) * HEAD_DIM] = mixed.T


def _cmp_win(qt, kc, vct, band, ovt, qkv, vwt, near, gates_t):
    b, s, _ = qkv.shape
    n_cmp = kc.shape[2]
    n_slc = ovt.shape[0]
    nq = s // Q_TILE
    pen_w = -(-n_slc // LANES) * LANES
    n_band = band.shape[0] - 1
    return pl.pallas_call(
        _cmp_win_kernel,
        grid=(b, NSA_GROUPS, nq),
        in_specs=[
            pl.BlockSpec((None, NSA_HPG, HEAD_DIM, Q_TILE), lambda i, g, q: (i, g, 0, q)),
            pl.BlockSpec((None, None, n_cmp, HEAD_DIM), lambda i, g, q: (i, g, 0, 0)),
            pl.BlockSpec((None, None, HEAD_DIM, n_cmp), lambda i, g, q: (i, g, 0, 0)),
            pl.BlockSpec((None, None, BAND_ROWS, ROWS), lambda i, g, q: (jnp.minimum(q, n_band), g, 0, 0)),
            pl.BlockSpec((n_slc, n_cmp), lambda i, g, q: (0, 0)),
            pl.BlockSpec((None, s, HEAD_DIM), lambda i, g, q: (i, 0, CB_KW + g)),
            pl.BlockSpec((None, None, s // LANES, HEAD_DIM, LANES), lambda i, g, q: (i, g, 0, 0, 0)),
            pl.BlockSpec((None, None, NEAR_KEYS, ROWS), lambda i, g, q: (jnp.minimum(q, 1), g, 0, 0)),
            pl.BlockSpec((None, None, GATE_ROWS, Q_TILE), lambda i, g, q: (i, g, 0, q)),
        ],
        out_specs=[
            pl.BlockSpec((None, Q_TILE, NSA_HPG * HEAD_DIM), lambda i, g, q: (i, q, g)),
            pl.BlockSpec((None, None, pen_w, Q_TILE), lambda i, g, q: (i, g, 0, q)),
        ],
        out_shape=[
            jax.ShapeDtypeStruct((b, s, NSA_Q_DIM), jnp.float32),
            jax.ShapeDtypeStruct((b, NSA_GROUPS, pen_w, s), jnp.bfloat16),
        ],
        scratch_shapes=[pltpu.VMEM((n_cmp, ROWS), jnp.float32), pltpu.VMEM((WIN_KEYS, ROWS), jnp.float32)],
        compiler_params=_params(("parallel", "parallel", "arbitrary")),
        name="nsa_cmp_topk_win",
    )(qt, kc, vct, band, ovt, qkv, vwt, near, gates_t)


def _range_penalty(pen, first_blk, lo_ok, hi_ok):
    n_half = pen.shape[0] // LANES
    c = lax.broadcasted_iota(jnp.int32, (LANES, pen.shape[1]), 0)
    out = pen[0:LANES, :]
    blk = c
    for hf in range(1, n_half):
        in_lower = (c + (hf - 1) * LANES >= first_blk) & (first_blk < hf * LANES)
        out = jnp.where(in_lower, out, pen[hf * LANES:(hf + 1) * LANES, :])
        blk = jnp.where(in_lower, blk, c + hf * LANES)
    return jnp.where((blk >= lo_ok) & (blk < hi_ok), out, jnp.asarray(NEG, out.dtype))


def _sel_kernel(qt_ref, pen_ref, ks_ref, oh_ref, vt_ref, near_ref, ycw_ref, gt_ref, y_ref,
                qaug_ref, m_ref, l_ref, acc_ref, s_ref, mx_ref, p_ref):
    qb = pl.program_id(2)
    n_half = pen_ref.shape[0] // LANES
    n_blocks = pen_ref.shape[0]
    chunks_per_half = LANES * SLC_BLOCK // KEY_CHUNK
    blocks_per_chunk = KEY_CHUNK // SLC_BLOCK

    t0 = qb * Q_TILE
    near_end = t0 - Q_TILE
    nf = jnp.maximum(near_end // KEY_CHUNK, 0)
    ms = pl.multiple_of(jnp.maximum(near_end - KEY_CHUNK, 0), Q_TILE)
    ns = pl.multiple_of(jnp.maximum(near_end, 0), Q_TILE)

    q4t = _group_queries_t(qt_ref)
    pen = pen_ref[...]
    operands = [pen[hf * LANES:(hf + 1) * LANES, :] for hf in range(n_half)]
    operands.append(_range_penalty(pen, ms // SLC_BLOCK, nf * blocks_per_chunk, near_end // SLC_BLOCK))
    operands.append(_range_penalty(pen, ns // SLC_BLOCK, 0, n_blocks))
    for idx, channels in enumerate(operands):
        qaug_ref[idx, 0:HEAD_DIM, :] = q4t
        qaug_ref[idx, HEAD_DIM:, :] = jnp.concatenate([channels] * NSA_HPG, axis=1)

    _flash_init(m_ref, l_ref, acc_ref)

    def load_keys(c):
        k0 = pl.multiple_of(c * KEY_CHUNK, KEY_CHUNK)
        return jnp.concatenate([ks_ref[pl.ds(k0, KEY_CHUNK), :], oh_ref[pl.ds(k0, KEY_CHUNK), :]], axis=1)

    def get_q_for(c):
        half = c // chunks_per_half
        return lambda lo, hi: qaug_ref[half, :, lo:hi]

    _flash_far_chunks(nf, load_keys, lambda c: _values_t(vt_ref, pl.multiple_of(c * KEY_CHUNK, KEY_CHUNK), KEY_CHUNK),
                      get_q_for, ROWS, m_ref, l_ref, acc_ref, s_ref, mx_ref, p_ref)

    kaug, vt = _key_chunk(ks_ref, oh_ref, vt_ref, ms, KEY_CHUNK)
    _flash_step(kaug, vt, lambda lo, hi: qaug_ref[n_half, :, lo:hi], ROWS, m_ref, l_ref, acc_ref)
    kaug, vt = _key_chunk(ks_ref, oh_ref, vt_ref, ns, NEAR_KEYS)
    _flash_step(kaug, vt, lambda lo, hi: qaug_ref[n_half + 1, :, lo:hi], ROWS, m_ref, l_ref, acc_ref,
                extra=lambda s, lo, hi: s + near_ref[:, lo:hi])

    o = acc_ref[...] / l_ref[...]
    gates_t = jax.nn.sigmoid(gt_ref[...])
    for h in range(NSA_HPG):
        sel_h = (_branch_gate(gates_t, h, 1) * o[:, h * Q_TILE:(h + 1) * Q_TILE]).T
        y_ref[:, h * HEAD_DIM:(h + 1) * HEAD_DIM] = (
            ycw_ref[:, h * HEAD_DIM:(h + 1) * HEAD_DIM] + sel_h).astype(y_ref.dtype)


def _sel(qkv, qt, pen, onehot, vt, near, ycw, gates_t):
    b, s, _ = qkv.shape
    nq = s // Q_TILE
    pen_w = pen.shape[2]
    return pl.pallas_call(
        _sel_kernel,
        grid=(b, NSA_GROUPS, nq),
        in_specs=[
            pl.BlockSpec((None, NSA_HPG, HEAD_DIM, Q_TILE), lambda i, g, q: (i, g, 0, q)),
            pl.BlockSpec((None, None, pen_w, Q_TILE), lambda i, g, q: (i, g, 0, q)),
            pl.BlockSpec((None, s, HEAD_DIM), lambda i, g, q: (i, 0, CB_KS + g)),
            pl.BlockSpec((s, LANES), lambda i, g, q: (0, 0)),
            pl.BlockSpec((None, None, s // LANES, HEAD_DIM, LANES), lambda i, g, q: (i, g, 0, 0, 0)),
            pl.BlockSpec((None, None, NEAR_KEYS, ROWS), lambda i, g, q: (jnp.minimum(q, 1), g, 0, 0)),
            pl.BlockSpec((None, Q_TILE, NSA_HPG * HEAD_DIM), lambda i, g, q: (i, q, g)),
            pl.BlockSpec((None, None, GATE_ROWS, Q_TILE), lambda i, g, q: (i, g, 0, q)),
        ],
        out_specs=pl.BlockSpec((None, Q_TILE, NSA_HPG * HEAD_DIM), lambda i, g, q: (i, q, g)),
        out_shape=jax.ShapeDtypeStruct((b, s, NSA_Q_DIM), jnp.bfloat16),
        scratch_shapes=[
            pltpu.VMEM((pen_w // LANES + 2, 2 * HEAD_DIM, ROWS), jnp.bfloat16),
            pltpu.VMEM((1, ROWS), jnp.float32),
            pltpu.VMEM((1, ROWS), jnp.float32),
            pltpu.VMEM((HEAD_DIM, ROWS), jnp.float32),
            pltpu.VMEM((2, KEY_CHUNK, ROWS), jnp.float32),
            pltpu.VMEM((2, 1, ROWS), jnp.float32),
            pltpu.VMEM((2, KEY_CHUNK, ROWS), jnp.bfloat16),
        ],
        compiler_params=_params(("parallel", "parallel", "arbitrary")),
        name="nsa_selected",
    )(qt, pen, qkv, onehot, vt, near, ycw, gates_t)


def _fox_kernel(qt_ref, k_ref, dec_ref, vt_ref, o_ref, qaug_ref, m_ref, l_ref, acc_ref, s_ref, mx_ref, p_ref):
    qi = pl.program_id(2)
    tq = qt_ref.shape[1]
    row = lax.broadcasted_iota(jnp.int32, (HEAD_DIM, tq), 0)
    qaug_ref[0:HEAD_DIM, :] = qt_ref[...]
    qaug_ref[HEAD_DIM:, :] = jnp.where(row < DECAY_TERMS, -1.0, 0.0).astype(qaug_ref.dtype)
    _flash_init(m_ref, l_ref, acc_ref)

    def get_q(lo, hi):
        return qaug_ref[:, lo:hi]

    def load_keys(c):
        k0 = pl.multiple_of(c * tq, tq)
        return jnp.concatenate([k_ref[pl.ds(k0, tq), :], dec_ref[pl.ds(k0, tq), :]], axis=1)

    _flash_far_chunks(qi, load_keys, lambda c: _values_t(vt_ref, pl.multiple_of(c * tq, tq), tq),
                      lambda c: get_q, tq, m_ref, l_ref, acc_ref, s_ref, mx_ref, p_ref)

    def causal(s, lo, hi):
        key = lax.broadcasted_iota(jnp.int32, s.shape, 0)
        qry = lo + lax.broadcasted_iota(jnp.int32, s.shape, 1)
        return jnp.where(key <= qry, s, NEG)

    kaug, vt = _key_chunk(k_ref, dec_ref, vt_ref, pl.multiple_of(qi * tq, tq), tq)
    _flash_step(kaug, vt, get_q, tq, m_ref, l_ref, acc_ref, extra=causal)
    o_ref[...] = (acc_ref[...] / l_ref[...]).T.astype(o_ref.dtype)


def _fox(qkv, qt, dec, vt, tq):
    b, s, _ = qkv.shape
    return pl.pallas_call(
        _fox_kernel,
        grid=(b, FOX_HEADS, s // tq),
        in_specs=[
            pl.BlockSpec((None, None, HEAD_DIM, tq), lambda i, h, q: (i, NSA_HEADS + h, 0, q)),
            pl.BlockSpec((None, s, HEAD_DIM), lambda i, h, q: (i, 0, CB_KF + h)),
            pl.BlockSpec((None, None, s, LANES), lambda i, h, q: (i, h, 0, 0)),
            pl.BlockSpec((None, None, s // LANES, HEAD_DIM, LANES), lambda i, h, q: (i, h, 0, 0, 0)),
        ],
        out_specs=pl.BlockSpec((None, tq, HEAD_DIM), lambda i, h, q: (i, q, h)),
        out_shape=jax.ShapeDtypeStruct((b, s, FOX_DIM), jnp.bfloat16),
        scratch_shapes=[
            pltpu.VMEM((2 * HEAD_DIM, tq), jnp.bfloat16),
            pltpu.VMEM((1, tq), jnp.float32),
            pltpu.VMEM((1, tq), jnp.float32),
            pltpu.VMEM((HEAD_DIM, tq), jnp.float32),
            pltpu.VMEM((2, tq, tq), jnp.float32),
            pltpu.VMEM((2, 1, tq), jnp.float32),
            pltpu.VMEM((2, tq, tq), jnp.bfloat16),
        ],
        compiler_params=_params(("parallel", "parallel", "arbitrary")),
        name="fox_attention",
    )(qt, qkv, dec, vt)


def _merge_kernel(x_ref, g_ref, wa_ref, wb_ref, pa_ref, pb_ref, wo_ref, ya_ref, yb_ref, o_ref,
                  h_ref, acc_ref):
    j = pl.program_id(1)

    @pl.when(j == 0)
    def _():
        h_ref[...] = _norm_rows(x_ref[...], g_ref[...]).astype(h_ref.dtype)
        acc_ref[...] = jnp.zeros_like(acc_ref)

    h = h_ref[...]
    ga = jax.nn.sigmoid(jnp.dot(h, wa_ref[...], preferred_element_type=jnp.float32))
    gb = jax.nn.sigmoid(jnp.dot(h, wb_ref[...], preferred_element_type=jnp.float32))
    a = jnp.dot(ya_ref[...], pa_ref[...], preferred_element_type=jnp.float32)
    bb = jnp.dot(yb_ref[...], pb_ref[...], preferred_element_type=jnp.float32)
    merged = (ga * a + gb * bb).astype(jnp.bfloat16)
    acc_ref[...] += jnp.dot(merged, wo_ref[...], preferred_element_type=jnp.float32)

    @pl.when(j == pl.num_programs(1) - 1)
    def _():
        o_ref[...] = x_ref[...] + acc_ref[...]


def _merge(x2d, g, w_ma, w_mb, p_a, p_b, w_out, y_a, y_b, tm, tn):
    t, d = x2d.shape
    ka = p_a.shape[0]
    kb = p_b.shape[0]
    return pl.pallas_call(
        _merge_kernel,
        grid=(t // tm, d // tn),
        in_specs=[
            pl.BlockSpec((tm, d), lambda i, j: (i, 0)),
            pl.BlockSpec((1, d), lambda i, j: (0, 0)),
            pl.BlockSpec((d, tn), lambda i, j: (0, j)),
            pl.BlockSpec((d, tn), lambda i, j: (0, j)),
            pl.BlockSpec((ka, tn), lambda i, j: (0, j)),
            pl.BlockSpec((kb, tn), lambda i, j: (0, j)),
            pl.BlockSpec((tn, d), lambda i, j: (j, 0)),
            pl.BlockSpec((tm, ka), lambda i, j: (i, 0)),
            pl.BlockSpec((tm, kb), lambda i, j: (i, 0)),
        ],
        out_specs=pl.BlockSpec((tm, d), lambda i, j: (i, 0)),
        out_shape=jax.ShapeDtypeStruct((t, d), jnp.float32),
        scratch_shapes=[pltpu.VMEM((tm, d), jnp.bfloat16), pltpu.VMEM((tm, d), jnp.float32)],
        compiler_params=_params(("parallel", "arbitrary")),
        name="merge_out_proj",
    )(x2d, g.reshape(1, d), w_ma, w_mb, p_a, p_b, w_out, y_a, y_b)


def _ffn_kernel(x_ref, xh_ref, g_ref, wu_ref, wv_ref, cw_ref, cb_ref, wd_ref, gf_ref, o_ref,
                h_ref, u_ref, acc_ref, *, seq, tm):
    i = pl.program_id(0)
    j = pl.program_id(1)
    halo = BF16_SUBLANES

    @pl.when(j == 0)
    def _():
        g = g_ref[...]
        keep = jnp.where((i * tm) % seq == 0, 0.0, 1.0)
        h_ref[0:halo, :] = (_norm_rows(xh_ref[...], g) * keep).astype(h_ref.dtype)
        h_ref[halo:, :] = _norm_rows(x_ref[...], g).astype(h_ref.dtype)
        acc_ref[...] = jnp.zeros_like(acc_ref)

    u_ref[...] = jnp.dot(h_ref[...], wu_ref[...], preferred_element_type=jnp.float32)
    v = jnp.dot(h_ref[halo:, :], wv_ref[...], preferred_element_type=jnp.float32)
    conv = cb_ref[...]
    for k in range(CONV_WIDTH):
        conv = conv + cw_ref[k:k + 1, :] * u_ref[pl.ds(halo - (CONV_WIDTH - 1) + k, tm), :]
    act = (jax.nn.gelu(conv) * v).astype(jnp.bfloat16)
    acc_ref[...] += jnp.dot(act, wd_ref[...], preferred_element_type=jnp.float32)

    @pl.when(j == pl.num_programs(1) - 1)
    def _():
        o_ref[...] = _norm_rows(x_ref[...] + acc_ref[...], gf_ref[...])


def _ffn(x2d, g, w_up, conv_w, conv_b, w_down, g_final, seq, tm, tn):
    t, d = x2d.shape
    d_ff = w_down.shape[0]
    nt = d_ff // tn
    halo = BF16_SUBLANES
    per = tm // halo
    return pl.pallas_call(
        functools.partial(_ffn_kernel, seq=seq, tm=tm),
        grid=(t // tm, nt),
        in_specs=[
            pl.BlockSpec((tm, d), lambda i, j: (i, 0)),
            pl.BlockSpec((halo, d), lambda i, j: (jnp.maximum(i * per - 1, 0), 0)),
            pl.BlockSpec((1, d), lambda i, j: (0, 0)),
            pl.BlockSpec((d, tn), lambda i, j: (0, j)),
            pl.BlockSpec((d, tn), lambda i, j: (0, nt + j)),
            pl.BlockSpec((CONV_WIDTH, tn), lambda i, j: (0, j)),
            pl.BlockSpec((1, tn), lambda i, j: (0, j)),
            pl.BlockSpec((tn, d), lambda i, j: (j, 0)),
            pl.BlockSpec((1, d), lambda i, j: (0, 0)),
        ],
        out_specs=pl.BlockSpec((tm, d), lambda i, j: (i, 0)),
        out_shape=jax.ShapeDtypeStruct((t, d), jnp.float32),
        scratch_shapes=[
            pltpu.VMEM((tm + halo, d), jnp.bfloat16),
            pltpu.VMEM((tm + halo, tn), jnp.float32),
            pltpu.VMEM((tm, d), jnp.float32),
        ],
        compiler_params=_params(("parallel", "arbitrary")),
        name="ffn_final_norm",
    )(x2d, x2d, g.reshape(1, d), w_up, w_up, conv_w, conv_b.reshape(1, d_ff), w_down, g_final.reshape(1, d))


def _t5_bucket_np(dist):
    n = np.maximum(dist, 0)
    max_exact = REL_BUCKETS // 2
    nf = np.maximum(n, 1).astype(np.float32)
    large = max_exact + (np.log(nf / np.float32(max_exact)) / np.float32(math.log(REL_MAX_DIST / max_exact))
                         * np.float32(REL_BUCKETS - max_exact)).astype(np.int32)
    return np.where(n < max_exact, n, np.minimum(large, REL_BUCKETS - 1)).astype(np.int32)


def _bias_by_distance(rel_table, far_shift):
    vals = rel_table[jnp.asarray(_t5_bucket_np(np.arange(REL_MAX_DIST + 1)))]
    if far_shift:
        vals = vals - rel_table[REL_BUCKETS - 1]
    return vals * LOG2E


def _near_bias(rel_table):
    fd = _bias_by_distance(rel_table, True)
    h = fd.shape[1]
    lo = NEAR_KEYS - 1
    vec = jnp.concatenate([jnp.full((lo, h), NEG, fd.dtype), fd[:REL_MAX_DIST],
                           jnp.zeros((NEAR_KEYS - REL_MAX_DIST, h), fd.dtype)], axis=0)
    tiles = jnp.stack([jnp.stack([vec[lo + a - r:lo + a - r + Q_TILE] for r in range(NEAR_KEYS)])
                       for a in (0, Q_TILE)])
    tiles = tiles.reshape(2, NEAR_KEYS, Q_TILE, NSA_GROUPS, NSA_HPG).transpose(0, 3, 1, 4, 2)
    return tiles.reshape(2, NSA_GROUPS, NEAR_KEYS, ROWS)


def _band_bias(rel_table):
    fd = _bias_by_distance(rel_table, True)
    tl = np.arange(Q_TILE)[None, :]
    r = np.arange(BAND_ROWS)[:, None]
    dist = np.stack([tl - CMP_STRIDE * (r - off) - (CMP_BLOCK - 1) for off in (0, 8, 16)])
    vals = fd[jnp.asarray(np.clip(dist, 0, REL_MAX_DIST))]
    vals = jnp.where(jnp.asarray(dist >= 0)[..., None], vals, NEG)
    v, rr, q, _ = vals.shape
    return vals.transpose(0, 1, 3, 2).reshape(v, rr, NSA_GROUPS, NSA_HPG * q).transpose(0, 2, 1, 3)


def _overlap_t(n_cmp_pad, n_slc):
    i = np.arange(n_cmp_pad)[None, :]
    jj = np.arange(n_slc)[:, None]
    c_start = i * CMP_STRIDE
    ov = (c_start < jj * SLC_BLOCK + SLC_BLOCK) & (c_start + CMP_BLOCK - 1 >= jj * SLC_BLOCK)
    ov = ov & (i < n_cmp_pad - 1)
    return jnp.asarray(ov.astype(np.float32), jnp.bfloat16)


def _block_onehot(seq):
    blk = (np.arange(seq) // SLC_BLOCK) % LANES
    return jnp.asarray((blk[:, None] == np.arange(LANES)[None, :]).astype(np.float32), jnp.bfloat16)


def _pick_tile(n, pref):
    return pref if n % pref == 0 else n


def kernel(x, attn_norm_g, w_in, cmp_pos_k, cmp_w1_k, cmp_w2_k, cmp_pos_v, cmp_w1_v, cmp_w2_v,
           rel_bias_table, fox_forget_bias, w_branch_nsa, w_branch_fox, w_out,
           ffn_norm_g, w_up, conv_w, conv_b, w_down, final_norm_g):
    assert w_in.shape[0] == 1, "the final norm is fused into the single layer's FFN kernel"
    bsz, seq, d = x.shape
    t = bsz * seq
    bf = jnp.bfloat16
    scale = HEAD_DIM ** -0.5 * LOG2E
    x2d = x.reshape(t, d)
    w_in = w_in[0]

    o = np.cumsum([0, NSA_Q_DIM] + [NSA_KV_DIM] * 6 + [3 * NSA_HEADS, FOX_DIM, FOX_DIM, FOX_DIM, FOX_HEADS, d, d])
    w_qkv = jnp.concatenate([w_in[:, o[0]:o[1]] * scale, w_in[:, o[1]:o[7]],
                             w_in[:, o[8]:o[9]] * scale, w_in[:, o[9]:o[11]]], axis=1).astype(bf)
    gate_cols = []
    for grp in range(NSA_GROUPS):
        gate_cols += [w_in[:, o[7] + grp * GATES_PER_GROUP:o[7] + (grp + 1) * GATES_PER_GROUP],
                      jnp.zeros((d, LANES - GATES_PER_GROUP), w_in.dtype)]
    gate_cols += [w_in[:, o[11]:o[12]], jnp.zeros((d, LANES - FOX_HEADS), w_in.dtype)]
    w_gate = jnp.concatenate(gate_cols, axis=1).astype(bf)
    w_ma = w_in[:, o[12]:o[13]].astype(bf)
    w_mb = w_in[:, o[13]:o[14]].astype(bf)

    tm = _pick_tile(t, 1024)
    qkv = _norm_matmul(x2d, attn_norm_g[0], w_qkv, bf, tm, 512).reshape(bsz, seq, N_QKV)
    gates = _norm_matmul(x2d, attn_norm_g[0], w_gate, jnp.float32, tm, LANES).reshape(bsz, seq, N_GATE)

    q_all = jnp.concatenate([qkv[:, :, :NSA_Q_DIM], qkv[:, :, CB_QF * HEAD_DIM:(CB_QF + FOX_HEADS) * HEAD_DIM]],
                            axis=-1)
    qt = q_all.reshape(bsz, seq, NSA_HEADS + FOX_HEADS, HEAD_DIM).transpose(0, 2, 3, 1)

    def values_t(cb, heads):
        v = qkv[:, :, cb * HEAD_DIM:(cb + heads) * HEAD_DIM]
        return v.reshape(bsz, seq // LANES, LANES, heads, HEAD_DIM).transpose(0, 3, 1, 4, 2)

    n_ch = seq // CMP_STRIDE

    def chunked(cb):
        sl = qkv[:, :, cb * HEAD_DIM:(cb + NSA_GROUPS) * HEAD_DIM]
        sl = sl.reshape(bsz, n_ch, CMP_STRIDE, NSA_GROUPS, HEAD_DIM).transpose(0, 3, 1, 2, 4)
        return sl.reshape(bsz, NSA_GROUPS, n_ch, CMP_STRIDE * HEAD_DIM)

    def posflat(pos):
        return jnp.broadcast_to(pos.reshape(1, CMP_BLOCK * HEAD_DIM), (8, CMP_BLOCK * HEAD_DIM)).astype(bf)

    kc = _compress(chunked(CB_KC), cmp_w1_k[0].astype(bf), posflat(cmp_pos_k[0]), cmp_w2_k[0].astype(bf))
    vc = _compress(chunked(CB_VC), cmp_w1_v[0].astype(bf), posflat(cmp_pos_v[0]), cmp_w2_v[0].astype(bf))
    vct = vc.transpose(0, 1, 3, 2)

    f_t = gates[:, :, NSA_GROUPS * LANES:NSA_GROUPS * LANES + FOX_HEADS]
    f_t = f_t.transpose(0, 2, 1).reshape(bsz * FOX_HEADS, seq)
    bias_col = jnp.tile(fox_forget_bias[0].astype(jnp.float32), bsz).reshape(bsz * FOX_HEADS, 1)
    terms = _decay_cumsum(f_t, bias_col, _pick_tile(seq, 2048))
    dec = jnp.pad(terms.transpose(1, 2, 0), ((0, 0), (0, 0), (0, LANES - DECAY_TERMS)))
    dec = dec.reshape(bsz, FOX_HEADS, seq, LANES)

    n_slc = seq // SLC_BLOCK
    near = _near_bias(rel_bias_table)
    gates_t = gates[:, :, :NSA_GROUPS * LANES].reshape(bsz, seq, NSA_GROUPS, LANES)[..., :GATE_ROWS]
    gates_t = gates_t.transpose(0, 2, 3, 1)
    ycw, pen = _cmp_win(qt, kc, vct, _band_bias(rel_bias_table), _overlap_t(n_ch, n_slc), qkv,
                        values_t(CB_VW, NSA_GROUPS), near, gates_t)
    y_nsa = _sel(qkv, qt, pen, _block_onehot(seq), values_t(CB_VS, NSA_GROUPS), near, ycw, gates_t)

    y_fox = _fox(qkv, qt, dec, values_t(CB_VF, FOX_HEADS), _pick_tile(seq, FOX_Q_TILE))

    tm2 = _pick_tile(t, 512)
    x_mid = _merge(x2d, attn_norm_g[0], w_ma, w_mb, w_branch_nsa[0].astype(bf), w_branch_fox[0].astype(bf),
                   w_out[0].astype(bf), y_nsa.reshape(t, NSA_Q_DIM), y_fox.reshape(t, FOX_DIM),
                   tm2, _pick_tile(d, 512))
    d_ff = w_down.shape[1]
    out = _ffn(x_mid, ffn_norm_g[0], w_up[0].astype(bf), conv_w[0], conv_b[0], w_down[0].astype(bf),
               final_norm_g, seq, tm2, _pick_tile(d_ff, 512))
    return out.reshape(bsz, seq, d)
```

```python
import functools
import math

import jax
import jax.numpy as jnp
import numpy as np
from jax import lax
from jax.experimental import pallas as pl
from jax.experimental.pallas import tpu as pltpu

HEAD_DIM = 128
NSA_HEADS = 8
NSA_GROUPS = 2
NSA_HPG = NSA_HEADS // NSA_GROUPS
FOX_HEADS = 8
CMP_BLOCK = 32
CMP_STRIDE = 16
SLC_BLOCK = 64
SLC_TOPK = 16
WINDOW = 512
REL_BUCKETS = 32
REL_MAX_DIST = 128
CONV_WIDTH = 3
EPS = 1e-6
NEG = -1e30
FORCED_SCORE = 1e4
LOG2E = math.log2(math.e)

LANES = 128
BF16_SUBLANES = 16
VMEM_LIMIT = 56 * 1024 * 1024

NSA_Q_DIM = NSA_HEADS * HEAD_DIM
NSA_KV_DIM = NSA_GROUPS * HEAD_DIM
FOX_DIM = FOX_HEADS * HEAD_DIM
N_QKV = NSA_Q_DIM + 6 * NSA_KV_DIM + 3 * FOX_DIM
GATES_PER_GROUP = 3 * NSA_HPG
N_GATE = (NSA_GROUPS + 1) * LANES

CB_KC = 8
CB_VC = 10
CB_KS = 12
CB_VS = 14
CB_KW = 16
CB_VW = 18
CB_QF = 20
CB_KF = 28
CB_VF = 36

Q_TILE = 128
ROWS = NSA_HPG * Q_TILE
KEY_CHUNK = 512
FOX_Q_TILE = 512
N_SPLIT = 2
ROW_BLOCK = 64
DECAY_TERMS = 3
BAND_ROWS = 24
WIN_KEYS = WINDOW + Q_TILE
NEAR_KEYS = 2 * Q_TILE
GATE_ROWS = 16
SUM_ROW = HEAD_DIM
ACC_ROWS = HEAD_DIM + 8


def _params(sem):
    return pltpu.CompilerParams(dimension_semantics=sem, vmem_limit_bytes=VMEM_LIMIT)


def _norm_rows(x, g):
    return (x * lax.rsqrt(jnp.mean(x * x, axis=-1, keepdims=True) + EPS)) * g


def _norm_matmul_kernel(x_ref, g_ref, w_ref, o_ref, h_ref):
    @pl.when(pl.program_id(1) == 0)
    def _():
        h_ref[...] = _norm_rows(x_ref[...], g_ref[...]).astype(h_ref.dtype)

    o_ref[...] = jnp.dot(h_ref[...], w_ref[...], preferred_element_type=jnp.float32).astype(o_ref.dtype)


def _norm_matmul(x2d, g, w, out_dtype, tm, tn):
    t, d = x2d.shape
    n = w.shape[1]
    return pl.pallas_call(
        _norm_matmul_kernel,
        grid=(t // tm, n // tn),
        in_specs=[
            pl.BlockSpec((tm, d), lambda i, j: (i, 0)),
            pl.BlockSpec((1, d), lambda i, j: (0, 0)),
            pl.BlockSpec((d, tn), lambda i, j: (0, j)),
        ],
        out_specs=pl.BlockSpec((tm, tn), lambda i, j: (i, j)),
        out_shape=jax.ShapeDtypeStruct((t, n), out_dtype),
        scratch_shapes=[pltpu.VMEM((tm, d), jnp.bfloat16)],
        compiler_params=_params(("parallel", "arbitrary")),
        name="norm_in_proj",
    )(x2d, g.reshape(1, d), w)


def _compress_kernel(ch_ref, w1_ref, posf_ref, w2_ref, o_ref):
    half = ch_ref.shape[1]
    ch = ch_ref[...]
    pa = jnp.dot(ch, w1_ref[:half, :], preferred_element_type=jnp.float32)
    pb = jnp.dot(ch, w1_ref[half:, :], preferred_element_type=jnp.float32)
    pos = jnp.dot(posf_ref[...], w1_ref[...], preferred_element_type=jnp.float32)[0:1, :]
    n = pa.shape[0]
    pre = pa + pltpu.roll(pb, n - 1, 0) + pos
    act = jax.nn.gelu(pre)
    o_ref[...] = jnp.dot(act.astype(jnp.bfloat16), w2_ref[...],
                         preferred_element_type=jnp.float32).astype(o_ref.dtype)


def _compress(chunks, w1, posf, w2):
    b, g, n, k = chunks.shape
    return pl.pallas_call(
        _compress_kernel,
        grid=(b, g),
        in_specs=[
            pl.BlockSpec((None, None, n, k), lambda i, j: (i, j, 0, 0)),
            pl.BlockSpec(w1.shape, lambda i, j: (0, 0)),
            pl.BlockSpec(posf.shape, lambda i, j: (0, 0)),
            pl.BlockSpec(w2.shape, lambda i, j: (0, 0)),
        ],
        out_specs=pl.BlockSpec((None, None, n, HEAD_DIM), lambda i, j: (i, j, 0, 0)),
        out_shape=jax.ShapeDtypeStruct((b, g, n, HEAD_DIM), jnp.bfloat16),
        compiler_params=_params(("parallel", "parallel")),
        name="compress_tokens",
    )(chunks, w1, posf, w2)


def _decay_kernel(f_ref, b_ref, tri_ref, o_ref, carry_ref):
    @pl.when(pl.program_id(0) == 0)
    def _():
        carry_ref[...] = jnp.zeros_like(carry_ref)

    x = f_ref[...] + b_ref[...]
    logf = (jnp.minimum(x, 0.0) - jnp.log1p(jnp.exp(-jnp.abs(x)))) * LOG2E
    carry = carry_ref[...]
    for seg in range(f_ref.shape[1] // LANES):
        part = jnp.dot(logf[:, seg * LANES:(seg + 1) * LANES], tri_ref[...],
                       preferred_element_type=jnp.float32, precision=lax.Precision.HIGHEST) + carry
        carry = part[:, LANES - 1:LANES]
        rest = part
        for term in range(DECAY_TERMS):
            piece = rest.astype(o_ref.dtype)
            o_ref[term, :, seg * LANES:(seg + 1) * LANES] = piece
            rest = rest - piece.astype(jnp.float32)
    carry_ref[...] = carry


def _decay_cumsum(f_t, bias_col, width):
    rows, s = f_t.shape
    tri = jnp.asarray(np.triu(np.ones((LANES, LANES), np.float32)))
    return pl.pallas_call(
        _decay_kernel,
        grid=(s // width,),
        in_specs=[
            pl.BlockSpec((rows, width), lambda i: (0, i)),
            pl.BlockSpec((rows, 1), lambda i: (0, 0)),
            pl.BlockSpec((LANES, LANES), lambda i: (0, 0)),
        ],
        out_specs=pl.BlockSpec((DECAY_TERMS, rows, width), lambda i: (0, 0, i)),
        out_shape=jax.ShapeDtypeStruct((DECAY_TERMS, rows, s), jnp.bfloat16),
        scratch_shapes=[pltpu.VMEM((rows, 1), jnp.float32)],
        compiler_params=_params(("arbitrary",)),
        name="decay_cumsum",
    )(f_t, bias_col, tri)


def _col_reduce(op, x):
    reduce = {jnp.maximum: jnp.max, jnp.minimum: jnp.min, jnp.add: jnp.sum}[op]
    return reduce(x, axis=0, keepdims=True)


def _flash_init(m_ref, acc_ref):
    m_ref[...] = jnp.full(m_ref.shape, NEG, jnp.float32)
    acc_ref[...] = jnp.zeros(acc_ref.shape, jnp.float32)


def _probabilities(s, m_new):
    return jnp.exp2((s - m_new).astype(jnp.bfloat16))


def _flash_result(acc_ref):
    return acc_ref[0:HEAD_DIM, :] / acc_ref[SUM_ROW:SUM_ROW + 1, :]


def _flash_step(kaug, vt, get_q, width, m_ref, acc_ref, extra=None):
    w = width // N_SPLIT
    strips = [(i * w, (i + 1) * w) for i in range(N_SPLIT)]
    scores = [jnp.dot(kaug, get_q(lo, hi), preferred_element_type=jnp.float32) for lo, hi in strips]
    for (lo, hi), s in zip(strips, scores):
        if extra is not None:
            s = extra(s, lo, hi)
        m_prev = m_ref[:, lo:hi]
        m_new = jnp.maximum(m_prev, _col_reduce(jnp.maximum, s))
        alpha = jnp.exp2(m_prev - m_new)
        acc_ref[:, lo:hi] = alpha * acc_ref[:, lo:hi] + jnp.dot(
            vt, _probabilities(s, m_new), preferred_element_type=jnp.float32)
        m_ref[:, lo:hi] = m_new


def _flash_far_chunks(n, load_keys, load_values, get_q_for, width, m_ref, acc_ref, s_ref, mx_ref, p_ref):
    w = width // N_SPLIT
    strips = [(i * w, (i + 1) * w) for i in range(N_SPLIT)]
    base = n % 2
    pairs = n // 2

    def scores_into(c, slot):
        kaug = load_keys(c)
        get_q = get_q_for(c)
        for lo, hi in strips:
            s = jnp.dot(kaug, get_q(lo, hi), preferred_element_type=jnp.float32)
            s_ref[slot, :, lo:hi] = s
            mx_ref[slot, :, lo:hi] = _col_reduce(jnp.maximum, s)

    def value_product(c, slot):
        vt = load_values(c)
        return jnp.concatenate([jnp.dot(vt, p_ref[slot, :, lo:hi], preferred_element_type=jnp.float32)
                                for lo, hi in strips], axis=1)

    @pl.when(base == 1)
    def _():
        _flash_step(load_keys(0), load_values(0), get_q_for(0), width, m_ref, acc_ref)

    @pl.when(pairs > 0)
    def _():
        scores_into(base, 0)
        p_ref[1] = jnp.zeros(p_ref.shape[1:], p_ref.dtype)

        def pair(j, carry):
            for cur in (0, 1):
                c = base + 2 * j + cur
                nxt = 1 - cur
                pv = value_product(jnp.maximum(c - 1, base), nxt)
                scores_into(jnp.minimum(c + 1, n - 1), nxt)
                alphas = []
                for lo, hi in strips:
                    m_prev = m_ref[:, lo:hi]
                    m_new = jnp.maximum(m_prev, mx_ref[cur, :, lo:hi])
                    m_ref[:, lo:hi] = m_new
                    for r in range(0, s_ref.shape[1], ROW_BLOCK):
                        p_ref[cur, r:r + ROW_BLOCK, lo:hi] = _probabilities(
                            s_ref[cur, r:r + ROW_BLOCK, lo:hi], m_new)
                    alphas.append(jnp.exp2(m_prev - m_new))
                acc_ref[...] = jnp.concatenate(alphas, axis=1) * (acc_ref[...] + pv)
            return carry

        lax.fori_loop(0, pairs, pair, 0)
        acc_ref[...] = acc_ref[...] + value_product(n - 1, 1)


def _group_queries_t(qt_ref):
    return jnp.concatenate([qt_ref[h] for h in range(NSA_HPG)], axis=1)


def _branch_gate(gates_t, head, branch):
    row = head * 3 + branch
    return gates_t[row:row + 1, :]


def _values_t(vt_ref, k0, n_keys):
    p0 = k0 // LANES
    return jnp.concatenate([vt_ref[p0 + j] for j in range(n_keys // LANES)], axis=1)


def _cmp_win_kernel(qt_ref, kc_ref, vct_ref, band_ref, ovt_ref, kw_ref, vwt_ref, near_ref, gt_ref,
                    ycw_ref, pen_ref, sc_ref, sw_ref):
    qb = pl.program_id(2)
    n_cmp = kc_ref.shape[0]
    n_slc = ovt_ref.shape[0]
    q4t = _group_queries_t(qt_ref)

    sc_ref[...] = jnp.dot(kc_ref[...], q4t, preferred_element_type=jnp.float32)
    i0 = pl.multiple_of(jnp.maximum(8 * qb - 16, 0), 8)
    sc_ref[pl.ds(i0, BAND_ROWS), :] = sc_ref[pl.ds(i0, BAND_ROWS), :] + band_ref[...]
    row = lax.broadcasted_iota(jnp.int32, (n_cmp, ROWS), 0)
    sc = jnp.where(row < i0 + BAND_ROWS, sc_ref[...], NEG)
    m = _col_reduce(jnp.maximum, sc)
    p = jnp.exp2(sc - m)
    l = _col_reduce(jnp.add, p)
    pn = p * jnp.where(m > 0.5 * NEG, 1.0 / l, 0.0)
    oct_ = jnp.dot(vct_ref[...], pn.astype(jnp.bfloat16), preferred_element_type=jnp.float32)
    psum = pn[:, 0:Q_TILE]
    for h in range(1, NSA_HPG):
        psum = psum + pn[:, h * Q_TILE:(h + 1) * Q_TILE]
    imp = jnp.dot(ovt_ref[...], psum.astype(jnp.bfloat16), preferred_element_type=jnp.float32)

    ji = lax.broadcasted_iota(jnp.int32, (n_slc, Q_TILE), 0)
    jf = ji.astype(jnp.float32)
    t = qb * Q_TILE + lax.broadcasted_iota(jnp.int32, (n_slc, Q_TILE), 1)
    cur = t // SLC_BLOCK
    forced = (ji == 0) | (ji == cur) | (ji == cur - 1)
    score = jnp.where(ji <= cur, jnp.where(forced, FORCED_SCORE, imp), -1.0)
    pen_t = jnp.full((n_slc, Q_TILE), NEG, jnp.float32)
    for _ in range(min(SLC_TOPK, n_slc)):
        mx = _col_reduce(jnp.maximum, score)
        idx = _col_reduce(jnp.minimum, jnp.where(score == mx, jf, float(n_slc)))
        pick = jf == idx
        pen_t = jnp.where(pick, 0.0, pen_t)
        score = jnp.where(pick, -2.0, score)
    pad = pen_ref.shape[0] - n_slc
    if pad:
        pen_t = jnp.concatenate([pen_t, jnp.full((pad, Q_TILE), NEG, jnp.float32)], axis=0)
    pen_ref[...] = pen_t.astype(pen_ref.dtype)

    t0 = qb * Q_TILE
    ws = pl.multiple_of(jnp.maximum(t0 - WINDOW, 0), Q_TILE)
    ns = pl.multiple_of(jnp.maximum(t0 - Q_TILE, 0), Q_TILE)
    sw_ref[...] = jnp.dot(kw_ref[pl.ds(ws, WIN_KEYS), :], q4t, preferred_element_type=jnp.float32)
    off = pl.multiple_of(ns - ws, Q_TILE)
    sw_ref[pl.ds(off, NEAR_KEYS), :] = sw_ref[pl.ds(off, NEAR_KEYS), :] + near_ref[...]
    key = ws + lax.broadcasted_iota(jnp.int32, (WIN_KEYS, ROWS), 0)
    qry = t0 + (lax.broadcasted_iota(jnp.int32, (WIN_KEYS, ROWS), 1) & (Q_TILE - 1))
    sw = jnp.where((key <= qry) & (key > qry - WINDOW), sw_ref[...], NEG)
    mw = _col_reduce(jnp.maximum, sw)
    ow_sum = jnp.dot(_values_t(vwt_ref, ws, WIN_KEYS), _probabilities(sw, mw),
                     preferred_element_type=jnp.float32)
    owt = ow_sum[0:HEAD_DIM, :] / ow_sum[SUM_ROW:SUM_ROW + 1, :]

    gates_t = jax.nn.sigmoid(gt_ref[...])
    for h in range(NSA_HPG):
        cols = slice(h * Q_TILE, (h + 1) * Q_TILE)
        mixed = _branch_gate(gates_t, h, 0) * oct_[:, cols] + _branch_gate(gates_t, h, 2) * owt[:, cols]
        ycw_ref[:, h * HEAD_DIM:(h + 1) * HEAD_DIM] = mixed.T


def _cmp_win(qt, kc, vct, band, ovt, qkv, vwt, near, gates_t):
    b, s, _ = qkv.shape
    n_cmp = kc.shape[2]
    n_slc = ovt.shape[0]
    nq = s // Q_TILE
    pen_w = -(-n_slc // LANES) * LANES
    n_band = band.shape[0] - 1
    return pl.pallas_call(
        _cmp_win_kernel,
        grid=(b, NSA_GROUPS, nq),
        in_specs=[
            pl.BlockSpec((None, NSA_HPG, HEAD_DIM, Q_TILE), lambda i, g, q: (i, g, 0, q)),
            pl.BlockSpec((None, None, n_cmp, HEAD_DIM), lambda i, g, q: (i, g, 0, 0)),
            pl.BlockSpec((None, None, HEAD_DIM, n_cmp), lambda i, g, q: (i, g, 0, 0)),
            pl.BlockSpec((None, None, BAND_ROWS, ROWS), lambda i, g, q: (jnp.minimum(q, n_band), g, 0, 0)),
            pl.BlockSpec((n_slc, n_cmp), lambda i, g, q: (0, 0)),
            pl.BlockSpec((None, s, HEAD_DIM), lambda i, g, q: (i, 0, CB_KW + g)),
            pl.BlockSpec((None, None, s // LANES, ACC_ROWS, LANES), lambda i, g, q: (i, g, 0, 0, 0)),
            pl.BlockSpec((None, None, NEAR_KEYS, ROWS), lambda i, g, q: (jnp.minimum(q, 1), g, 0, 0)),
            pl.BlockSpec((None, None, GATE_ROWS, Q_TILE), lambda i, g, q: (i, g, 0, q)),
        ],
        out_specs=[
            pl.BlockSpec((None, Q_TILE, NSA_HPG * HEAD_DIM), lambda i, g, q: (i, q, g)),
            pl.BlockSpec((None, None, pen_w, Q_TILE), lambda i, g, q: (i, g, 0, q)),
        ],
        out_shape=[
            jax.ShapeDtypeStruct((b, s, NSA_Q_DIM), jnp.float32),
            jax.ShapeDtypeStruct((b, NSA_GROUPS, pen_w, s), jnp.bfloat16),
        ],
        scratch_shapes=[pltpu.VMEM((n_cmp, ROWS), jnp.float32), pltpu.VMEM((WIN_KEYS, ROWS), jnp.float32)],
        compiler_params=_params(("parallel", "parallel", "arbitrary")),
        name="nsa_cmp_topk_win",
    )(qt, kc, vct, band, ovt, qkv, vwt, near, gates_t)


def _range_penalty(pen, first_blk, lo_ok, hi_ok):
    n_half = pen.shape[0] // LANES
    c = lax.broadcasted_iota(jnp.int32, (LANES, pen.shape[1]), 0)
    out = pen[0:LANES, :]
    blk = c
    for hf in range(1, n_half):
        in_lower = (c + (hf - 1) * LANES >= first_blk) & (first_blk < hf * LANES)
        out = jnp.where(in_lower, out, pen[hf * LANES:(hf + 1) * LANES, :])
        blk = jnp.where(in_lower, blk, c + hf * LANES)
    return jnp.where((blk >= lo_ok) & (blk < hi_ok), out, jnp.asarray(NEG, out.dtype))


def _sel_kernel(qt_ref, pen_ref, kaug_ref, vtc_ref, vtp_ref, near_ref, ycw_ref, gt_ref, y_ref,
                qaug_ref, m_ref, acc_ref, s_ref, mx_ref, p_ref):
    qb = pl.program_id(2)
    n_half = pen_ref.shape[0] // LANES
    n_blocks = pen_ref.shape[0]
    chunks_per_half = LANES * SLC_BLOCK // KEY_CHUNK
    blocks_per_chunk = KEY_CHUNK // SLC_BLOCK

    t0 = qb * Q_TILE
    near_end = t0 - Q_TILE
    nf = jnp.maximum(near_end // KEY_CHUNK, 0)
    ms = pl.multiple_of(jnp.maximum(near_end - KEY_CHUNK, 0), Q_TILE)
    ns = pl.multiple_of(jnp.maximum(near_end, 0), Q_TILE)

    q4t = _group_queries_t(qt_ref)
    pen = pen_ref[...]
    operands = [pen[hf * LANES:(hf + 1) * LANES, :] for hf in range(n_half)]
    operands.append(_range_penalty(pen, ms // SLC_BLOCK, nf * blocks_per_chunk, near_end // SLC_BLOCK))
    operands.append(_range_penalty(pen, ns // SLC_BLOCK, 0, n_blocks))
    for idx, channels in enumerate(operands):
        qaug_ref[idx, 0:HEAD_DIM, :] = q4t
        qaug_ref[idx, HEAD_DIM:, :] = jnp.concatenate([channels] * NSA_HPG, axis=1)

    _flash_init(m_ref, acc_ref)

    def get_q_for(c):
        half = c // chunks_per_half
        return lambda lo, hi: qaug_ref[half, :, lo:hi]

    _flash_far_chunks(nf, lambda c: kaug_ref[pl.ds(pl.multiple_of(c * KEY_CHUNK, KEY_CHUNK), KEY_CHUNK), :],
                      lambda c: vtc_ref[c], get_q_for, ROWS, m_ref, acc_ref, s_ref, mx_ref, p_ref)

    _flash_step(kaug_ref[pl.ds(ms, KEY_CHUNK), :], _values_t(vtp_ref, ms, KEY_CHUNK),
                lambda lo, hi: qaug_ref[n_half, :, lo:hi], ROWS, m_ref, acc_ref)
    _flash_step(kaug_ref[pl.ds(ns, NEAR_KEYS), :], _values_t(vtp_ref, ns, NEAR_KEYS),
                lambda lo, hi: qaug_ref[n_half + 1, :, lo:hi], ROWS, m_ref, acc_ref,
                extra=lambda s, lo, hi: s + near_ref[:, lo:hi])

    o = _flash_result(acc_ref)
    gates_t = jax.nn.sigmoid(gt_ref[...])
    for h in range(NSA_HPG):
        sel_h = (_branch_gate(gates_t, h, 1) * o[:, h * Q_TILE:(h + 1) * Q_TILE]).T
        y_ref[:, h * HEAD_DIM:(h + 1) * HEAD_DIM] = (
            ycw_ref[:, h * HEAD_DIM:(h + 1) * HEAD_DIM] + sel_h).astype(y_ref.dtype)


def _sel(qt, pen, kaug, vt_chunks, vt_pieces, near, ycw, gates_t):
    b, _, s, _ = kaug.shape
    nq = s // Q_TILE
    pen_w = pen.shape[2]
    return pl.pallas_call(
        _sel_kernel,
        grid=(b, NSA_GROUPS, nq),
        in_specs=[
            pl.BlockSpec((None, NSA_HPG, HEAD_DIM, Q_TILE), lambda i, g, q: (i, g, 0, q)),
            pl.BlockSpec((None, None, pen_w, Q_TILE), lambda i, g, q: (i, g, 0, q)),
            pl.BlockSpec((None, None, s, 2 * HEAD_DIM), lambda i, g, q: (i, g, 0, 0)),
            pl.BlockSpec((None, None, s // KEY_CHUNK, ACC_ROWS, KEY_CHUNK), lambda i, g, q: (i, g, 0, 0, 0)),
            pl.BlockSpec((None, None, s // LANES, ACC_ROWS, LANES), lambda i, g, q: (i, g, 0, 0, 0)),
            pl.BlockSpec((None, None, NEAR_KEYS, ROWS), lambda i, g, q: (jnp.minimum(q, 1), g, 0, 0)),
            pl.BlockSpec((None, Q_TILE, NSA_HPG * HEAD_DIM), lambda i, g, q: (i, q, g)),
            pl.BlockSpec((None, None, GATE_ROWS, Q_TILE), lambda i, g, q: (i, g, 0, q)),
        ],
        out_specs=pl.BlockSpec((None, Q_TILE, NSA_HPG * HEAD_DIM), lambda i, g, q: (i, q, g)),
        out_shape=jax.ShapeDtypeStruct((b, s, NSA_Q_DIM), jnp.bfloat16),
        scratch_shapes=[
            pltpu.VMEM((pen_w // LANES + 2, 2 * HEAD_DIM, ROWS), jnp.bfloat16),
            pltpu.VMEM((1, ROWS), jnp.float32),
            pltpu.VMEM((ACC_ROWS, ROWS), jnp.float32),
            pltpu.VMEM((2, KEY_CHUNK, ROWS), jnp.float32),
            pltpu.VMEM((2, 1, ROWS), jnp.float32),
            pltpu.VMEM((2, KEY_CHUNK, ROWS), jnp.bfloat16),
        ],
        compiler_params=_params(("parallel", "parallel", "arbitrary")),
        name="nsa_selected",
    )(qt, pen, kaug, vt_chunks, vt_pieces, near, ycw, gates_t)


def _fox_kernel(qt_ref, kaug_ref, vt_ref, o_ref, qaug_ref, m_ref, acc_ref, s_ref, mx_ref, p_ref):
    qi = pl.program_id(2)
    tq = qt_ref.shape[1]
    row = lax.broadcasted_iota(jnp.int32, (HEAD_DIM, tq), 0)
    qaug_ref[0:HEAD_DIM, :] = qt_ref[...]
    qaug_ref[HEAD_DIM:, :] = jnp.where(row < DECAY_TERMS, -1.0, 0.0).astype(qaug_ref.dtype)
    _flash_init(m_ref, acc_ref)

    def get_q(lo, hi):
        return qaug_ref[:, lo:hi]

    def load_keys(c):
        return kaug_ref[pl.ds(pl.multiple_of(c * tq, tq), tq), :]

    _flash_far_chunks(qi, load_keys, lambda c: vt_ref[c], lambda c: get_q, tq, m_ref, acc_ref,
                      s_ref, mx_ref, p_ref)

    def causal(s, lo, hi):
        key = lax.broadcasted_iota(jnp.int32, s.shape, 0)
        qry = lo + lax.broadcasted_iota(jnp.int32, s.shape, 1)
        return jnp.where(key <= qry, s, NEG)

    _flash_step(load_keys(qi), vt_ref[qi], get_q, tq, m_ref, acc_ref, extra=causal)
    o_ref[...] = _flash_result(acc_ref).T.astype(o_ref.dtype)


def _fox(qt, kaug, vt, tq):
    b, _, s, _ = kaug.shape
    return pl.pallas_call(
        _fox_kernel,
        grid=(b, FOX_HEADS, s // tq),
        in_specs=[
            pl.BlockSpec((None, None, HEAD_DIM, tq), lambda i, h, q: (i, NSA_HEADS + h, 0, q)),
            pl.BlockSpec((None, None, s, 2 * HEAD_DIM), lambda i, h, q: (i, h, 0, 0)),
            pl.BlockSpec((None, None, s // tq, ACC_ROWS, tq), lambda i, h, q: (i, h, 0, 0, 0)),
        ],
        out_specs=pl.BlockSpec((None, tq, HEAD_DIM), lambda i, h, q: (i, q, h)),
        out_shape=jax.ShapeDtypeStruct((b, s, FOX_DIM), jnp.bfloat16),
        scratch_shapes=[
            pltpu.VMEM((2 * HEAD_DIM, tq), jnp.bfloat16),
            pltpu.VMEM((1, tq), jnp.float32),
            pltpu.VMEM((ACC_ROWS, tq), jnp.float32),
            pltpu.VMEM((2, tq, tq), jnp.float32),
            pltpu.VMEM((2, 1, tq), jnp.float32),
            pltpu.VMEM((2, tq, tq), jnp.bfloat16),
        ],
        compiler_params=_params(("parallel", "parallel", "arbitrary")),
        name="fox_attention",
    )(qt, kaug, vt)


def _merge_kernel(x_ref, g_ref, wa_ref, wb_ref, pa_ref, pb_ref, wo_ref, ya_ref, yb_ref, o_ref,
                  h_ref, acc_ref):
    j = pl.program_id(1)

    @pl.when(j == 0)
    def _():
        h_ref[...] = _norm_rows(x_ref[...], g_ref[...]).astype(h_ref.dtype)
        acc_ref[...] = jnp.zeros_like(acc_ref)

    h = h_ref[...]
    ga = jax.nn.sigmoid(jnp.dot(h, wa_ref[...], preferred_element_type=jnp.float32))
    gb = jax.nn.sigmoid(jnp.dot(h, wb_ref[...], preferred_element_type=jnp.float32))
    a = jnp.dot(ya_ref[...], pa_ref[...], preferred_element_type=jnp.float32)
    bb = jnp.dot(yb_ref[...], pb_ref[...], preferred_element_type=jnp.float32)
    merged = (ga * a + gb * bb).astype(jnp.bfloat16)
    acc_ref[...] += jnp.dot(merged, wo_ref[...], preferred_element_type=jnp.float32)

    @pl.when(j == pl.num_programs(1) - 1)
    def _():
        o_ref[...] = x_ref[...] + acc_ref[...]


def _merge(x2d, g, w_ma, w_mb, p_a, p_b, w_out, y_a, y_b, tm, tn):
    t, d = x2d.shape
    ka = p_a.shape[0]
    kb = p_b.shape[0]
    return pl.pallas_call(
        _merge_kernel,
        grid=(t // tm, d // tn),
        in_specs=[
            pl.BlockSpec((tm, d), lambda i, j: (i, 0)),
            pl.BlockSpec((1, d), lambda i, j: (0, 0)),
            pl.BlockSpec((d, tn), lambda i, j: (0, j)),
            pl.BlockSpec((d, tn), lambda i, j: (0, j)),
            pl.BlockSpec((ka, tn), lambda i, j: (0, j)),
            pl.BlockSpec((kb, tn), lambda i, j: (0, j)),
            pl.BlockSpec((tn, d), lambda i, j: (j, 0)),
            pl.BlockSpec((tm, ka), lambda i, j: (i, 0)),
            pl.BlockSpec((tm, kb), lambda i, j: (i, 0)),
        ],
        out_specs=pl.BlockSpec((tm, d), lambda i, j: (i, 0)),
        out_shape=jax.ShapeDtypeStruct((t, d), jnp.float32),
        scratch_shapes=[pltpu.VMEM((tm, d), jnp.bfloat16), pltpu.VMEM((tm, d), jnp.float32)],
        compiler_params=_params(("parallel", "arbitrary")),
        name="merge_out_proj",
    )(x2d, g.reshape(1, d), w_ma, w_mb, p_a, p_b, w_out, y_a, y_b)


def _ffn_kernel(x_ref, xh_ref, g_ref, wu_ref, wv_ref, cw_ref, cb_ref, wd_ref, gf_ref, o_ref,
                h_ref, u_ref, acc_ref, *, seq, tm):
    i = pl.program_id(0)
    j = pl.program_id(1)
    halo = BF16_SUBLANES

    @pl.when(j == 0)
    def _():
        g = g_ref[...]
        keep = jnp.where((i * tm) % seq == 0, 0.0, 1.0)
        h_ref[0:halo, :] = (_norm_rows(xh_ref[...], g) * keep).astype(h_ref.dtype)
        h_ref[halo:, :] = _norm_rows(x_ref[...], g).astype(h_ref.dtype)
        acc_ref[...] = jnp.zeros_like(acc_ref)

    u_ref[...] = jnp.dot(h_ref[...], wu_ref[...], preferred_element_type=jnp.float32)
    v = jnp.dot(h_ref[halo:, :], wv_ref[...], preferred_element_type=jnp.float32)
    conv = cb_ref[...]
    for k in range(CONV_WIDTH):
        conv = conv + cw_ref[k:k + 1, :] * u_ref[pl.ds(halo - (CONV_WIDTH - 1) + k, tm), :]
    act = (jax.nn.gelu(conv) * v).astype(jnp.bfloat16)
    acc_ref[...] += jnp.dot(act, wd_ref[...], preferred_element_type=jnp.float32)

    @pl.when(j == pl.num_programs(1) - 1)
    def _():
        o_ref[...] = _norm_rows(x_ref[...] + acc_ref[...], gf_ref[...])


def _ffn(x2d, g, w_up, conv_w, conv_b, w_down, g_final, seq, tm, tn):
    t, d = x2d.shape
    d_ff = w_down.shape[0]
    nt = d_ff // tn
    halo = BF16_SUBLANES
    per = tm // halo
    return pl.pallas_call(
        functools.partial(_ffn_kernel, seq=seq, tm=tm),
        grid=(t // tm, nt),
        in_specs=[
            pl.BlockSpec((tm, d), lambda i, j: (i, 0)),
            pl.BlockSpec((halo, d), lambda i, j: (jnp.maximum(i * per - 1, 0), 0)),
            pl.BlockSpec((1, d), lambda i, j: (0, 0)),
            pl.BlockSpec((d, tn), lambda i, j: (0, j)),
            pl.BlockSpec((d, tn), lambda i, j: (0, nt + j)),
            pl.BlockSpec((CONV_WIDTH, tn), lambda i, j: (0, j)),
            pl.BlockSpec((1, tn), lambda i, j: (0, j)),
            pl.BlockSpec((tn, d), lambda i, j: (j, 0)),
            pl.BlockSpec((1, d), lambda i, j: (0, 0)),
        ],
        out_specs=pl.BlockSpec((tm, d), lambda i, j: (i, 0)),
        out_shape=jax.ShapeDtypeStruct((t, d), jnp.float32),
        scratch_shapes=[
            pltpu.VMEM((tm + halo, d), jnp.bfloat16),
            pltpu.VMEM((tm + halo, tn), jnp.float32),
            pltpu.VMEM((tm, d), jnp.float32),
        ],
        compiler_params=_params(("parallel", "arbitrary")),
        name="ffn_final_norm",
    )(x2d, x2d, g.reshape(1, d), w_up, w_up, conv_w, conv_b.reshape(1, d_ff), w_down, g_final.reshape(1, d))


def _t5_bucket_np(dist):
    n = np.maximum(dist, 0)
    max_exact = REL_BUCKETS // 2
    nf = np.maximum(n, 1).astype(np.float32)
    large = max_exact + (np.log(nf / np.float32(max_exact)) / np.float32(math.log(REL_MAX_DIST / max_exact))
                         * np.float32(REL_BUCKETS - max_exact)).astype(np.int32)
    return np.where(n < max_exact, n, np.minimum(large, REL_BUCKETS - 1)).astype(np.int32)


def _bias_by_distance(rel_table, far_shift):
    vals = rel_table[jnp.asarray(_t5_bucket_np(np.arange(REL_MAX_DIST + 1)))]
    if far_shift:
        vals = vals - rel_table[REL_BUCKETS - 1]
    return vals * LOG2E


def _near_bias(rel_table):
    fd = _bias_by_distance(rel_table, True)
    h = fd.shape[1]
    lo = NEAR_KEYS - 1
    vec = jnp.concatenate([jnp.full((lo, h), NEG, fd.dtype), fd[:REL_MAX_DIST],
                           jnp.zeros((NEAR_KEYS - REL_MAX_DIST, h), fd.dtype)], axis=0)
    tiles = jnp.stack([jnp.stack([vec[lo + a - r:lo + a - r + Q_TILE] for r in range(NEAR_KEYS)])
                       for a in (0, Q_TILE)])
    tiles = tiles.reshape(2, NEAR_KEYS, Q_TILE, NSA_GROUPS, NSA_HPG).transpose(0, 3, 1, 4, 2)
    return tiles.reshape(2, NSA_GROUPS, NEAR_KEYS, ROWS)


def _band_bias(rel_table):
    fd = _bias_by_distance(rel_table, True)
    tl = np.arange(Q_TILE)[None, :]
    r = np.arange(BAND_ROWS)[:, None]
    dist = np.stack([tl - CMP_STRIDE * (r - off) - (CMP_BLOCK - 1) for off in (0, 8, 16)])
    vals = fd[jnp.asarray(np.clip(dist, 0, REL_MAX_DIST))]
    vals = jnp.where(jnp.asarray(dist >= 0)[..., None], vals, NEG)
    v, rr, q, _ = vals.shape
    return vals.transpose(0, 1, 3, 2).reshape(v, rr, NSA_GROUPS, NSA_HPG * q).transpose(0, 2, 1, 3)


def _overlap_t(n_cmp_pad, n_slc):
    i = np.arange(n_cmp_pad)[None, :]
    jj = np.arange(n_slc)[:, None]
    c_start = i * CMP_STRIDE
    ov = (c_start < jj * SLC_BLOCK + SLC_BLOCK) & (c_start + CMP_BLOCK - 1 >= jj * SLC_BLOCK)
    ov = ov & (i < n_cmp_pad - 1)
    return jnp.asarray(ov.astype(np.float32), jnp.bfloat16)


def _block_onehot(seq):
    blk = (np.arange(seq) // SLC_BLOCK) % LANES
    return jnp.asarray((blk[:, None] == np.arange(LANES)[None, :]).astype(np.float32), jnp.bfloat16)


def _pick_tile(n, pref):
    return pref if n % pref == 0 else n


def kernel(x, attn_norm_g, w_in, cmp_pos_k, cmp_w1_k, cmp_w2_k, cmp_pos_v, cmp_w1_v, cmp_w2_v,
           rel_bias_table, fox_forget_bias, w_branch_nsa, w_branch_fox, w_out,
           ffn_norm_g, w_up, conv_w, conv_b, w_down, final_norm_g):
    assert w_in.shape[0] == 1, "the final norm is fused into the single layer's FFN kernel"
    bsz, seq, d = x.shape
    t = bsz * seq
    bf = jnp.bfloat16
    scale = HEAD_DIM ** -0.5 * LOG2E
    x2d = x.reshape(t, d)
    w_in = w_in[0]

    o = np.cumsum([0, NSA_Q_DIM] + [NSA_KV_DIM] * 6 + [3 * NSA_HEADS, FOX_DIM, FOX_DIM, FOX_DIM, FOX_HEADS, d, d])
    w_qkv = jnp.concatenate([w_in[:, o[0]:o[1]] * scale, w_in[:, o[1]:o[7]],
                             w_in[:, o[8]:o[9]] * scale, w_in[:, o[9]:o[11]]], axis=1).astype(bf)
    gate_cols = []
    for grp in range(NSA_GROUPS):
        gate_cols += [w_in[:, o[7] + grp * GATES_PER_GROUP:o[7] + (grp + 1) * GATES_PER_GROUP],
                      jnp.zeros((d, LANES - GATES_PER_GROUP), w_in.dtype)]
    gate_cols += [w_in[:, o[11]:o[12]], jnp.zeros((d, LANES - FOX_HEADS), w_in.dtype)]
    w_gate = jnp.concatenate(gate_cols, axis=1).astype(bf)
    w_ma = w_in[:, o[12]:o[13]].astype(bf)
    w_mb = w_in[:, o[13]:o[14]].astype(bf)

    tm = _pick_tile(t, 1024)
    qkv = _norm_matmul(x2d, attn_norm_g[0], w_qkv, bf, tm, 512).reshape(bsz, seq, N_QKV)
    gates = _norm_matmul(x2d, attn_norm_g[0], w_gate, jnp.float32, tm, LANES).reshape(bsz, seq, N_GATE)

    q_all = jnp.concatenate([qkv[:, :, :NSA_Q_DIM], qkv[:, :, CB_QF * HEAD_DIM:(CB_QF + FOX_HEADS) * HEAD_DIM]],
                            axis=-1)
    qt = q_all.reshape(bsz, seq, NSA_HEADS + FOX_HEADS, HEAD_DIM).transpose(0, 2, 3, 1)

    def heads_major(cb, heads):
        return qkv[:, :, cb * HEAD_DIM:(cb + heads) * HEAD_DIM].reshape(
            bsz, seq, heads, HEAD_DIM).transpose(0, 2, 1, 3)

    def values_t(cb, heads, piece):
        v = qkv[:, :, cb * HEAD_DIM:(cb + heads) * HEAD_DIM]
        v = v.reshape(bsz, seq // piece, piece, heads, HEAD_DIM).transpose(0, 3, 1, 4, 2)
        ones_row = (jnp.arange(ACC_ROWS - HEAD_DIM) == 0).astype(bf)[:, None]
        extra = jnp.broadcast_to(ones_row, v.shape[:3] + (ACC_ROWS - HEAD_DIM, piece))
        return jnp.concatenate([v, extra], axis=3)

    n_ch = seq // CMP_STRIDE

    def chunked(cb):
        sl = qkv[:, :, cb * HEAD_DIM:(cb + NSA_GROUPS) * HEAD_DIM]
        sl = sl.reshape(bsz, n_ch, CMP_STRIDE, NSA_GROUPS, HEAD_DIM).transpose(0, 3, 1, 2, 4)
        return sl.reshape(bsz, NSA_GROUPS, n_ch, CMP_STRIDE * HEAD_DIM)

    def posflat(pos):
        return jnp.broadcast_to(pos.reshape(1, CMP_BLOCK * HEAD_DIM), (8, CMP_BLOCK * HEAD_DIM)).astype(bf)

    kc = _compress(chunked(CB_KC), cmp_w1_k[0].astype(bf), posflat(cmp_pos_k[0]), cmp_w2_k[0].astype(bf))
    vc = _compress(chunked(CB_VC), cmp_w1_v[0].astype(bf), posflat(cmp_pos_v[0]), cmp_w2_v[0].astype(bf))
    vct = vc.transpose(0, 1, 3, 2)

    f_t = gates[:, :, NSA_GROUPS * LANES:NSA_GROUPS * LANES + FOX_HEADS]
    f_t = f_t.transpose(0, 2, 1).reshape(bsz * FOX_HEADS, seq)
    bias_col = jnp.tile(fox_forget_bias[0].astype(jnp.float32), bsz).reshape(bsz * FOX_HEADS, 1)
    terms = _decay_cumsum(f_t, bias_col, _pick_tile(seq, 2048))
    dec = jnp.pad(terms.transpose(1, 2, 0), ((0, 0), (0, 0), (0, LANES - DECAY_TERMS)))
    dec = dec.reshape(bsz, FOX_HEADS, seq, LANES)

    n_slc = seq // SLC_BLOCK
    near = _near_bias(rel_bias_table)
    gates_t = gates[:, :, :NSA_GROUPS * LANES].reshape(bsz, seq, NSA_GROUPS, LANES)[..., :GATE_ROWS]
    gates_t = gates_t.transpose(0, 2, 3, 1)
    ycw, pen = _cmp_win(qt, kc, vct, _band_bias(rel_bias_table), _overlap_t(n_ch, n_slc), qkv,
                        values_t(CB_VW, NSA_GROUPS, LANES), near, gates_t)
    ks = heads_major(CB_KS, NSA_GROUPS)
    kaug_sel = jnp.concatenate([ks, jnp.broadcast_to(_block_onehot(seq), ks.shape)], axis=-1)
    y_nsa = _sel(qt, pen, kaug_sel, values_t(CB_VS, NSA_GROUPS, _pick_tile(seq, KEY_CHUNK)),
                 values_t(CB_VS, NSA_GROUPS, LANES), near, ycw, gates_t)

    fox_tq = _pick_tile(seq, FOX_Q_TILE)
    kaug_fox = jnp.concatenate([heads_major(CB_KF, FOX_HEADS), dec], axis=-1)
    y_fox = _fox(qt, kaug_fox, values_t(CB_VF, FOX_HEADS, fox_tq), fox_tq)

    tm2 = _pick_tile(t, 512)
    x_mid = _merge(x2d, attn_norm_g[0], w_ma, w_mb, w_branch_nsa[0].astype(bf), w_branch_fox[0].astype(bf),
                   w_out[0].astype(bf), y_nsa.reshape(t, NSA_Q_DIM), y_fox.reshape(t, FOX_DIM),
                   tm2, _pick_tile(d, 512))
    d_ff = w_down.shape[1]
    out = _ffn(x_mid, ffn_norm_g[0], w_up[0].astype(bf), conv_w[0], conv_b[0], w_down[0].astype(bf),
               final_norm_g, seq, tm2, _pick_tile(d_ff, 512))
    return out.reshape(bsz, seq, d)
```

```python
import functools
import math

import jax
import jax.numpy as jnp
import numpy as np
from jax import lax
from jax.experimental import pallas as pl
from jax.experimental.pallas import tpu as pltpu

HEAD_DIM = 128
NSA_HEADS = 8
NSA_GROUPS = 2
NSA_HPG = NSA_HEADS // NSA_GROUPS
FOX_HEADS = 8
CMP_BLOCK = 32
CMP_STRIDE = 16
SLC_BLOCK = 64
SLC_TOPK = 16
WINDOW = 512
REL_BUCKETS = 32
REL_MAX_DIST = 128
CONV_WIDTH = 3
EPS = 1e-6
NEG = -1e30
FORCED_SCORE = 1e4
LOG2E = math.log2(math.e)

LANES = 128
BF16_SUBLANES = 16
VMEM_LIMIT = 56 * 1024 * 1024

NSA_Q_DIM = NSA_HEADS * HEAD_DIM
NSA_KV_DIM = NSA_GROUPS * HEAD_DIM
FOX_DIM = FOX_HEADS * HEAD_DIM
N_QKV = NSA_Q_DIM + 6 * NSA_KV_DIM + 3 * FOX_DIM
GATES_PER_GROUP = 3 * NSA_HPG
N_GATE = (NSA_GROUPS + 1) * LANES

CB_KC = 8
CB_VC = 10
CB_KS = 12
CB_VS = 14
CB_KW = 16
CB_VW = 18
CB_QF = 20
CB_KF = 28
CB_VF = 36

Q_TILE = 128
ROWS = NSA_HPG * Q_TILE
KEY_CHUNK = 512
FOX_Q_TILE = 512
N_SPLIT = 2
ROW_BLOCK = 64
DECAY_TERMS = 3
BAND_ROWS = 24
CMP_TIERS = 4
WIN_KEYS = WINDOW + Q_TILE
NEAR_KEYS = 2 * Q_TILE
GATE_ROWS = 16
SUM_ROW = HEAD_DIM
ACC_ROWS = HEAD_DIM + 8


def _params(sem):
    return pltpu.CompilerParams(dimension_semantics=sem, vmem_limit_bytes=VMEM_LIMIT)


def _norm_rows(x, g):
    return (x * lax.rsqrt(jnp.mean(x * x, axis=-1, keepdims=True) + EPS)) * g


def _norm_matmul_kernel(x_ref, g_ref, w_ref, o_ref, h_ref):
    @pl.when(pl.program_id(1) == 0)
    def _():
        h_ref[...] = _norm_rows(x_ref[...], g_ref[...]).astype(h_ref.dtype)

    o_ref[...] = jnp.dot(h_ref[...], w_ref[...], preferred_element_type=jnp.float32).astype(o_ref.dtype)


def _norm_matmul(x2d, g, w, out_dtype, tm, tn):
    t, d = x2d.shape
    n = w.shape[1]
    return pl.pallas_call(
        _norm_matmul_kernel,
        grid=(t // tm, n // tn),
        in_specs=[
            pl.BlockSpec((tm, d), lambda i, j: (i, 0)),
            pl.BlockSpec((1, d), lambda i, j: (0, 0)),
            pl.BlockSpec((d, tn), lambda i, j: (0, j)),
        ],
        out_specs=pl.BlockSpec((tm, tn), lambda i, j: (i, j)),
        out_shape=jax.ShapeDtypeStruct((t, n), out_dtype),
        scratch_shapes=[pltpu.VMEM((tm, d), jnp.bfloat16)],
        compiler_params=_params(("parallel", "arbitrary")),
        name="norm_in_proj",
    )(x2d, g.reshape(1, d), w)


def _compress_kernel(ch_ref, w1_ref, posf_ref, w2_ref, o_ref):
    half = ch_ref.shape[1]
    ch = ch_ref[...]
    pa = jnp.dot(ch, w1_ref[:half, :], preferred_element_type=jnp.float32)
    pb = jnp.dot(ch, w1_ref[half:, :], preferred_element_type=jnp.float32)
    pos = jnp.dot(posf_ref[...], w1_ref[...], preferred_element_type=jnp.float32)[0:1, :]
    n = pa.shape[0]
    pre = pa + pltpu.roll(pb, n - 1, 0) + pos
    act = jax.nn.gelu(pre)
    o_ref[...] = jnp.dot(act.astype(jnp.bfloat16), w2_ref[...],
                         preferred_element_type=jnp.float32).astype(o_ref.dtype)


def _compress(chunks, w1, posf, w2):
    b, g, n, k = chunks.shape
    return pl.pallas_call(
        _compress_kernel,
        grid=(b, g),
        in_specs=[
            pl.BlockSpec((None, None, n, k), lambda i, j: (i, j, 0, 0)),
            pl.BlockSpec(w1.shape, lambda i, j: (0, 0)),
            pl.BlockSpec(posf.shape, lambda i, j: (0, 0)),
            pl.BlockSpec(w2.shape, lambda i, j: (0, 0)),
        ],
        out_specs=pl.BlockSpec((None, None, n, HEAD_DIM), lambda i, j: (i, j, 0, 0)),
        out_shape=jax.ShapeDtypeStruct((b, g, n, HEAD_DIM), jnp.bfloat16),
        compiler_params=_params(("parallel", "parallel")),
        name="compress_tokens",
    )(chunks, w1, posf, w2)


def _decay_kernel(f_ref, b_ref, tri_ref, o_ref, carry_ref):
    @pl.when(pl.program_id(0) == 0)
    def _():
        carry_ref[...] = jnp.zeros_like(carry_ref)

    x = f_ref[...] + b_ref[...]
    logf = (jnp.minimum(x, 0.0) - jnp.log1p(jnp.exp(-jnp.abs(x)))) * LOG2E
    carry = carry_ref[...]
    for seg in range(f_ref.shape[1] // LANES):
        part = jnp.dot(logf[:, seg * LANES:(seg + 1) * LANES], tri_ref[...],
                       preferred_element_type=jnp.float32, precision=lax.Precision.HIGHEST) + carry
        carry = part[:, LANES - 1:LANES]
        rest = part
        for term in range(DECAY_TERMS):
            piece = rest.astype(o_ref.dtype)
            o_ref[term, :, seg * LANES:(seg + 1) * LANES] = piece
            rest = rest - piece.astype(jnp.float32)
    carry_ref[...] = carry


def _decay_cumsum(f_t, bias_col, width):
    rows, s = f_t.shape
    tri = jnp.asarray(np.triu(np.ones((LANES, LANES), np.float32)))
    return pl.pallas_call(
        _decay_kernel,
        grid=(s // width,),
        in_specs=[
            pl.BlockSpec((rows, width), lambda i: (0, i)),
            pl.BlockSpec((rows, 1), lambda i: (0, 0)),
            pl.BlockSpec((LANES, LANES), lambda i: (0, 0)),
        ],
        out_specs=pl.BlockSpec((DECAY_TERMS, rows, width), lambda i: (0, 0, i)),
        out_shape=jax.ShapeDtypeStruct((DECAY_TERMS, rows, s), jnp.bfloat16),
        scratch_shapes=[pltpu.VMEM((rows, 1), jnp.float32)],
        compiler_params=_params(("arbitrary",)),
        name="decay_cumsum",
    )(f_t, bias_col, tri)


def _col_reduce(op, x):
    reduce = {jnp.maximum: jnp.max, jnp.minimum: jnp.min, jnp.add: jnp.sum}[op]
    return reduce(x, axis=0, keepdims=True)


def _flash_init(m_ref, acc_ref):
    m_ref[...] = jnp.full(m_ref.shape, NEG, jnp.float32)
    acc_ref[...] = jnp.zeros(acc_ref.shape, jnp.float32)


def _probabilities(s, m_new):
    return jnp.exp2((s - m_new).astype(jnp.bfloat16))


def _flash_result(acc_ref):
    return acc_ref[0:HEAD_DIM, :] / acc_ref[SUM_ROW:SUM_ROW + 1, :]


def _flash_step(kaug, vt, get_q, width, m_ref, acc_ref, extra=None):
    w = width // N_SPLIT
    strips = [(i * w, (i + 1) * w) for i in range(N_SPLIT)]
    scores = [jnp.dot(kaug, get_q(lo, hi), preferred_element_type=jnp.float32) for lo, hi in strips]
    for (lo, hi), s in zip(strips, scores):
        if extra is not None:
            s = extra(s, lo, hi)
        m_prev = m_ref[:, lo:hi]
        m_new = jnp.maximum(m_prev, _col_reduce(jnp.maximum, s))
        alpha = jnp.exp2(m_prev - m_new)
        acc_ref[:, lo:hi] = alpha * acc_ref[:, lo:hi] + jnp.dot(
            vt, _probabilities(s, m_new), preferred_element_type=jnp.float32)
        m_ref[:, lo:hi] = m_new


def _flash_far_chunks(n, load_keys, load_values, get_q_for, width, m_ref, acc_ref, s_ref, mx_ref, p_ref):
    w = width // N_SPLIT
    strips = [(i * w, (i + 1) * w) for i in range(N_SPLIT)]
    base = n % 2
    pairs = n // 2

    def scores_into(c, slot):
        kaug = load_keys(c)
        get_q = get_q_for(c)
        for lo, hi in strips:
            s = jnp.dot(kaug, get_q(lo, hi), preferred_element_type=jnp.float32)
            s_ref[slot, :, lo:hi] = s
            mx_ref[slot, :, lo:hi] = _col_reduce(jnp.maximum, s)

    def value_product(c, slot):
        vt = load_values(c)
        return jnp.concatenate([jnp.dot(vt, p_ref[slot, :, lo:hi], preferred_element_type=jnp.float32)
                                for lo, hi in strips], axis=1)

    @pl.when(base == 1)
    def _():
        _flash_step(load_keys(0), load_values(0), get_q_for(0), width, m_ref, acc_ref)

    @pl.when(pairs > 0)
    def _():
        scores_into(base, 0)
        p_ref[1] = jnp.zeros(p_ref.shape[1:], p_ref.dtype)

        def pair(j, carry):
            for cur in (0, 1):
                c = base + 2 * j + cur
                nxt = 1 - cur
                pv = value_product(jnp.maximum(c - 1, base), nxt)
                scores_into(jnp.minimum(c + 1, n - 1), nxt)
                alphas = []
                for lo, hi in strips:
                    m_prev = m_ref[:, lo:hi]
                    m_new = jnp.maximum(m_prev, mx_ref[cur, :, lo:hi])
                    m_ref[:, lo:hi] = m_new
                    for r in range(0, s_ref.shape[1], ROW_BLOCK):
                        p_ref[cur, r:r + ROW_BLOCK, lo:hi] = _probabilities(
                            s_ref[cur, r:r + ROW_BLOCK, lo:hi], m_new)
                    alphas.append(jnp.exp2(m_prev - m_new))
                acc_ref[...] = jnp.concatenate(alphas, axis=1) * (acc_ref[...] + pv)
            return carry

        lax.fori_loop(0, pairs, pair, 0)
        acc_ref[...] = acc_ref[...] + value_product(n - 1, 1)


def _group_queries_t(qt_ref):
    return jnp.concatenate([qt_ref[h] for h in range(NSA_HPG)], axis=1)


def _branch_gate(gates_t, head, branch):
    row = head * 3 + branch
    return gates_t[row:row + 1, :]


def _values_t(vt_ref, k0, n_keys):
    p0 = k0 // LANES
    return jnp.concatenate([vt_ref[p0 + j] for j in range(n_keys // LANES)], axis=1)


def _compressed_branch(n_rows, i0, q4t, kc_ref, vct_ref, band_ref, ovt_ref, sc_ref, oct_ref, imp_ref):
    sc_ref[0:n_rows, :] = jnp.dot(kc_ref[0:n_rows, :], q4t, preferred_element_type=jnp.float32)
    sc_ref[pl.ds(i0, BAND_ROWS), :] = sc_ref[pl.ds(i0, BAND_ROWS), :] + band_ref[...]
    row = lax.broadcasted_iota(jnp.int32, (n_rows, ROWS), 0)
    sc = jnp.where(row < i0 + BAND_ROWS, sc_ref[0:n_rows, :], NEG)
    m = _col_reduce(jnp.maximum, sc)
    p = jnp.exp2(sc - m)
    l = _col_reduce(jnp.add, p)
    pn = p * jnp.where(m > 0.5 * NEG, 1.0 / l, 0.0)
    oct_ref[...] = jnp.dot(vct_ref[:, 0:n_rows], pn.astype(jnp.bfloat16), preferred_element_type=jnp.float32)
    psum = pn[:, 0:Q_TILE]
    for h in range(1, NSA_HPG):
        psum = psum + pn[:, h * Q_TILE:(h + 1) * Q_TILE]
    imp_ref[...] = jnp.dot(ovt_ref[:, 0:n_rows], psum.astype(jnp.bfloat16), preferred_element_type=jnp.float32)


def _cmp_win_kernel(qt_ref, kc_ref, vct_ref, band_ref, ovt_ref, kw_ref, vwt_ref, near_ref, gt_ref,
                    ycw_ref, pen_ref, sc_ref, sw_ref, oct_ref, imp_ref):
    qb = pl.program_id(2)
    n_cmp = kc_ref.shape[0]
    n_slc = ovt_ref.shape[0]
    q4t = _group_queries_t(qt_ref)

    i0 = pl.multiple_of(jnp.maximum(8 * qb - 16, 0), 8)
    n_tiers = CMP_TIERS if n_cmp % (CMP_TIERS * LANES) == 0 else 1
    step = n_cmp // n_tiers
    for tier in range(1, n_tiers + 1):
        in_tier = (i0 + BAND_ROWS <= tier * step) & (i0 + BAND_ROWS > (tier - 1) * step)
        pl.when(in_tier)(functools.partial(_compressed_branch, tier * step, i0, q4t, kc_ref, vct_ref,
                                           band_ref, ovt_ref, sc_ref, oct_ref, imp_ref))
    oct_ = oct_ref[...]
    imp = imp_ref[...]

    ji = lax.broadcasted_iota(jnp.int32, (n_slc, Q_TILE), 0)
    jf = ji.astype(jnp.float32)
    t = qb * Q_TILE + lax.broadcasted_iota(jnp.int32, (n_slc, Q_TILE), 1)
    cur = t // SLC_BLOCK
    forced = (ji == 0) | (ji == cur) | (ji == cur - 1)
    score = jnp.where(ji <= cur, jnp.where(forced, FORCED_SCORE, imp), -1.0)
    pen_t = jnp.full((n_slc, Q_TILE), NEG, jnp.float32)
    for _ in range(min(SLC_TOPK, n_slc)):
        mx = _col_reduce(jnp.maximum, score)
        idx = _col_reduce(jnp.minimum, jnp.where(score == mx, jf, float(n_slc)))
        pick = jf == idx
        pen_t = jnp.where(pick, 0.0, pen_t)
        score = jnp.where(pick, -2.0, score)
    pad = pen_ref.shape[0] - n_slc
    if pad:
        pen_t = jnp.concatenate([pen_t, jnp.full((pad, Q_TILE), NEG, jnp.float32)], axis=0)
    pen_ref[...] = pen_t.astype(pen_ref.dtype)

    t0 = qb * Q_TILE
    ws = pl.multiple_of(jnp.maximum(t0 - WINDOW, 0), Q_TILE)
    ns = pl.multiple_of(jnp.maximum(t0 - Q_TILE, 0), Q_TILE)
    sw_ref[...] = jnp.dot(kw_ref[pl.ds(ws, WIN_KEYS), :], q4t, preferred_element_type=jnp.float32)
    off = pl.multiple_of(ns - ws, Q_TILE)
    sw_ref[pl.ds(off, NEAR_KEYS), :] = sw_ref[pl.ds(off, NEAR_KEYS), :] + near_ref[...]
    key = ws + lax.broadcasted_iota(jnp.int32, (WIN_KEYS, ROWS), 0)
    qry = t0 + (lax.broadcasted_iota(jnp.int32, (WIN_KEYS, ROWS), 1) & (Q_TILE - 1))
    sw = jnp.where((key <= qry) & (key > qry - WINDOW), sw_ref[...], NEG)
    mw = _col_reduce(jnp.maximum, sw)
    ow_sum = jnp.dot(_values_t(vwt_ref, ws, WIN_KEYS), _probabilities(sw, mw),
                     preferred_element_type=jnp.float32)
    owt = ow_sum[0:HEAD_DIM, :] / ow_sum[SUM_ROW:SUM_ROW + 1, :]

    gates_t = jax.nn.sigmoid(gt_ref[...])
    for h in range(NSA_HPG):
        cols = slice(h * Q_TILE, (h + 1) * Q_TILE)
        mixed = _branch_gate(gates_t, h, 0) * oct_[:, cols] + _branch_gate(gates_t, h, 2) * owt[:, cols]
        ycw_ref[:, h * HEAD_DIM:(h + 1) * HEAD_DIM] = mixed.T


def _cmp_win(qt, kc, vct, band, ovt, qkv, vwt, near, gates_t):
    b, s, _ = qkv.shape
    n_cmp = kc.shape[2]
    n_slc = ovt.shape[0]
    nq = s // Q_TILE
    pen_w = -(-n_slc // LANES) * LANES
    n_band = band.shape[0] - 1
    return pl.pallas_call(
        _cmp_win_kernel,
        grid=(b, NSA_GROUPS, nq),
        in_specs=[
            pl.BlockSpec((None, NSA_HPG, HEAD_DIM, Q_TILE), lambda i, g, q: (i, g, 0, q)),
            pl.BlockSpec((None, None, n_cmp, HEAD_DIM), lambda i, g, q: (i, g, 0, 0)),
            pl.BlockSpec((None, None, HEAD_DIM, n_cmp), lambda i, g, q: (i, g, 0, 0)),
            pl.BlockSpec((None, None, BAND_ROWS, ROWS), lambda i, g, q: (jnp.minimum(q, n_band), g, 0, 0)),
            pl.BlockSpec((n_slc, n_cmp), lambda i, g, q: (0, 0)),
            pl.BlockSpec((None, s, HEAD_DIM), lambda i, g, q: (i, 0, CB_KW + g)),
            pl.BlockSpec((None, None, s // LANES, ACC_ROWS, LANES), lambda i, g, q: (i, g, 0, 0, 0)),
            pl.BlockSpec((None, None, NEAR_KEYS, ROWS), lambda i, g, q: (jnp.minimum(q, 1), g, 0, 0)),
            pl.BlockSpec((None, None, GATE_ROWS, Q_TILE), lambda i, g, q: (i, g, 0, q)),
        ],
        out_specs=[
            pl.BlockSpec((None, Q_TILE, NSA_HPG * HEAD_DIM), lambda i, g, q: (i, q, g)),
            pl.BlockSpec((None, None, pen_w, Q_TILE), lambda i, g, q: (i, g, 0, q)),
        ],
        out_shape=[
            jax.ShapeDtypeStruct((b, s, NSA_Q_DIM), jnp.float32),
            jax.ShapeDtypeStruct((b, NSA_GROUPS, pen_w, s), jnp.bfloat16),
        ],
        scratch_shapes=[pltpu.VMEM((n_cmp, ROWS), jnp.float32), pltpu.VMEM((WIN_KEYS, ROWS), jnp.float32),
                        pltpu.VMEM((HEAD_DIM, ROWS), jnp.float32), pltpu.VMEM((n_slc, Q_TILE), jnp.float32)],
        compiler_params=_params(("parallel", "parallel", "arbitrary")),
        name="nsa_cmp_topk_win",
    )(qt, kc, vct, band, ovt, qkv, vwt, near, gates_t)


def _range_penalty(pen, first_blk, lo_ok, hi_ok):
    n_half = pen.shape[0] // LANES
    c = lax.broadcasted_iota(jnp.int32, (LANES, pen.shape[1]), 0)
    out = pen[0:LANES, :]
    blk = c
    for hf in range(1, n_half):
        in_lower = (c + (hf - 1) * LANES >= first_blk) & (first_blk < hf * LANES)
        out = jnp.where(in_lower, out, pen[hf * LANES:(hf + 1) * LANES, :])
        blk = jnp.where(in_lower, blk, c + hf * LANES)
    return jnp.where((blk >= lo_ok) & (blk < hi_ok), out, jnp.asarray(NEG, out.dtype))


def _sel_kernel(qt_ref, pen_ref, kaug_ref, vtc_ref, vtp_ref, near_ref, ycw_ref, gt_ref, y_ref,
                qaug_ref, m_ref, acc_ref, s_ref, mx_ref, p_ref):
    qb = pl.program_id(2)
    n_half = pen_ref.shape[0] // LANES
    n_blocks = pen_ref.shape[0]
    chunks_per_half = LANES * SLC_BLOCK // KEY_CHUNK
    blocks_per_chunk = KEY_CHUNK // SLC_BLOCK

    t0 = qb * Q_TILE
    near_end = t0 - Q_TILE
    nf = jnp.maximum(near_end // KEY_CHUNK, 0)
    ms = pl.multiple_of(jnp.maximum(near_end - KEY_CHUNK, 0), Q_TILE)
    ns = pl.multiple_of(jnp.maximum(near_end, 0), Q_TILE)

    q4t = _group_queries_t(qt_ref)
    pen = pen_ref[...]
    operands = [pen[hf * LANES:(hf + 1) * LANES, :] for hf in range(n_half)]
    operands.append(_range_penalty(pen, ms // SLC_BLOCK, nf * blocks_per_chunk, near_end // SLC_BLOCK))
    operands.append(_range_penalty(pen, ns // SLC_BLOCK, 0, n_blocks))
    for idx, channels in enumerate(operands):
        qaug_ref[idx, 0:HEAD_DIM, :] = q4t
        qaug_ref[idx, HEAD_DIM:, :] = jnp.concatenate([channels] * NSA_HPG, axis=1)

    _flash_init(m_ref, acc_ref)

    def get_q_for(c):
        half = c // chunks_per_half
        return lambda lo, hi: qaug_ref[half, :, lo:hi]

    _flash_far_chunks(nf, lambda c: kaug_ref[pl.ds(pl.multiple_of(c * KEY_CHUNK, KEY_CHUNK), KEY_CHUNK), :],
                      lambda c: vtc_ref[c], get_q_for, ROWS, m_ref, acc_ref, s_ref, mx_ref, p_ref)

    _flash_step(kaug_ref[pl.ds(ms, KEY_CHUNK), :], _values_t(vtp_ref, ms, KEY_CHUNK),
                lambda lo, hi: qaug_ref[n_half, :, lo:hi], ROWS, m_ref, acc_ref)
    _flash_step(kaug_ref[pl.ds(ns, NEAR_KEYS), :], _values_t(vtp_ref, ns, NEAR_KEYS),
                lambda lo, hi: qaug_ref[n_half + 1, :, lo:hi], ROWS, m_ref, acc_ref,
                extra=lambda s, lo, hi: s + near_ref[:, lo:hi])

    o = _flash_result(acc_ref)
    gates_t = jax.nn.sigmoid(gt_ref[...])
    for h in range(NSA_HPG):
        sel_h = (_branch_gate(gates_t, h, 1) * o[:, h * Q_TILE:(h + 1) * Q_TILE]).T
        y_ref[:, h * HEAD_DIM:(h + 1) * HEAD_DIM] = (
            ycw_ref[:, h * HEAD_DIM:(h + 1) * HEAD_DIM] + sel_h).astype(y_ref.dtype)


def _sel(qt, pen, kaug, vt_chunks, vt_pieces, near, ycw, gates_t):
    b, _, s, _ = kaug.shape
    nq = s // Q_TILE
    pen_w = pen.shape[2]
    return pl.pallas_call(
        _sel_kernel,
        grid=(b, NSA_GROUPS, nq),
        in_specs=[
            pl.BlockSpec((None, NSA_HPG, HEAD_DIM, Q_TILE), lambda i, g, q: (i, g, 0, q)),
            pl.BlockSpec((None, None, pen_w, Q_TILE), lambda i, g, q: (i, g, 0, q)),
            pl.BlockSpec((None, None, s, 2 * HEAD_DIM), lambda i, g, q: (i, g, 0, 0)),
            pl.BlockSpec((None, None, s // KEY_CHUNK, ACC_ROWS, KEY_CHUNK), lambda i, g, q: (i, g, 0, 0, 0)),
            pl.BlockSpec((None, None, s // LANES, ACC_ROWS, LANES), lambda i, g, q: (i, g, 0, 0, 0)),
            pl.BlockSpec((None, None, NEAR_KEYS, ROWS), lambda i, g, q: (jnp.minimum(q, 1), g, 0, 0)),
            pl.BlockSpec((None, Q_TILE, NSA_HPG * HEAD_DIM), lambda i, g, q: (i, q, g)),
            pl.BlockSpec((None, None, GATE_ROWS, Q_TILE), lambda i, g, q: (i, g, 0, q)),
        ],
        out_specs=pl.BlockSpec((None, Q_TILE, NSA_HPG * HEAD_DIM), lambda i, g, q: (i, q, g)),
        out_shape=jax.ShapeDtypeStruct((b, s, NSA_Q_DIM), jnp.bfloat16),
        scratch_shapes=[
            pltpu.VMEM((pen_w // LANES + 2, 2 * HEAD_DIM, ROWS), jnp.bfloat16),
            pltpu.VMEM((1, ROWS), jnp.float32),
            pltpu.VMEM((ACC_ROWS, ROWS), jnp.float32),
            pltpu.VMEM((2, KEY_CHUNK, ROWS), jnp.float32),
            pltpu.VMEM((2, 1, ROWS), jnp.float32),
            pltpu.VMEM((2, KEY_CHUNK, ROWS), jnp.bfloat16),
        ],
        compiler_params=_params(("parallel", "parallel", "arbitrary")),
        name="nsa_selected",
    )(qt, pen, kaug, vt_chunks, vt_pieces, near, ycw, gates_t)


def _fox_kernel(qt_ref, k_ref, dec_ref, vt_ref, o_ref, qaug_ref, m_ref, acc_ref, s_ref, mx_ref, p_ref):
    qi = pl.program_id(2)
    tq = qt_ref.shape[1]
    row = lax.broadcasted_iota(jnp.int32, (HEAD_DIM, tq), 0)
    qaug_ref[0:HEAD_DIM, :] = qt_ref[...]
    qaug_ref[HEAD_DIM:, :] = jnp.where(row < DECAY_TERMS, -1.0, 0.0).astype(qaug_ref.dtype)
    _flash_init(m_ref, acc_ref)

    def get_q(lo, hi):
        return qaug_ref[:, lo:hi]

    def load_keys(c):
        k0 = pl.multiple_of(c * tq, tq)
        return jnp.concatenate([k_ref[pl.ds(k0, tq), :], dec_ref[pl.ds(k0, tq), :]], axis=1)

    _flash_far_chunks(qi, load_keys, lambda c: vt_ref[c], lambda c: get_q, tq, m_ref, acc_ref,
                      s_ref, mx_ref, p_ref)

    def causal(s, lo, hi):
        key = lax.broadcasted_iota(jnp.int32, s.shape, 0)
        qry = lo + lax.broadcasted_iota(jnp.int32, s.shape, 1)
        return jnp.where(key <= qry, s, NEG)

    _flash_step(load_keys(qi), vt_ref[qi], get_q, tq, m_ref, acc_ref, extra=causal)
    o_ref[...] = _flash_result(acc_ref).T.astype(o_ref.dtype)


def _fox(qkv, qt, dec, vt, tq):
    b, s, _ = qkv.shape
    return pl.pallas_call(
        _fox_kernel,
        grid=(b, FOX_HEADS, s // tq),
        in_specs=[
            pl.BlockSpec((None, None, HEAD_DIM, tq), lambda i, h, q: (i, NSA_HEADS + h, 0, q)),
            pl.BlockSpec((None, s, HEAD_DIM), lambda i, h, q: (i, 0, CB_KF + h)),
            pl.BlockSpec((None, None, s, LANES), lambda i, h, q: (i, h, 0, 0)),
            pl.BlockSpec((None, None, s // tq, ACC_ROWS, tq), lambda i, h, q: (i, h, 0, 0, 0)),
        ],
        out_specs=pl.BlockSpec((None, tq, HEAD_DIM), lambda i, h, q: (i, q, h)),
        out_shape=jax.ShapeDtypeStruct((b, s, FOX_DIM), jnp.bfloat16),
        scratch_shapes=[
            pltpu.VMEM((2 * HEAD_DIM, tq), jnp.bfloat16),
            pltpu.VMEM((1, tq), jnp.float32),
            pltpu.VMEM((ACC_ROWS, tq), jnp.float32),
            pltpu.VMEM((2, tq, tq), jnp.float32),
            pltpu.VMEM((2, 1, tq), jnp.float32),
            pltpu.VMEM((2, tq, tq), jnp.bfloat16),
        ],
        compiler_params=_params(("parallel", "parallel", "arbitrary")),
        name="fox_attention",
    )(qt, qkv, dec, vt)


def _merge_kernel(x_ref, g_ref, wa_ref, wb_ref, pa_ref, pb_ref, wo_ref, ya_ref, yb_ref, o_ref,
                  h_ref, acc_ref):
    j = pl.program_id(1)

    @pl.when(j == 0)
    def _():
        h_ref[...] = _norm_rows(x_ref[...], g_ref[...]).astype(h_ref.dtype)
        acc_ref[...] = jnp.zeros_like(acc_ref)

    h = h_ref[...]
    ga = jax.nn.sigmoid(jnp.dot(h, wa_ref[...], preferred_element_type=jnp.float32))
    gb = jax.nn.sigmoid(jnp.dot(h, wb_ref[...], preferred_element_type=jnp.float32))
    a = jnp.dot(ya_ref[...], pa_ref[...], preferred_element_type=jnp.float32)
    bb = jnp.dot(yb_ref[...], pb_ref[...], preferred_element_type=jnp.float32)
    merged = (ga * a + gb * bb).astype(jnp.bfloat16)
    acc_ref[...] += jnp.dot(merged, wo_ref[...], preferred_element_type=jnp.float32)

    @pl.when(j == pl.num_programs(1) - 1)
    def _():
        o_ref[...] = x_ref[...] + acc_ref[...]


def _merge(x2d, g, w_ma, w_mb, p_a, p_b, w_out, y_a, y_b, tm, tn):
    t, d = x2d.shape
    ka = p_a.shape[0]
    kb = p_b.shape[0]
    return pl.pallas_call(
        _merge_kernel,
        grid=(t // tm, d // tn),
        in_specs=[
            pl.BlockSpec((tm, d), lambda i, j: (i, 0)),
            pl.BlockSpec((1, d), lambda i, j: (0, 0)),
            pl.BlockSpec((d, tn), lambda i, j: (0, j)),
            pl.BlockSpec((d, tn), lambda i, j: (0, j)),
            pl.BlockSpec((ka, tn), lambda i, j: (0, j)),
            pl.BlockSpec((kb, tn), lambda i, j: (0, j)),
            pl.BlockSpec((tn, d), lambda i, j: (j, 0)),
            pl.BlockSpec((tm, ka), lambda i, j: (i, 0)),
            pl.BlockSpec((tm, kb), lambda i, j: (i, 0)),
        ],
        out_specs=pl.BlockSpec((tm, d), lambda i, j: (i, 0)),
        out_shape=jax.ShapeDtypeStruct((t, d), jnp.float32),
        scratch_shapes=[pltpu.VMEM((tm, d), jnp.bfloat16), pltpu.VMEM((tm, d), jnp.float32)],
        compiler_params=_params(("parallel", "arbitrary")),
        name="merge_out_proj",
    )(x2d, g.reshape(1, d), w_ma, w_mb, p_a, p_b, w_out, y_a, y_b)


def _ffn_kernel(x_ref, xh_ref, g_ref, wu_ref, wv_ref, cw_ref, cb_ref, wd_ref, gf_ref, o_ref,
                h_ref, u_ref, acc_ref, *, seq, tm):
    i = pl.program_id(0)
    j = pl.program_id(1)
    halo = BF16_SUBLANES

    @pl.when(j == 0)
    def _():
        g = g_ref[...]
        keep = jnp.where((i * tm) % seq == 0, 0.0, 1.0)
        h_ref[0:halo, :] = (_norm_rows(xh_ref[...], g) * keep).astype(h_ref.dtype)
        h_ref[halo:, :] = _norm_rows(x_ref[...], g).astype(h_ref.dtype)
        acc_ref[...] = jnp.zeros_like(acc_ref)

    u_ref[...] = jnp.dot(h_ref[...], wu_ref[...], preferred_element_type=jnp.float32)
    v = jnp.dot(h_ref[halo:, :], wv_ref[...], preferred_element_type=jnp.float32)
    conv = cb_ref[...]
    for k in range(CONV_WIDTH):
        conv = conv + cw_ref[k:k + 1, :] * u_ref[pl.ds(halo - (CONV_WIDTH - 1) + k, tm), :]
    act = (jax.nn.gelu(conv) * v).astype(jnp.bfloat16)
    acc_ref[...] += jnp.dot(act, wd_ref[...], preferred_element_type=jnp.float32)

    @pl.when(j == pl.num_programs(1) - 1)
    def _():
        o_ref[...] = _norm_rows(x_ref[...] + acc_ref[...], gf_ref[...])


def _ffn(x2d, g, w_up, conv_w, conv_b, w_down, g_final, seq, tm, tn):
    t, d = x2d.shape
    d_ff = w_down.shape[0]
    nt = d_ff // tn
    halo = BF16_SUBLANES
    per = tm // halo
    return pl.pallas_call(
        functools.partial(_ffn_kernel, seq=seq, tm=tm),
        grid=(t // tm, nt),
        in_specs=[
            pl.BlockSpec((tm, d), lambda i, j: (i, 0)),
            pl.BlockSpec((halo, d), lambda i, j: (jnp.maximum(i * per - 1, 0), 0)),
            pl.BlockSpec((1, d), lambda i, j: (0, 0)),
            pl.BlockSpec((d, tn), lambda i, j: (0, j)),
            pl.BlockSpec((d, tn), lambda i, j: (0, nt + j)),
            pl.BlockSpec((CONV_WIDTH, tn), lambda i, j: (0, j)),
            pl.BlockSpec((1, tn), lambda i, j: (0, j)),
            pl.BlockSpec((tn, d), lambda i, j: (j, 0)),
            pl.BlockSpec((1, d), lambda i, j: (0, 0)),
        ],
        out_specs=pl.BlockSpec((tm, d), lambda i, j: (i, 0)),
        out_shape=jax.ShapeDtypeStruct((t, d), jnp.float32),
        scratch_shapes=[
            pltpu.VMEM((tm + halo, d), jnp.bfloat16),
            pltpu.VMEM((tm + halo, tn), jnp.float32),
            pltpu.VMEM((tm, d), jnp.float32),
        ],
        compiler_params=_params(("parallel", "arbitrary")),
        name="ffn_final_norm",
    )(x2d, x2d, g.reshape(1, d), w_up, w_up, conv_w, conv_b.reshape(1, d_ff), w_down, g_final.reshape(1, d))


def _t5_bucket_np(dist):
    n = np.maximum(dist, 0)
    max_exact = REL_BUCKETS // 2
    nf = np.maximum(n, 1).astype(np.float32)
    large = max_exact + (np.log(nf / np.float32(max_exact)) / np.float32(math.log(REL_MAX_DIST / max_exact))
                         * np.float32(REL_BUCKETS - max_exact)).astype(np.int32)
    return np.where(n < max_exact, n, np.minimum(large, REL_BUCKETS - 1)).astype(np.int32)


def _bias_by_distance(rel_table, far_shift):
    vals = rel_table[jnp.asarray(_t5_bucket_np(np.arange(REL_MAX_DIST + 1)))]
    if far_shift:
        vals = vals - rel_table[REL_BUCKETS - 1]
    return vals * LOG2E


def _near_bias(rel_table):
    fd = _bias_by_distance(rel_table, True)
    h = fd.shape[1]
    lo = NEAR_KEYS - 1
    vec = jnp.concatenate([jnp.full((lo, h), NEG, fd.dtype), fd[:REL_MAX_DIST],
                           jnp.zeros((NEAR_KEYS - REL_MAX_DIST, h), fd.dtype)], axis=0)
    def toeplitz(start):
        period = Q_TILE + NEAR_KEYS - 1
        window = jnp.concatenate([vec[start:start + Q_TILE], vec[start - (NEAR_KEYS - 1):start]], axis=0)
        flat = jnp.tile(window, (NEAR_KEYS, 1))[:NEAR_KEYS * (period - 1)]
        return flat.reshape(NEAR_KEYS, period - 1, h)[:, :Q_TILE]

    tiles = jnp.stack([toeplitz(lo + a) for a in (0, Q_TILE)])
    tiles = tiles.reshape(2, NEAR_KEYS, Q_TILE, NSA_GROUPS, NSA_HPG).transpose(0, 3, 1, 4, 2)
    return tiles.reshape(2, NSA_GROUPS, NEAR_KEYS, ROWS)


def _band_bias(rel_table):
    fd = _bias_by_distance(rel_table, True)
    tl = np.arange(Q_TILE)[None, :]
    r = np.arange(BAND_ROWS)[:, None]
    dist = np.stack([tl - CMP_STRIDE * (r - off) - (CMP_BLOCK - 1) for off in (0, 8, 16)])
    vals = fd[jnp.asarray(np.clip(dist, 0, REL_MAX_DIST))]
    vals = jnp.where(jnp.asarray(dist >= 0)[..., None], vals, NEG)
    v, rr, q, _ = vals.shape
    return vals.transpose(0, 1, 3, 2).reshape(v, rr, NSA_GROUPS, NSA_HPG * q).transpose(0, 2, 1, 3)


def _overlap_t(n_cmp_pad, n_slc):
    i = np.arange(n_cmp_pad)[None, :]
    jj = np.arange(n_slc)[:, None]
    c_start = i * CMP_STRIDE
    ov = (c_start < jj * SLC_BLOCK + SLC_BLOCK) & (c_start + CMP_BLOCK - 1 >= jj * SLC_BLOCK)
    ov = ov & (i < n_cmp_pad - 1)
    return jnp.asarray(ov.astype(np.float32), jnp.bfloat16)


def _block_onehot(seq):
    blk = (np.arange(seq) // SLC_BLOCK) % LANES
    return jnp.asarray((blk[:, None] == np.arange(LANES)[None, :]).astype(np.float32), jnp.bfloat16)


def _pick_tile(n, pref):
    return pref if n % pref == 0 else n


def kernel(x, attn_norm_g, w_in, cmp_pos_k, cmp_w1_k, cmp_w2_k, cmp_pos_v, cmp_w1_v, cmp_w2_v,
           rel_bias_table, fox_forget_bias, w_branch_nsa, w_branch_fox, w_out,
           ffn_norm_g, w_up, conv_w, conv_b, w_down, final_norm_g):
    assert w_in.shape[0] == 1, "the final norm is fused into the single layer's FFN kernel"
    bsz, seq, d = x.shape
    t = bsz * seq
    bf = jnp.bfloat16
    scale = HEAD_DIM ** -0.5 * LOG2E
    x2d = x.reshape(t, d)
    w_in = w_in[0]

    o = np.cumsum([0, NSA_Q_DIM] + [NSA_KV_DIM] * 6 + [3 * NSA_HEADS, FOX_DIM, FOX_DIM, FOX_DIM, FOX_HEADS, d, d])
    w_qkv = jnp.concatenate([w_in[:, o[0]:o[1]] * scale, w_in[:, o[1]:o[7]],
                             w_in[:, o[8]:o[9]] * scale, w_in[:, o[9]:o[11]]], axis=1).astype(bf)
    gate_cols = []
    for grp in range(NSA_GROUPS):
        gate_cols += [w_in[:, o[7] + grp * GATES_PER_GROUP:o[7] + (grp + 1) * GATES_PER_GROUP],
                      jnp.zeros((d, LANES - GATES_PER_GROUP), w_in.dtype)]
    gate_cols += [w_in[:, o[11]:o[12]], jnp.zeros((d, LANES - FOX_HEADS), w_in.dtype)]
    w_gate = jnp.concatenate(gate_cols, axis=1).astype(bf)
    w_ma = w_in[:, o[12]:o[13]].astype(bf)
    w_mb = w_in[:, o[13]:o[14]].astype(bf)

    tm = _pick_tile(t, 1024)
    qkv = _norm_matmul(x2d, attn_norm_g[0], w_qkv, bf, tm, 512).reshape(bsz, seq, N_QKV)
    gates = _norm_matmul(x2d, attn_norm_g[0], w_gate, jnp.float32, tm, LANES).reshape(bsz, seq, N_GATE)

    q_all = jnp.concatenate([qkv[:, :, :NSA_Q_DIM], qkv[:, :, CB_QF * HEAD_DIM:(CB_QF + FOX_HEADS) * HEAD_DIM]],
                            axis=-1)
    qt = q_all.reshape(bsz, seq, NSA_HEADS + FOX_HEADS, HEAD_DIM).transpose(0, 2, 3, 1)

    def heads_major(cb, heads):
        return qkv[:, :, cb * HEAD_DIM:(cb + heads) * HEAD_DIM].reshape(
            bsz, seq, heads, HEAD_DIM).transpose(0, 2, 1, 3)

    def values_t(cb, heads, piece):
        v = qkv[:, :, cb * HEAD_DIM:(cb + heads) * HEAD_DIM]
        v = v.reshape(bsz, seq // piece, piece, heads, HEAD_DIM).transpose(0, 3, 1, 4, 2)
        ones_row = (jnp.arange(ACC_ROWS - HEAD_DIM) == 0).astype(bf)[:, None]
        extra = jnp.broadcast_to(ones_row, v.shape[:3] + (ACC_ROWS - HEAD_DIM, piece))
        return jnp.concatenate([v, extra], axis=3)

    n_ch = seq // CMP_STRIDE

    def chunked(cb):
        sl = qkv[:, :, cb * HEAD_DIM:(cb + NSA_GROUPS) * HEAD_DIM]
        sl = sl.reshape(bsz, n_ch, CMP_STRIDE, NSA_GROUPS, HEAD_DIM).transpose(0, 3, 1, 2, 4)
        return sl.reshape(bsz, NSA_GROUPS, n_ch, CMP_STRIDE * HEAD_DIM)

    def posflat(pos):
        return jnp.broadcast_to(pos.reshape(1, CMP_BLOCK * HEAD_DIM), (8, CMP_BLOCK * HEAD_DIM)).astype(bf)

    kc = _compress(chunked(CB_KC), cmp_w1_k[0].astype(bf), posflat(cmp_pos_k[0]), cmp_w2_k[0].astype(bf))
    vc = _compress(chunked(CB_VC), cmp_w1_v[0].astype(bf), posflat(cmp_pos_v[0]), cmp_w2_v[0].astype(bf))
    vct = vc.transpose(0, 1, 3, 2)

    f_t = gates[:, :, NSA_GROUPS * LANES:NSA_GROUPS * LANES + FOX_HEADS]
    f_t = f_t.transpose(0, 2, 1).reshape(bsz * FOX_HEADS, seq)
    bias_col = jnp.tile(fox_forget_bias[0].astype(jnp.float32), bsz).reshape(bsz * FOX_HEADS, 1)
    terms = _decay_cumsum(f_t, bias_col, _pick_tile(seq, 2048))
    dec = jnp.pad(terms.transpose(1, 2, 0), ((0, 0), (0, 0), (0, LANES - DECAY_TERMS)))
    dec = dec.reshape(bsz, FOX_HEADS, seq, LANES)

    n_slc = seq // SLC_BLOCK
    near = _near_bias(rel_bias_table)
    gates_t = gates[:, :, :NSA_GROUPS * LANES].reshape(bsz, seq, NSA_GROUPS, LANES)[..., :GATE_ROWS]
    gates_t = gates_t.transpose(0, 2, 3, 1)
    ycw, pen = _cmp_win(qt, kc, vct, _band_bias(rel_bias_table), _overlap_t(n_ch, n_slc), qkv,
                        values_t(CB_VW, NSA_GROUPS, LANES), near, gates_t)
    ks = heads_major(CB_KS, NSA_GROUPS)
    kaug_sel = jnp.concatenate([ks, jnp.broadcast_to(_block_onehot(seq), ks.shape)], axis=-1)
    y_nsa = _sel(qt, pen, kaug_sel, values_t(CB_VS, NSA_GROUPS, _pick_tile(seq, KEY_CHUNK)),
                 values_t(CB_VS, NSA_GROUPS, LANES), near, ycw, gates_t)

    fox_tq = _pick_tile(seq, FOX_Q_TILE)
    y_fox = _fox(qkv, qt, dec, values_t(CB_VF, FOX_HEADS, fox_tq), fox_tq)

    tm2 = _pick_tile(t, 512)
    x_mid = _merge(x2d, attn_norm_g[0], w_ma, w_mb, w_branch_nsa[0].astype(bf), w_branch_fox[0].astype(bf),
                   w_out[0].astype(bf), y_nsa.reshape(t, NSA_Q_DIM), y_fox.reshape(t, FOX_DIM),
                   tm2, _pick_tile(d, 512))
    d_ff = w_down.shape[1]
    out = _ffn(x_mid, ffn_norm_g[0], w_up[0].astype(bf), conv_w[0], conv_b[0], w_down[0].astype(bf),
               final_norm_g, seq, tm2, _pick_tile(d_ff, 512))
    return out.reshape(bsz, seq, d)
```

```python
import functools
import math

import jax
import jax.numpy as jnp
import numpy as np
from jax import lax
from jax.experimental import pallas as pl
from jax.experimental.pallas import tpu as pltpu

HEAD_DIM = 128
NSA_HEADS = 8
NSA_GROUPS = 2
NSA_HPG = NSA_HEADS // NSA_GROUPS
FOX_HEADS = 8
CMP_BLOCK = 32
CMP_STRIDE = 16
SLC_BLOCK = 64
SLC_TOPK = 16
WINDOW = 512
REL_BUCKETS = 32
REL_MAX_DIST = 128
CONV_WIDTH = 3
EPS = 1e-6
NEG = -1e30
FORCED_SCORE = 1e4
LOG2E = math.log2(math.e)

LANES = 128
BF16_SUBLANES = 16
VMEM_LIMIT = 56 * 1024 * 1024

NSA_Q_DIM = NSA_HEADS * HEAD_DIM
NSA_KV_DIM = NSA_GROUPS * HEAD_DIM
FOX_DIM = FOX_HEADS * HEAD_DIM
N_QKV = NSA_Q_DIM + 6 * NSA_KV_DIM + 3 * FOX_DIM
GATES_PER_GROUP = 3 * NSA_HPG
N_GATE = (NSA_GROUPS + 1) * LANES
IN_PROJ_TILE = 512

CB_KC = 8
CB_VC = 10
CB_KS = 12
CB_VS = 14
CB_KW = 16
CB_VW = 18
CB_QF = 20
CB_KF = 28
CB_VF = 36

Q_TILE = 128
ROWS = NSA_HPG * Q_TILE
KEY_CHUNK = 512
FOX_Q_TILE = 512
N_SPLIT = 2
ROW_BLOCK = 32
DECAY_TERMS = 3
BAND_ROWS = 24
CMP_TIERS = 4
WIN_KEYS = WINDOW + Q_TILE
NEAR_KEYS = 2 * Q_TILE
GATE_ROWS = 16
SUM_ROW = HEAD_DIM
ACC_ROWS = HEAD_DIM + 8


def _params(sem):
    return pltpu.CompilerParams(dimension_semantics=sem, vmem_limit_bytes=VMEM_LIMIT)


def _norm_rows(x, g):
    return (x * lax.rsqrt(jnp.mean(x * x, axis=-1, keepdims=True) + EPS)) * g


def _in_proj_kernel(x_ref, g_ref, w_ref, qkv_ref, gate_ref, h_ref):
    j = pl.program_id(1)
    last = pl.num_programs(1) - 1

    @pl.when(j == 0)
    def _():
        h_ref[...] = _norm_rows(x_ref[...], g_ref[...]).astype(h_ref.dtype)

    y = jnp.dot(h_ref[...], w_ref[...], preferred_element_type=jnp.float32)

    @pl.when(j < last)
    def _():
        qkv_ref[...] = y.astype(qkv_ref.dtype)

    @pl.when(j == last)
    def _():
        gate_ref[...] = y


def _in_proj(x2d, g, w, tm, tn):
    t, d = x2d.shape
    n_tiles = w.shape[1] // tn
    return pl.pallas_call(
        _in_proj_kernel,
        grid=(t // tm, n_tiles),
        in_specs=[
            pl.BlockSpec((tm, d), lambda i, j: (i, 0)),
            pl.BlockSpec((1, d), lambda i, j: (0, 0)),
            pl.BlockSpec((d, tn), lambda i, j: (0, j)),
        ],
        out_specs=[
            pl.BlockSpec((tm, tn), lambda i, j: (i, jnp.minimum(j, n_tiles - 2))),
            pl.BlockSpec((tm, tn), lambda i, j: (i, 0)),
        ],
        out_shape=[
            jax.ShapeDtypeStruct((t, (n_tiles - 1) * tn), jnp.bfloat16),
            jax.ShapeDtypeStruct((t, tn), jnp.float32),
        ],
        scratch_shapes=[pltpu.VMEM((tm, d), jnp.bfloat16)],
        compiler_params=_params(("parallel", "arbitrary")),
        name="norm_in_proj",
    )(x2d, g.reshape(1, d), w)


def _compress_kernel(ch_ref, w1_ref, posf_ref, w2_ref, o_ref):
    half = ch_ref.shape[1]
    ch = ch_ref[...]
    pa = jnp.dot(ch, w1_ref[:half, :], preferred_element_type=jnp.float32)
    pb = jnp.dot(ch, w1_ref[half:, :], preferred_element_type=jnp.float32)
    pos = jnp.dot(posf_ref[...], w1_ref[...], preferred_element_type=jnp.float32)[0:1, :]
    n = pa.shape[0]
    pre = pa + pltpu.roll(pb, n - 1, 0) + pos
    act = jax.nn.gelu(pre)
    o_ref[...] = jnp.dot(act.astype(jnp.bfloat16), w2_ref[...],
                         preferred_element_type=jnp.float32).astype(o_ref.dtype)


def _compress(chunks, w1, posf, w2):
    b, g, n, k = chunks.shape
    return pl.pallas_call(
        _compress_kernel,
        grid=(b, g),
        in_specs=[
            pl.BlockSpec((None, None, n, k), lambda i, j: (i, j, 0, 0)),
            pl.BlockSpec(w1.shape, lambda i, j: (0, 0)),
            pl.BlockSpec(posf.shape, lambda i, j: (0, 0)),
            pl.BlockSpec(w2.shape, lambda i, j: (0, 0)),
        ],
        out_specs=pl.BlockSpec((None, None, n, HEAD_DIM), lambda i, j: (i, j, 0, 0)),
        out_shape=jax.ShapeDtypeStruct((b, g, n, HEAD_DIM), jnp.bfloat16),
        compiler_params=_params(("parallel", "parallel")),
        name="compress_tokens",
    )(chunks, w1, posf, w2)


def _decay_kernel(f_ref, b_ref, tri_ref, o_ref, carry_ref):
    @pl.when(pl.program_id(0) == 0)
    def _():
        carry_ref[...] = jnp.zeros_like(carry_ref)

    x = f_ref[...] + b_ref[...]
    logf = (jnp.minimum(x, 0.0) - jnp.log1p(jnp.exp(-jnp.abs(x)))) * LOG2E
    carry = carry_ref[...]
    for seg in range(f_ref.shape[1] // LANES):
        part = jnp.dot(logf[:, seg * LANES:(seg + 1) * LANES], tri_ref[...],
                       preferred_element_type=jnp.float32, precision=lax.Precision.HIGHEST) + carry
        carry = part[:, LANES - 1:LANES]
        rest = part
        for term in range(DECAY_TERMS):
            piece = rest.astype(o_ref.dtype)
            o_ref[term, :, seg * LANES:(seg + 1) * LANES] = piece
            rest = rest - piece.astype(jnp.float32)
    carry_ref[...] = carry


def _decay_cumsum(f_t, bias_col, width):
    rows, s = f_t.shape
    tri = jnp.asarray(np.triu(np.ones((LANES, LANES), np.float32)))
    return pl.pallas_call(
        _decay_kernel,
        grid=(s // width,),
        in_specs=[
            pl.BlockSpec((rows, width), lambda i: (0, i)),
            pl.BlockSpec((rows, 1), lambda i: (0, 0)),
            pl.BlockSpec((LANES, LANES), lambda i: (0, 0)),
        ],
        out_specs=pl.BlockSpec((DECAY_TERMS, rows, width), lambda i: (0, 0, i)),
        out_shape=jax.ShapeDtypeStruct((DECAY_TERMS, rows, s), jnp.bfloat16),
        scratch_shapes=[pltpu.VMEM((rows, 1), jnp.float32)],
        compiler_params=_params(("arbitrary",)),
        name="decay_cumsum",
    )(f_t, bias_col, tri)


def _col_reduce(op, x):
    reduce = {jnp.maximum: jnp.max, jnp.minimum: jnp.min, jnp.add: jnp.sum}[op]
    return reduce(x, axis=0, keepdims=True)


def _flash_init(m_ref, acc_ref):
    m_ref[...] = jnp.full(m_ref.shape, NEG, jnp.float32)
    acc_ref[...] = jnp.zeros(acc_ref.shape, jnp.float32)


def _probabilities(s, m_new):
    return jnp.exp2((s - m_new).astype(jnp.bfloat16))


def _flash_result(acc_ref):
    return acc_ref[0:HEAD_DIM, :] / acc_ref[SUM_ROW:SUM_ROW + 1, :]


def _flash_step(kaug, vt, get_q, width, m_ref, acc_ref, extra=None):
    w = width // N_SPLIT
    strips = [(i * w, (i + 1) * w) for i in range(N_SPLIT)]
    scores = [jnp.dot(kaug, get_q(lo, hi), preferred_element_type=jnp.float32) for lo, hi in strips]
    for (lo, hi), s in zip(strips, scores):
        if extra is not None:
            s = extra(s, lo, hi)
        m_prev = m_ref[:, lo:hi]
        m_new = jnp.maximum(m_prev, _col_reduce(jnp.maximum, s))
        alpha = jnp.exp2(m_prev - m_new)
        acc_ref[:, lo:hi] = alpha * acc_ref[:, lo:hi] + jnp.dot(
            vt, _probabilities(s, m_new), preferred_element_type=jnp.float32)
        m_ref[:, lo:hi] = m_new


def _flash_far_chunks(n, load_keys, load_values, get_q_for, width, m_ref, acc_ref, s_ref, mx_ref, p_ref):
    w = width // N_SPLIT
    strips = [(i * w, (i + 1) * w) for i in range(N_SPLIT)]
    base = n % 2
    pairs = n // 2

    def scores_into(c, slot):
        kaug = load_keys(c)
        get_q = get_q_for(c)
        for lo, hi in strips:
            s = jnp.dot(kaug, get_q(lo, hi), preferred_element_type=jnp.float32)
            s_ref[slot, :, lo:hi] = s
            mx_ref[slot, :, lo:hi] = _col_reduce(jnp.maximum, s)

    def value_product(c, slot):
        vt = load_values(c)
        return jnp.concatenate([jnp.dot(vt, p_ref[slot, :, lo:hi], preferred_element_type=jnp.float32)
                                for lo, hi in strips], axis=1)

    @pl.when(base == 1)
    def _():
        _flash_step(load_keys(0), load_values(0), get_q_for(0), width, m_ref, acc_ref)

    @pl.when(pairs > 0)
    def _():
        scores_into(base, 0)
        p_ref[1] = jnp.zeros(p_ref.shape[1:], p_ref.dtype)

        def pair(j, carry):
            for cur in (0, 1):
                c = base + 2 * j + cur
                nxt = 1 - cur
                pv = value_product(jnp.maximum(c - 1, base), nxt)
                scores_into(jnp.minimum(c + 1, n - 1), nxt)
                alphas = []
                for lo, hi in strips:
                    m_prev = m_ref[:, lo:hi]
                    m_new = jnp.maximum(m_prev, mx_ref[cur, :, lo:hi])
                    m_ref[:, lo:hi] = m_new
                    for r in range(0, s_ref.shape[1], ROW_BLOCK):
                        p_ref[cur, r:r + ROW_BLOCK, lo:hi] = _probabilities(
                            s_ref[cur, r:r + ROW_BLOCK, lo:hi], m_new)
                    alphas.append(jnp.exp2(m_prev - m_new))
                acc_ref[...] = jnp.concatenate(alphas, axis=1) * (acc_ref[...] + pv)
            return carry

        lax.fori_loop(0, pairs, pair, 0)
        acc_ref[...] = acc_ref[...] + value_product(n - 1, 1)


def _group_queries_t(qt_ref):
    return jnp.concatenate([qt_ref[h] for h in range(NSA_HPG)], axis=1)


def _branch_gate(gates_t, head, branch):
    row = head * 3 + branch
    return gates_t[row:row + 1, :]


def _values_t(vt_ref, k0, n_keys):
    p0 = k0 // LANES
    return jnp.concatenate([vt_ref[p0 + j] for j in range(n_keys // LANES)], axis=1)


def _compressed_branch(n_rows, i0, q4t, kc_ref, vct_ref, band_ref, ovt_ref, sc_ref, oct_ref, imp_ref):
    sc_ref[0:n_rows, :] = jnp.dot(kc_ref[0:n_rows, :], q4t, preferred_element_type=jnp.float32)
    sc_ref[pl.ds(i0, BAND_ROWS), :] = sc_ref[pl.ds(i0, BAND_ROWS), :] + band_ref[...]
    row = lax.broadcasted_iota(jnp.int32, (n_rows, ROWS), 0)
    sc = jnp.where(row < i0 + BAND_ROWS, sc_ref[0:n_rows, :], NEG)
    m = _col_reduce(jnp.maximum, sc)
    p = jnp.exp2(sc - m)
    l = _col_reduce(jnp.add, p)
    pn = p * jnp.where(m > 0.5 * NEG, 1.0 / l, 0.0)
    oct_ref[...] = jnp.dot(vct_ref[:, 0:n_rows], pn.astype(jnp.bfloat16), preferred_element_type=jnp.float32)
    psum = pn[:, 0:Q_TILE]
    for h in range(1, NSA_HPG):
        psum = psum + pn[:, h * Q_TILE:(h + 1) * Q_TILE]
    imp_ref[...] = jnp.dot(ovt_ref[:, 0:n_rows], psum.astype(jnp.bfloat16), preferred_element_type=jnp.float32)


def _cmp_win_kernel(qt_ref, kc_ref, vct_ref, band_ref, ovt_ref, kw_ref, vwt_ref, near_ref, gt_ref,
                    ycw_ref, pen_ref, sc_ref, sw_ref, oct_ref, imp_ref):
    qb = pl.program_id(2)
    n_cmp = kc_ref.shape[0]
    n_slc = ovt_ref.shape[0]
    q4t = _group_queries_t(qt_ref)

    i0 = pl.multiple_of(jnp.maximum(8 * qb - 16, 0), 8)
    n_tiers = CMP_TIERS if n_cmp % (CMP_TIERS * LANES) == 0 else 1
    step = n_cmp // n_tiers
    for tier in range(1, n_tiers + 1):
        in_tier = (i0 + BAND_ROWS <= tier * step) & (i0 + BAND_ROWS > (tier - 1) * step)
        pl.when(in_tier)(functools.partial(_compressed_branch, tier * step, i0, q4t, kc_ref, vct_ref,
                                           band_ref, ovt_ref, sc_ref, oct_ref, imp_ref))
    oct_ = oct_ref[...]
    imp = imp_ref[...]

    ji = lax.broadcasted_iota(jnp.int32, (n_slc, Q_TILE), 0)
    jf = ji.astype(jnp.float32)
    t = qb * Q_TILE + lax.broadcasted_iota(jnp.int32, (n_slc, Q_TILE), 1)
    cur = t // SLC_BLOCK
    forced = (ji == 0) | (ji == cur) | (ji == cur - 1)
    score = jnp.where(ji <= cur, jnp.where(forced, FORCED_SCORE, imp), -1.0)
    pen_t = jnp.full((n_slc, Q_TILE), NEG, jnp.float32)
    for _ in range(min(SLC_TOPK, n_slc)):
        mx = _col_reduce(jnp.maximum, score)
        idx = _col_reduce(jnp.minimum, jnp.where(score == mx, jf, float(n_slc)))
        pick = jf == idx
        pen_t = jnp.where(pick, 0.0, pen_t)
        score = jnp.where(pick, -2.0, score)
    pad = pen_ref.shape[0] - n_slc
    if pad:
        pen_t = jnp.concatenate([pen_t, jnp.full((pad, Q_TILE), NEG, jnp.float32)], axis=0)
    pen_ref[...] = pen_t.astype(pen_ref.dtype)

    t0 = qb * Q_TILE
    ws = pl.multiple_of(jnp.maximum(t0 - WINDOW, 0), Q_TILE)
    ns = pl.multiple_of(jnp.maximum(t0 - Q_TILE, 0), Q_TILE)
    sw_ref[...] = jnp.dot(kw_ref[pl.ds(ws, WIN_KEYS), :], q4t, preferred_element_type=jnp.float32)
    off = pl.multiple_of(ns - ws, Q_TILE)
    sw_ref[pl.ds(off, NEAR_KEYS), :] = sw_ref[pl.ds(off, NEAR_KEYS), :] + near_ref[...]
    key = ws + lax.broadcasted_iota(jnp.int32, (WIN_KEYS, ROWS), 0)
    qry = t0 + (lax.broadcasted_iota(jnp.int32, (WIN_KEYS, ROWS), 1) & (Q_TILE - 1))
    sw = jnp.where((key <= qry) & (key > qry - WINDOW), sw_ref[...], NEG)
    mw = _col_reduce(jnp.maximum, sw)
    ow_sum = jnp.dot(_values_t(vwt_ref, ws, WIN_KEYS), _probabilities(sw, mw),
                     preferred_element_type=jnp.float32)
    owt = ow_sum[0:HEAD_DIM, :] / ow_sum[SUM_ROW:SUM_ROW + 1, :]

    gates_t = jax.nn.sigmoid(gt_ref[...])
    for h in range(NSA_HPG):
        cols = slice(h * Q_TILE, (h + 1) * Q_TILE)
        mixed = _branch_gate(gates_t, h, 0) * oct_[:, cols] + _branch_gate(gates_t, h, 2) * owt[:, cols]
        ycw_ref[:, h * HEAD_DIM:(h + 1) * HEAD_DIM] = mixed.T


def _cmp_win(qt, kc, vct, band, ovt, qkv, vwt, near, gates_t):
    b, s, _ = qkv.shape
    n_cmp = kc.shape[2]
    n_slc = ovt.shape[0]
    nq = s // Q_TILE
    pen_w = -(-n_slc // LANES) * LANES
    n_band = band.shape[0] - 1
    return pl.pallas_call(
        _cmp_win_kernel,
        grid=(b, NSA_GROUPS, nq),
        in_specs=[
            pl.BlockSpec((None, NSA_HPG, HEAD_DIM, Q_TILE), lambda i, g, q: (i, g, 0, q)),
            pl.BlockSpec((None, None, n_cmp, HEAD_DIM), lambda i, g, q: (i, g, 0, 0)),
            pl.BlockSpec((None, None, HEAD_DIM, n_cmp), lambda i, g, q: (i, g, 0, 0)),
            pl.BlockSpec((None, None, BAND_ROWS, ROWS), lambda i, g, q: (jnp.minimum(q, n_band), g, 0, 0)),
            pl.BlockSpec((n_slc, n_cmp), lambda i, g, q: (0, 0)),
            pl.BlockSpec((None, s, HEAD_DIM), lambda i, g, q: (i, 0, CB_KW + g)),
            pl.BlockSpec((None, None, s // LANES, ACC_ROWS, LANES), lambda i, g, q: (i, g, 0, 0, 0)),
            pl.BlockSpec((None, None, NEAR_KEYS, ROWS), lambda i, g, q: (jnp.minimum(q, 1), g, 0, 0)),
            pl.BlockSpec((None, None, GATE_ROWS, Q_TILE), lambda i, g, q: (i, g, 0, q)),
        ],
        out_specs=[
            pl.BlockSpec((None, Q_TILE, NSA_HPG * HEAD_DIM), lambda i, g, q: (i, q, g)),
            pl.BlockSpec((None, None, pen_w, Q_TILE), lambda i, g, q: (i, g, 0, q)),
        ],
        out_shape=[
            jax.ShapeDtypeStruct((b, s, NSA_Q_DIM), jnp.float32),
            jax.ShapeDtypeStruct((b, NSA_GROUPS, pen_w, s), jnp.bfloat16),
        ],
        scratch_shapes=[pltpu.VMEM((n_cmp, ROWS), jnp.float32), pltpu.VMEM((WIN_KEYS, ROWS), jnp.float32),
                        pltpu.VMEM((HEAD_DIM, ROWS), jnp.float32), pltpu.VMEM((n_slc, Q_TILE), jnp.float32)],
        compiler_params=_params(("parallel", "parallel", "arbitrary")),
        name="nsa_cmp_topk_win",
    )(qt, kc, vct, band, ovt, qkv, vwt, near, gates_t)


def _range_penalty(pen, first_blk, lo_ok, hi_ok):
    n_half = pen.shape[0] // LANES
    c = lax.broadcasted_iota(jnp.int32, (LANES, pen.shape[1]), 0)
    out = pen[0:LANES, :]
    blk = c
    for hf in range(1, n_half):
        in_lower = (c + (hf - 1) * LANES >= first_blk) & (first_blk < hf * LANES)
        out = jnp.where(in_lower, out, pen[hf * LANES:(hf + 1) * LANES, :])
        blk = jnp.where(in_lower, blk, c + hf * LANES)
    return jnp.where((blk >= lo_ok) & (blk < hi_ok), out, jnp.asarray(NEG, out.dtype))


def _sel_kernel(qt_ref, pen_ref, kaug_ref, vtc_ref, vtp_ref, near_ref, ycw_ref, gt_ref, y_ref,
                qaug_ref, m_ref, acc_ref, s_ref, mx_ref, p_ref):
    qb = pl.program_id(2)
    n_half = pen_ref.shape[0] // LANES
    n_blocks = pen_ref.shape[0]
    chunks_per_half = LANES * SLC_BLOCK // KEY_CHUNK
    blocks_per_chunk = KEY_CHUNK // SLC_BLOCK

    t0 = qb * Q_TILE
    near_end = t0 - Q_TILE
    nf = jnp.maximum(near_end // KEY_CHUNK, 0)
    ms = pl.multiple_of(jnp.maximum(near_end - KEY_CHUNK, 0), Q_TILE)
    ns = pl.multiple_of(jnp.maximum(near_end, 0), Q_TILE)

    q4t = _group_queries_t(qt_ref)
    pen = pen_ref[...]
    operands = [pen[hf * LANES:(hf + 1) * LANES, :] for hf in range(n_half)]
    operands.append(_range_penalty(pen, ms // SLC_BLOCK, nf * blocks_per_chunk, near_end // SLC_BLOCK))
    operands.append(_range_penalty(pen, ns // SLC_BLOCK, 0, n_blocks))
    for idx, channels in enumerate(operands):
        qaug_ref[idx, 0:HEAD_DIM, :] = q4t
        qaug_ref[idx, HEAD_DIM:, :] = jnp.concatenate([channels] * NSA_HPG, axis=1)

    _flash_init(m_ref, acc_ref)

    def get_q_for(c):
        half = c // chunks_per_half
        return lambda lo, hi: qaug_ref[half, :, lo:hi]

    _flash_far_chunks(nf, lambda c: kaug_ref[pl.ds(pl.multiple_of(c * KEY_CHUNK, KEY_CHUNK), KEY_CHUNK), :],
                      lambda c: vtc_ref[c], get_q_for, ROWS, m_ref, acc_ref, s_ref, mx_ref, p_ref)

    _flash_step(kaug_ref[pl.ds(ms, KEY_CHUNK), :], _values_t(vtp_ref, ms, KEY_CHUNK),
                lambda lo, hi: qaug_ref[n_half, :, lo:hi], ROWS, m_ref, acc_ref)
    _flash_step(kaug_ref[pl.ds(ns, NEAR_KEYS), :], _values_t(vtp_ref, ns, NEAR_KEYS),
                lambda lo, hi: qaug_ref[n_half + 1, :, lo:hi], ROWS, m_ref, acc_ref,
                extra=lambda s, lo, hi: s + near_ref[:, lo:hi])

    o = _flash_result(acc_ref)
    gates_t = jax.nn.sigmoid(gt_ref[...])
    for h in range(NSA_HPG):
        sel_h = (_branch_gate(gates_t, h, 1) * o[:, h * Q_TILE:(h + 1) * Q_TILE]).T
        y_ref[:, h * HEAD_DIM:(h + 1) * HEAD_DIM] = (
            ycw_ref[:, h * HEAD_DIM:(h + 1) * HEAD_DIM] + sel_h).astype(y_ref.dtype)


def _sel(qt, pen, kaug, vt_chunks, vt_pieces, near, ycw, gates_t):
    b, _, s, _ = kaug.shape
    nq = s // Q_TILE
    pen_w = pen.shape[2]
    return pl.pallas_call(
        _sel_kernel,
        grid=(b, NSA_GROUPS, nq),
        in_specs=[
            pl.BlockSpec((None, NSA_HPG, HEAD_DIM, Q_TILE), lambda i, g, q: (i, g, 0, q)),
            pl.BlockSpec((None, None, pen_w, Q_TILE), lambda i, g, q: (i, g, 0, q)),
            pl.BlockSpec((None, None, s, 2 * HEAD_DIM), lambda i, g, q: (i, g, 0, 0)),
            pl.BlockSpec((None, None, s // KEY_CHUNK, ACC_ROWS, KEY_CHUNK), lambda i, g, q: (i, g, 0, 0, 0)),
            pl.BlockSpec((None, None, s // LANES, ACC_ROWS, LANES), lambda i, g, q: (i, g, 0, 0, 0)),
            pl.BlockSpec((None, None, NEAR_KEYS, ROWS), lambda i, g, q: (jnp.minimum(q, 1), g, 0, 0)),
            pl.BlockSpec((None, Q_TILE, NSA_HPG * HEAD_DIM), lambda i, g, q: (i, q, g)),
            pl.BlockSpec((None, None, GATE_ROWS, Q_TILE), lambda i, g, q: (i, g, 0, q)),
        ],
        out_specs=pl.BlockSpec((None, Q_TILE, NSA_HPG * HEAD_DIM), lambda i, g, q: (i, q, g)),
        out_shape=jax.ShapeDtypeStruct((b, s, NSA_Q_DIM), jnp.bfloat16),
        scratch_shapes=[
            pltpu.VMEM((pen_w // LANES + 2, 2 * HEAD_DIM, ROWS), jnp.bfloat16),
            pltpu.VMEM((1, ROWS), jnp.float32),
            pltpu.VMEM((ACC_ROWS, ROWS), jnp.float32),
            pltpu.VMEM((2, KEY_CHUNK, ROWS), jnp.float32),
            pltpu.VMEM((2, 1, ROWS), jnp.float32),
            pltpu.VMEM((2, KEY_CHUNK, ROWS), jnp.bfloat16),
        ],
        compiler_params=_params(("parallel", "parallel", "arbitrary")),
        name="nsa_selected",
    )(qt, pen, kaug, vt_chunks, vt_pieces, near, ycw, gates_t)


def _fox_kernel(qt_ref, k_ref, dec_ref, vt_ref, o_ref, qaug_ref, m_ref, acc_ref, s_ref, mx_ref, p_ref):
    qi = pl.program_id(2)
    tq = qt_ref.shape[1]
    row = lax.broadcasted_iota(jnp.int32, (HEAD_DIM, tq), 0)
    qaug_ref[0:HEAD_DIM, :] = qt_ref[...]
    qaug_ref[HEAD_DIM:, :] = jnp.where(row < DECAY_TERMS, -1.0, 0.0).astype(qaug_ref.dtype)
    _flash_init(m_ref, acc_ref)

    def get_q(lo, hi):
        return qaug_ref[:, lo:hi]

    def load_keys(c):
        k0 = pl.multiple_of(c * tq, tq)
        return jnp.concatenate([k_ref[pl.ds(k0, tq), :], dec_ref[pl.ds(k0, tq), :]], axis=1)

    _flash_far_chunks(qi, load_keys, lambda c: vt_ref[c], lambda c: get_q, tq, m_ref, acc_ref,
                      s_ref, mx_ref, p_ref)

    def causal(s, lo, hi):
        key = lax.broadcasted_iota(jnp.int32, s.shape, 0)
        qry = lo + lax.broadcasted_iota(jnp.int32, s.shape, 1)
        return jnp.where(key <= qry, s, NEG)

    _flash_step(load_keys(qi), vt_ref[qi], get_q, tq, m_ref, acc_ref, extra=causal)
    o_ref[...] = _flash_result(acc_ref).T.astype(o_ref.dtype)


def _fox(qkv, qt, dec, vt, tq):
    b, s, _ = qkv.shape
    return pl.pallas_call(
        _fox_kernel,
        grid=(b, FOX_HEADS, s // tq),
        in_specs=[
            pl.BlockSpec((None, None, HEAD_DIM, tq), lambda i, h, q: (i, NSA_HEADS + h, 0, q)),
            pl.BlockSpec((None, s, HEAD_DIM), lambda i, h, q: (i, 0, CB_KF + h)),
            pl.BlockSpec((None, None, s, LANES), lambda i, h, q: (i, h, 0, 0)),
            pl.BlockSpec((None, None, s // tq, ACC_ROWS, tq), lambda i, h, q: (i, h, 0, 0, 0)),
        ],
        out_specs=pl.BlockSpec((None, tq, HEAD_DIM), lambda i, h, q: (i, q, h)),
        out_shape=jax.ShapeDtypeStruct((b, s, FOX_DIM), jnp.bfloat16),
        scratch_shapes=[
            pltpu.VMEM((2 * HEAD_DIM, tq), jnp.bfloat16),
            pltpu.VMEM((1, tq), jnp.float32),
            pltpu.VMEM((ACC_ROWS, tq), jnp.float32),
            pltpu.VMEM((2, tq, tq), jnp.float32),
            pltpu.VMEM((2, 1, tq), jnp.float32),
            pltpu.VMEM((2, tq, tq), jnp.bfloat16),
        ],
        compiler_params=_params(("parallel", "parallel", "arbitrary")),
        name="fox_attention",
    )(qt, qkv, dec, vt)


def _merge_kernel(x_ref, g_ref, wa_ref, wb_ref, pa_ref, pb_ref, wo_ref, ya_ref, yb_ref, o_ref,
                  h_ref, acc_ref):
    j = pl.program_id(1)

    @pl.when(j == 0)
    def _():
        h_ref[...] = _norm_rows(x_ref[...], g_ref[...]).astype(h_ref.dtype)
        acc_ref[...] = jnp.zeros_like(acc_ref)

    h = h_ref[...]
    ga = jax.nn.sigmoid(jnp.dot(h, wa_ref[...], preferred_element_type=jnp.float32))
    gb = jax.nn.sigmoid(jnp.dot(h, wb_ref[...], preferred_element_type=jnp.float32))
    a = jnp.dot(ya_ref[...], pa_ref[...], preferred_element_type=jnp.float32)
    bb = jnp.dot(yb_ref[...], pb_ref[...], preferred_element_type=jnp.float32)
    merged = (ga * a + gb * bb).astype(jnp.bfloat16)
    acc_ref[...] += jnp.dot(merged, wo_ref[...], preferred_element_type=jnp.float32)

    @pl.when(j == pl.num_programs(1) - 1)
    def _():
        o_ref[...] = x_ref[...] + acc_ref[...]


def _merge(x2d, g, w_ma, w_mb, p_a, p_b, w_out, y_a, y_b, tm, tn):
    t, d = x2d.shape
    ka = p_a.shape[0]
    kb = p_b.shape[0]
    return pl.pallas_call(
        _merge_kernel,
        grid=(t // tm, d // tn),
        in_specs=[
            pl.BlockSpec((tm, d), lambda i, j: (i, 0)),
            pl.BlockSpec((1, d), lambda i, j: (0, 0)),
            pl.BlockSpec((d, tn), lambda i, j: (0, j)),
            pl.BlockSpec((d, tn), lambda i, j: (0, j)),
            pl.BlockSpec((ka, tn), lambda i, j: (0, j)),
            pl.BlockSpec((kb, tn), lambda i, j: (0, j)),
            pl.BlockSpec((tn, d), lambda i, j: (j, 0)),
            pl.BlockSpec((tm, ka), lambda i, j: (i, 0)),
            pl.BlockSpec((tm, kb), lambda i, j: (i, 0)),
        ],
        out_specs=pl.BlockSpec((tm, d), lambda i, j: (i, 0)),
        out_shape=jax.ShapeDtypeStruct((t, d), jnp.float32),
        scratch_shapes=[pltpu.VMEM((tm, d), jnp.bfloat16), pltpu.VMEM((tm, d), jnp.float32)],
        compiler_params=_params(("parallel", "arbitrary")),
        name="merge_out_proj",
    )(x2d, g.reshape(1, d), w_ma, w_mb, p_a, p_b, w_out, y_a, y_b)


def _ffn_kernel(x_ref, xh_ref, g_ref, wu_ref, wv_ref, cw_ref, cb_ref, wd_ref, gf_ref, o_ref,
                h_ref, u_ref, acc_ref, *, seq, tm):
    i = pl.program_id(0)
    j = pl.program_id(1)
    halo = BF16_SUBLANES

    @pl.when(j == 0)
    def _():
        g = g_ref[...]
        keep = jnp.where((i * tm) % seq == 0, 0.0, 1.0)
        h_ref[0:halo, :] = (_norm_rows(xh_ref[...], g) * keep).astype(h_ref.dtype)
        h_ref[halo:, :] = _norm_rows(x_ref[...], g).astype(h_ref.dtype)
        acc_ref[...] = jnp.zeros_like(acc_ref)

    u_ref[...] = jnp.dot(h_ref[...], wu_ref[...], preferred_element_type=jnp.float32)
    v = jnp.dot(h_ref[halo:, :], wv_ref[...], preferred_element_type=jnp.float32)
    conv = cb_ref[...]
    for k in range(CONV_WIDTH):
        conv = conv + cw_ref[k:k + 1, :] * u_ref[pl.ds(halo - (CONV_WIDTH - 1) + k, tm), :]
    act = (jax.nn.gelu(conv) * v).astype(jnp.bfloat16)
    acc_ref[...] += jnp.dot(act, wd_ref[...], preferred_element_type=jnp.float32)

    @pl.when(j == pl.num_programs(1) - 1)
    def _():
        o_ref[...] = _norm_rows(x_ref[...] + acc_ref[...], gf_ref[...])


def _ffn(x2d, g, w_up, conv_w, conv_b, w_down, g_final, seq, tm, tn):
    t, d = x2d.shape
    d_ff = w_down.shape[0]
    nt = d_ff // tn
    halo = BF16_SUBLANES
    per = tm // halo
    return pl.pallas_call(
        functools.partial(_ffn_kernel, seq=seq, tm=tm),
        grid=(t // tm, nt),
        in_specs=[
            pl.BlockSpec((tm, d), lambda i, j: (i, 0)),
            pl.BlockSpec((halo, d), lambda i, j: (jnp.maximum(i * per - 1, 0), 0)),
            pl.BlockSpec((1, d), lambda i, j: (0, 0)),
            pl.BlockSpec((d, tn), lambda i, j: (0, j)),
            pl.BlockSpec((d, tn), lambda i, j: (0, nt + j)),
            pl.BlockSpec((CONV_WIDTH, tn), lambda i, j: (0, j)),
            pl.BlockSpec((1, tn), lambda i, j: (0, j)),
            pl.BlockSpec((tn, d), lambda i, j: (j, 0)),
            pl.BlockSpec((1, d), lambda i, j: (0, 0)),
        ],
        out_specs=pl.BlockSpec((tm, d), lambda i, j: (i, 0)),
        out_shape=jax.ShapeDtypeStruct((t, d), jnp.float32),
        scratch_shapes=[
            pltpu.VMEM((tm + halo, d), jnp.bfloat16),
            pltpu.VMEM((tm + halo, tn), jnp.float32),
            pltpu.VMEM((tm, d), jnp.float32),
        ],
        compiler_params=_params(("parallel", "arbitrary")),
        name="ffn_final_norm",
    )(x2d, x2d, g.reshape(1, d), w_up, w_up, conv_w, conv_b.reshape(1, d_ff), w_down, g_final.reshape(1, d))


def _t5_bucket_np(dist):
    n = np.maximum(dist, 0)
    max_exact = REL_BUCKETS // 2
    nf = np.maximum(n, 1).astype(np.float32)
    large = max_exact + (np.log(nf / np.float32(max_exact)) / np.float32(math.log(REL_MAX_DIST / max_exact))
                         * np.float32(REL_BUCKETS - max_exact)).astype(np.int32)
    return np.where(n < max_exact, n, np.minimum(large, REL_BUCKETS - 1)).astype(np.int32)


def _bias_by_distance(rel_table, far_shift):
    vals = rel_table[jnp.asarray(_t5_bucket_np(np.arange(REL_MAX_DIST + 1)))]
    if far_shift:
        vals = vals - rel_table[REL_BUCKETS - 1]
    return vals * LOG2E


def _near_bias(rel_table):
    fd = _bias_by_distance(rel_table, True)
    h = fd.shape[1]
    lo = NEAR_KEYS - 1
    vec = jnp.concatenate([jnp.full((lo, h), NEG, fd.dtype), fd[:REL_MAX_DIST],
                           jnp.zeros((NEAR_KEYS - REL_MAX_DIST, h), fd.dtype)], axis=0)
    def toeplitz(start):
        period = Q_TILE + NEAR_KEYS - 1
        window = jnp.concatenate([vec[start:start + Q_TILE], vec[start - (NEAR_KEYS - 1):start]], axis=0)
        flat = jnp.tile(window, (NEAR_KEYS, 1))[:NEAR_KEYS * (period - 1)]
        return flat.reshape(NEAR_KEYS, period - 1, h)[:, :Q_TILE]

    tiles = jnp.stack([toeplitz(lo + a) for a in (0, Q_TILE)])
    tiles = tiles.reshape(2, NEAR_KEYS, Q_TILE, NSA_GROUPS, NSA_HPG).transpose(0, 3, 1, 4, 2)
    return tiles.reshape(2, NSA_GROUPS, NEAR_KEYS, ROWS)


def _band_bias(rel_table):
    fd = _bias_by_distance(rel_table, True)
    tl = np.arange(Q_TILE)[None, :]
    r = np.arange(BAND_ROWS)[:, None]
    dist = np.stack([tl - CMP_STRIDE * (r - off) - (CMP_BLOCK - 1) for off in (0, 8, 16)])
    vals = fd[jnp.asarray(np.clip(dist, 0, REL_MAX_DIST))]
    vals = jnp.where(jnp.asarray(dist >= 0)[..., None], vals, NEG)
    v, rr, q, _ = vals.shape
    return vals.transpose(0, 1, 3, 2).reshape(v, rr, NSA_GROUPS, NSA_HPG * q).transpose(0, 2, 1, 3)


def _overlap_t(n_cmp_pad, n_slc):
    i = np.arange(n_cmp_pad)[None, :]
    jj = np.arange(n_slc)[:, None]
    c_start = i * CMP_STRIDE
    ov = (c_start < jj * SLC_BLOCK + SLC_BLOCK) & (c_start + CMP_BLOCK - 1 >= jj * SLC_BLOCK)
    ov = ov & (i < n_cmp_pad - 1)
    return jnp.asarray(ov.astype(np.float32), jnp.bfloat16)


def _block_onehot(seq):
    blk = (np.arange(seq) // SLC_BLOCK) % LANES
    return jnp.asarray((blk[:, None] == np.arange(LANES)[None, :]).astype(np.float32), jnp.bfloat16)


def _pick_tile(n, pref):
    return pref if n % pref == 0 else n


def kernel(x, attn_norm_g, w_in, cmp_pos_k, cmp_w1_k, cmp_w2_k, cmp_pos_v, cmp_w1_v, cmp_w2_v,
           rel_bias_table, fox_forget_bias, w_branch_nsa, w_branch_fox, w_out,
           ffn_norm_g, w_up, conv_w, conv_b, w_down, final_norm_g):
    assert w_in.shape[0] == 1, "the final norm is fused into the single layer's FFN kernel"
    bsz, seq, d = x.shape
    t = bsz * seq
    bf = jnp.bfloat16
    scale = HEAD_DIM ** -0.5 * LOG2E
    x2d = x.reshape(t, d)
    w_in = w_in[0]

    o = np.cumsum([0, NSA_Q_DIM] + [NSA_KV_DIM] * 6 + [3 * NSA_HEADS, FOX_DIM, FOX_DIM, FOX_DIM, FOX_HEADS, d, d])
    w_qkv = jnp.concatenate([w_in[:, o[0]:o[1]] * scale, w_in[:, o[1]:o[7]],
                             w_in[:, o[8]:o[9]] * scale, w_in[:, o[9]:o[11]]], axis=1).astype(bf)
    gate_cols = []
    for grp in range(NSA_GROUPS):
        gate_cols += [w_in[:, o[7] + grp * GATES_PER_GROUP:o[7] + (grp + 1) * GATES_PER_GROUP],
                      jnp.zeros((d, LANES - GATES_PER_GROUP), w_in.dtype)]
    gate_cols += [w_in[:, o[11]:o[12]], jnp.zeros((d, IN_PROJ_TILE - N_GATE + LANES - FOX_HEADS), w_in.dtype)]
    w_gate = jnp.concatenate(gate_cols, axis=1).astype(bf)
    w_ma = w_in[:, o[12]:o[13]].astype(bf)
    w_mb = w_in[:, o[13]:o[14]].astype(bf)

    qkv, gates = _in_proj(x2d, attn_norm_g[0], jnp.concatenate([w_qkv, w_gate], axis=1),
                          _pick_tile(t, 1024), IN_PROJ_TILE)
    qkv = qkv.reshape(bsz, seq, N_QKV)
    gates = gates.reshape(bsz, seq, IN_PROJ_TILE)

    q_all = jnp.concatenate([qkv[:, :, :NSA_Q_DIM], qkv[:, :, CB_QF * HEAD_DIM:(CB_QF + FOX_HEADS) * HEAD_DIM]],
                            axis=-1)
    qt = q_all.reshape(bsz, seq, NSA_HEADS + FOX_HEADS, HEAD_DIM).transpose(0, 2, 3, 1)

    def heads_major(cb, heads):
        return qkv[:, :, cb * HEAD_DIM:(cb + heads) * HEAD_DIM].reshape(
            bsz, seq, heads, HEAD_DIM).transpose(0, 2, 1, 3)

    def values_t(cb, heads, piece):
        v = qkv[:, :, cb * HEAD_DIM:(cb + heads) * HEAD_DIM]
        v = v.reshape(bsz, seq // piece, piece, heads, HEAD_DIM).transpose(0, 3, 1, 4, 2)
        ones_row = (jnp.arange(ACC_ROWS - HEAD_DIM) == 0).astype(bf)[:, None]
        extra = jnp.broadcast_to(ones_row, v.shape[:3] + (ACC_ROWS - HEAD_DIM, piece))
        return jnp.concatenate([v, extra], axis=3)

    n_ch = seq // CMP_STRIDE

    def chunked(cb):
        sl = qkv[:, :, cb * HEAD_DIM:(cb + NSA_GROUPS) * HEAD_DIM]
        sl = sl.reshape(bsz, n_ch, CMP_STRIDE, NSA_GROUPS, HEAD_DIM).transpose(0, 3, 1, 2, 4)
        return sl.reshape(bsz, NSA_GROUPS, n_ch, CMP_STRIDE * HEAD_DIM)

    def posflat(pos):
        return jnp.broadcast_to(pos.reshape(1, CMP_BLOCK * HEAD_DIM), (8, CMP_BLOCK * HEAD_DIM)).astype(bf)

    kc = _compress(chunked(CB_KC), cmp_w1_k[0].astype(bf), posflat(cmp_pos_k[0]), cmp_w2_k[0].astype(bf))
    vc = _compress(chunked(CB_VC), cmp_w1_v[0].astype(bf), posflat(cmp_pos_v[0]), cmp_w2_v[0].astype(bf))
    vct = vc.transpose(0, 1, 3, 2)

    f_t = gates[:, :, NSA_GROUPS * LANES:NSA_GROUPS * LANES + FOX_HEADS]
    f_t = f_t.transpose(0, 2, 1).reshape(bsz * FOX_HEADS, seq)
    bias_col = jnp.tile(fox_forget_bias[0].astype(jnp.float32), bsz).reshape(bsz * FOX_HEADS, 1)
    terms = _decay_cumsum(f_t, bias_col, _pick_tile(seq, 2048))
    dec = jnp.pad(terms.transpose(1, 2, 0), ((0, 0), (0, 0), (0, LANES - DECAY_TERMS)))
    dec = dec.reshape(bsz, FOX_HEADS, seq, LANES)

    n_slc = seq // SLC_BLOCK
    near = _near_bias(rel_bias_table)
    gates_t = gates[:, :, :NSA_GROUPS * LANES].reshape(bsz, seq, NSA_GROUPS, LANES)[..., :GATE_ROWS]
    gates_t = gates_t.transpose(0, 2, 3, 1)
    ycw, pen = _cmp_win(qt, kc, vct, _band_bias(rel_bias_table), _overlap_t(n_ch, n_slc), qkv,
                        values_t(CB_VW, NSA_GROUPS, LANES), near, gates_t)
    ks = heads_major(CB_KS, NSA_GROUPS)
    kaug_sel = jnp.concatenate([ks, jnp.broadcast_to(_block_onehot(seq), ks.shape)], axis=-1)
    y_nsa = _sel(qt, pen, kaug_sel, values_t(CB_VS, NSA_GROUPS, _pick_tile(seq, KEY_CHUNK)),
                 values_t(CB_VS, NSA_GROUPS, LANES), near, ycw, gates_t)

    fox_tq = _pick_tile(seq, FOX_Q_TILE)
    y_fox = _fox(qkv, qt, dec, values_t(CB_VF, FOX_HEADS, fox_tq), fox_tq)

    tm2 = _pick_tile(t, 512)
    x_mid = _merge(x2d, attn_norm_g[0], w_ma, w_mb, w_branch_nsa[0].astype(bf), w_branch_fox[0].astype(bf),
                   w_out[0].astype(bf), y_nsa.reshape(t, NSA_Q_DIM), y_fox.reshape(t, FOX_DIM),
                   tm2, _pick_tile(d, 512))
    d_ff = w_down.shape[1]
    out = _ffn(x_mid, ffn_norm_g[0], w_up[0].astype(bf), conv_w[0], conv_b[0], w_down[0].astype(bf),
               final_norm_g, seq, tm2, _pick_tile(d_ff, 512))
    return out.reshape(bsz, seq, d)
```

```python
import functools
import math

import jax
import jax.numpy as jnp
import numpy as np
from jax import lax
from jax.experimental import pallas as pl
from jax.experimental.pallas import tpu as pltpu

HEAD_DIM = 128
NSA_HEADS = 8
NSA_GROUPS = 2
NSA_HPG = NSA_HEADS // NSA_GROUPS
FOX_HEADS = 8
CMP_BLOCK = 32
CMP_STRIDE = 16
SLC_BLOCK = 64
SLC_TOPK = 16
WINDOW = 512
REL_BUCKETS = 32
REL_MAX_DIST = 128
CONV_WIDTH = 3
EPS = 1e-6
NEG = -1e30
FORCED_SCORE = 1e4
LOG2E = math.log2(math.e)

LANES = 128
BF16_SUBLANES = 16
VMEM_LIMIT = 56 * 1024 * 1024

NSA_Q_DIM = NSA_HEADS * HEAD_DIM
NSA_KV_DIM = NSA_GROUPS * HEAD_DIM
FOX_DIM = FOX_HEADS * HEAD_DIM
N_QKV = NSA_Q_DIM + 6 * NSA_KV_DIM + 3 * FOX_DIM
GATES_PER_GROUP = 3 * NSA_HPG
N_GATE = (NSA_GROUPS + 1) * LANES
IN_PROJ_TILE = 512

CB_KC = 8
CB_VC = 10
CB_KS = 12
CB_VS = 14
CB_KW = 16
CB_VW = 18
CB_QF = 20
CB_KF = 28
CB_VF = 36

Q_TILE = 128
ROWS = NSA_HPG * Q_TILE
KEY_CHUNK = 512
FOX_Q_TILE = 512
N_SPLIT = 2
SKIP_MARGIN_LOG2 = 64.0
ROW_BLOCK = 32
DECAY_TERMS = 3
BAND_ROWS = 24
CMP_TIERS = 4
WIN_KEYS = WINDOW + Q_TILE
NEAR_KEYS = 2 * Q_TILE
GATE_ROWS = 16
SUM_ROW = HEAD_DIM
ACC_ROWS = HEAD_DIM + 8


def _params(sem):
    return pltpu.CompilerParams(dimension_semantics=sem, vmem_limit_bytes=VMEM_LIMIT)


def _norm_rows(x, g):
    return (x * lax.rsqrt(jnp.mean(x * x, axis=-1, keepdims=True) + EPS)) * g


def _in_proj_kernel(x_ref, g_ref, w_ref, qkv_ref, gate_ref, h_ref):
    j = pl.program_id(1)
    last = pl.num_programs(1) - 1

    @pl.when(j == 0)
    def _():
        h_ref[...] = _norm_rows(x_ref[...], g_ref[...]).astype(h_ref.dtype)

    y = jnp.dot(h_ref[...], w_ref[...], preferred_element_type=jnp.float32)

    @pl.when(j < last)
    def _():
        qkv_ref[...] = y.astype(qkv_ref.dtype)

    @pl.when(j == last)
    def _():
        gate_ref[...] = y


def _in_proj(x2d, g, w, tm, tn):
    t, d = x2d.shape
    n_tiles = w.shape[1] // tn
    return pl.pallas_call(
        _in_proj_kernel,
        grid=(t // tm, n_tiles),
        in_specs=[
            pl.BlockSpec((tm, d), lambda i, j: (i, 0)),
            pl.BlockSpec((1, d), lambda i, j: (0, 0)),
            pl.BlockSpec((d, tn), lambda i, j: (0, j)),
        ],
        out_specs=[
            pl.BlockSpec((tm, tn), lambda i, j: (i, jnp.minimum(j, n_tiles - 2))),
            pl.BlockSpec((tm, tn), lambda i, j: (i, 0)),
        ],
        out_shape=[
            jax.ShapeDtypeStruct((t, (n_tiles - 1) * tn), jnp.bfloat16),
            jax.ShapeDtypeStruct((t, tn), jnp.float32),
        ],
        scratch_shapes=[pltpu.VMEM((tm, d), jnp.bfloat16)],
        compiler_params=_params(("parallel", "arbitrary")),
        name="norm_in_proj",
    )(x2d, g.reshape(1, d), w)


def _compress_kernel(ch_ref, w1_ref, posf_ref, w2_ref, o_ref):
    half = ch_ref.shape[1]
    ch = ch_ref[...]
    pa = jnp.dot(ch, w1_ref[:half, :], preferred_element_type=jnp.float32)
    pb = jnp.dot(ch, w1_ref[half:, :], preferred_element_type=jnp.float32)
    pos = jnp.dot(posf_ref[...], w1_ref[...], preferred_element_type=jnp.float32)[0:1, :]
    n = pa.shape[0]
    pre = pa + pltpu.roll(pb, n - 1, 0) + pos
    act = jax.nn.gelu(pre)
    o_ref[...] = jnp.dot(act.astype(jnp.bfloat16), w2_ref[...],
                         preferred_element_type=jnp.float32).astype(o_ref.dtype)


def _compress(chunks, w1, posf, w2):
    b, g, n, k = chunks.shape
    return pl.pallas_call(
        _compress_kernel,
        grid=(b, g),
        in_specs=[
            pl.BlockSpec((None, None, n, k), lambda i, j: (i, j, 0, 0)),
            pl.BlockSpec(w1.shape, lambda i, j: (0, 0)),
            pl.BlockSpec(posf.shape, lambda i, j: (0, 0)),
            pl.BlockSpec(w2.shape, lambda i, j: (0, 0)),
        ],
        out_specs=pl.BlockSpec((None, None, n, HEAD_DIM), lambda i, j: (i, j, 0, 0)),
        out_shape=jax.ShapeDtypeStruct((b, g, n, HEAD_DIM), jnp.bfloat16),
        compiler_params=_params(("parallel", "parallel")),
        name="compress_tokens",
    )(chunks, w1, posf, w2)


def _decay_kernel(f_ref, b_ref, tri_ref, o_ref, carry_ref):
    @pl.when(pl.program_id(0) == 0)
    def _():
        carry_ref[...] = jnp.zeros_like(carry_ref)

    x = f_ref[...] + b_ref[...]
    logf = (jnp.minimum(x, 0.0) - jnp.log1p(jnp.exp(-jnp.abs(x)))) * LOG2E
    carry = carry_ref[...]
    for seg in range(f_ref.shape[1] // LANES):
        part = jnp.dot(logf[:, seg * LANES:(seg + 1) * LANES], tri_ref[...],
                       preferred_element_type=jnp.float32, precision=lax.Precision.HIGHEST) + carry
        carry = part[:, LANES - 1:LANES]
        rest = part
        for term in range(DECAY_TERMS):
            piece = rest.astype(o_ref.dtype)
            o_ref[term, :, seg * LANES:(seg + 1) * LANES] = piece
            rest = rest - piece.astype(jnp.float32)
    carry_ref[...] = carry


def _decay_cumsum(f_t, bias_col, width):
    rows, s = f_t.shape
    tri = jnp.asarray(np.triu(np.ones((LANES, LANES), np.float32)))
    return pl.pallas_call(
        _decay_kernel,
        grid=(s // width,),
        in_specs=[
            pl.BlockSpec((rows, width), lambda i: (0, i)),
            pl.BlockSpec((rows, 1), lambda i: (0, 0)),
            pl.BlockSpec((LANES, LANES), lambda i: (0, 0)),
        ],
        out_specs=pl.BlockSpec((DECAY_TERMS, rows, width), lambda i: (0, 0, i)),
        out_shape=jax.ShapeDtypeStruct((DECAY_TERMS, rows, s), jnp.bfloat16),
        scratch_shapes=[pltpu.VMEM((rows, 1), jnp.float32)],
        compiler_params=_params(("arbitrary",)),
        name="decay_cumsum",
    )(f_t, bias_col, tri)


def _col_reduce(op, x):
    reduce = {jnp.maximum: jnp.max, jnp.minimum: jnp.min, jnp.add: jnp.sum}[op]
    return reduce(x, axis=0, keepdims=True)


def _flash_init(m_ref, acc_ref):
    m_ref[...] = jnp.full(m_ref.shape, NEG, jnp.float32)
    acc_ref[...] = jnp.zeros(acc_ref.shape, jnp.float32)


def _probabilities(s, m_new):
    return jnp.exp2((s - m_new).astype(jnp.bfloat16))


def _flash_result(acc_ref):
    return acc_ref[0:HEAD_DIM, :] / acc_ref[SUM_ROW:SUM_ROW + 1, :]


def _flash_step(kaug, vt, get_q, width, m_ref, acc_ref, extra=None):
    w = width // N_SPLIT
    strips = [(i * w, (i + 1) * w) for i in range(N_SPLIT)]
    scores = [jnp.dot(kaug, get_q(lo, hi), preferred_element_type=jnp.float32) for lo, hi in strips]
    for (lo, hi), s in zip(strips, scores):
        if extra is not None:
            s = extra(s, lo, hi)
        m_prev = m_ref[:, lo:hi]
        m_new = jnp.maximum(m_prev, _col_reduce(jnp.maximum, s))
        alpha = jnp.exp2(m_prev - m_new)
        acc_ref[:, lo:hi] = alpha * acc_ref[:, lo:hi] + jnp.dot(
            vt, _probabilities(s, m_new), preferred_element_type=jnp.float32)
        m_ref[:, lo:hi] = m_new


def _flash_far_chunks(n, load_keys, load_values, get_q_for, width, m_ref, acc_ref, s_ref, mx_ref, p_ref):
    w = width // N_SPLIT
    strips = [(i * w, (i + 1) * w) for i in range(N_SPLIT)]
    base = n % 2
    pairs = n // 2

    def scores_into(c, slot):
        kaug = load_keys(c)
        get_q = get_q_for(c)
        for lo, hi in strips:
            s = jnp.dot(kaug, get_q(lo, hi), preferred_element_type=jnp.float32)
            s_ref[slot, :, lo:hi] = s
            mx_ref[slot, :, lo:hi] = _col_reduce(jnp.maximum, s)

    def value_product(c, slot):
        vt = load_values(c)
        return jnp.concatenate([jnp.dot(vt, p_ref[slot, :, lo:hi], preferred_element_type=jnp.float32)
                                for lo, hi in strips], axis=1)

    @pl.when(base == 1)
    def _():
        _flash_step(load_keys(0), load_values(0), get_q_for(0), width, m_ref, acc_ref)

    @pl.when(pairs > 0)
    def _():
        scores_into(base, 0)
        p_ref[1] = jnp.zeros(p_ref.shape[1:], p_ref.dtype)

        def pair(j, carry):
            for cur in (0, 1):
                c = base + 2 * j + cur
                nxt = 1 - cur
                pv = value_product(jnp.maximum(c - 1, base), nxt)
                scores_into(jnp.minimum(c + 1, n - 1), nxt)
                alphas = []
                for lo, hi in strips:
                    m_prev = m_ref[:, lo:hi]
                    m_new = jnp.maximum(m_prev, mx_ref[cur, :, lo:hi])
                    m_ref[:, lo:hi] = m_new
                    for r in range(0, s_ref.shape[1], ROW_BLOCK):
                        p_ref[cur, r:r + ROW_BLOCK, lo:hi] = _probabilities(
                            s_ref[cur, r:r + ROW_BLOCK, lo:hi], m_new)
                    alphas.append(jnp.exp2(m_prev - m_new))
                acc_ref[...] = jnp.concatenate(alphas, axis=1) * (acc_ref[...] + pv)
            return carry

        lax.fori_loop(0, pairs, pair, 0)
        acc_ref[...] = acc_ref[...] + value_product(n - 1, 1)


def _group_queries_t(qt_ref):
    return jnp.concatenate([qt_ref[h] for h in range(NSA_HPG)], axis=1)


def _branch_gate(gates_t, head, branch):
    row = head * 3 + branch
    return gates_t[row:row + 1, :]


def _values_t(vt_ref, k0, n_keys):
    p0 = k0 // LANES
    return jnp.concatenate([vt_ref[p0 + j] for j in range(n_keys // LANES)], axis=1)


def _compressed_branch(n_rows, i0, q4t, kc_ref, vct_ref, band_ref, ovt_ref, sc_ref, oct_ref, imp_ref):
    sc_ref[0:n_rows, :] = jnp.dot(kc_ref[0:n_rows, :], q4t, preferred_element_type=jnp.float32)
    sc_ref[pl.ds(i0, BAND_ROWS), :] = sc_ref[pl.ds(i0, BAND_ROWS), :] + band_ref[...]
    row = lax.broadcasted_iota(jnp.int32, (n_rows, ROWS), 0)
    sc = jnp.where(row < i0 + BAND_ROWS, sc_ref[0:n_rows, :], NEG)
    m = _col_reduce(jnp.maximum, sc)
    p = jnp.exp2(sc - m)
    l = _col_reduce(jnp.add, p)
    pn = p * jnp.where(m > 0.5 * NEG, 1.0 / l, 0.0)
    oct_ref[...] = jnp.dot(vct_ref[:, 0:n_rows], pn.astype(jnp.bfloat16), preferred_element_type=jnp.float32)
    psum = pn[:, 0:Q_TILE]
    for h in range(1, NSA_HPG):
        psum = psum + pn[:, h * Q_TILE:(h + 1) * Q_TILE]
    imp_ref[...] = jnp.dot(ovt_ref[:, 0:n_rows], psum.astype(jnp.bfloat16), preferred_element_type=jnp.float32)


def _cmp_win_kernel(qt_ref, kc_ref, vct_ref, band_ref, ovt_ref, kw_ref, vwt_ref, near_ref, gt_ref,
                    ycw_ref, pen_ref, sc_ref, sw_ref, oct_ref, imp_ref):
    qb = pl.program_id(2)
    n_cmp = kc_ref.shape[0]
    n_slc = ovt_ref.shape[0]
    q4t = _group_queries_t(qt_ref)

    i0 = pl.multiple_of(jnp.maximum(8 * qb - 16, 0), 8)
    n_tiers = CMP_TIERS if n_cmp % (CMP_TIERS * LANES) == 0 else 1
    step = n_cmp // n_tiers
    for tier in range(1, n_tiers + 1):
        in_tier = (i0 + BAND_ROWS <= tier * step) & (i0 + BAND_ROWS > (tier - 1) * step)
        pl.when(in_tier)(functools.partial(_compressed_branch, tier * step, i0, q4t, kc_ref, vct_ref,
                                           band_ref, ovt_ref, sc_ref, oct_ref, imp_ref))
    oct_ = oct_ref[...]
    imp = imp_ref[...]

    ji = lax.broadcasted_iota(jnp.int32, (n_slc, Q_TILE), 0)
    jf = ji.astype(jnp.float32)
    t = qb * Q_TILE + lax.broadcasted_iota(jnp.int32, (n_slc, Q_TILE), 1)
    cur = t // SLC_BLOCK
    forced = (ji == 0) | (ji == cur) | (ji == cur - 1)
    score = jnp.where(ji <= cur, jnp.where(forced, FORCED_SCORE, imp), -1.0)
    pen_t = jnp.full((n_slc, Q_TILE), NEG, jnp.float32)
    for _ in range(min(SLC_TOPK, n_slc)):
        mx = _col_reduce(jnp.maximum, score)
        idx = _col_reduce(jnp.minimum, jnp.where(score == mx, jf, float(n_slc)))
        pick = jf == idx
        pen_t = jnp.where(pick, 0.0, pen_t)
        score = jnp.where(pick, -2.0, score)
    pad = pen_ref.shape[0] - n_slc
    if pad:
        pen_t = jnp.concatenate([pen_t, jnp.full((pad, Q_TILE), NEG, jnp.float32)], axis=0)
    pen_ref[...] = pen_t.astype(pen_ref.dtype)

    t0 = qb * Q_TILE
    ws = pl.multiple_of(jnp.maximum(t0 - WINDOW, 0), Q_TILE)
    ns = pl.multiple_of(jnp.maximum(t0 - Q_TILE, 0), Q_TILE)
    sw_ref[...] = jnp.dot(kw_ref[pl.ds(ws, WIN_KEYS), :], q4t, preferred_element_type=jnp.float32)
    off = pl.multiple_of(ns - ws, Q_TILE)
    sw_ref[pl.ds(off, NEAR_KEYS), :] = sw_ref[pl.ds(off, NEAR_KEYS), :] + near_ref[...]
    key = ws + lax.broadcasted_iota(jnp.int32, (WIN_KEYS, ROWS), 0)
    qry = t0 + (lax.broadcasted_iota(jnp.int32, (WIN_KEYS, ROWS), 1) & (Q_TILE - 1))
    sw = jnp.where((key <= qry) & (key > qry - WINDOW), sw_ref[...], NEG)
    mw = _col_reduce(jnp.maximum, sw)
    ow_sum = jnp.dot(_values_t(vwt_ref, ws, WIN_KEYS), _probabilities(sw, mw),
                     preferred_element_type=jnp.float32)
    owt = ow_sum[0:HEAD_DIM, :] / ow_sum[SUM_ROW:SUM_ROW + 1, :]

    gates_t = jax.nn.sigmoid(gt_ref[...])
    for h in range(NSA_HPG):
        cols = slice(h * Q_TILE, (h + 1) * Q_TILE)
        mixed = _branch_gate(gates_t, h, 0) * oct_[:, cols] + _branch_gate(gates_t, h, 2) * owt[:, cols]
        ycw_ref[:, h * HEAD_DIM:(h + 1) * HEAD_DIM] = mixed.T


def _cmp_win(qt, kc, vct, band, ovt, qkv, vwt, near, gates_t):
    b, s, _ = qkv.shape
    n_cmp = kc.shape[2]
    n_slc = ovt.shape[0]
    nq = s // Q_TILE
    pen_w = -(-n_slc // LANES) * LANES
    n_band = band.shape[0] - 1
    return pl.pallas_call(
        _cmp_win_kernel,
        grid=(b, NSA_GROUPS, nq),
        in_specs=[
            pl.BlockSpec((None, NSA_HPG, HEAD_DIM, Q_TILE), lambda i, g, q: (i, g, 0, q)),
            pl.BlockSpec((None, None, n_cmp, HEAD_DIM), lambda i, g, q: (i, g, 0, 0)),
            pl.BlockSpec((None, None, HEAD_DIM, n_cmp), lambda i, g, q: (i, g, 0, 0)),
            pl.BlockSpec((None, None, BAND_ROWS, ROWS), lambda i, g, q: (jnp.minimum(q, n_band), g, 0, 0)),
            pl.BlockSpec((n_slc, n_cmp), lambda i, g, q: (0, 0)),
            pl.BlockSpec((None, s, HEAD_DIM), lambda i, g, q: (i, 0, CB_KW + g)),
            pl.BlockSpec((None, None, s // LANES, ACC_ROWS, LANES), lambda i, g, q: (i, g, 0, 0, 0)),
            pl.BlockSpec((None, None, NEAR_KEYS, ROWS), lambda i, g, q: (jnp.minimum(q, 1), g, 0, 0)),
            pl.BlockSpec((None, None, GATE_ROWS, Q_TILE), lambda i, g, q: (i, g, 0, q)),
        ],
        out_specs=[
            pl.BlockSpec((None, Q_TILE, NSA_HPG * HEAD_DIM), lambda i, g, q: (i, q, g)),
            pl.BlockSpec((None, None, pen_w, Q_TILE), lambda i, g, q: (i, g, 0, q)),
        ],
        out_shape=[
            jax.ShapeDtypeStruct((b, s, NSA_Q_DIM), jnp.float32),
            jax.ShapeDtypeStruct((b, NSA_GROUPS, pen_w, s), jnp.bfloat16),
        ],
        scratch_shapes=[pltpu.VMEM((n_cmp, ROWS), jnp.float32), pltpu.VMEM((WIN_KEYS, ROWS), jnp.float32),
                        pltpu.VMEM((HEAD_DIM, ROWS), jnp.float32), pltpu.VMEM((n_slc, Q_TILE), jnp.float32)],
        compiler_params=_params(("parallel", "parallel", "arbitrary")),
        name="nsa_cmp_topk_win",
    )(qt, kc, vct, band, ovt, qkv, vwt, near, gates_t)


def _range_penalty(pen, first_blk, lo_ok, hi_ok):
    n_half = pen.shape[0] // LANES
    c = lax.broadcasted_iota(jnp.int32, (LANES, pen.shape[1]), 0)
    out = pen[0:LANES, :]
    blk = c
    for hf in range(1, n_half):
        in_lower = (c + (hf - 1) * LANES >= first_blk) & (first_blk < hf * LANES)
        out = jnp.where(in_lower, out, pen[hf * LANES:(hf + 1) * LANES, :])
        blk = jnp.where(in_lower, blk, c + hf * LANES)
    return jnp.where((blk >= lo_ok) & (blk < hi_ok), out, jnp.asarray(NEG, out.dtype))


def _sel_kernel(qt_ref, pen_ref, kaug_ref, vtc_ref, vtp_ref, near_ref, ycw_ref, gt_ref, y_ref,
                qaug_ref, m_ref, acc_ref, s_ref, mx_ref, p_ref):
    qb = pl.program_id(2)
    n_half = pen_ref.shape[0] // LANES
    n_blocks = pen_ref.shape[0]
    chunks_per_half = LANES * SLC_BLOCK // KEY_CHUNK
    blocks_per_chunk = KEY_CHUNK // SLC_BLOCK

    t0 = qb * Q_TILE
    near_end = t0 - Q_TILE
    nf = jnp.maximum(near_end // KEY_CHUNK, 0)
    ms = pl.multiple_of(jnp.maximum(near_end - KEY_CHUNK, 0), Q_TILE)
    ns = pl.multiple_of(jnp.maximum(near_end, 0), Q_TILE)

    q4t = _group_queries_t(qt_ref)
    pen = pen_ref[...]
    operands = [pen[hf * LANES:(hf + 1) * LANES, :] for hf in range(n_half)]
    operands.append(_range_penalty(pen, ms // SLC_BLOCK, nf * blocks_per_chunk, near_end // SLC_BLOCK))
    operands.append(_range_penalty(pen, ns // SLC_BLOCK, 0, n_blocks))
    for idx, channels in enumerate(operands):
        qaug_ref[idx, 0:HEAD_DIM, :] = q4t
        qaug_ref[idx, HEAD_DIM:, :] = jnp.concatenate([channels] * NSA_HPG, axis=1)

    _flash_init(m_ref, acc_ref)

    def get_q_for(c):
        half = c // chunks_per_half
        return lambda lo, hi: qaug_ref[half, :, lo:hi]

    _flash_far_chunks(nf, lambda c: kaug_ref[pl.ds(pl.multiple_of(c * KEY_CHUNK, KEY_CHUNK), KEY_CHUNK), :],
                      lambda c: vtc_ref[c], get_q_for, ROWS, m_ref, acc_ref, s_ref, mx_ref, p_ref)

    _flash_step(kaug_ref[pl.ds(ms, KEY_CHUNK), :], _values_t(vtp_ref, ms, KEY_CHUNK),
                lambda lo, hi: qaug_ref[n_half, :, lo:hi], ROWS, m_ref, acc_ref)
    _flash_step(kaug_ref[pl.ds(ns, NEAR_KEYS), :], _values_t(vtp_ref, ns, NEAR_KEYS),
                lambda lo, hi: qaug_ref[n_half + 1, :, lo:hi], ROWS, m_ref, acc_ref,
                extra=lambda s, lo, hi: s + near_ref[:, lo:hi])

    o = _flash_result(acc_ref)
    gates_t = jax.nn.sigmoid(gt_ref[...])
    for h in range(NSA_HPG):
        sel_h = (_branch_gate(gates_t, h, 1) * o[:, h * Q_TILE:(h + 1) * Q_TILE]).T
        y_ref[:, h * HEAD_DIM:(h + 1) * HEAD_DIM] = (
            ycw_ref[:, h * HEAD_DIM:(h + 1) * HEAD_DIM] + sel_h).astype(y_ref.dtype)


def _sel(qt, pen, kaug, vt_chunks, vt_pieces, near, ycw, gates_t):
    b, _, s, _ = kaug.shape
    nq = s // Q_TILE
    pen_w = pen.shape[2]
    return pl.pallas_call(
        _sel_kernel,
        grid=(b, NSA_GROUPS, nq),
        in_specs=[
            pl.BlockSpec((None, NSA_HPG, HEAD_DIM, Q_TILE), lambda i, g, q: (i, g, 0, q)),
            pl.BlockSpec((None, None, pen_w, Q_TILE), lambda i, g, q: (i, g, 0, q)),
            pl.BlockSpec((None, None, s, 2 * HEAD_DIM), lambda i, g, q: (i, g, 0, 0)),
            pl.BlockSpec((None, None, s // KEY_CHUNK, ACC_ROWS, KEY_CHUNK), lambda i, g, q: (i, g, 0, 0, 0)),
            pl.BlockSpec((None, None, s // LANES, ACC_ROWS, LANES), lambda i, g, q: (i, g, 0, 0, 0)),
            pl.BlockSpec((None, None, NEAR_KEYS, ROWS), lambda i, g, q: (jnp.minimum(q, 1), g, 0, 0)),
            pl.BlockSpec((None, Q_TILE, NSA_HPG * HEAD_DIM), lambda i, g, q: (i, q, g)),
            pl.BlockSpec((None, None, GATE_ROWS, Q_TILE), lambda i, g, q: (i, g, 0, q)),
        ],
        out_specs=pl.BlockSpec((None, Q_TILE, NSA_HPG * HEAD_DIM), lambda i, g, q: (i, q, g)),
        out_shape=jax.ShapeDtypeStruct((b, s, NSA_Q_DIM), jnp.bfloat16),
        scratch_shapes=[
            pltpu.VMEM((pen_w // LANES + 2, 2 * HEAD_DIM, ROWS), jnp.bfloat16),
            pltpu.VMEM((1, ROWS), jnp.float32),
            pltpu.VMEM((ACC_ROWS, ROWS), jnp.float32),
            pltpu.VMEM((2, KEY_CHUNK, ROWS), jnp.float32),
            pltpu.VMEM((2, 1, ROWS), jnp.float32),
            pltpu.VMEM((2, KEY_CHUNK, ROWS), jnp.bfloat16),
        ],
        compiler_params=_params(("parallel", "parallel", "arbitrary")),
        name="nsa_selected",
    )(qt, pen, kaug, vt_chunks, vt_pieces, near, ycw, gates_t)


def _fox_kernel(qk_ref, nce_ref, qt_ref, k_ref, dec_ref, vt_ref, o_ref,
                qaug_ref, m_ref, acc_ref, s_ref, mx_ref, p_ref):
    qi = pl.program_id(2)
    bh = pl.program_id(0) * pl.num_programs(1) + pl.program_id(1)
    tq = qt_ref.shape[1]
    row = lax.broadcasted_iota(jnp.int32, (HEAD_DIM, tq), 0)
    qaug_ref[0:HEAD_DIM, :] = qt_ref[...]
    qaug_ref[HEAD_DIM:, :] = jnp.where(row < DECAY_TERMS, -1.0, 0.0).astype(qaug_ref.dtype)
    _flash_init(m_ref, acc_ref)

    def get_q(lo, hi):
        return qaug_ref[:, lo:hi]

    def load_keys(c):
        k0 = pl.multiple_of(c * tq, tq)
        return jnp.concatenate([k_ref[pl.ds(k0, tq), :], dec_ref[pl.ds(k0, tq), :]], axis=1)

    def causal(s, lo, hi):
        key = lax.broadcasted_iota(jnp.int32, s.shape, 0)
        qry = lo + lax.broadcasted_iota(jnp.int32, s.shape, 1)
        return jnp.where(key <= qry, s, NEG)

    _flash_step(load_keys(qi), vt_ref[qi], get_q, tq, m_ref, acc_ref, extra=causal)

    m_low = jnp.min(m_ref[...])
    reach = qk_ref[bh, qi] + SKIP_MARGIN_LOG2

    def count(c, n_skip):
        return n_skip + jnp.where(reach + nce_ref[bh, c] < m_low, 1, 0)

    first = lax.fori_loop(0, qi, count, jnp.int32(0))
    _flash_far_chunks(qi - first, lambda c: load_keys(c + first), lambda c: vt_ref[c + first],
                      lambda c: get_q, tq, m_ref, acc_ref, s_ref, mx_ref, p_ref)
    o_ref[...] = _flash_result(acc_ref).T.astype(o_ref.dtype)


def _fox(qk_bound, neg_cum_end, qkv, qt, dec, vt, tq):
    b, s, _ = qkv.shape
    return pl.pallas_call(
        _fox_kernel,
        grid=(b, FOX_HEADS, s // tq),
        in_specs=[
            pl.BlockSpec(memory_space=pltpu.SMEM),
            pl.BlockSpec(memory_space=pltpu.SMEM),
            pl.BlockSpec((None, None, HEAD_DIM, tq), lambda i, h, q: (i, NSA_HEADS + h, 0, q)),
            pl.BlockSpec((None, s, HEAD_DIM), lambda i, h, q: (i, 0, CB_KF + h)),
            pl.BlockSpec((None, None, s, LANES), lambda i, h, q: (i, h, 0, 0)),
            pl.BlockSpec((None, None, s // tq, ACC_ROWS, tq), lambda i, h, q: (i, h, 0, 0, 0)),
        ],
        out_specs=pl.BlockSpec((None, tq, HEAD_DIM), lambda i, h, q: (i, q, h)),
        out_shape=jax.ShapeDtypeStruct((b, s, FOX_DIM), jnp.bfloat16),
        scratch_shapes=[
            pltpu.VMEM((2 * HEAD_DIM, tq), jnp.bfloat16),
            pltpu.VMEM((1, tq), jnp.float32),
            pltpu.VMEM((ACC_ROWS, tq), jnp.float32),
            pltpu.VMEM((2, tq, tq), jnp.float32),
            pltpu.VMEM((2, 1, tq), jnp.float32),
            pltpu.VMEM((2, tq, tq), jnp.bfloat16),
        ],
        compiler_params=_params(("parallel", "parallel", "arbitrary")),
        name="fox_attention",
    )(qk_bound, neg_cum_end, qt, qkv, dec, vt)


def _merge_kernel(x_ref, g_ref, wa_ref, wb_ref, pa_ref, pb_ref, wo_ref, ya_ref, yb_ref, o_ref,
                  h_ref, acc_ref):
    j = pl.program_id(1)

    @pl.when(j == 0)
    def _():
        h_ref[...] = _norm_rows(x_ref[...], g_ref[...]).astype(h_ref.dtype)
        acc_ref[...] = jnp.zeros_like(acc_ref)

    h = h_ref[...]
    ga = jax.nn.sigmoid(jnp.dot(h, wa_ref[...], preferred_element_type=jnp.float32))
    gb = jax.nn.sigmoid(jnp.dot(h, wb_ref[...], preferred_element_type=jnp.float32))
    a = jnp.dot(ya_ref[...], pa_ref[...], preferred_element_type=jnp.float32)
    bb = jnp.dot(yb_ref[...], pb_ref[...], preferred_element_type=jnp.float32)
    merged = (ga * a + gb * bb).astype(jnp.bfloat16)
    acc_ref[...] += jnp.dot(merged, wo_ref[...], preferred_element_type=jnp.float32)

    @pl.when(j == pl.num_programs(1) - 1)
    def _():
        o_ref[...] = x_ref[...] + acc_ref[...]


def _merge(x2d, g, w_ma, w_mb, p_a, p_b, w_out, y_a, y_b, tm, tn):
    t, d = x2d.shape
    ka = p_a.shape[0]
    kb = p_b.shape[0]
    return pl.pallas_call(
        _merge_kernel,
        grid=(t // tm, d // tn),
        in_specs=[
            pl.BlockSpec((tm, d), lambda i, j: (i, 0)),
            pl.BlockSpec((1, d), lambda i, j: (0, 0)),
            pl.BlockSpec((d, tn), lambda i, j: (0, j)),
            pl.BlockSpec((d, tn), lambda i, j: (0, j)),
            pl.BlockSpec((ka, tn), lambda i, j: (0, j)),
            pl.BlockSpec((kb, tn), lambda i, j: (0, j)),
            pl.BlockSpec((tn, d), lambda i, j: (j, 0)),
            pl.BlockSpec((tm, ka), lambda i, j: (i, 0)),
            pl.BlockSpec((tm, kb), lambda i, j: (i, 0)),
        ],
        out_specs=pl.BlockSpec((tm, d), lambda i, j: (i, 0)),
        out_shape=jax.ShapeDtypeStruct((t, d), jnp.float32),
        scratch_shapes=[pltpu.VMEM((tm, d), jnp.bfloat16), pltpu.VMEM((tm, d), jnp.float32)],
        compiler_params=_params(("parallel", "arbitrary")),
        name="merge_out_proj",
    )(x2d, g.reshape(1, d), w_ma, w_mb, p_a, p_b, w_out, y_a, y_b)


def _ffn_kernel(x_ref, xh_ref, g_ref, wu_ref, wv_ref, cw_ref, cb_ref, wd_ref, gf_ref, o_ref,
                h_ref, u_ref, acc_ref, *, seq, tm):
    i = pl.program_id(0)
    j = pl.program_id(1)
    halo = BF16_SUBLANES

    @pl.when(j == 0)
    def _():
        g = g_ref[...]
        keep = jnp.where((i * tm) % seq == 0, 0.0, 1.0)
        h_ref[0:halo, :] = (_norm_rows(xh_ref[...], g) * keep).astype(h_ref.dtype)
        h_ref[halo:, :] = _norm_rows(x_ref[...], g).astype(h_ref.dtype)
        acc_ref[...] = jnp.zeros_like(acc_ref)

    u_ref[...] = jnp.dot(h_ref[...], wu_ref[...], preferred_element_type=jnp.float32)
    v = jnp.dot(h_ref[halo:, :], wv_ref[...], preferred_element_type=jnp.float32)
    conv = cb_ref[...]
    for k in range(CONV_WIDTH):
        conv = conv + cw_ref[k:k + 1, :] * u_ref[pl.ds(halo - (CONV_WIDTH - 1) + k, tm), :]
    act = (jax.nn.gelu(conv) * v).astype(jnp.bfloat16)
    acc_ref[...] += jnp.dot(act, wd_ref[...], preferred_element_type=jnp.float32)

    @pl.when(j == pl.num_programs(1) - 1)
    def _():
        o_ref[...] = _norm_rows(x_ref[...] + acc_ref[...], gf_ref[...])


def _ffn(x2d, g, w_up, conv_w, conv_b, w_down, g_final, seq, tm, tn):
    t, d = x2d.shape
    d_ff = w_down.shape[0]
    nt = d_ff // tn
    halo = BF16_SUBLANES
    per = tm // halo
    return pl.pallas_call(
        functools.partial(_ffn_kernel, seq=seq, tm=tm),
        grid=(t // tm, nt),
        in_specs=[
            pl.BlockSpec((tm, d), lambda i, j: (i, 0)),
            pl.BlockSpec((halo, d), lambda i, j: (jnp.maximum(i * per - 1, 0), 0)),
            pl.BlockSpec((1, d), lambda i, j: (0, 0)),
            pl.BlockSpec((d, tn), lambda i, j: (0, j)),
            pl.BlockSpec((d, tn), lambda i, j: (0, nt + j)),
            pl.BlockSpec((CONV_WIDTH, tn), lambda i, j: (0, j)),
            pl.BlockSpec((1, tn), lambda i, j: (0, j)),
            pl.BlockSpec((tn, d), lambda i, j: (j, 0)),
            pl.BlockSpec((1, d), lambda i, j: (0, 0)),
        ],
        out_specs=pl.BlockSpec((tm, d), lambda i, j: (i, 0)),
        out_shape=jax.ShapeDtypeStruct((t, d), jnp.float32),
        scratch_shapes=[
            pltpu.VMEM((tm + halo, d), jnp.bfloat16),
            pltpu.VMEM((tm + halo, tn), jnp.float32),
            pltpu.VMEM((tm, d), jnp.float32),
        ],
        compiler_params=_params(("parallel", "arbitrary")),
        name="ffn_final_norm",
    )(x2d, x2d, g.reshape(1, d), w_up, w_up, conv_w, conv_b.reshape(1, d_ff), w_down, g_final.reshape(1, d))


def _t5_bucket_np(dist):
    n = np.maximum(dist, 0)
    max_exact = REL_BUCKETS // 2
    nf = np.maximum(n, 1).astype(np.float32)
    large = max_exact + (np.log(nf / np.float32(max_exact)) / np.float32(math.log(REL_MAX_DIST / max_exact))
                         * np.float32(REL_BUCKETS - max_exact)).astype(np.int32)
    return np.where(n < max_exact, n, np.minimum(large, REL_BUCKETS - 1)).astype(np.int32)


def _bias_by_distance(rel_table, far_shift):
    vals = rel_table[jnp.asarray(_t5_bucket_np(np.arange(REL_MAX_DIST + 1)))]
    if far_shift:
        vals = vals - rel_table[REL_BUCKETS - 1]
    return vals * LOG2E


def _near_bias(rel_table):
    fd = _bias_by_distance(rel_table, True)
    h = fd.shape[1]
    lo = NEAR_KEYS - 1
    vec = jnp.concatenate([jnp.full((lo, h), NEG, fd.dtype), fd[:REL_MAX_DIST],
                           jnp.zeros((NEAR_KEYS - REL_MAX_DIST, h), fd.dtype)], axis=0)
    def toeplitz(start):
        period = Q_TILE + NEAR_KEYS - 1
        window = jnp.concatenate([vec[start:start + Q_TILE], vec[start - (NEAR_KEYS - 1):start]], axis=0)
        flat = jnp.tile(window, (NEAR_KEYS, 1))[:NEAR_KEYS * (period - 1)]
        return flat.reshape(NEAR_KEYS, period - 1, h)[:, :Q_TILE]

    tiles = jnp.stack([toeplitz(lo + a) for a in (0, Q_TILE)])
    tiles = tiles.reshape(2, NEAR_KEYS, Q_TILE, NSA_GROUPS, NSA_HPG).transpose(0, 3, 1, 4, 2)
    return tiles.reshape(2, NSA_GROUPS, NEAR_KEYS, ROWS)


def _band_bias(rel_table):
    fd = _bias_by_distance(rel_table, True)
    tl = np.arange(Q_TILE)[None, :]
    r = np.arange(BAND_ROWS)[:, None]
    dist = np.stack([tl - CMP_STRIDE * (r - off) - (CMP_BLOCK - 1) for off in (0, 8, 16)])
    vals = fd[jnp.asarray(np.clip(dist, 0, REL_MAX_DIST))]
    vals = jnp.where(jnp.asarray(dist >= 0)[..., None], vals, NEG)
    v, rr, q, _ = vals.shape
    return vals.transpose(0, 1, 3, 2).reshape(v, rr, NSA_GROUPS, NSA_HPG * q).transpose(0, 2, 1, 3)


def _overlap_t(n_cmp_pad, n_slc):
    i = np.arange(n_cmp_pad)[None, :]
    jj = np.arange(n_slc)[:, None]
    c_start = i * CMP_STRIDE
    ov = (c_start < jj * SLC_BLOCK + SLC_BLOCK) & (c_start + CMP_BLOCK - 1 >= jj * SLC_BLOCK)
    ov = ov & (i < n_cmp_pad - 1)
    return jnp.asarray(ov.astype(np.float32), jnp.bfloat16)


def _block_onehot(seq):
    blk = (np.arange(seq) // SLC_BLOCK) % LANES
    return jnp.asarray((blk[:, None] == np.arange(LANES)[None, :]).astype(np.float32), jnp.bfloat16)


def _pick_tile(n, pref):
    return pref if n % pref == 0 else n


def kernel(x, attn_norm_g, w_in, cmp_pos_k, cmp_w1_k, cmp_w2_k, cmp_pos_v, cmp_w1_v, cmp_w2_v,
           rel_bias_table, fox_forget_bias, w_branch_nsa, w_branch_fox, w_out,
           ffn_norm_g, w_up, conv_w, conv_b, w_down, final_norm_g):
    assert w_in.shape[0] == 1, "the final norm is fused into the single layer's FFN kernel"
    bsz, seq, d = x.shape
    t = bsz * seq
    bf = jnp.bfloat16
    scale = HEAD_DIM ** -0.5 * LOG2E
    x2d = x.reshape(t, d)
    w_in = w_in[0]

    o = np.cumsum([0, NSA_Q_DIM] + [NSA_KV_DIM] * 6 + [3 * NSA_HEADS, FOX_DIM, FOX_DIM, FOX_DIM, FOX_HEADS, d, d])
    w_qkv = jnp.concatenate([w_in[:, o[0]:o[1]] * scale, w_in[:, o[1]:o[7]],
                             w_in[:, o[8]:o[9]] * scale, w_in[:, o[9]:o[11]]], axis=1).astype(bf)
    gate_cols = []
    for grp in range(NSA_GROUPS):
        gate_cols += [w_in[:, o[7] + grp * GATES_PER_GROUP:o[7] + (grp + 1) * GATES_PER_GROUP],
                      jnp.zeros((d, LANES - GATES_PER_GROUP), w_in.dtype)]
    gate_cols += [w_in[:, o[11]:o[12]], jnp.zeros((d, IN_PROJ_TILE - N_GATE + LANES - FOX_HEADS), w_in.dtype)]
    w_gate = jnp.concatenate(gate_cols, axis=1).astype(bf)
    w_ma = w_in[:, o[12]:o[13]].astype(bf)
    w_mb = w_in[:, o[13]:o[14]].astype(bf)

    qkv, gates = _in_proj(x2d, attn_norm_g[0], jnp.concatenate([w_qkv, w_gate], axis=1),
                          _pick_tile(t, 1024), IN_PROJ_TILE)
    qkv = qkv.reshape(bsz, seq, N_QKV)
    gates = gates.reshape(bsz, seq, IN_PROJ_TILE)

    q_all = jnp.concatenate([qkv[:, :, :NSA_Q_DIM], qkv[:, :, CB_QF * HEAD_DIM:(CB_QF + FOX_HEADS) * HEAD_DIM]],
                            axis=-1)
    qt = q_all.reshape(bsz, seq, NSA_HEADS + FOX_HEADS, HEAD_DIM).transpose(0, 2, 3, 1)

    def heads_major(cb, heads):
        return qkv[:, :, cb * HEAD_DIM:(cb + heads) * HEAD_DIM].reshape(
            bsz, seq, heads, HEAD_DIM).transpose(0, 2, 1, 3)

    def values_t(cb, heads, piece):
        v = qkv[:, :, cb * HEAD_DIM:(cb + heads) * HEAD_DIM]
        v = v.reshape(bsz, seq // piece, piece, heads, HEAD_DIM).transpose(0, 3, 1, 4, 2)
        ones_row = (jnp.arange(ACC_ROWS - HEAD_DIM) == 0).astype(bf)[:, None]
        extra = jnp.broadcast_to(ones_row, v.shape[:3] + (ACC_ROWS - HEAD_DIM, piece))
        return jnp.concatenate([v, extra], axis=3)

    n_ch = seq // CMP_STRIDE

    def chunked(cb):
        sl = qkv[:, :, cb * HEAD_DIM:(cb + NSA_GROUPS) * HEAD_DIM]
        sl = sl.reshape(bsz, n_ch, CMP_STRIDE, NSA_GROUPS, HEAD_DIM).transpose(0, 3, 1, 2, 4)
        return sl.reshape(bsz, NSA_GROUPS, n_ch, CMP_STRIDE * HEAD_DIM)

    def posflat(pos):
        return jnp.broadcast_to(pos.reshape(1, CMP_BLOCK * HEAD_DIM), (8, CMP_BLOCK * HEAD_DIM)).astype(bf)

    kc = _compress(chunked(CB_KC), cmp_w1_k[0].astype(bf), posflat(cmp_pos_k[0]), cmp_w2_k[0].astype(bf))
    vc = _compress(chunked(CB_VC), cmp_w1_v[0].astype(bf), posflat(cmp_pos_v[0]), cmp_w2_v[0].astype(bf))
    vct = vc.transpose(0, 1, 3, 2)

    f_t = gates[:, :, NSA_GROUPS * LANES:NSA_GROUPS * LANES + FOX_HEADS]
    f_t = f_t.transpose(0, 2, 1).reshape(bsz * FOX_HEADS, seq)
    bias_col = jnp.tile(fox_forget_bias[0].astype(jnp.float32), bsz).reshape(bsz * FOX_HEADS, 1)
    terms = _decay_cumsum(f_t, bias_col, _pick_tile(seq, 2048))
    dec = jnp.pad(terms.transpose(1, 2, 0), ((0, 0), (0, 0), (0, LANES - DECAY_TERMS)))
    dec = dec.reshape(bsz, FOX_HEADS, seq, LANES)

    n_slc = seq // SLC_BLOCK
    near = _near_bias(rel_bias_table)
    gates_t = gates[:, :, :NSA_GROUPS * LANES].reshape(bsz, seq, NSA_GROUPS, LANES)[..., :GATE_ROWS]
    gates_t = gates_t.transpose(0, 2, 3, 1)
    ycw, pen = _cmp_win(qt, kc, vct, _band_bias(rel_bias_table), _overlap_t(n_ch, n_slc), qkv,
                        values_t(CB_VW, NSA_GROUPS, LANES), near, gates_t)
    ks = heads_major(CB_KS, NSA_GROUPS)
    kaug_sel = jnp.concatenate([ks, jnp.broadcast_to(_block_onehot(seq), ks.shape)], axis=-1)
    y_nsa = _sel(qt, pen, kaug_sel, values_t(CB_VS, NSA_GROUPS, _pick_tile(seq, KEY_CHUNK)),
                 values_t(CB_VS, NSA_GROUPS, LANES), near, ycw, gates_t)

    fox_tq = _pick_tile(seq, FOX_Q_TILE)
    def row_norms(cb):
        v = qkv[:, :, cb * HEAD_DIM:(cb + FOX_HEADS) * HEAD_DIM].astype(jnp.float32)
        v = v.reshape(bsz, seq // fox_tq, fox_tq, FOX_HEADS, HEAD_DIM)
        return jnp.sqrt(jnp.max(jnp.sum(v * v, axis=-1), axis=2)).transpose(0, 2, 1)

    qk_bound = row_norms(CB_QF) * jnp.max(row_norms(CB_KF), axis=2, keepdims=True)
    qk_bound = qk_bound.reshape(bsz * FOX_HEADS, seq // fox_tq)
    neg_cum_end = -jnp.sum(terms[:, :, fox_tq - 1::fox_tq].astype(jnp.float32), axis=0)
    y_fox = _fox(qk_bound, neg_cum_end, qkv, qt, dec, values_t(CB_VF, FOX_HEADS, fox_tq), fox_tq)

    tm2 = _pick_tile(t, 512)
    x_mid = _merge(x2d, attn_norm_g[0], w_ma, w_mb, w_branch_nsa[0].astype(bf), w_branch_fox[0].astype(bf),
                   w_out[0].astype(bf), y_nsa.reshape(t, NSA_Q_DIM), y_fox.reshape(t, FOX_DIM),
                   tm2, _pick_tile(d, 512))
    d_ff = w_down.shape[1]
    out = _ffn(x_mid, ffn_norm_g[0], w_up[0].astype(bf), conv_w[0], conv_b[0], w_down[0].astype(bf),
               final_norm_g, seq, tm2, _pick_tile(d_ff, 512))
    return out.reshape(bsz, seq, d)
```

```python
import functools
import math

import jax
import jax.numpy as jnp
import numpy as np
from jax import lax
from jax.experimental import pallas as pl
from jax.experimental.pallas import tpu as pltpu

HEAD_DIM = 128
NSA_HEADS = 8
NSA_GROUPS = 2
NSA_HPG = NSA_HEADS // NSA_GROUPS
FOX_HEADS = 8
CMP_BLOCK = 32
CMP_STRIDE = 16
SLC_BLOCK = 64
SLC_TOPK = 16
WINDOW = 512
REL_BUCKETS = 32
REL_MAX_DIST = 128
CONV_WIDTH = 3
EPS = 1e-6
NEG = -1e30
FORCED_SCORE = 1e4
LOG2E = math.log2(math.e)

LANES = 128
BF16_SUBLANES = 16
VMEM_LIMIT = 56 * 1024 * 1024

NSA_Q_DIM = NSA_HEADS * HEAD_DIM
NSA_KV_DIM = NSA_GROUPS * HEAD_DIM
FOX_DIM = FOX_HEADS * HEAD_DIM
N_QKV = NSA_Q_DIM + 6 * NSA_KV_DIM + 3 * FOX_DIM
GATES_PER_GROUP = 3 * NSA_HPG
N_GATE = (NSA_GROUPS + 1) * LANES
IN_PROJ_TILE = 512

CB_KC = 8
CB_VC = 10
CB_KS = 12
CB_VS = 14
CB_KW = 16
CB_VW = 18
CB_QF = 20
CB_KF = 28
CB_VF = 36

Q_TILE = 128
ROWS = NSA_HPG * Q_TILE
KEY_CHUNK = 512
FOX_Q_TILE = 512
N_SPLIT = 2
SKIP_MARGIN_LOG2 = 64.0
ROW_BLOCK = 32
DECAY_TERMS = 3
BAND_ROWS = 24
CMP_TIERS = 4
WIN_KEYS = WINDOW + Q_TILE
NEAR_KEYS = 2 * Q_TILE
GATE_ROWS = 16
SUM_ROW = HEAD_DIM
ACC_ROWS = HEAD_DIM + 8


def _params(sem):
    return pltpu.CompilerParams(dimension_semantics=sem, vmem_limit_bytes=VMEM_LIMIT)


def _norm_rows(x, g):
    return (x * lax.rsqrt(jnp.mean(x * x, axis=-1, keepdims=True) + EPS)) * g


def _in_proj_kernel(x_ref, g_ref, w_ref, qkv_ref, gate_ref, h_ref):
    j = pl.program_id(1)
    last = pl.num_programs(1) - 1

    @pl.when(j == 0)
    def _():
        h_ref[...] = _norm_rows(x_ref[...], g_ref[...]).astype(h_ref.dtype)

    y = jnp.dot(h_ref[...], w_ref[...], preferred_element_type=jnp.float32)

    @pl.when(j < last)
    def _():
        qkv_ref[...] = y.astype(qkv_ref.dtype)

    @pl.when(j == last)
    def _():
        gate_ref[...] = y


def _in_proj(x2d, g, w, tm, tn):
    t, d = x2d.shape
    n_tiles = w.shape[1] // tn
    return pl.pallas_call(
        _in_proj_kernel,
        grid=(t // tm, n_tiles),
        in_specs=[
            pl.BlockSpec((tm, d), lambda i, j: (i, 0)),
            pl.BlockSpec((1, d), lambda i, j: (0, 0)),
            pl.BlockSpec((d, tn), lambda i, j: (0, j)),
        ],
        out_specs=[
            pl.BlockSpec((tm, tn), lambda i, j: (i, jnp.minimum(j, n_tiles - 2))),
            pl.BlockSpec((tm, tn), lambda i, j: (i, 0)),
        ],
        out_shape=[
            jax.ShapeDtypeStruct((t, (n_tiles - 1) * tn), jnp.bfloat16),
            jax.ShapeDtypeStruct((t, tn), jnp.float32),
        ],
        scratch_shapes=[pltpu.VMEM((tm, d), jnp.bfloat16)],
        compiler_params=_params(("parallel", "arbitrary")),
        name="norm_in_proj",
    )(x2d, g.reshape(1, d), w)


def _compress_kernel(ch_ref, w1_ref, posf_ref, w2_ref, o_ref):
    half = ch_ref.shape[1]
    ch = ch_ref[...]
    pa = jnp.dot(ch, w1_ref[:half, :], preferred_element_type=jnp.float32)
    pb = jnp.dot(ch, w1_ref[half:, :], preferred_element_type=jnp.float32)
    pos = jnp.dot(posf_ref[...], w1_ref[...], preferred_element_type=jnp.float32)[0:1, :]
    n = pa.shape[0]
    pre = pa + pltpu.roll(pb, n - 1, 0) + pos
    act = jax.nn.gelu(pre)
    o_ref[...] = jnp.dot(act.astype(jnp.bfloat16), w2_ref[...],
                         preferred_element_type=jnp.float32).astype(o_ref.dtype)


def _compress(chunks, w1, posf, w2):
    b, g, n, k = chunks.shape
    return pl.pallas_call(
        _compress_kernel,
        grid=(b, g),
        in_specs=[
            pl.BlockSpec((None, None, n, k), lambda i, j: (i, j, 0, 0)),
            pl.BlockSpec(w1.shape, lambda i, j: (0, 0)),
            pl.BlockSpec(posf.shape, lambda i, j: (0, 0)),
            pl.BlockSpec(w2.shape, lambda i, j: (0, 0)),
        ],
        out_specs=pl.BlockSpec((None, None, n, HEAD_DIM), lambda i, j: (i, j, 0, 0)),
        out_shape=jax.ShapeDtypeStruct((b, g, n, HEAD_DIM), jnp.bfloat16),
        compiler_params=_params(("parallel", "parallel")),
        name="compress_tokens",
    )(chunks, w1, posf, w2)


def _decay_kernel(f_ref, b_ref, tri_ref, o_ref, carry_ref):
    @pl.when(pl.program_id(0) == 0)
    def _():
        carry_ref[...] = jnp.zeros_like(carry_ref)

    x = f_ref[...] + b_ref[...]
    logf = (jnp.minimum(x, 0.0) - jnp.log1p(jnp.exp(-jnp.abs(x)))) * LOG2E
    carry = carry_ref[...]
    for seg in range(f_ref.shape[1] // LANES):
        part = jnp.dot(logf[:, seg * LANES:(seg + 1) * LANES], tri_ref[...],
                       preferred_element_type=jnp.float32, precision=lax.Precision.HIGHEST) + carry
        carry = part[:, LANES - 1:LANES]
        rest = part
        for term in range(DECAY_TERMS):
            piece = rest.astype(o_ref.dtype)
            o_ref[term, :, seg * LANES:(seg + 1) * LANES] = piece
            rest = rest - piece.astype(jnp.float32)
    carry_ref[...] = carry


def _decay_cumsum(f_t, bias_col, width):
    rows, s = f_t.shape
    tri = jnp.asarray(np.triu(np.ones((LANES, LANES), np.float32)))
    return pl.pallas_call(
        _decay_kernel,
        grid=(s // width,),
        in_specs=[
            pl.BlockSpec((rows, width), lambda i: (0, i)),
            pl.BlockSpec((rows, 1), lambda i: (0, 0)),
            pl.BlockSpec((LANES, LANES), lambda i: (0, 0)),
        ],
        out_specs=pl.BlockSpec((DECAY_TERMS, rows, width), lambda i: (0, 0, i)),
        out_shape=jax.ShapeDtypeStruct((DECAY_TERMS, rows, s), jnp.bfloat16),
        scratch_shapes=[pltpu.VMEM((rows, 1), jnp.float32)],
        compiler_params=_params(("arbitrary",)),
        name="decay_cumsum",
    )(f_t, bias_col, tri)


def _col_reduce(op, x):
    reduce = {jnp.maximum: jnp.max, jnp.minimum: jnp.min, jnp.add: jnp.sum}[op]
    return reduce(x, axis=0, keepdims=True)


def _flash_init(m_ref, acc_ref):
    m_ref[...] = jnp.full(m_ref.shape, NEG, jnp.float32)
    acc_ref[...] = jnp.zeros(acc_ref.shape, jnp.float32)


def _probabilities(s, m_new):
    return jnp.exp2((s - m_new).astype(jnp.bfloat16))


def _flash_result(acc_ref):
    return acc_ref[0:HEAD_DIM, :] / acc_ref[SUM_ROW:SUM_ROW + 1, :]


def _flash_step(kaug, vt, get_q, width, m_ref, acc_ref, extra=None):
    w = width // N_SPLIT
    strips = [(i * w, (i + 1) * w) for i in range(N_SPLIT)]
    scores = [jnp.dot(kaug, get_q(lo, hi), preferred_element_type=jnp.float32) for lo, hi in strips]
    for (lo, hi), s in zip(strips, scores):
        if extra is not None:
            s = extra(s, lo, hi)
        m_prev = m_ref[:, lo:hi]
        m_new = jnp.maximum(m_prev, _col_reduce(jnp.maximum, s))
        alpha = jnp.exp2(m_prev - m_new)
        acc_ref[:, lo:hi] = alpha * acc_ref[:, lo:hi] + jnp.dot(
            vt, _probabilities(s, m_new), preferred_element_type=jnp.float32)
        m_ref[:, lo:hi] = m_new


def _flash_far_chunks(n, load_keys, load_values, get_q_for, width, m_ref, acc_ref, s_ref, mx_ref, p_ref):
    w = width // N_SPLIT
    strips = [(i * w, (i + 1) * w) for i in range(N_SPLIT)]
    base = n % 2
    pairs = n // 2

    def scores_into(c, slot):
        kaug = load_keys(c)
        get_q = get_q_for(c)
        for lo, hi in strips:
            s = jnp.dot(kaug, get_q(lo, hi), preferred_element_type=jnp.float32)
            s_ref[slot, :, lo:hi] = s
            mx_ref[slot, :, lo:hi] = _col_reduce(jnp.maximum, s)

    def value_product(c, slot):
        vt = load_values(c)
        return jnp.concatenate([jnp.dot(vt, p_ref[slot, :, lo:hi], preferred_element_type=jnp.float32)
                                for lo, hi in strips], axis=1)

    @pl.when(base == 1)
    def _():
        _flash_step(load_keys(0), load_values(0), get_q_for(0), width, m_ref, acc_ref)

    @pl.when(pairs > 0)
    def _():
        scores_into(base, 0)
        p_ref[1] = jnp.zeros(p_ref.shape[1:], p_ref.dtype)

        def pair(j, carry):
            for cur in (0, 1):
                c = base + 2 * j + cur
                nxt = 1 - cur
                pv = value_product(jnp.maximum(c - 1, base), nxt)
                scores_into(jnp.minimum(c + 1, n - 1), nxt)
                alphas = []
                for lo, hi in strips:
                    m_prev = m_ref[:, lo:hi]
                    m_new = jnp.maximum(m_prev, mx_ref[cur, :, lo:hi])
                    m_ref[:, lo:hi] = m_new
                    for r in range(0, s_ref.shape[1], ROW_BLOCK):
                        p_ref[cur, r:r + ROW_BLOCK, lo:hi] = _probabilities(
                            s_ref[cur, r:r + ROW_BLOCK, lo:hi], m_new)
                    alphas.append(jnp.exp2(m_prev - m_new))
                acc_ref[...] = jnp.concatenate(alphas, axis=1) * (acc_ref[...] + pv)
            return carry

        lax.fori_loop(0, pairs, pair, 0)
        acc_ref[...] = acc_ref[...] + value_product(n - 1, 1)


def _group_queries_t(qt_ref):
    return jnp.concatenate([qt_ref[h] for h in range(NSA_HPG)], axis=1)


def _branch_gate(gates_t, head, branch):
    row = head * 3 + branch
    return gates_t[row:row + 1, :]


def _values_t(vt_ref, k0, n_keys):
    p0 = k0 // LANES
    return jnp.concatenate([vt_ref[p0 + j] for j in range(n_keys // LANES)], axis=1)


def _compressed_branch(n_rows, i0, q4t, kc_ref, vct_ref, band_ref, ovt_ref, sc_ref, oct_ref, imp_ref):
    sc_ref[0:n_rows, :] = jnp.dot(kc_ref[0:n_rows, :], q4t, preferred_element_type=jnp.float32)
    sc_ref[pl.ds(i0, BAND_ROWS), :] = sc_ref[pl.ds(i0, BAND_ROWS), :] + band_ref[...]
    row = lax.broadcasted_iota(jnp.int32, (n_rows, ROWS), 0)
    sc = jnp.where(row < i0 + BAND_ROWS, sc_ref[0:n_rows, :], NEG)
    m = _col_reduce(jnp.maximum, sc)
    p = jnp.exp2(sc - m)
    l = _col_reduce(jnp.add, p)
    pn = p * jnp.where(m > 0.5 * NEG, 1.0 / l, 0.0)
    oct_ref[...] = jnp.dot(vct_ref[:, 0:n_rows], pn.astype(jnp.bfloat16), preferred_element_type=jnp.float32)
    psum = pn[:, 0:Q_TILE]
    for h in range(1, NSA_HPG):
        psum = psum + pn[:, h * Q_TILE:(h + 1) * Q_TILE]
    imp_ref[...] = jnp.dot(ovt_ref[:, 0:n_rows], psum.astype(jnp.bfloat16), preferred_element_type=jnp.float32)


def _cmp_win_kernel(qt_ref, kc_ref, vct_ref, band_ref, ovt_ref, kw_ref, vwt_ref, near_ref, gt_ref,
                    ycw_ref, pen_ref, sc_ref, sw_ref, oct_ref, imp_ref):
    qb = pl.program_id(2)
    n_cmp = kc_ref.shape[0]
    n_slc = ovt_ref.shape[0]
    q4t = _group_queries_t(qt_ref)

    i0 = pl.multiple_of(jnp.maximum(8 * qb - 16, 0), 8)
    n_tiers = CMP_TIERS if n_cmp % (CMP_TIERS * LANES) == 0 else 1
    step = n_cmp // n_tiers
    for tier in range(1, n_tiers + 1):
        in_tier = (i0 + BAND_ROWS <= tier * step) & (i0 + BAND_ROWS > (tier - 1) * step)
        pl.when(in_tier)(functools.partial(_compressed_branch, tier * step, i0, q4t, kc_ref, vct_ref,
                                           band_ref, ovt_ref, sc_ref, oct_ref, imp_ref))
    oct_ = oct_ref[...]
    imp = imp_ref[...]

    ji = lax.broadcasted_iota(jnp.int32, (n_slc, Q_TILE), 0)
    jf = ji.astype(jnp.float32)
    t = qb * Q_TILE + lax.broadcasted_iota(jnp.int32, (n_slc, Q_TILE), 1)
    cur = t // SLC_BLOCK
    forced = (ji == 0) | (ji == cur) | (ji == cur - 1)
    score = jnp.where(ji <= cur, jnp.where(forced, FORCED_SCORE, imp), -1.0)
    pen_t = jnp.full((n_slc, Q_TILE), NEG, jnp.float32)
    for _ in range(min(SLC_TOPK, n_slc)):
        mx = _col_reduce(jnp.maximum, score)
        idx = _col_reduce(jnp.minimum, jnp.where(score == mx, jf, float(n_slc)))
        pick = jf == idx
        pen_t = jnp.where(pick, 0.0, pen_t)
        score = jnp.where(pick, -2.0, score)
    pad = pen_ref.shape[0] - n_slc
    if pad:
        pen_t = jnp.concatenate([pen_t, jnp.full((pad, Q_TILE), NEG, jnp.float32)], axis=0)
    pen_ref[...] = pen_t.astype(pen_ref.dtype)

    t0 = qb * Q_TILE
    ws = pl.multiple_of(jnp.maximum(t0 - WINDOW, 0), Q_TILE)
    ns = pl.multiple_of(jnp.maximum(t0 - Q_TILE, 0), Q_TILE)
    sw_ref[...] = jnp.dot(kw_ref[pl.ds(ws, WIN_KEYS), :], q4t, preferred_element_type=jnp.float32)
    off = pl.multiple_of(ns - ws, Q_TILE)
    sw_ref[pl.ds(off, NEAR_KEYS), :] = sw_ref[pl.ds(off, NEAR_KEYS), :] + near_ref[...]
    key = ws + lax.broadcasted_iota(jnp.int32, (WIN_KEYS, ROWS), 0)
    qry = t0 + (lax.broadcasted_iota(jnp.int32, (WIN_KEYS, ROWS), 1) & (Q_TILE - 1))
    sw = jnp.where((key <= qry) & (key > qry - WINDOW), sw_ref[...], NEG)
    mw = _col_reduce(jnp.maximum, sw)
    ow_sum = jnp.dot(_values_t(vwt_ref, ws, WIN_KEYS), _probabilities(sw, mw),
                     preferred_element_type=jnp.float32)
    owt = ow_sum[0:HEAD_DIM, :] / ow_sum[SUM_ROW:SUM_ROW + 1, :]

    gates_t = jax.nn.sigmoid(gt_ref[...])
    for h in range(NSA_HPG):
        cols = slice(h * Q_TILE, (h + 1) * Q_TILE)
        mixed = _branch_gate(gates_t, h, 0) * oct_[:, cols] + _branch_gate(gates_t, h, 2) * owt[:, cols]
        ycw_ref[:, h * HEAD_DIM:(h + 1) * HEAD_DIM] = mixed.T


def _cmp_win(qt, kc, vct, band, ovt, qkv, vwt, near, gates_t):
    b, s, _ = qkv.shape
    n_cmp = kc.shape[2]
    n_slc = ovt.shape[0]
    nq = s // Q_TILE
    pen_w = -(-n_slc // LANES) * LANES
    n_band = band.shape[0] - 1
    return pl.pallas_call(
        _cmp_win_kernel,
        grid=(b, NSA_GROUPS, nq),
        in_specs=[
            pl.BlockSpec((None, NSA_HPG, HEAD_DIM, Q_TILE), lambda i, g, q: (i, g, 0, q)),
            pl.BlockSpec((None, None, n_cmp, HEAD_DIM), lambda i, g, q: (i, g, 0, 0)),
            pl.BlockSpec((None, None, HEAD_DIM, n_cmp), lambda i, g, q: (i, g, 0, 0)),
            pl.BlockSpec((None, None, BAND_ROWS, ROWS), lambda i, g, q: (jnp.minimum(q, n_band), g, 0, 0)),
            pl.BlockSpec((n_slc, n_cmp), lambda i, g, q: (0, 0)),
            pl.BlockSpec((None, s, HEAD_DIM), lambda i, g, q: (i, 0, CB_KW + g)),
            pl.BlockSpec((None, None, s // LANES, ACC_ROWS, LANES), lambda i, g, q: (i, g, 0, 0, 0)),
            pl.BlockSpec((None, None, NEAR_KEYS, ROWS), lambda i, g, q: (jnp.minimum(q, 1), g, 0, 0)),
            pl.BlockSpec((None, None, GATE_ROWS, Q_TILE), lambda i, g, q: (i, g, 0, q)),
        ],
        out_specs=[
            pl.BlockSpec((None, Q_TILE, NSA_HPG * HEAD_DIM), lambda i, g, q: (i, q, g)),
            pl.BlockSpec((None, None, pen_w, Q_TILE), lambda i, g, q: (i, g, 0, q)),
        ],
        out_shape=[
            jax.ShapeDtypeStruct((b, s, NSA_Q_DIM), jnp.float32),
            jax.ShapeDtypeStruct((b, NSA_GROUPS, pen_w, s), jnp.bfloat16),
        ],
        scratch_shapes=[pltpu.VMEM((n_cmp, ROWS), jnp.float32), pltpu.VMEM((WIN_KEYS, ROWS), jnp.float32),
                        pltpu.VMEM((HEAD_DIM, ROWS), jnp.float32), pltpu.VMEM((n_slc, Q_TILE), jnp.float32)],
        compiler_params=_params(("parallel", "parallel", "arbitrary")),
        name="nsa_cmp_topk_win",
    )(qt, kc, vct, band, ovt, qkv, vwt, near, gates_t)


def _range_penalty(pen, first_blk, lo_ok, hi_ok):
    n_half = pen.shape[0] // LANES
    c = lax.broadcasted_iota(jnp.int32, (LANES, pen.shape[1]), 0)
    out = pen[0:LANES, :]
    blk = c
    for hf in range(1, n_half):
        in_lower = (c + (hf - 1) * LANES >= first_blk) & (first_blk < hf * LANES)
        out = jnp.where(in_lower, out, pen[hf * LANES:(hf + 1) * LANES, :])
        blk = jnp.where(in_lower, blk, c + hf * LANES)
    return jnp.where((blk >= lo_ok) & (blk < hi_ok), out, jnp.asarray(NEG, out.dtype))


def _sel_kernel(qt_ref, pen_ref, kaug_ref, vt_ref, near_ref, ycw_ref, gt_ref, y_ref,
                qaug_ref, m_ref, acc_ref, s_ref, mx_ref, p_ref):
    qb = pl.program_id(2)
    n_half = pen_ref.shape[0] // LANES
    n_blocks = pen_ref.shape[0]
    chunks_per_half = LANES * SLC_BLOCK // KEY_CHUNK
    blocks_per_chunk = KEY_CHUNK // SLC_BLOCK

    t0 = qb * Q_TILE
    near_end = t0 - Q_TILE
    nf = jnp.maximum(near_end // KEY_CHUNK, 0)
    ms = pl.multiple_of(jnp.maximum(near_end - KEY_CHUNK, 0), Q_TILE)
    ns = pl.multiple_of(jnp.maximum(near_end, 0), Q_TILE)

    q4t = _group_queries_t(qt_ref)
    pen = pen_ref[...]
    operands = [pen[hf * LANES:(hf + 1) * LANES, :] for hf in range(n_half)]
    operands.append(_range_penalty(pen, ms // SLC_BLOCK, nf * blocks_per_chunk, near_end // SLC_BLOCK))
    operands.append(_range_penalty(pen, ns // SLC_BLOCK, 0, n_blocks))
    for idx, channels in enumerate(operands):
        qaug_ref[idx, 0:HEAD_DIM, :] = q4t
        qaug_ref[idx, HEAD_DIM:, :] = jnp.concatenate([channels] * NSA_HPG, axis=1)

    _flash_init(m_ref, acc_ref)

    def key_start(c):
        return pl.multiple_of(jnp.where(c < nf, c * KEY_CHUNK, ms), Q_TILE)

    def get_q_for(c):
        operand = jnp.where(c < nf, c // chunks_per_half, n_half)
        return lambda lo, hi: qaug_ref[operand, :, lo:hi]

    _flash_far_chunks(nf + 1, lambda c: kaug_ref[pl.ds(key_start(c), KEY_CHUNK), :],
                      lambda c: _values_t(vt_ref, key_start(c), KEY_CHUNK), get_q_for,
                      ROWS, m_ref, acc_ref, s_ref, mx_ref, p_ref)

    _flash_step(kaug_ref[pl.ds(ns, NEAR_KEYS), :], _values_t(vt_ref, ns, NEAR_KEYS),
                lambda lo, hi: qaug_ref[n_half + 1, :, lo:hi], ROWS, m_ref, acc_ref,
                extra=lambda s, lo, hi: s + near_ref[:, lo:hi])

    o = _flash_result(acc_ref)
    gates_t = jax.nn.sigmoid(gt_ref[...])
    for h in range(NSA_HPG):
        sel_h = (_branch_gate(gates_t, h, 1) * o[:, h * Q_TILE:(h + 1) * Q_TILE]).T
        y_ref[:, h * HEAD_DIM:(h + 1) * HEAD_DIM] = (
            ycw_ref[:, h * HEAD_DIM:(h + 1) * HEAD_DIM] + sel_h).astype(y_ref.dtype)


def _sel(qt, pen, kaug, vt_pieces, near, ycw, gates_t):
    b, _, s, _ = kaug.shape
    nq = s // Q_TILE
    pen_w = pen.shape[2]
    return pl.pallas_call(
        _sel_kernel,
        grid=(b, NSA_GROUPS, nq),
        in_specs=[
            pl.BlockSpec((None, NSA_HPG, HEAD_DIM, Q_TILE), lambda i, g, q: (i, g, 0, q)),
            pl.BlockSpec((None, None, pen_w, Q_TILE), lambda i, g, q: (i, g, 0, q)),
            pl.BlockSpec((None, None, s, 2 * HEAD_DIM), lambda i, g, q: (i, g, 0, 0)),
            pl.BlockSpec((None, None, s // LANES, ACC_ROWS, LANES), lambda i, g, q: (i, g, 0, 0, 0)),
            pl.BlockSpec((None, None, NEAR_KEYS, ROWS), lambda i, g, q: (jnp.minimum(q, 1), g, 0, 0)),
            pl.BlockSpec((None, Q_TILE, NSA_HPG * HEAD_DIM), lambda i, g, q: (i, q, g)),
            pl.BlockSpec((None, None, GATE_ROWS, Q_TILE), lambda i, g, q: (i, g, 0, q)),
        ],
        out_specs=pl.BlockSpec((None, Q_TILE, NSA_HPG * HEAD_DIM), lambda i, g, q: (i, q, g)),
        out_shape=jax.ShapeDtypeStruct((b, s, NSA_Q_DIM), jnp.bfloat16),
        scratch_shapes=[
            pltpu.VMEM((pen_w // LANES + 2, 2 * HEAD_DIM, ROWS), jnp.bfloat16),
            pltpu.VMEM((1, ROWS), jnp.float32),
            pltpu.VMEM((ACC_ROWS, ROWS), jnp.float32),
            pltpu.VMEM((2, KEY_CHUNK, ROWS), jnp.float32),
            pltpu.VMEM((2, 1, ROWS), jnp.float32),
            pltpu.VMEM((2, KEY_CHUNK, ROWS), jnp.bfloat16),
        ],
        compiler_params=_params(("parallel", "parallel", "arbitrary")),
        name="nsa_selected",
    )(qt, pen, kaug, vt_pieces, near, ycw, gates_t)


def _fox_kernel(nce_ref, qt_ref, k_ref, dec_ref, vt_ref, o_ref,
                qaug_ref, m_ref, acc_ref, s_ref, mx_ref, p_ref, knorm_ref):
    qi = pl.program_id(2)
    bh = pl.program_id(0) * pl.num_programs(1) + pl.program_id(1)
    tq = qt_ref.shape[1]
    row = lax.broadcasted_iota(jnp.int32, (HEAD_DIM, tq), 0)
    qaug_ref[0:HEAD_DIM, :] = qt_ref[...]
    qaug_ref[HEAD_DIM:, :] = jnp.where(row < DECAY_TERMS, -1.0, 0.0).astype(qaug_ref.dtype)
    _flash_init(m_ref, acc_ref)

    @pl.when(qi == 0)
    def _():
        def chunk_max(c, best):
            kf = k_ref[pl.ds(pl.multiple_of(c * tq, tq), tq), :].astype(jnp.float32)
            return jnp.maximum(best, jnp.max(jnp.sum(kf * kf, axis=1, keepdims=True)))

        knorm_ref[0] = jnp.sqrt(lax.fori_loop(0, k_ref.shape[0] // tq, chunk_max, jnp.float32(0.0)))

    qf = qt_ref[...].astype(jnp.float32)
    q_norm = jnp.sqrt(jnp.max(jnp.sum(qf * qf, axis=0, keepdims=True)))

    def get_q(lo, hi):
        return qaug_ref[:, lo:hi]

    def load_keys(c):
        k0 = pl.multiple_of(c * tq, tq)
        return jnp.concatenate([k_ref[pl.ds(k0, tq), :], dec_ref[pl.ds(k0, tq), :]], axis=1)

    def causal(s, lo, hi):
        key = lax.broadcasted_iota(jnp.int32, s.shape, 0)
        qry = lo + lax.broadcasted_iota(jnp.int32, s.shape, 1)
        return jnp.where(key <= qry, s, NEG)

    _flash_step(load_keys(qi), vt_ref[qi], get_q, tq, m_ref, acc_ref, extra=causal)

    m_low = jnp.min(m_ref[...])
    reach = q_norm * knorm_ref[0] + SKIP_MARGIN_LOG2

    def count(c, n_skip):
        return n_skip + jnp.where(reach + nce_ref[bh, c] < m_low, 1, 0)

    first = lax.fori_loop(0, qi, count, jnp.int32(0))
    _flash_far_chunks(qi - first, lambda c: load_keys(c + first), lambda c: vt_ref[c + first],
                      lambda c: get_q, tq, m_ref, acc_ref, s_ref, mx_ref, p_ref)
    o_ref[...] = _flash_result(acc_ref).T.astype(o_ref.dtype)


def _fox(neg_cum_end, qkv, qt, dec, vt, tq):
    b, s, _ = qkv.shape
    return pl.pallas_call(
        _fox_kernel,
        grid=(b, FOX_HEADS, s // tq),
        in_specs=[
            pl.BlockSpec(memory_space=pltpu.SMEM),
            pl.BlockSpec((None, None, HEAD_DIM, tq), lambda i, h, q: (i, NSA_HEADS + h, 0, q)),
            pl.BlockSpec((None, s, HEAD_DIM), lambda i, h, q: (i, 0, CB_KF + h)),
            pl.BlockSpec((None, None, s, LANES), lambda i, h, q: (i, h, 0, 0)),
            pl.BlockSpec((None, None, s // tq, ACC_ROWS, tq), lambda i, h, q: (i, h, 0, 0, 0)),
        ],
        out_specs=pl.BlockSpec((None, tq, HEAD_DIM), lambda i, h, q: (i, q, h)),
        out_shape=jax.ShapeDtypeStruct((b, s, FOX_DIM), jnp.bfloat16),
        scratch_shapes=[
            pltpu.VMEM((2 * HEAD_DIM, tq), jnp.bfloat16),
            pltpu.VMEM((1, tq), jnp.float32),
            pltpu.VMEM((ACC_ROWS, tq), jnp.float32),
            pltpu.VMEM((2, tq, tq), jnp.float32),
            pltpu.VMEM((2, 1, tq), jnp.float32),
            pltpu.VMEM((2, tq, tq), jnp.bfloat16),
            pltpu.SMEM((1,), jnp.float32),
        ],
        compiler_params=_params(("parallel", "parallel", "arbitrary")),
        name="fox_attention",
    )(neg_cum_end, qt, qkv, dec, vt)


def _merge_kernel(x_ref, g_ref, wa_ref, wb_ref, pa_ref, pb_ref, wo_ref, ya_ref, yb_ref, o_ref,
                  h_ref, acc_ref):
    j = pl.program_id(1)

    @pl.when(j == 0)
    def _():
        h_ref[...] = _norm_rows(x_ref[...], g_ref[...]).astype(h_ref.dtype)
        acc_ref[...] = jnp.zeros_like(acc_ref)

    h = h_ref[...]
    ga = jax.nn.sigmoid(jnp.dot(h, wa_ref[...], preferred_element_type=jnp.float32))
    gb = jax.nn.sigmoid(jnp.dot(h, wb_ref[...], preferred_element_type=jnp.float32))
    a = jnp.dot(ya_ref[...], pa_ref[...], preferred_element_type=jnp.float32)
    bb = jnp.dot(yb_ref[...], pb_ref[...], preferred_element_type=jnp.float32)
    merged = (ga * a + gb * bb).astype(jnp.bfloat16)
    acc_ref[...] += jnp.dot(merged, wo_ref[...], preferred_element_type=jnp.float32)

    @pl.when(j == pl.num_programs(1) - 1)
    def _():
        o_ref[...] = x_ref[...] + acc_ref[...]


def _merge(x2d, g, w_ma, w_mb, p_a, p_b, w_out, y_a, y_b, tm, tn):
    t, d = x2d.shape
    ka = p_a.shape[0]
    kb = p_b.shape[0]
    return pl.pallas_call(
        _merge_kernel,
        grid=(t // tm, d // tn),
        in_specs=[
            pl.BlockSpec((tm, d), lambda i, j: (i, 0)),
            pl.BlockSpec((1, d), lambda i, j: (0, 0)),
            pl.BlockSpec((d, tn), lambda i, j: (0, j)),
            pl.BlockSpec((d, tn), lambda i, j: (0, j)),
            pl.BlockSpec((ka, tn), lambda i, j: (0, j)),
            pl.BlockSpec((kb, tn), lambda i, j: (0, j)),
            pl.BlockSpec((tn, d), lambda i, j: (j, 0)),
            pl.BlockSpec((tm, ka), lambda i, j: (i, 0)),
            pl.BlockSpec((tm, kb), lambda i, j: (i, 0)),
        ],
        out_specs=pl.BlockSpec((tm, d), lambda i, j: (i, 0)),
        out_shape=jax.ShapeDtypeStruct((t, d), jnp.float32),
        scratch_shapes=[pltpu.VMEM((tm, d), jnp.bfloat16), pltpu.VMEM((tm, d), jnp.float32)],
        compiler_params=_params(("parallel", "arbitrary")),
        name="merge_out_proj",
    )(x2d, g.reshape(1, d), w_ma, w_mb, p_a, p_b, w_out, y_a, y_b)


def _ffn_kernel(x_ref, xh_ref, g_ref, wu_ref, wv_ref, cw_ref, cb_ref, wd_ref, gf_ref, o_ref,
                h_ref, u_ref, acc_ref, *, seq, tm):
    i = pl.program_id(0)
    j = pl.program_id(1)
    halo = BF16_SUBLANES

    @pl.when(j == 0)
    def _():
        g = g_ref[...]
        keep = jnp.where((i * tm) % seq == 0, 0.0, 1.0)
        h_ref[0:halo, :] = (_norm_rows(xh_ref[...], g) * keep).astype(h_ref.dtype)
        h_ref[halo:, :] = _norm_rows(x_ref[...], g).astype(h_ref.dtype)
        acc_ref[...] = jnp.zeros_like(acc_ref)

    u_ref[...] = jnp.dot(h_ref[...], wu_ref[...], preferred_element_type=jnp.float32)
    v = jnp.dot(h_ref[halo:, :], wv_ref[...], preferred_element_type=jnp.float32)
    conv = cb_ref[...]
    for k in range(CONV_WIDTH):
        conv = conv + cw_ref[k:k + 1, :] * u_ref[pl.ds(halo - (CONV_WIDTH - 1) + k, tm), :]
    act = (jax.nn.gelu(conv) * v).astype(jnp.bfloat16)
    acc_ref[...] += jnp.dot(act, wd_ref[...], preferred_element_type=jnp.float32)

    @pl.when(j == pl.num_programs(1) - 1)
    def _():
        o_ref[...] = _norm_rows(x_ref[...] + acc_ref[...], gf_ref[...])


def _ffn(x2d, g, w_up, conv_w, conv_b, w_down, g_final, seq, tm, tn):
    t, d = x2d.shape
    d_ff = w_down.shape[0]
    nt = d_ff // tn
    halo = BF16_SUBLANES
    per = tm // halo
    return pl.pallas_call(
        functools.partial(_ffn_kernel, seq=seq, tm=tm),
        grid=(t // tm, nt),
        in_specs=[
            pl.BlockSpec((tm, d), lambda i, j: (i, 0)),
            pl.BlockSpec((halo, d), lambda i, j: (jnp.maximum(i * per - 1, 0), 0)),
            pl.BlockSpec((1, d), lambda i, j: (0, 0)),
            pl.BlockSpec((d, tn), lambda i, j: (0, j)),
            pl.BlockSpec((d, tn), lambda i, j: (0, nt + j)),
            pl.BlockSpec((CONV_WIDTH, tn), lambda i, j: (0, j)),
            pl.BlockSpec((1, tn), lambda i, j: (0, j)),
            pl.BlockSpec((tn, d), lambda i, j: (j, 0)),
            pl.BlockSpec((1, d), lambda i, j: (0, 0)),
        ],
        out_specs=pl.BlockSpec((tm, d), lambda i, j: (i, 0)),
        out_shape=jax.ShapeDtypeStruct((t, d), jnp.float32),
        scratch_shapes=[
            pltpu.VMEM((tm + halo, d), jnp.bfloat16),
            pltpu.VMEM((tm + halo, tn), jnp.float32),
            pltpu.VMEM((tm, d), jnp.float32),
        ],
        compiler_params=_params(("parallel", "arbitrary")),
        name="ffn_final_norm",
    )(x2d, x2d, g.reshape(1, d), w_up, w_up, conv_w, conv_b.reshape(1, d_ff), w_down, g_final.reshape(1, d))


def _t5_bucket_np(dist):
    n = np.maximum(dist, 0)
    max_exact = REL_BUCKETS // 2
    nf = np.maximum(n, 1).astype(np.float32)
    large = max_exact + (np.log(nf / np.float32(max_exact)) / np.float32(math.log(REL_MAX_DIST / max_exact))
                         * np.float32(REL_BUCKETS - max_exact)).astype(np.int32)
    return np.where(n < max_exact, n, np.minimum(large, REL_BUCKETS - 1)).astype(np.int32)


def _bias_by_distance(rel_table, far_shift):
    vals = rel_table[jnp.asarray(_t5_bucket_np(np.arange(REL_MAX_DIST + 1)))]
    if far_shift:
        vals = vals - rel_table[REL_BUCKETS - 1]
    return vals * LOG2E


def _near_bias(rel_table):
    fd = _bias_by_distance(rel_table, True)
    h = fd.shape[1]
    lo = NEAR_KEYS - 1
    vec = jnp.concatenate([jnp.full((lo, h), NEG, fd.dtype), fd[:REL_MAX_DIST],
                           jnp.zeros((NEAR_KEYS - REL_MAX_DIST, h), fd.dtype)], axis=0)
    def toeplitz(start):
        period = Q_TILE + NEAR_KEYS - 1
        window = jnp.concatenate([vec[start:start + Q_TILE], vec[start - (NEAR_KEYS - 1):start]], axis=0)
        flat = jnp.tile(window, (NEAR_KEYS, 1))[:NEAR_KEYS * (period - 1)]
        return flat.reshape(NEAR_KEYS, period - 1, h)[:, :Q_TILE]

    tiles = jnp.stack([toeplitz(lo + a) for a in (0, Q_TILE)])
    tiles = tiles.reshape(2, NEAR_KEYS, Q_TILE, NSA_GROUPS, NSA_HPG).transpose(0, 3, 1, 4, 2)
    return tiles.reshape(2, NSA_GROUPS, NEAR_KEYS, ROWS)


def _band_bias(rel_table):
    fd = _bias_by_distance(rel_table, True)
    tl = np.arange(Q_TILE)[None, :]
    r = np.arange(BAND_ROWS)[:, None]
    dist = np.stack([tl - CMP_STRIDE * (r - off) - (CMP_BLOCK - 1) for off in (0, 8, 16)])
    vals = fd[jnp.asarray(np.clip(dist, 0, REL_MAX_DIST))]
    vals = jnp.where(jnp.asarray(dist >= 0)[..., None], vals, NEG)
    v, rr, q, _ = vals.shape
    return vals.transpose(0, 1, 3, 2).reshape(v, rr, NSA_GROUPS, NSA_HPG * q).transpose(0, 2, 1, 3)


def _overlap_t(n_cmp_pad, n_slc):
    i = np.arange(n_cmp_pad)[None, :]
    jj = np.arange(n_slc)[:, None]
    c_start = i * CMP_STRIDE
    ov = (c_start < jj * SLC_BLOCK + SLC_BLOCK) & (c_start + CMP_BLOCK - 1 >= jj * SLC_BLOCK)
    ov = ov & (i < n_cmp_pad - 1)
    return jnp.asarray(ov.astype(np.float32), jnp.bfloat16)


def _block_onehot(seq):
    blk = (np.arange(seq) // SLC_BLOCK) % LANES
    return jnp.asarray((blk[:, None] == np.arange(LANES)[None, :]).astype(np.float32), jnp.bfloat16)


def _pick_tile(n, pref):
    return pref if n % pref == 0 else n


def kernel(x, attn_norm_g, w_in, cmp_pos_k, cmp_w1_k, cmp_w2_k, cmp_pos_v, cmp_w1_v, cmp_w2_v,
           rel_bias_table, fox_forget_bias, w_branch_nsa, w_branch_fox, w_out,
           ffn_norm_g, w_up, conv_w, conv_b, w_down, final_norm_g):
    assert w_in.shape[0] == 1, "the final norm is fused into the single layer's FFN kernel"
    bsz, seq, d = x.shape
    t = bsz * seq
    bf = jnp.bfloat16
    scale = HEAD_DIM ** -0.5 * LOG2E
    x2d = x.reshape(t, d)
    w_in = w_in[0]

    o = np.cumsum([0, NSA_Q_DIM] + [NSA_KV_DIM] * 6 + [3 * NSA_HEADS, FOX_DIM, FOX_DIM, FOX_DIM, FOX_HEADS, d, d])
    w_qkv = jnp.concatenate([w_in[:, o[0]:o[1]] * scale, w_in[:, o[1]:o[7]],
                             w_in[:, o[8]:o[9]] * scale, w_in[:, o[9]:o[11]]], axis=1).astype(bf)
    gate_cols = []
    for grp in range(NSA_GROUPS):
        gate_cols += [w_in[:, o[7] + grp * GATES_PER_GROUP:o[7] + (grp + 1) * GATES_PER_GROUP],
                      jnp.zeros((d, LANES - GATES_PER_GROUP), w_in.dtype)]
    gate_cols += [w_in[:, o[11]:o[12]], jnp.zeros((d, IN_PROJ_TILE - N_GATE + LANES - FOX_HEADS), w_in.dtype)]
    w_gate = jnp.concatenate(gate_cols, axis=1).astype(bf)
    w_ma = w_in[:, o[12]:o[13]].astype(bf)
    w_mb = w_in[:, o[13]:o[14]].astype(bf)

    qkv, gates = _in_proj(x2d, attn_norm_g[0], jnp.concatenate([w_qkv, w_gate], axis=1),
                          _pick_tile(t, 1024), IN_PROJ_TILE)
    qkv = qkv.reshape(bsz, seq, N_QKV)
    gates = gates.reshape(bsz, seq, IN_PROJ_TILE)

    q_all = jnp.concatenate([qkv[:, :, :NSA_Q_DIM], qkv[:, :, CB_QF * HEAD_DIM:(CB_QF + FOX_HEADS) * HEAD_DIM]],
                            axis=-1)
    qt = q_all.reshape(bsz, seq, NSA_HEADS + FOX_HEADS, HEAD_DIM).transpose(0, 2, 3, 1)

    def heads_major(cb, heads):
        return qkv[:, :, cb * HEAD_DIM:(cb + heads) * HEAD_DIM].reshape(
            bsz, seq, heads, HEAD_DIM).transpose(0, 2, 1, 3)

    def values_t(cb, heads, piece):
        v = qkv[:, :, cb * HEAD_DIM:(cb + heads) * HEAD_DIM]
        v = v.reshape(bsz, seq // piece, piece, heads, HEAD_DIM).transpose(0, 3, 1, 4, 2)
        ones_row = (jnp.arange(ACC_ROWS - HEAD_DIM) == 0).astype(bf)[:, None]
        extra = jnp.broadcast_to(ones_row, v.shape[:3] + (ACC_ROWS - HEAD_DIM, piece))
        return jnp.concatenate([v, extra], axis=3)

    n_ch = seq // CMP_STRIDE

    def chunked(cb):
        sl = qkv[:, :, cb * HEAD_DIM:(cb + NSA_GROUPS) * HEAD_DIM]
        sl = sl.reshape(bsz, n_ch, CMP_STRIDE, NSA_GROUPS, HEAD_DIM).transpose(0, 3, 1, 2, 4)
        return sl.reshape(bsz, NSA_GROUPS, n_ch, CMP_STRIDE * HEAD_DIM)

    def posflat(pos):
        return jnp.broadcast_to(pos.reshape(1, CMP_BLOCK * HEAD_DIM), (8, CMP_BLOCK * HEAD_DIM)).astype(bf)

    kc = _compress(chunked(CB_KC), cmp_w1_k[0].astype(bf), posflat(cmp_pos_k[0]), cmp_w2_k[0].astype(bf))
    vc = _compress(chunked(CB_VC), cmp_w1_v[0].astype(bf), posflat(cmp_pos_v[0]), cmp_w2_v[0].astype(bf))
    vct = vc.transpose(0, 1, 3, 2)

    f_t = gates[:, :, NSA_GROUPS * LANES:NSA_GROUPS * LANES + FOX_HEADS]
    f_t = f_t.transpose(0, 2, 1).reshape(bsz * FOX_HEADS, seq)
    bias_col = jnp.tile(fox_forget_bias[0].astype(jnp.float32), bsz).reshape(bsz * FOX_HEADS, 1)
    terms = _decay_cumsum(f_t, bias_col, _pick_tile(seq, 2048))
    dec = jnp.pad(terms.transpose(1, 2, 0), ((0, 0), (0, 0), (0, LANES - DECAY_TERMS)))
    dec = dec.reshape(bsz, FOX_HEADS, seq, LANES)

    n_slc = seq // SLC_BLOCK
    near = _near_bias(rel_bias_table)
    gates_t = gates[:, :, :NSA_GROUPS * LANES].reshape(bsz, seq, NSA_GROUPS, LANES)[..., :GATE_ROWS]
    gates_t = gates_t.transpose(0, 2, 3, 1)
    ycw, pen = _cmp_win(qt, kc, vct, _band_bias(rel_bias_table), _overlap_t(n_ch, n_slc), qkv,
                        values_t(CB_VW, NSA_GROUPS, LANES), near, gates_t)
    ks = heads_major(CB_KS, NSA_GROUPS)
    kaug_sel = jnp.concatenate([ks, jnp.broadcast_to(_block_onehot(seq), ks.shape)], axis=-1)
    y_nsa = _sel(qt, pen, kaug_sel, values_t(CB_VS, NSA_GROUPS, LANES), near, ycw, gates_t)

    fox_tq = _pick_tile(seq, FOX_Q_TILE)
    neg_cum_end = -jnp.sum(terms[:, :, fox_tq - 1::fox_tq].astype(jnp.float32), axis=0)
    y_fox = _fox(neg_cum_end, qkv, qt, dec, values_t(CB_VF, FOX_HEADS, fox_tq), fox_tq)

    tm2 = _pick_tile(t, 512)
    x_mid = _merge(x2d, attn_norm_g[0], w_ma, w_mb, w_branch_nsa[0].astype(bf), w_branch_fox[0].astype(bf),
                   w_out[0].astype(bf), y_nsa.reshape(t, NSA_Q_DIM), y_fox.reshape(t, FOX_DIM),
                   tm2, _pick_tile(d, 512))
    d_ff = w_down.shape[1]
    out = _ffn(x_mid, ffn_norm_g[0], w_up[0].astype(bf), conv_w[0], conv_b[0], w_down[0].astype(bf),
               final_norm_g, seq, tm2, _pick_tile(d_ff, 512))
    return out.reshape(bsz, seq, d)
```

```python
import functools
import math

import jax
import jax.numpy as jnp
import numpy as np
from jax import lax
from jax.experimental import pallas as pl
from jax.experimental.pallas import tpu as pltpu

HEAD_DIM = 128
NSA_HEADS = 8
NSA_GROUPS = 2
NSA_HPG = NSA_HEADS // NSA_GROUPS
FOX_HEADS = 8
CMP_BLOCK = 32
CMP_STRIDE = 16
SLC_BLOCK = 64
SLC_TOPK = 16
WINDOW = 512
REL_BUCKETS = 32
REL_MAX_DIST = 128
CONV_WIDTH = 3
EPS = 1e-6
NEG = -1e30
FORCED_SCORE = 1e4
LOG2E = math.log2(math.e)

LANES = 128
BF16_SUBLANES = 16
VMEM_LIMIT = 56 * 1024 * 1024

NSA_Q_DIM = NSA_HEADS * HEAD_DIM
NSA_KV_DIM = NSA_GROUPS * HEAD_DIM
FOX_DIM = FOX_HEADS * HEAD_DIM
N_QKV = NSA_Q_DIM + 6 * NSA_KV_DIM + 3 * FOX_DIM
GATES_PER_GROUP = 3 * NSA_HPG
N_GATE = (NSA_GROUPS + 1) * LANES
IN_PROJ_TILE = 512

CB_KC = 8
CB_VC = 10
CB_KS = 12
CB_VS = 14
CB_KW = 16
CB_VW = 18
CB_QF = 20
CB_KF = 28
CB_VF = 36

Q_TILE = 128
ROWS = NSA_HPG * Q_TILE
KEY_CHUNK = 512
FOX_Q_TILE = 512
N_SPLIT = 2
WIDE_CHUNKS = 4
SKIP_MARGIN_LOG2 = 64.0
ROW_BLOCK = 32
DECAY_TERMS = 3
BAND_ROWS = 24
CMP_TIERS = 4
WIN_KEYS = WINDOW + Q_TILE
NEAR_KEYS = 2 * Q_TILE
GATE_ROWS = 16
SUM_ROW = HEAD_DIM
ACC_ROWS = HEAD_DIM + 8


def _params(sem):
    return pltpu.CompilerParams(dimension_semantics=sem, vmem_limit_bytes=VMEM_LIMIT)


def _norm_rows(x, g):
    return (x * lax.rsqrt(jnp.mean(x * x, axis=-1, keepdims=True) + EPS)) * g


def _in_proj_kernel(x_ref, g_ref, w_ref, qkv_ref, gate_ref, h_ref):
    j = pl.program_id(1)
    last = pl.num_programs(1) - 1

    @pl.when(j == 0)
    def _():
        h_ref[...] = _norm_rows(x_ref[...], g_ref[...]).astype(h_ref.dtype)

    y = jnp.dot(h_ref[...], w_ref[...], preferred_element_type=jnp.float32)

    @pl.when(j < last)
    def _():
        qkv_ref[...] = y.astype(qkv_ref.dtype)

    @pl.when(j == last)
    def _():
        gate_ref[...] = y


def _in_proj(x2d, g, w, tm, tn):
    t, d = x2d.shape
    n_tiles = w.shape[1] // tn
    return pl.pallas_call(
        _in_proj_kernel,
        grid=(t // tm, n_tiles),
        in_specs=[
            pl.BlockSpec((tm, d), lambda i, j: (i, 0)),
            pl.BlockSpec((1, d), lambda i, j: (0, 0)),
            pl.BlockSpec((d, tn), lambda i, j: (0, j)),
        ],
        out_specs=[
            pl.BlockSpec((tm, tn), lambda i, j: (i, jnp.minimum(j, n_tiles - 2))),
            pl.BlockSpec((tm, tn), lambda i, j: (i, 0)),
        ],
        out_shape=[
            jax.ShapeDtypeStruct((t, (n_tiles - 1) * tn), jnp.bfloat16),
            jax.ShapeDtypeStruct((t, tn), jnp.float32),
        ],
        scratch_shapes=[pltpu.VMEM((tm, d), jnp.bfloat16)],
        compiler_params=_params(("parallel", "arbitrary")),
        name="norm_in_proj",
    )(x2d, g.reshape(1, d), w)


def _compress_kernel(ch_ref, w1_ref, posf_ref, w2_ref, o_ref):
    half = ch_ref.shape[1]
    ch = ch_ref[...]
    pa = jnp.dot(ch, w1_ref[:half, :], preferred_element_type=jnp.float32)
    pb = jnp.dot(ch, w1_ref[half:, :], preferred_element_type=jnp.float32)
    pos = jnp.dot(posf_ref[...], w1_ref[...], preferred_element_type=jnp.float32)[0:1, :]
    n = pa.shape[0]
    pre = pa + pltpu.roll(pb, n - 1, 0) + pos
    act = jax.nn.gelu(pre)
    o_ref[...] = jnp.dot(act.astype(jnp.bfloat16), w2_ref[...],
                         preferred_element_type=jnp.float32).astype(o_ref.dtype)


def _compress(chunks, w1, posf, w2):
    b, g, n, k = chunks.shape
    return pl.pallas_call(
        _compress_kernel,
        grid=(b, g),
        in_specs=[
            pl.BlockSpec((None, None, n, k), lambda i, j: (i, j, 0, 0)),
            pl.BlockSpec(w1.shape, lambda i, j: (0, 0)),
            pl.BlockSpec(posf.shape, lambda i, j: (0, 0)),
            pl.BlockSpec(w2.shape, lambda i, j: (0, 0)),
        ],
        out_specs=pl.BlockSpec((None, None, n, HEAD_DIM), lambda i, j: (i, j, 0, 0)),
        out_shape=jax.ShapeDtypeStruct((b, g, n, HEAD_DIM), jnp.bfloat16),
        compiler_params=_params(("parallel", "parallel")),
        name="compress_tokens",
    )(chunks, w1, posf, w2)


def _decay_kernel(f_ref, b_ref, tri_ref, o_ref, carry_ref):
    @pl.when(pl.program_id(0) == 0)
    def _():
        carry_ref[...] = jnp.zeros_like(carry_ref)

    x = f_ref[...] + b_ref[...]
    logf = (jnp.minimum(x, 0.0) - jnp.log1p(jnp.exp(-jnp.abs(x)))) * LOG2E
    carry = carry_ref[...]
    for seg in range(f_ref.shape[1] // LANES):
        part = jnp.dot(logf[:, seg * LANES:(seg + 1) * LANES], tri_ref[...],
                       preferred_element_type=jnp.float32, precision=lax.Precision.HIGHEST) + carry
        carry = part[:, LANES - 1:LANES]
        rest = part
        for term in range(DECAY_TERMS):
            piece = rest.astype(o_ref.dtype)
            o_ref[term, :, seg * LANES:(seg + 1) * LANES] = piece
            rest = rest - piece.astype(jnp.float32)
    carry_ref[...] = carry


def _decay_cumsum(f_t, bias_col, width):
    rows, s = f_t.shape
    tri = jnp.asarray(np.triu(np.ones((LANES, LANES), np.float32)))
    return pl.pallas_call(
        _decay_kernel,
        grid=(s // width,),
        in_specs=[
            pl.BlockSpec((rows, width), lambda i: (0, i)),
            pl.BlockSpec((rows, 1), lambda i: (0, 0)),
            pl.BlockSpec((LANES, LANES), lambda i: (0, 0)),
        ],
        out_specs=pl.BlockSpec((DECAY_TERMS, rows, width), lambda i: (0, 0, i)),
        out_shape=jax.ShapeDtypeStruct((DECAY_TERMS, rows, s), jnp.bfloat16),
        scratch_shapes=[pltpu.VMEM((rows, 1), jnp.float32)],
        compiler_params=_params(("arbitrary",)),
        name="decay_cumsum",
    )(f_t, bias_col, tri)


def _col_reduce(op, x):
    reduce = {jnp.maximum: jnp.max, jnp.minimum: jnp.min, jnp.add: jnp.sum}[op]
    return reduce(x, axis=0, keepdims=True)


def _flash_init(m_ref, acc_ref):
    m_ref[...] = jnp.full(m_ref.shape, NEG, jnp.float32)
    acc_ref[...] = jnp.zeros(acc_ref.shape, jnp.float32)


def _probabilities(s, m_new):
    return jnp.exp2((s - m_new).astype(jnp.bfloat16))


def _flash_result(acc_ref):
    return acc_ref[0:HEAD_DIM, :] / acc_ref[SUM_ROW:SUM_ROW + 1, :]


def _flash_step(kaug, vt, get_q, width, m_ref, acc_ref, extra=None):
    w = width // N_SPLIT
    strips = [(i * w, (i + 1) * w) for i in range(N_SPLIT)]
    scores = [jnp.dot(kaug, get_q(lo, hi), preferred_element_type=jnp.float32) for lo, hi in strips]
    for (lo, hi), s in zip(strips, scores):
        if extra is not None:
            s = extra(s, lo, hi)
        m_prev = m_ref[:, lo:hi]
        m_new = jnp.maximum(m_prev, _col_reduce(jnp.maximum, s))
        alpha = jnp.exp2(m_prev - m_new)
        acc_ref[:, lo:hi] = alpha * acc_ref[:, lo:hi] + jnp.dot(
            vt, _probabilities(s, m_new), preferred_element_type=jnp.float32)
        m_ref[:, lo:hi] = m_new


def _flash_far_chunks(n, load_keys, load_values, get_q_for, width, m_ref, acc_ref, s_ref, mx_ref, p_ref):
    w = width // N_SPLIT
    strips = [(i * w, (i + 1) * w) for i in range(N_SPLIT)]
    base = n % 2
    pairs = n // 2

    def scores_into(c, slot):
        kaug = load_keys(c)
        get_q = get_q_for(c)
        for lo, hi in strips:
            s = jnp.dot(kaug, get_q(lo, hi), preferred_element_type=jnp.float32)
            s_ref[slot, :, lo:hi] = s
            mx_ref[slot, :, lo:hi] = _col_reduce(jnp.maximum, s)

    def value_product(c, slot):
        vt = load_values(c)
        return jnp.concatenate([jnp.dot(vt, p_ref[slot, :, lo:hi], preferred_element_type=jnp.float32)
                                for lo, hi in strips], axis=1)

    @pl.when(base == 1)
    def _():
        _flash_step(load_keys(0), load_values(0), get_q_for(0), width, m_ref, acc_ref)

    @pl.when(pairs > 0)
    def _():
        scores_into(base, 0)
        p_ref[1] = jnp.zeros(p_ref.shape[1:], p_ref.dtype)

        def pair(j, carry):
            for cur in (0, 1):
                c = base + 2 * j + cur
                nxt = 1 - cur
                pv = value_product(jnp.maximum(c - 1, base), nxt)
                scores_into(jnp.minimum(c + 1, n - 1), nxt)
                alphas = []
                for lo, hi in strips:
                    m_prev = m_ref[:, lo:hi]
                    m_new = jnp.maximum(m_prev, mx_ref[cur, :, lo:hi])
                    m_ref[:, lo:hi] = m_new
                    for r in range(0, s_ref.shape[1], ROW_BLOCK):
                        p_ref[cur, r:r + ROW_BLOCK, lo:hi] = _probabilities(
                            s_ref[cur, r:r + ROW_BLOCK, lo:hi], m_new)
                    alphas.append(jnp.exp2(m_prev - m_new))
                acc_ref[...] = jnp.concatenate(alphas, axis=1) * (acc_ref[...] + pv)
            return carry

        lax.fori_loop(0, pairs, pair, 0)
        acc_ref[...] = acc_ref[...] + value_product(n - 1, 1)


def _group_queries_t(qt_ref):
    return jnp.concatenate([qt_ref[h] for h in range(NSA_HPG)], axis=1)


def _branch_gate(gates_t, head, branch):
    row = head * 3 + branch
    return gates_t[row:row + 1, :]


def _values_t(vt_ref, k0, n_keys):
    p0 = k0 // LANES
    return jnp.concatenate([vt_ref[p0 + j] for j in range(n_keys // LANES)], axis=1)


def _compressed_branch(n_rows, i0, q4t, kc_ref, vct_ref, band_ref, ovt_ref, sc_ref, oct_ref, imp_ref):
    sc_ref[0:n_rows, :] = jnp.dot(kc_ref[0:n_rows, :], q4t, preferred_element_type=jnp.float32)
    sc_ref[pl.ds(i0, BAND_ROWS), :] = sc_ref[pl.ds(i0, BAND_ROWS), :] + band_ref[...]
    row = lax.broadcasted_iota(jnp.int32, (n_rows, ROWS), 0)
    sc = jnp.where(row < i0 + BAND_ROWS, sc_ref[0:n_rows, :], NEG)
    m = _col_reduce(jnp.maximum, sc)
    p = jnp.exp2(sc - m)
    l = _col_reduce(jnp.add, p)
    pn = p * jnp.where(m > 0.5 * NEG, 1.0 / l, 0.0)
    oct_ref[...] = jnp.dot(vct_ref[:, 0:n_rows], pn.astype(jnp.bfloat16), preferred_element_type=jnp.float32)
    psum = pn[:, 0:Q_TILE]
    for h in range(1, NSA_HPG):
        psum = psum + pn[:, h * Q_TILE:(h + 1) * Q_TILE]
    imp_ref[...] = jnp.dot(ovt_ref[:, 0:n_rows], psum.astype(jnp.bfloat16), preferred_element_type=jnp.float32)


def _cmp_win_kernel(qt_ref, kc_ref, vct_ref, band_ref, ovt_ref, kw_ref, vwt_ref, near_ref, gt_ref,
                    ycw_ref, pen_ref, sc_ref, sw_ref, oct_ref, imp_ref):
    qb = pl.program_id(2)
    n_cmp = kc_ref.shape[0]
    n_slc = ovt_ref.shape[0]
    q4t = _group_queries_t(qt_ref)

    i0 = pl.multiple_of(jnp.maximum(8 * qb - 16, 0), 8)
    n_tiers = CMP_TIERS if n_cmp % (CMP_TIERS * LANES) == 0 else 1
    step = n_cmp // n_tiers
    for tier in range(1, n_tiers + 1):
        in_tier = (i0 + BAND_ROWS <= tier * step) & (i0 + BAND_ROWS > (tier - 1) * step)
        pl.when(in_tier)(functools.partial(_compressed_branch, tier * step, i0, q4t, kc_ref, vct_ref,
                                           band_ref, ovt_ref, sc_ref, oct_ref, imp_ref))
    oct_ = oct_ref[...]
    imp = imp_ref[...]

    ji = lax.broadcasted_iota(jnp.int32, (n_slc, Q_TILE), 0)
    jf = ji.astype(jnp.float32)
    t = qb * Q_TILE + lax.broadcasted_iota(jnp.int32, (n_slc, Q_TILE), 1)
    cur = t // SLC_BLOCK
    forced = (ji == 0) | (ji == cur) | (ji == cur - 1)
    score = jnp.where(ji <= cur, jnp.where(forced, FORCED_SCORE, imp), -1.0)
    pen_t = jnp.full((n_slc, Q_TILE), NEG, jnp.float32)
    for _ in range(min(SLC_TOPK, n_slc)):
        mx = _col_reduce(jnp.maximum, score)
        idx = _col_reduce(jnp.minimum, jnp.where(score == mx, jf, float(n_slc)))
        pick = jf == idx
        pen_t = jnp.where(pick, 0.0, pen_t)
        score = jnp.where(pick, -2.0, score)
    pad = pen_ref.shape[0] - n_slc
    if pad:
        pen_t = jnp.concatenate([pen_t, jnp.full((pad, Q_TILE), NEG, jnp.float32)], axis=0)
    pen_ref[...] = pen_t.astype(pen_ref.dtype)

    t0 = qb * Q_TILE
    ws = pl.multiple_of(jnp.maximum(t0 - WINDOW, 0), Q_TILE)
    ns = pl.multiple_of(jnp.maximum(t0 - Q_TILE, 0), Q_TILE)
    sw_ref[...] = jnp.dot(kw_ref[pl.ds(ws, WIN_KEYS), :], q4t, preferred_element_type=jnp.float32)
    off = pl.multiple_of(ns - ws, Q_TILE)
    sw_ref[pl.ds(off, NEAR_KEYS), :] = sw_ref[pl.ds(off, NEAR_KEYS), :] + near_ref[...]
    key = ws + lax.broadcasted_iota(jnp.int32, (WIN_KEYS, ROWS), 0)
    qry = t0 + (lax.broadcasted_iota(jnp.int32, (WIN_KEYS, ROWS), 1) & (Q_TILE - 1))
    sw = jnp.where((key <= qry) & (key > qry - WINDOW), sw_ref[...], NEG)
    mw = _col_reduce(jnp.maximum, sw)
    ow_sum = jnp.dot(_values_t(vwt_ref, ws, WIN_KEYS), _probabilities(sw, mw),
                     preferred_element_type=jnp.float32)
    owt = ow_sum[0:HEAD_DIM, :] / ow_sum[SUM_ROW:SUM_ROW + 1, :]

    gates_t = jax.nn.sigmoid(gt_ref[...])
    for h in range(NSA_HPG):
        cols = slice(h * Q_TILE, (h + 1) * Q_TILE)
        mixed = _branch_gate(gates_t, h, 0) * oct_[:, cols] + _branch_gate(gates_t, h, 2) * owt[:, cols]
        ycw_ref[:, h * HEAD_DIM:(h + 1) * HEAD_DIM] = mixed.T


def _cmp_win(qt, kc, vct, band, ovt, qkv, vwt, near, gates_t):
    b, s, _ = qkv.shape
    n_cmp = kc.shape[2]
    n_slc = ovt.shape[0]
    nq = s // Q_TILE
    pen_w = -(-n_slc // LANES) * LANES
    n_band = band.shape[0] - 1
    return pl.pallas_call(
        _cmp_win_kernel,
        grid=(b, NSA_GROUPS, nq),
        in_specs=[
            pl.BlockSpec((None, NSA_HPG, HEAD_DIM, Q_TILE), lambda i, g, q: (i, g, 0, q)),
            pl.BlockSpec((None, None, n_cmp, HEAD_DIM), lambda i, g, q: (i, g, 0, 0)),
            pl.BlockSpec((None, None, HEAD_DIM, n_cmp), lambda i, g, q: (i, g, 0, 0)),
            pl.BlockSpec((None, None, BAND_ROWS, ROWS), lambda i, g, q: (jnp.minimum(q, n_band), g, 0, 0)),
            pl.BlockSpec((n_slc, n_cmp), lambda i, g, q: (0, 0)),
            pl.BlockSpec((None, s, HEAD_DIM), lambda i, g, q: (i, 0, CB_KW + g)),
            pl.BlockSpec((None, None, s // LANES, ACC_ROWS, LANES), lambda i, g, q: (i, g, 0, 0, 0)),
            pl.BlockSpec((None, None, NEAR_KEYS, ROWS), lambda i, g, q: (jnp.minimum(q, 1), g, 0, 0)),
            pl.BlockSpec((None, None, GATE_ROWS, Q_TILE), lambda i, g, q: (i, g, 0, q)),
        ],
        out_specs=[
            pl.BlockSpec((None, Q_TILE, NSA_HPG * HEAD_DIM), lambda i, g, q: (i, q, g)),
            pl.BlockSpec((None, None, pen_w, Q_TILE), lambda i, g, q: (i, g, 0, q)),
        ],
        out_shape=[
            jax.ShapeDtypeStruct((b, s, NSA_Q_DIM), jnp.float32),
            jax.ShapeDtypeStruct((b, NSA_GROUPS, pen_w, s), jnp.bfloat16),
        ],
        scratch_shapes=[pltpu.VMEM((n_cmp, ROWS), jnp.float32), pltpu.VMEM((WIN_KEYS, ROWS), jnp.float32),
                        pltpu.VMEM((HEAD_DIM, ROWS), jnp.float32), pltpu.VMEM((n_slc, Q_TILE), jnp.float32)],
        compiler_params=_params(("parallel", "parallel", "arbitrary")),
        name="nsa_cmp_topk_win",
    )(qt, kc, vct, band, ovt, qkv, vwt, near, gates_t)


def _range_penalty(pen, first_blk, lo_ok, hi_ok):
    n_half = pen.shape[0] // LANES
    c = lax.broadcasted_iota(jnp.int32, (LANES, pen.shape[1]), 0)
    out = pen[0:LANES, :]
    blk = c
    for hf in range(1, n_half):
        in_lower = (c + (hf - 1) * LANES >= first_blk) & (first_blk < hf * LANES)
        out = jnp.where(in_lower, out, pen[hf * LANES:(hf + 1) * LANES, :])
        blk = jnp.where(in_lower, blk, c + hf * LANES)
    return jnp.where((blk >= lo_ok) & (blk < hi_ok), out, jnp.asarray(NEG, out.dtype))


def _sel_kernel(qt_ref, pen_ref, kaug_ref, vt_ref, near_ref, ycw_ref, gt_ref, y_ref,
                qaug_ref, m_ref, acc_ref, s_ref, mx_ref, p_ref):
    qb = pl.program_id(2)
    n_half = pen_ref.shape[0] // LANES
    n_blocks = pen_ref.shape[0]
    chunks_per_half = LANES * SLC_BLOCK // KEY_CHUNK
    blocks_per_chunk = KEY_CHUNK // SLC_BLOCK

    t0 = qb * Q_TILE
    near_end = t0 - Q_TILE
    nf = jnp.maximum(near_end // KEY_CHUNK, 0)
    ms = pl.multiple_of(jnp.maximum(near_end - KEY_CHUNK, 0), Q_TILE)
    ns = pl.multiple_of(jnp.maximum(near_end, 0), Q_TILE)

    q4t = _group_queries_t(qt_ref)
    pen = pen_ref[...]
    operands = [pen[hf * LANES:(hf + 1) * LANES, :] for hf in range(n_half)]
    operands.append(_range_penalty(pen, ms // SLC_BLOCK, nf * blocks_per_chunk, near_end // SLC_BLOCK))
    operands.append(_range_penalty(pen, ns // SLC_BLOCK, 0, n_blocks))
    for idx, channels in enumerate(operands):
        qaug_ref[idx, 0:HEAD_DIM, :] = q4t
        qaug_ref[idx, HEAD_DIM:, :] = jnp.concatenate([channels] * NSA_HPG, axis=1)

    _flash_init(m_ref, acc_ref)

    def key_start(c):
        return pl.multiple_of(jnp.where(c < nf, c * KEY_CHUNK, ms), Q_TILE)

    def get_q_for(c):
        operand = jnp.where(c < nf, c // chunks_per_half, n_half)
        return lambda lo, hi: qaug_ref[operand, :, lo:hi]

    _flash_far_chunks(nf + 1, lambda c: kaug_ref[pl.ds(key_start(c), KEY_CHUNK), :],
                      lambda c: _values_t(vt_ref, key_start(c), KEY_CHUNK), get_q_for,
                      ROWS, m_ref, acc_ref, s_ref, mx_ref, p_ref)

    _flash_step(kaug_ref[pl.ds(ns, NEAR_KEYS), :], _values_t(vt_ref, ns, NEAR_KEYS),
                lambda lo, hi: qaug_ref[n_half + 1, :, lo:hi], ROWS, m_ref, acc_ref,
                extra=lambda s, lo, hi: s + near_ref[:, lo:hi])

    o = _flash_result(acc_ref)
    gates_t = jax.nn.sigmoid(gt_ref[...])
    for h in range(NSA_HPG):
        sel_h = (_branch_gate(gates_t, h, 1) * o[:, h * Q_TILE:(h + 1) * Q_TILE]).T
        y_ref[:, h * HEAD_DIM:(h + 1) * HEAD_DIM] = (
            ycw_ref[:, h * HEAD_DIM:(h + 1) * HEAD_DIM] + sel_h).astype(y_ref.dtype)


def _sel(qt, pen, kaug, vt_pieces, near, ycw, gates_t):
    b, _, s, _ = kaug.shape
    nq = s // Q_TILE
    pen_w = pen.shape[2]
    return pl.pallas_call(
        _sel_kernel,
        grid=(b, NSA_GROUPS, nq),
        in_specs=[
            pl.BlockSpec((None, NSA_HPG, HEAD_DIM, Q_TILE), lambda i, g, q: (i, g, 0, q)),
            pl.BlockSpec((None, None, pen_w, Q_TILE), lambda i, g, q: (i, g, 0, q)),
            pl.BlockSpec((None, None, s, 2 * HEAD_DIM), lambda i, g, q: (i, g, 0, 0)),
            pl.BlockSpec((None, None, s // LANES, ACC_ROWS, LANES), lambda i, g, q: (i, g, 0, 0, 0)),
            pl.BlockSpec((None, None, NEAR_KEYS, ROWS), lambda i, g, q: (jnp.minimum(q, 1), g, 0, 0)),
            pl.BlockSpec((None, Q_TILE, NSA_HPG * HEAD_DIM), lambda i, g, q: (i, q, g)),
            pl.BlockSpec((None, None, GATE_ROWS, Q_TILE), lambda i, g, q: (i, g, 0, q)),
        ],
        out_specs=pl.BlockSpec((None, Q_TILE, NSA_HPG * HEAD_DIM), lambda i, g, q: (i, q, g)),
        out_shape=jax.ShapeDtypeStruct((b, s, NSA_Q_DIM), jnp.bfloat16),
        scratch_shapes=[
            pltpu.VMEM((pen_w // LANES + 2, 2 * HEAD_DIM, ROWS), jnp.bfloat16),
            pltpu.VMEM((1, ROWS), jnp.float32),
            pltpu.VMEM((ACC_ROWS, ROWS), jnp.float32),
            pltpu.VMEM((2, KEY_CHUNK, ROWS), jnp.float32),
            pltpu.VMEM((2, 1, ROWS), jnp.float32),
            pltpu.VMEM((2, KEY_CHUNK, ROWS), jnp.bfloat16),
        ],
        compiler_params=_params(("parallel", "parallel", "arbitrary")),
        name="nsa_selected",
    )(qt, pen, kaug, vt_pieces, near, ycw, gates_t)


def _fox_kernel(nce_ref, qt_ref, k_ref, dec_ref, vt_ref, o_ref,
                qaug_ref, m_ref, acc_ref, s_ref, mx_ref, p_ref, knorm_ref):
    qi = pl.program_id(2)
    bh = pl.program_id(0) * pl.num_programs(1) + pl.program_id(1)
    tq = qt_ref.shape[1]
    row = lax.broadcasted_iota(jnp.int32, (HEAD_DIM, tq), 0)
    qaug_ref[0:HEAD_DIM, :] = qt_ref[...]
    qaug_ref[HEAD_DIM:, :] = jnp.where(row < DECAY_TERMS, -1.0, 0.0).astype(qaug_ref.dtype)
    _flash_init(m_ref, acc_ref)

    @pl.when(qi == 0)
    def _():
        def chunk_max(c, best):
            kf = k_ref[pl.ds(pl.multiple_of(c * tq, tq), tq), :].astype(jnp.float32)
            return jnp.maximum(best, jnp.max(jnp.sum(kf * kf, axis=1, keepdims=True)))

        knorm_ref[0] = jnp.sqrt(lax.fori_loop(0, k_ref.shape[0] // tq, chunk_max, jnp.float32(0.0)))

    qf = qt_ref[...].astype(jnp.float32)
    q_norm = jnp.sqrt(jnp.max(jnp.sum(qf * qf, axis=0, keepdims=True)))

    def get_q(lo, hi):
        return qaug_ref[:, lo:hi]

    def load_keys(c):
        k0 = pl.multiple_of(c * tq, tq)
        return jnp.concatenate([k_ref[pl.ds(k0, tq), :], dec_ref[pl.ds(k0, tq), :]], axis=1)

    def causal(s, lo, hi):
        key = lax.broadcasted_iota(jnp.int32, s.shape, 0)
        qry = lo + lax.broadcasted_iota(jnp.int32, s.shape, 1)
        return jnp.where(key <= qry, s, NEG)

    reach = 2.0 * q_norm * knorm_ref[0] + SKIP_MARGIN_LOG2
    anchor = nce_ref[bh, jnp.maximum(qi - 1, 0)]

    def count(c, n_skip):
        return n_skip + jnp.where(anchor - nce_ref[bh, c] > reach, 1, 0)

    first = lax.fori_loop(0, qi, count, jnp.int32(0))
    wide = (qi >= WIDE_CHUNKS - 1) & (qi - first < WIDE_CHUNKS)

    @pl.when(wide)
    def _():
        far_keys = (WIDE_CHUNKS - 1) * tq
        c0 = qi - (WIDE_CHUNKS - 1)
        k0 = pl.multiple_of(c0 * tq, tq)
        kaug = jnp.concatenate([k_ref[pl.ds(k0, far_keys + tq), :], dec_ref[pl.ds(k0, far_keys + tq), :]], axis=1)
        vt = jnp.concatenate([vt_ref[c0 + j] for j in range(WIDE_CHUNKS)], axis=1)
        _flash_step(kaug, vt, get_q, tq, m_ref, acc_ref,
                    extra=lambda s, lo, hi: jnp.concatenate([s[:far_keys], causal(s[far_keys:], lo, hi)], axis=0))

    @pl.when(jnp.logical_not(wide))
    def _():
        _flash_step(load_keys(qi), vt_ref[qi], get_q, tq, m_ref, acc_ref, extra=causal)
        _flash_far_chunks(qi - first, lambda c: load_keys(c + first), lambda c: vt_ref[c + first],
                          lambda c: get_q, tq, m_ref, acc_ref, s_ref, mx_ref, p_ref)

    o_ref[...] = _flash_result(acc_ref).T.astype(o_ref.dtype)


def _fox(neg_cum_end, qkv, qt, dec, vt, tq):
    b, s, _ = qkv.shape
    return pl.pallas_call(
        _fox_kernel,
        grid=(b, FOX_HEADS, s // tq),
        in_specs=[
            pl.BlockSpec(memory_space=pltpu.SMEM),
            pl.BlockSpec((None, None, HEAD_DIM, tq), lambda i, h, q: (i, NSA_HEADS + h, 0, q)),
            pl.BlockSpec((None, s, HEAD_DIM), lambda i, h, q: (i, 0, CB_KF + h)),
            pl.BlockSpec((None, None, s, LANES), lambda i, h, q: (i, h, 0, 0)),
            pl.BlockSpec((None, None, s // tq, ACC_ROWS, tq), lambda i, h, q: (i, h, 0, 0, 0)),
        ],
        out_specs=pl.BlockSpec((None, tq, HEAD_DIM), lambda i, h, q: (i, q, h)),
        out_shape=jax.ShapeDtypeStruct((b, s, FOX_DIM), jnp.bfloat16),
        scratch_shapes=[
            pltpu.VMEM((2 * HEAD_DIM, tq), jnp.bfloat16),
            pltpu.VMEM((1, tq), jnp.float32),
            pltpu.VMEM((ACC_ROWS, tq), jnp.float32),
            pltpu.VMEM((2, tq, tq), jnp.float32),
            pltpu.VMEM((2, 1, tq), jnp.float32),
            pltpu.VMEM((2, tq, tq), jnp.bfloat16),
            pltpu.SMEM((1,), jnp.float32),
        ],
        compiler_params=_params(("parallel", "parallel", "arbitrary")),
        name="fox_attention",
    )(neg_cum_end, qt, qkv, dec, vt)


def _merge_kernel(x_ref, g_ref, wa_ref, wb_ref, pa_ref, pb_ref, wo_ref, ya_ref, yb_ref, o_ref,
                  h_ref, acc_ref):
    j = pl.program_id(1)

    @pl.when(j == 0)
    def _():
        h_ref[...] = _norm_rows(x_ref[...], g_ref[...]).astype(h_ref.dtype)
        acc_ref[...] = jnp.zeros_like(acc_ref)

    h = h_ref[...]
    ga = jax.nn.sigmoid(jnp.dot(h, wa_ref[...], preferred_element_type=jnp.float32))
    gb = jax.nn.sigmoid(jnp.dot(h, wb_ref[...], preferred_element_type=jnp.float32))
    a = jnp.dot(ya_ref[...], pa_ref[...], preferred_element_type=jnp.float32)
    bb = jnp.dot(yb_ref[...], pb_ref[...], preferred_element_type=jnp.float32)
    merged = (ga * a + gb * bb).astype(jnp.bfloat16)
    acc_ref[...] += jnp.dot(merged, wo_ref[...], preferred_element_type=jnp.float32)

    @pl.when(j == pl.num_programs(1) - 1)
    def _():
        o_ref[...] = x_ref[...] + acc_ref[...]


def _merge(x2d, g, w_ma, w_mb, p_a, p_b, w_out, y_a, y_b, tm, tn):
    t, d = x2d.shape
    ka = p_a.shape[0]
    kb = p_b.shape[0]
    return pl.pallas_call(
        _merge_kernel,
        grid=(t // tm, d // tn),
        in_specs=[
            pl.BlockSpec((tm, d), lambda i, j: (i, 0)),
            pl.BlockSpec((1, d), lambda i, j: (0, 0)),
            pl.BlockSpec((d, tn), lambda i, j: (0, j)),
            pl.BlockSpec((d, tn), lambda i, j: (0, j)),
            pl.BlockSpec((ka, tn), lambda i, j: (0, j)),
            pl.BlockSpec((kb, tn), lambda i, j: (0, j)),
            pl.BlockSpec((tn, d), lambda i, j: (j, 0)),
            pl.BlockSpec((tm, ka), lambda i, j: (i, 0)),
            pl.BlockSpec((tm, kb), lambda i, j: (i, 0)),
        ],
        out_specs=pl.BlockSpec((tm, d), lambda i, j: (i, 0)),
        out_shape=jax.ShapeDtypeStruct((t, d), jnp.float32),
        scratch_shapes=[pltpu.VMEM((tm, d), jnp.bfloat16), pltpu.VMEM((tm, d), jnp.float32)],
        compiler_params=_params(("parallel", "arbitrary")),
        name="merge_out_proj",
    )(x2d, g.reshape(1, d), w_ma, w_mb, p_a, p_b, w_out, y_a, y_b)


def _ffn_kernel(x_ref, xh_ref, g_ref, wu_ref, wv_ref, cw_ref, cb_ref, wd_ref, gf_ref, o_ref,
                h_ref, u_ref, acc_ref, *, seq, tm):
    i = pl.program_id(0)
    j = pl.program_id(1)
    halo = BF16_SUBLANES

    @pl.when(j == 0)
    def _():
        g = g_ref[...]
        keep = jnp.where((i * tm) % seq == 0, 0.0, 1.0)
        h_ref[0:halo, :] = (_norm_rows(xh_ref[...], g) * keep).astype(h_ref.dtype)
        h_ref[halo:, :] = _norm_rows(x_ref[...], g).astype(h_ref.dtype)
        acc_ref[...] = jnp.zeros_like(acc_ref)

    u_ref[...] = jnp.dot(h_ref[...], wu_ref[...], preferred_element_type=jnp.float32)
    v = jnp.dot(h_ref[halo:, :], wv_ref[...], preferred_element_type=jnp.float32)
    conv = cb_ref[...]
    for k in range(CONV_WIDTH):
        conv = conv + cw_ref[k:k + 1, :] * u_ref[pl.ds(halo - (CONV_WIDTH - 1) + k, tm), :]
    act = (jax.nn.gelu(conv) * v).astype(jnp.bfloat16)
    acc_ref[...] += jnp.dot(act, wd_ref[...], preferred_element_type=jnp.float32)

    @pl.when(j == pl.num_programs(1) - 1)
    def _():
        o_ref[...] = _norm_rows(x_ref[...] + acc_ref[...], gf_ref[...])


def _ffn(x2d, g, w_up, conv_w, conv_b, w_down, g_final, seq, tm, tn):
    t, d = x2d.shape
    d_ff = w_down.shape[0]
    nt = d_ff // tn
    halo = BF16_SUBLANES
    per = tm // halo
    return pl.pallas_call(
        functools.partial(_ffn_kernel, seq=seq, tm=tm),
        grid=(t // tm, nt),
        in_specs=[
            pl.BlockSpec((tm, d), lambda i, j: (i, 0)),
            pl.BlockSpec((halo, d), lambda i, j: (jnp.maximum(i * per - 1, 0), 0)),
            pl.BlockSpec((1, d), lambda i, j: (0, 0)),
            pl.BlockSpec((d, tn), lambda i, j: (0, j)),
            pl.BlockSpec((d, tn), lambda i, j: (0, nt + j)),
            pl.BlockSpec((CONV_WIDTH, tn), lambda i, j: (0, j)),
            pl.BlockSpec((1, tn), lambda i, j: (0, j)),
            pl.BlockSpec((tn, d), lambda i, j: (j, 0)),
            pl.BlockSpec((1, d), lambda i, j: (0, 0)),
        ],
        out_specs=pl.BlockSpec((tm, d), lambda i, j: (i, 0)),
        out_shape=jax.ShapeDtypeStruct((t, d), jnp.float32),
        scratch_shapes=[
            pltpu.VMEM((tm + halo, d), jnp.bfloat16),
            pltpu.VMEM((tm + halo, tn), jnp.float32),
            pltpu.VMEM((tm, d), jnp.float32),
        ],
        compiler_params=_params(("parallel", "arbitrary")),
        name="ffn_final_norm",
    )(x2d, x2d, g.reshape(1, d), w_up, w_up, conv_w, conv_b.reshape(1, d_ff), w_down, g_final.reshape(1, d))


def _t5_bucket_np(dist):
    n = np.maximum(dist, 0)
    max_exact = REL_BUCKETS // 2
    nf = np.maximum(n, 1).astype(np.float32)
    large = max_exact + (np.log(nf / np.float32(max_exact)) / np.float32(math.log(REL_MAX_DIST / max_exact))
                         * np.float32(REL_BUCKETS - max_exact)).astype(np.int32)
    return np.where(n < max_exact, n, np.minimum(large, REL_BUCKETS - 1)).astype(np.int32)


def _bias_by_distance(rel_table, far_shift):
    vals = rel_table[jnp.asarray(_t5_bucket_np(np.arange(REL_MAX_DIST + 1)))]
    if far_shift:
        vals = vals - rel_table[REL_BUCKETS - 1]
    return vals * LOG2E


def _near_bias(rel_table):
    fd = _bias_by_distance(rel_table, True)
    h = fd.shape[1]
    lo = NEAR_KEYS - 1
    vec = jnp.concatenate([jnp.full((lo, h), NEG, fd.dtype), fd[:REL_MAX_DIST],
                           jnp.zeros((NEAR_KEYS - REL_MAX_DIST, h), fd.dtype)], axis=0)
    def toeplitz(start):
        period = Q_TILE + NEAR_KEYS - 1
        window = jnp.concatenate([vec[start:start + Q_TILE], vec[start - (NEAR_KEYS - 1):start]], axis=0)
        flat = jnp.tile(window, (NEAR_KEYS, 1))[:NEAR_KEYS * (period - 1)]
        return flat.reshape(NEAR_KEYS, period - 1, h)[:, :Q_TILE]

    tiles = jnp.stack([toeplitz(lo + a) for a in (0, Q_TILE)])
    tiles = tiles.reshape(2, NEAR_KEYS, Q_TILE, NSA_GROUPS, NSA_HPG).transpose(0, 3, 1, 4, 2)
    return tiles.reshape(2, NSA_GROUPS, NEAR_KEYS, ROWS)


def _band_bias(rel_table):
    fd = _bias_by_distance(rel_table, True)
    tl = np.arange(Q_TILE)[None, :]
    r = np.arange(BAND_ROWS)[:, None]
    dist = np.stack([tl - CMP_STRIDE * (r - off) - (CMP_BLOCK - 1) for off in (0, 8, 16)])
    vals = fd[jnp.asarray(np.clip(dist, 0, REL_MAX_DIST))]
    vals = jnp.where(jnp.asarray(dist >= 0)[..., None], vals, NEG)
    v, rr, q, _ = vals.shape
    return vals.transpose(0, 1, 3, 2).reshape(v, rr, NSA_GROUPS, NSA_HPG * q).transpose(0, 2, 1, 3)


def _overlap_t(n_cmp_pad, n_slc):
    i = np.arange(n_cmp_pad)[None, :]
    jj = np.arange(n_slc)[:, None]
    c_start = i * CMP_STRIDE
    ov = (c_start < jj * SLC_BLOCK + SLC_BLOCK) & (c_start + CMP_BLOCK - 1 >= jj * SLC_BLOCK)
    ov = ov & (i < n_cmp_pad - 1)
    return jnp.asarray(ov.astype(np.float32), jnp.bfloat16)


def _block_onehot(seq):
    blk = (np.arange(seq) // SLC_BLOCK) % LANES
    return jnp.asarray((blk[:, None] == np.arange(LANES)[None, :]).astype(np.float32), jnp.bfloat16)


def _pick_tile(n, pref):
    return pref if n % pref == 0 else n


def kernel(x, attn_norm_g, w_in, cmp_pos_k, cmp_w1_k, cmp_w2_k, cmp_pos_v, cmp_w1_v, cmp_w2_v,
           rel_bias_table, fox_forget_bias, w_branch_nsa, w_branch_fox, w_out,
           ffn_norm_g, w_up, conv_w, conv_b, w_down, final_norm_g):
    assert w_in.shape[0] == 1, "the final norm is fused into the single layer's FFN kernel"
    bsz, seq, d = x.shape
    t = bsz * seq
    bf = jnp.bfloat16
    scale = HEAD_DIM ** -0.5 * LOG2E
    x2d = x.reshape(t, d)
    w_in = w_in[0]

    o = np.cumsum([0, NSA_Q_DIM] + [NSA_KV_DIM] * 6 + [3 * NSA_HEADS, FOX_DIM, FOX_DIM, FOX_DIM, FOX_HEADS, d, d])
    w_qkv = jnp.concatenate([w_in[:, o[0]:o[1]] * scale, w_in[:, o[1]:o[7]],
                             w_in[:, o[8]:o[9]] * scale, w_in[:, o[9]:o[11]]], axis=1).astype(bf)
    gate_cols = []
    for grp in range(NSA_GROUPS):
        gate_cols += [w_in[:, o[7] + grp * GATES_PER_GROUP:o[7] + (grp + 1) * GATES_PER_GROUP],
                      jnp.zeros((d, LANES - GATES_PER_GROUP), w_in.dtype)]
    gate_cols += [w_in[:, o[11]:o[12]], jnp.zeros((d, IN_PROJ_TILE - N_GATE + LANES - FOX_HEADS), w_in.dtype)]
    w_gate = jnp.concatenate(gate_cols, axis=1).astype(bf)
    w_ma = w_in[:, o[12]:o[13]].astype(bf)
    w_mb = w_in[:, o[13]:o[14]].astype(bf)

    qkv, gates = _in_proj(x2d, attn_norm_g[0], jnp.concatenate([w_qkv, w_gate], axis=1),
                          _pick_tile(t, 1024), IN_PROJ_TILE)
    qkv = qkv.reshape(bsz, seq, N_QKV)
    gates = gates.reshape(bsz, seq, IN_PROJ_TILE)

    q_all = jnp.concatenate([qkv[:, :, :NSA_Q_DIM], qkv[:, :, CB_QF * HEAD_DIM:(CB_QF + FOX_HEADS) * HEAD_DIM]],
                            axis=-1)
    qt = q_all.reshape(bsz, seq, NSA_HEADS + FOX_HEADS, HEAD_DIM).transpose(0, 2, 3, 1)

    def heads_major(cb, heads):
        return qkv[:, :, cb * HEAD_DIM:(cb + heads) * HEAD_DIM].reshape(
            bsz, seq, heads, HEAD_DIM).transpose(0, 2, 1, 3)

    def values_t(cb, heads, piece):
        v = qkv[:, :, cb * HEAD_DIM:(cb + heads) * HEAD_DIM]
        v = v.reshape(bsz, seq // piece, piece, heads, HEAD_DIM).transpose(0, 3, 1, 4, 2)
        ones_row = (jnp.arange(ACC_ROWS - HEAD_DIM) == 0).astype(bf)[:, None]
        extra = jnp.broadcast_to(ones_row, v.shape[:3] + (ACC_ROWS - HEAD_DIM, piece))
        return jnp.concatenate([v, extra], axis=3)

    n_ch = seq // CMP_STRIDE

    def chunked(cb):
        sl = qkv[:, :, cb * HEAD_DIM:(cb + NSA_GROUPS) * HEAD_DIM]
        sl = sl.reshape(bsz, n_ch, CMP_STRIDE, NSA_GROUPS, HEAD_DIM).transpose(0, 3, 1, 2, 4)
        return sl.reshape(bsz, NSA_GROUPS, n_ch, CMP_STRIDE * HEAD_DIM)

    def posflat(pos):
        return jnp.broadcast_to(pos.reshape(1, CMP_BLOCK * HEAD_DIM), (8, CMP_BLOCK * HEAD_DIM)).astype(bf)

    kc = _compress(chunked(CB_KC), cmp_w1_k[0].astype(bf), posflat(cmp_pos_k[0]), cmp_w2_k[0].astype(bf))
    vc = _compress(chunked(CB_VC), cmp_w1_v[0].astype(bf), posflat(cmp_pos_v[0]), cmp_w2_v[0].astype(bf))
    vct = vc.transpose(0, 1, 3, 2)

    f_t = gates[:, :, NSA_GROUPS * LANES:NSA_GROUPS * LANES + FOX_HEADS]
    f_t = f_t.transpose(0, 2, 1).reshape(bsz * FOX_HEADS, seq)
    bias_col = jnp.tile(fox_forget_bias[0].astype(jnp.float32), bsz).reshape(bsz * FOX_HEADS, 1)
    terms = _decay_cumsum(f_t, bias_col, _pick_tile(seq, 2048))
    dec = jnp.pad(terms.transpose(1, 2, 0), ((0, 0), (0, 0), (0, LANES - DECAY_TERMS)))
    dec = dec.reshape(bsz, FOX_HEADS, seq, LANES)

    n_slc = seq // SLC_BLOCK
    near = _near_bias(rel_bias_table)
    gates_t = gates[:, :, :NSA_GROUPS * LANES].reshape(bsz, seq, NSA_GROUPS, LANES)[..., :GATE_ROWS]
    gates_t = gates_t.transpose(0, 2, 3, 1)
    ycw, pen = _cmp_win(qt, kc, vct, _band_bias(rel_bias_table), _overlap_t(n_ch, n_slc), qkv,
                        values_t(CB_VW, NSA_GROUPS, LANES), near, gates_t)
    ks = heads_major(CB_KS, NSA_GROUPS)
    kaug_sel = jnp.concatenate([ks, jnp.broadcast_to(_block_onehot(seq), ks.shape)], axis=-1)
    y_nsa = _sel(qt, pen, kaug_sel, values_t(CB_VS, NSA_GROUPS, LANES), near, ycw, gates_t)

    fox_tq = _pick_tile(seq, FOX_Q_TILE)
    neg_cum_end = -jnp.sum(terms[:, :, fox_tq - 1::fox_tq].astype(jnp.float32), axis=0)
    y_fox = _fox(neg_cum_end, qkv, qt, dec, values_t(CB_VF, FOX_HEADS, fox_tq), fox_tq)

    tm2 = _pick_tile(t, 512)
    x_mid = _merge(x2d, attn_norm_g[0], w_ma, w_mb, w_branch_nsa[0].astype(bf), w_branch_fox[0].astype(bf),
                   w_out[0].astype(bf), y_nsa.reshape(t, NSA_Q_DIM), y_fox.reshape(t, FOX_DIM),
                   tm2, _pick_tile(d, 512))
    d_ff = w_down.shape[1]
    out = _ffn(x_mid, ffn_norm_g[0], w_up[0].astype(bf), conv_w[0], conv_b[0], w_down[0].astype(bf),
               final_norm_g, seq, tm2, _pick_tile(d_ff, 512))
    return out.reshape(bsz, seq, d)
```

```python
import functools
import math

import jax
import jax.numpy as jnp
import numpy as np
from jax import lax
from jax.experimental import pallas as pl
from jax.experimental.pallas import tpu as pltpu

HEAD_DIM = 128
NSA_HEADS = 8
NSA_GROUPS = 2
NSA_HPG = NSA_HEADS // NSA_GROUPS
FOX_HEADS = 8
CMP_BLOCK = 32
CMP_STRIDE = 16
SLC_BLOCK = 64
SLC_TOPK = 16
WINDOW = 512
REL_BUCKETS = 32
REL_MAX_DIST = 128
CONV_WIDTH = 3
EPS = 1e-6
NEG = -1e30
FORCED_SCORE = 1e4
LOG2E = math.log2(math.e)

LANES = 128
BF16_SUBLANES = 16
VMEM_LIMIT = 56 * 1024 * 1024

NSA_Q_DIM = NSA_HEADS * HEAD_DIM
NSA_KV_DIM = NSA_GROUPS * HEAD_DIM
FOX_DIM = FOX_HEADS * HEAD_DIM
N_QKV = NSA_Q_DIM + 6 * NSA_KV_DIM + 3 * FOX_DIM
GATES_PER_GROUP = 3 * NSA_HPG
N_GATE = (NSA_GROUPS + 1) * LANES
IN_PROJ_TILE = 512

CB_KC = 8
CB_VC = 10
CB_KS = 12
CB_VS = 14
CB_KW = 16
CB_VW = 18
CB_QF = 20
CB_KF = 28
CB_VF = 36

Q_TILE = 128
ROWS = NSA_HPG * Q_TILE
KEY_CHUNK = 512
FOX_Q_TILE = 512
N_SPLIT = 2
SKIP_MARGIN_LOG2 = 64.0
ROW_BLOCK = 32
DECAY_TERMS = 3
BAND_ROWS = 24
CMP_TIERS = 4
WIN_KEYS = WINDOW + Q_TILE
NEAR_KEYS = 2 * Q_TILE
GATE_ROWS = 16
SUM_ROW = HEAD_DIM
ACC_ROWS = HEAD_DIM + 8


def _params(sem):
    return pltpu.CompilerParams(dimension_semantics=sem, vmem_limit_bytes=VMEM_LIMIT)


def _norm_rows(x, g):
    return (x * lax.rsqrt(jnp.mean(x * x, axis=-1, keepdims=True) + EPS)) * g


def _in_proj_kernel(x_ref, g_ref, w_ref, qkv_ref, gate_ref, h_ref):
    j = pl.program_id(1)
    last = pl.num_programs(1) - 1

    @pl.when(j == 0)
    def _():
        h_ref[...] = _norm_rows(x_ref[...], g_ref[...]).astype(h_ref.dtype)

    y = jnp.dot(h_ref[...], w_ref[...], preferred_element_type=jnp.float32)

    @pl.when(j < last)
    def _():
        qkv_ref[...] = y.astype(qkv_ref.dtype)

    @pl.when(j == last)
    def _():
        gate_ref[...] = y


def _in_proj(x2d, g, w, tm, tn):
    t, d = x2d.shape
    n_tiles = w.shape[1] // tn
    return pl.pallas_call(
        _in_proj_kernel,
        grid=(t // tm, n_tiles),
        in_specs=[
            pl.BlockSpec((tm, d), lambda i, j: (i, 0)),
            pl.BlockSpec((1, d), lambda i, j: (0, 0)),
            pl.BlockSpec((d, tn), lambda i, j: (0, j)),
        ],
        out_specs=[
            pl.BlockSpec((tm, tn), lambda i, j: (i, jnp.minimum(j, n_tiles - 2))),
            pl.BlockSpec((tm, tn), lambda i, j: (i, 0)),
        ],
        out_shape=[
            jax.ShapeDtypeStruct((t, (n_tiles - 1) * tn), jnp.bfloat16),
            jax.ShapeDtypeStruct((t, tn), jnp.float32),
        ],
        scratch_shapes=[pltpu.VMEM((tm, d), jnp.bfloat16)],
        compiler_params=_params(("parallel", "arbitrary")),
        name="norm_in_proj",
    )(x2d, g.reshape(1, d), w)


def _compress_kernel(ch_ref, w1_ref, posf_ref, w2_ref, o_ref):
    half = ch_ref.shape[1]
    ch = ch_ref[...]
    pa = jnp.dot(ch, w1_ref[:half, :], preferred_element_type=jnp.float32)
    pb = jnp.dot(ch, w1_ref[half:, :], preferred_element_type=jnp.float32)
    pos = jnp.dot(posf_ref[...], w1_ref[...], preferred_element_type=jnp.float32)[0:1, :]
    n = pa.shape[0]
    pre = pa + pltpu.roll(pb, n - 1, 0) + pos
    act = jax.nn.gelu(pre)
    o_ref[...] = jnp.dot(act.astype(jnp.bfloat16), w2_ref[...],
                         preferred_element_type=jnp.float32).astype(o_ref.dtype)


def _compress(chunks, w1, posf, w2):
    b, g, n, k = chunks.shape
    return pl.pallas_call(
        _compress_kernel,
        grid=(b, g),
        in_specs=[
            pl.BlockSpec((None, None, n, k), lambda i, j: (i, j, 0, 0)),
            pl.BlockSpec(w1.shape, lambda i, j: (0, 0)),
            pl.BlockSpec(posf.shape, lambda i, j: (0, 0)),
            pl.BlockSpec(w2.shape, lambda i, j: (0, 0)),
        ],
        out_specs=pl.BlockSpec((None, None, n, HEAD_DIM), lambda i, j: (i, j, 0, 0)),
        out_shape=jax.ShapeDtypeStruct((b, g, n, HEAD_DIM), jnp.bfloat16),
        compiler_params=_params(("parallel", "parallel")),
        name="compress_tokens",
    )(chunks, w1, posf, w2)


def _decay_kernel(f_ref, b_ref, tri_ref, o_ref, carry_ref):
    @pl.when(pl.program_id(0) == 0)
    def _():
        carry_ref[...] = jnp.zeros_like(carry_ref)

    x = f_ref[...] + b_ref[...]
    logf = (jnp.minimum(x, 0.0) - jnp.log1p(jnp.exp(-jnp.abs(x)))) * LOG2E
    carry = carry_ref[...]
    for seg in range(f_ref.shape[1] // LANES):
        part = jnp.dot(logf[:, seg * LANES:(seg + 1) * LANES], tri_ref[...],
                       preferred_element_type=jnp.float32, precision=lax.Precision.HIGHEST) + carry
        carry = part[:, LANES - 1:LANES]
        rest = part
        for term in range(DECAY_TERMS):
            piece = rest.astype(o_ref.dtype)
            o_ref[term, :, seg * LANES:(seg + 1) * LANES] = piece
            rest = rest - piece.astype(jnp.float32)
    carry_ref[...] = carry


def _decay_cumsum(f_t, bias_col, width):
    rows, s = f_t.shape
    tri = jnp.asarray(np.triu(np.ones((LANES, LANES), np.float32)))
    return pl.pallas_call(
        _decay_kernel,
        grid=(s // width,),
        in_specs=[
            pl.BlockSpec((rows, width), lambda i: (0, i)),
            pl.BlockSpec((rows, 1), lambda i: (0, 0)),
            pl.BlockSpec((LANES, LANES), lambda i: (0, 0)),
        ],
        out_specs=pl.BlockSpec((DECAY_TERMS, rows, width), lambda i: (0, 0, i)),
        out_shape=jax.ShapeDtypeStruct((DECAY_TERMS, rows, s), jnp.bfloat16),
        scratch_shapes=[pltpu.VMEM((rows, 1), jnp.float32)],
        compiler_params=_params(("arbitrary",)),
        name="decay_cumsum",
    )(f_t, bias_col, tri)


def _col_reduce(op, x):
    reduce = {jnp.maximum: jnp.max, jnp.minimum: jnp.min, jnp.add: jnp.sum}[op]
    return reduce(x, axis=0, keepdims=True)


def _flash_init(m_ref, acc_ref):
    m_ref[...] = jnp.full(m_ref.shape, NEG, jnp.float32)
    acc_ref[...] = jnp.zeros(acc_ref.shape, jnp.float32)


def _probabilities(s, m_new):
    return jnp.exp2((s - m_new).astype(jnp.bfloat16))


def _flash_result(acc_ref):
    return acc_ref[0:HEAD_DIM, :] / acc_ref[SUM_ROW:SUM_ROW + 1, :]


def _flash_step(kaug, vt, get_q, width, m_ref, acc_ref, extra=None):
    w = width // N_SPLIT
    strips = [(i * w, (i + 1) * w) for i in range(N_SPLIT)]
    scores = [jnp.dot(kaug, get_q(lo, hi), preferred_element_type=jnp.float32) for lo, hi in strips]
    for (lo, hi), s in zip(strips, scores):
        if extra is not None:
            s = extra(s, lo, hi)
        m_prev = m_ref[:, lo:hi]
        m_new = jnp.maximum(m_prev, _col_reduce(jnp.maximum, s))
        alpha = jnp.exp2(m_prev - m_new)
        acc_ref[:, lo:hi] = alpha * acc_ref[:, lo:hi] + jnp.dot(
            vt, _probabilities(s, m_new), preferred_element_type=jnp.float32)
        m_ref[:, lo:hi] = m_new


def _flash_far_chunks(n, load_keys, load_values, get_q_for, width, m_ref, acc_ref, s_ref, mx_ref, p_ref):
    w = width // N_SPLIT
    strips = [(i * w, (i + 1) * w) for i in range(N_SPLIT)]
    base = n % 2
    pairs = n // 2

    def scores_into(c, slot):
        kaug = load_keys(c)
        get_q = get_q_for(c)
        for lo, hi in strips:
            s = jnp.dot(kaug, get_q(lo, hi), preferred_element_type=jnp.float32)
            s_ref[slot, :, lo:hi] = s
            mx_ref[slot, :, lo:hi] = _col_reduce(jnp.maximum, s)

    def value_product(c, slot):
        vt = load_values(c)
        return jnp.concatenate([jnp.dot(vt, p_ref[slot, :, lo:hi], preferred_element_type=jnp.float32)
                                for lo, hi in strips], axis=1)

    @pl.when(base == 1)
    def _():
        _flash_step(load_keys(0), load_values(0), get_q_for(0), width, m_ref, acc_ref)

    @pl.when(pairs > 0)
    def _():
        scores_into(base, 0)
        p_ref[1] = jnp.zeros(p_ref.shape[1:], p_ref.dtype)

        def pair(j, carry):
            for cur in (0, 1):
                c = base + 2 * j + cur
                nxt = 1 - cur
                pv = value_product(jnp.maximum(c - 1, base), nxt)
                scores_into(jnp.minimum(c + 1, n - 1), nxt)
                alphas = []
                for lo, hi in strips:
                    m_prev = m_ref[:, lo:hi]
                    m_new = jnp.maximum(m_prev, mx_ref[cur, :, lo:hi])
                    m_ref[:, lo:hi] = m_new
                    for r in range(0, s_ref.shape[1], ROW_BLOCK):
                        p_ref[cur, r:r + ROW_BLOCK, lo:hi] = _probabilities(
                            s_ref[cur, r:r + ROW_BLOCK, lo:hi], m_new)
                    alphas.append(jnp.exp2(m_prev - m_new))
                acc_ref[...] = jnp.concatenate(alphas, axis=1) * (acc_ref[...] + pv)
            return carry

        lax.fori_loop(0, pairs, pair, 0)
        acc_ref[...] = acc_ref[...] + value_product(n - 1, 1)


def _group_queries_t(qt_ref):
    return jnp.concatenate([qt_ref[h] for h in range(NSA_HPG)], axis=1)


def _branch_gate(gates_t, head, branch):
    row = head * 3 + branch
    return gates_t[row:row + 1, :]


def _values_t(vt_ref, k0, n_keys):
    p0 = k0 // LANES
    return jnp.concatenate([vt_ref[p0 + j] for j in range(n_keys // LANES)], axis=1)


def _compressed_branch(n_rows, i0, q4t, kc_ref, vct_ref, band_ref, ovt_ref, sc_ref, oct_ref, imp_ref):
    sc_ref[0:n_rows, :] = jnp.dot(kc_ref[0:n_rows, :], q4t, preferred_element_type=jnp.float32)
    sc_ref[pl.ds(i0, BAND_ROWS), :] = sc_ref[pl.ds(i0, BAND_ROWS), :] + band_ref[...]
    row = lax.broadcasted_iota(jnp.int32, (n_rows, ROWS), 0)
    sc = jnp.where(row < i0 + BAND_ROWS, sc_ref[0:n_rows, :], NEG)
    m = _col_reduce(jnp.maximum, sc)
    p = jnp.exp2(sc - m)
    l = _col_reduce(jnp.add, p)
    pn = p * jnp.where(m > 0.5 * NEG, 1.0 / l, 0.0)
    oct_ref[...] = jnp.dot(vct_ref[:, 0:n_rows], pn.astype(jnp.bfloat16), preferred_element_type=jnp.float32)
    psum = pn[:, 0:Q_TILE]
    for h in range(1, NSA_HPG):
        psum = psum + pn[:, h * Q_TILE:(h + 1) * Q_TILE]
    imp_ref[...] = jnp.dot(ovt_ref[:, 0:n_rows], psum.astype(jnp.bfloat16), preferred_element_type=jnp.float32)


def _cmp_win_kernel(qt_ref, kc_ref, vct_ref, band_ref, ovt_ref, kw_ref, vwt_ref, near_ref, gt_ref,
                    ycw_ref, pen_ref, sc_ref, sw_ref, oct_ref, imp_ref):
    qb = pl.program_id(2)
    n_cmp = kc_ref.shape[0]
    n_slc = ovt_ref.shape[0]
    q4t = _group_queries_t(qt_ref)

    i0 = pl.multiple_of(jnp.maximum(8 * qb - 16, 0), 8)
    n_tiers = CMP_TIERS if n_cmp % (CMP_TIERS * LANES) == 0 else 1
    step = n_cmp // n_tiers
    for tier in range(1, n_tiers + 1):
        in_tier = (i0 + BAND_ROWS <= tier * step) & (i0 + BAND_ROWS > (tier - 1) * step)
        pl.when(in_tier)(functools.partial(_compressed_branch, tier * step, i0, q4t, kc_ref, vct_ref,
                                           band_ref, ovt_ref, sc_ref, oct_ref, imp_ref))
    oct_ = oct_ref[...]
    imp = imp_ref[...]

    ji = lax.broadcasted_iota(jnp.int32, (n_slc, Q_TILE), 0)
    jf = ji.astype(jnp.float32)
    t = qb * Q_TILE + lax.broadcasted_iota(jnp.int32, (n_slc, Q_TILE), 1)
    cur = t // SLC_BLOCK
    forced = (ji == 0) | (ji == cur) | (ji == cur - 1)
    score = jnp.where(ji <= cur, jnp.where(forced, FORCED_SCORE, imp), -1.0)
    pen_t = jnp.full((n_slc, Q_TILE), NEG, jnp.float32)
    for _ in range(min(SLC_TOPK, n_slc)):
        mx = _col_reduce(jnp.maximum, score)
        idx = _col_reduce(jnp.minimum, jnp.where(score == mx, jf, float(n_slc)))
        pick = jf == idx
        pen_t = jnp.where(pick, 0.0, pen_t)
        score = jnp.where(pick, -2.0, score)
    pad = pen_ref.shape[0] - n_slc
    if pad:
        pen_t = jnp.concatenate([pen_t, jnp.full((pad, Q_TILE), NEG, jnp.float32)], axis=0)
    pen_ref[...] = pen_t.astype(pen_ref.dtype)

    t0 = qb * Q_TILE
    ws = pl.multiple_of(jnp.maximum(t0 - WINDOW, 0), Q_TILE)
    ns = pl.multiple_of(jnp.maximum(t0 - Q_TILE, 0), Q_TILE)
    sw_ref[...] = jnp.dot(kw_ref[pl.ds(ws, WIN_KEYS), :], q4t, preferred_element_type=jnp.float32)
    off = pl.multiple_of(ns - ws, Q_TILE)
    sw_ref[pl.ds(off, NEAR_KEYS), :] = sw_ref[pl.ds(off, NEAR_KEYS), :] + near_ref[...]
    key = ws + lax.broadcasted_iota(jnp.int32, (WIN_KEYS, ROWS), 0)
    qry = t0 + (lax.broadcasted_iota(jnp.int32, (WIN_KEYS, ROWS), 1) & (Q_TILE - 1))
    sw = jnp.where((key <= qry) & (key > qry - WINDOW), sw_ref[...], NEG)
    mw = _col_reduce(jnp.maximum, sw)
    ow_sum = jnp.dot(_values_t(vwt_ref, ws, WIN_KEYS), _probabilities(sw, mw),
                     preferred_element_type=jnp.float32)
    owt = ow_sum[0:HEAD_DIM, :] / ow_sum[SUM_ROW:SUM_ROW + 1, :]

    gates_t = jax.nn.sigmoid(gt_ref[...])
    for h in range(NSA_HPG):
        cols = slice(h * Q_TILE, (h + 1) * Q_TILE)
        mixed = _branch_gate(gates_t, h, 0) * oct_[:, cols] + _branch_gate(gates_t, h, 2) * owt[:, cols]
        ycw_ref[:, h * HEAD_DIM:(h + 1) * HEAD_DIM] = mixed.T


def _cmp_win(qt, kc, vct, band, ovt, qkv, vwt, near, gates_t):
    b, s, _ = qkv.shape
    n_cmp = kc.shape[2]
    n_slc = ovt.shape[0]
    nq = s // Q_TILE
    pen_w = -(-n_slc // LANES) * LANES
    n_band = band.shape[0] - 1
    return pl.pallas_call(
        _cmp_win_kernel,
        grid=(b, NSA_GROUPS, nq),
        in_specs=[
            pl.BlockSpec((None, NSA_HPG, HEAD_DIM, Q_TILE), lambda i, g, q: (i, g, 0, q)),
            pl.BlockSpec((None, None, n_cmp, HEAD_DIM), lambda i, g, q: (i, g, 0, 0)),
            pl.BlockSpec((None, None, HEAD_DIM, n_cmp), lambda i, g, q: (i, g, 0, 0)),
            pl.BlockSpec((None, None, BAND_ROWS, ROWS), lambda i, g, q: (jnp.minimum(q, n_band), g, 0, 0)),
            pl.BlockSpec((n_slc, n_cmp), lambda i, g, q: (0, 0)),
            pl.BlockSpec((None, s, HEAD_DIM), lambda i, g, q: (i, 0, CB_KW + g)),
            pl.BlockSpec((None, None, s // LANES, ACC_ROWS, LANES), lambda i, g, q: (i, g, 0, 0, 0)),
            pl.BlockSpec((None, None, NEAR_KEYS, ROWS), lambda i, g, q: (jnp.minimum(q, 1), g, 0, 0)),
            pl.BlockSpec((None, None, GATE_ROWS, Q_TILE), lambda i, g, q: (i, g, 0, q)),
        ],
        out_specs=[
            pl.BlockSpec((None, Q_TILE, NSA_HPG * HEAD_DIM), lambda i, g, q: (i, q, g)),
            pl.BlockSpec((None, None, pen_w, Q_TILE), lambda i, g, q: (i, g, 0, q)),
        ],
        out_shape=[
            jax.ShapeDtypeStruct((b, s, NSA_Q_DIM), jnp.float32),
            jax.ShapeDtypeStruct((b, NSA_GROUPS, pen_w, s), jnp.bfloat16),
        ],
        scratch_shapes=[pltpu.VMEM((n_cmp, ROWS), jnp.float32), pltpu.VMEM((WIN_KEYS, ROWS), jnp.float32),
                        pltpu.VMEM((HEAD_DIM, ROWS), jnp.float32), pltpu.VMEM((n_slc, Q_TILE), jnp.float32)],
        compiler_params=_params(("parallel", "parallel", "arbitrary")),
        name="nsa_cmp_topk_win",
    )(qt, kc, vct, band, ovt, qkv, vwt, near, gates_t)


def _range_penalty(pen, first_blk, lo_ok, hi_ok):
    n_half = pen.shape[0] // LANES
    c = lax.broadcasted_iota(jnp.int32, (LANES, pen.shape[1]), 0)
    out = pen[0:LANES, :]
    blk = c
    for hf in range(1, n_half):
        in_lower = (c + (hf - 1) * LANES >= first_blk) & (first_blk < hf * LANES)
        out = jnp.where(in_lower, out, pen[hf * LANES:(hf + 1) * LANES, :])
        blk = jnp.where(in_lower, blk, c + hf * LANES)
    return jnp.where((blk >= lo_ok) & (blk < hi_ok), out, jnp.asarray(NEG, out.dtype))


def _sel_kernel(qt_ref, pen_ref, kaug_ref, vt_ref, near_ref, ycw_ref, gt_ref, y_ref,
                qaug_ref, m_ref, acc_ref, s_ref, mx_ref, p_ref):
    qb = pl.program_id(2)
    n_half = pen_ref.shape[0] // LANES
    n_blocks = pen_ref.shape[0]
    chunks_per_half = LANES * SLC_BLOCK // KEY_CHUNK
    blocks_per_chunk = KEY_CHUNK // SLC_BLOCK

    t0 = qb * Q_TILE
    near_end = t0 - Q_TILE
    nf = jnp.maximum(near_end // KEY_CHUNK, 0)
    ms = pl.multiple_of(jnp.maximum(near_end - KEY_CHUNK, 0), Q_TILE)
    ns = pl.multiple_of(jnp.maximum(near_end, 0), Q_TILE)

    q4t = _group_queries_t(qt_ref)
    pen = pen_ref[...]
    operands = [pen[hf * LANES:(hf + 1) * LANES, :] for hf in range(n_half)]
    operands.append(_range_penalty(pen, ms // SLC_BLOCK, nf * blocks_per_chunk, near_end // SLC_BLOCK))
    operands.append(_range_penalty(pen, ns // SLC_BLOCK, 0, n_blocks))
    for idx, channels in enumerate(operands):
        qaug_ref[idx, 0:HEAD_DIM, :] = q4t
        qaug_ref[idx, HEAD_DIM:, :] = jnp.concatenate([channels] * NSA_HPG, axis=1)

    _flash_init(m_ref, acc_ref)

    def key_start(c):
        return pl.multiple_of(jnp.where(c < nf, c * KEY_CHUNK, ms), Q_TILE)

    def get_q_for(c):
        operand = jnp.where(c < nf, c // chunks_per_half, n_half)
        return lambda lo, hi: qaug_ref[operand, :, lo:hi]

    _flash_far_chunks(nf + 1, lambda c: kaug_ref[pl.ds(key_start(c), KEY_CHUNK), :],
                      lambda c: _values_t(vt_ref, key_start(c), KEY_CHUNK), get_q_for,
                      ROWS, m_ref, acc_ref, s_ref, mx_ref, p_ref)

    _flash_step(kaug_ref[pl.ds(ns, NEAR_KEYS), :], _values_t(vt_ref, ns, NEAR_KEYS),
                lambda lo, hi: qaug_ref[n_half + 1, :, lo:hi], ROWS, m_ref, acc_ref,
                extra=lambda s, lo, hi: s + near_ref[:, lo:hi])

    o = _flash_result(acc_ref)
    gates_t = jax.nn.sigmoid(gt_ref[...])
    for h in range(NSA_HPG):
        sel_h = (_branch_gate(gates_t, h, 1) * o[:, h * Q_TILE:(h + 1) * Q_TILE]).T
        y_ref[:, h * HEAD_DIM:(h + 1) * HEAD_DIM] = (
            ycw_ref[:, h * HEAD_DIM:(h + 1) * HEAD_DIM] + sel_h).astype(y_ref.dtype)


def _sel(qt, pen, kaug, vt_pieces, near, ycw, gates_t):
    b, _, s, _ = kaug.shape
    nq = s // Q_TILE
    pen_w = pen.shape[2]
    return pl.pallas_call(
        _sel_kernel,
        grid=(b, NSA_GROUPS, nq),
        in_specs=[
            pl.BlockSpec((None, NSA_HPG, HEAD_DIM, Q_TILE), lambda i, g, q: (i, g, 0, q)),
            pl.BlockSpec((None, None, pen_w, Q_TILE), lambda i, g, q: (i, g, 0, q)),
            pl.BlockSpec((None, None, s, 2 * HEAD_DIM), lambda i, g, q: (i, g, 0, 0)),
            pl.BlockSpec((None, None, s // LANES, ACC_ROWS, LANES), lambda i, g, q: (i, g, 0, 0, 0)),
            pl.BlockSpec((None, None, NEAR_KEYS, ROWS), lambda i, g, q: (jnp.minimum(q, 1), g, 0, 0)),
            pl.BlockSpec((None, Q_TILE, NSA_HPG * HEAD_DIM), lambda i, g, q: (i, q, g)),
            pl.BlockSpec((None, None, GATE_ROWS, Q_TILE), lambda i, g, q: (i, g, 0, q)),
        ],
        out_specs=pl.BlockSpec((None, Q_TILE, NSA_HPG * HEAD_DIM), lambda i, g, q: (i, q, g)),
        out_shape=jax.ShapeDtypeStruct((b, s, NSA_Q_DIM), jnp.bfloat16),
        scratch_shapes=[
            pltpu.VMEM((pen_w // LANES + 2, 2 * HEAD_DIM, ROWS), jnp.bfloat16),
            pltpu.VMEM((1, ROWS), jnp.float32),
            pltpu.VMEM((ACC_ROWS, ROWS), jnp.float32),
            pltpu.VMEM((2, KEY_CHUNK, ROWS), jnp.float32),
            pltpu.VMEM((2, 1, ROWS), jnp.float32),
            pltpu.VMEM((2, KEY_CHUNK, ROWS), jnp.bfloat16),
        ],
        compiler_params=_params(("parallel", "parallel", "arbitrary")),
        name="nsa_selected",
    )(qt, pen, kaug, vt_pieces, near, ycw, gates_t)


def _fox_kernel(nce_ref, qt_ref, k_ref, dec_ref, vt_ref, o_ref,
                qaug_ref, m_ref, acc_ref, s_ref, mx_ref, p_ref, knorm_ref):
    qi = pl.program_id(2)
    bh = pl.program_id(0) * pl.num_programs(1) + pl.program_id(1)
    tq = qt_ref.shape[1]
    row = lax.broadcasted_iota(jnp.int32, (HEAD_DIM, tq), 0)
    qaug_ref[0:HEAD_DIM, :] = qt_ref[...]
    qaug_ref[HEAD_DIM:, :] = jnp.where(row < DECAY_TERMS, -1.0, 0.0).astype(qaug_ref.dtype)
    _flash_init(m_ref, acc_ref)

    @pl.when(qi == 0)
    def _():
        def chunk_max(c, best):
            kf = k_ref[pl.ds(pl.multiple_of(c * tq, tq), tq), :].astype(jnp.float32)
            return jnp.maximum(best, jnp.max(jnp.sum(kf * kf, axis=1, keepdims=True)))

        knorm_ref[0] = jnp.sqrt(lax.fori_loop(0, k_ref.shape[0] // tq, chunk_max, jnp.float32(0.0)))

    qf = qt_ref[...].astype(jnp.float32)
    q_norm = jnp.sqrt(jnp.max(jnp.sum(qf * qf, axis=0, keepdims=True)))

    def get_q(lo, hi):
        return qaug_ref[:, lo:hi]

    def load_keys(c):
        k0 = pl.multiple_of(c * tq, tq)
        return jnp.concatenate([k_ref[pl.ds(k0, tq), :], dec_ref[pl.ds(k0, tq), :]], axis=1)

    def causal(s, lo, hi):
        key = lax.broadcasted_iota(jnp.int32, s.shape, 0)
        qry = lo + lax.broadcasted_iota(jnp.int32, s.shape, 1)
        return jnp.where(key <= qry, s, NEG)

    _flash_step(load_keys(qi), vt_ref[qi], get_q, tq, m_ref, acc_ref, extra=causal)

    m_low = jnp.min(m_ref[...])
    reach = q_norm * knorm_ref[0] + SKIP_MARGIN_LOG2

    def count(c, n_skip):
        return n_skip + jnp.where(reach + nce_ref[bh, c] < m_low, 1, 0)

    first = lax.fori_loop(0, qi, count, jnp.int32(0))
    _flash_far_chunks(qi - first, lambda c: load_keys(c + first), lambda c: vt_ref[c + first],
                      lambda c: get_q, tq, m_ref, acc_ref, s_ref, mx_ref, p_ref)
    o_ref[...] = _flash_result(acc_ref).T.astype(o_ref.dtype)


def _fox(neg_cum_end, qkv, qt, dec, vt, tq):
    b, s, _ = qkv.shape
    return pl.pallas_call(
        _fox_kernel,
        grid=(b, FOX_HEADS, s // tq),
        in_specs=[
            pl.BlockSpec(memory_space=pltpu.SMEM),
            pl.BlockSpec((None, None, HEAD_DIM, tq), lambda i, h, q: (i, NSA_HEADS + h, 0, q)),
            pl.BlockSpec((None, s, HEAD_DIM), lambda i, h, q: (i, 0, CB_KF + h)),
            pl.BlockSpec((None, None, s, LANES), lambda i, h, q: (i, h, 0, 0)),
            pl.BlockSpec((None, None, s // tq, ACC_ROWS, tq), lambda i, h, q: (i, h, 0, 0, 0)),
        ],
        out_specs=pl.BlockSpec((None, tq, HEAD_DIM), lambda i, h, q: (i, q, h)),
        out_shape=jax.ShapeDtypeStruct((b, s, FOX_DIM), jnp.bfloat16),
        scratch_shapes=[
            pltpu.VMEM((2 * HEAD_DIM, tq), jnp.bfloat16),
            pltpu.VMEM((1, tq), jnp.float32),
            pltpu.VMEM((ACC_ROWS, tq), jnp.float32),
            pltpu.VMEM((2, tq, tq), jnp.float32),
            pltpu.VMEM((2, 1, tq), jnp.float32),
            pltpu.VMEM((2, tq, tq), jnp.bfloat16),
            pltpu.SMEM((1,), jnp.float32),
        ],
        compiler_params=_params(("parallel", "parallel", "arbitrary")),
        name="fox_attention",
    )(neg_cum_end, qt, qkv, dec, vt)


def _merge_kernel(x_ref, g_ref, wa_ref, wb_ref, pa_ref, pb_ref, wo_ref, ya_ref, yb_ref, o_ref,
                  h_ref, acc_ref):
    j = pl.program_id(1)

    @pl.when(j == 0)
    def _():
        h_ref[...] = _norm_rows(x_ref[...], g_ref[...]).astype(h_ref.dtype)
        acc_ref[...] = jnp.zeros_like(acc_ref)

    h = h_ref[...]
    ga = jax.nn.sigmoid(jnp.dot(h, wa_ref[...], preferred_element_type=jnp.float32))
    gb = jax.nn.sigmoid(jnp.dot(h, wb_ref[...], preferred_element_type=jnp.float32))
    a = jnp.dot(ya_ref[...], pa_ref[...], preferred_element_type=jnp.float32)
    bb = jnp.dot(yb_ref[...], pb_ref[...], preferred_element_type=jnp.float32)
    merged = (ga * a + gb * bb).astype(jnp.bfloat16)
    acc_ref[...] += jnp.dot(merged, wo_ref[...], preferred_element_type=jnp.float32)

    @pl.when(j == pl.num_programs(1) - 1)
    def _():
        o_ref[...] = x_ref[...] + acc_ref[...]


def _merge(x2d, g, w_ma, w_mb, p_a, p_b, w_out, y_a, y_b, tm, tn):
    t, d = x2d.shape
    ka = p_a.shape[0]
    kb = p_b.shape[0]
    return pl.pallas_call(
        _merge_kernel,
        grid=(t // tm, d // tn),
        in_specs=[
            pl.BlockSpec((tm, d), lambda i, j: (i, 0)),
            pl.BlockSpec((1, d), lambda i, j: (0, 0)),
            pl.BlockSpec((d, tn), lambda i, j: (0, j)),
            pl.BlockSpec((d, tn), lambda i, j: (0, j)),
            pl.BlockSpec((ka, tn), lambda i, j: (0, j)),
            pl.BlockSpec((kb, tn), lambda i, j: (0, j)),
            pl.BlockSpec((tn, d), lambda i, j: (j, 0)),
            pl.BlockSpec((tm, ka), lambda i, j: (i, 0)),
            pl.BlockSpec((tm, kb), lambda i, j: (i, 0)),
        ],
        out_specs=pl.BlockSpec((tm, d), lambda i, j: (i, 0)),
        out_shape=jax.ShapeDtypeStruct((t, d), jnp.float32),
        scratch_shapes=[pltpu.VMEM((tm, d), jnp.bfloat16), pltpu.VMEM((tm, d), jnp.float32)],
        compiler_params=_params(("parallel", "arbitrary")),
        name="merge_out_proj",
    )(x2d, g.reshape(1, d), w_ma, w_mb, p_a, p_b, w_out, y_a, y_b)


def _ffn_kernel(x_ref, xh_ref, g_ref, wu_ref, wv_ref, cw_ref, cb_ref, wd_ref, gf_ref, o_ref,
                h_ref, u_ref, acc_ref, *, seq, tm):
    i = pl.program_id(0)
    j = pl.program_id(1)
    halo = BF16_SUBLANES

    @pl.when(j == 0)
    def _():
        g = g_ref[...]
        keep = jnp.where((i * tm) % seq == 0, 0.0, 1.0)
        h_ref[0:halo, :] = (_norm_rows(xh_ref[...], g) * keep).astype(h_ref.dtype)
        h_ref[halo:, :] = _norm_rows(x_ref[...], g).astype(h_ref.dtype)
        acc_ref[...] = jnp.zeros_like(acc_ref)

    u_ref[...] = jnp.dot(h_ref[...], wu_ref[...], preferred_element_type=jnp.float32)
    v = jnp.dot(h_ref[halo:, :], wv_ref[...], preferred_element_type=jnp.float32)
    conv = cb_ref[...]
    for k in range(CONV_WIDTH):
        conv = conv + cw_ref[k:k + 1, :] * u_ref[pl.ds(halo - (CONV_WIDTH - 1) + k, tm), :]
    act = (jax.nn.gelu(conv) * v).astype(jnp.bfloat16)
    acc_ref[...] += jnp.dot(act, wd_ref[...], preferred_element_type=jnp.float32)

    @pl.when(j == pl.num_programs(1) - 1)
    def _():
        o_ref[...] = _norm_rows(x_ref[...] + acc_ref[...], gf_ref[...])


def _ffn(x2d, g, w_up, conv_w, conv_b, w_down, g_final, seq, tm, tn):
    t, d = x2d.shape
    d_ff = w_down.shape[0]
    nt = d_ff // tn
    halo = BF16_SUBLANES
    per = tm // halo
    return pl.pallas_call(
        functools.partial(_ffn_kernel, seq=seq, tm=tm),
        grid=(t // tm, nt),
        in_specs=[
            pl.BlockSpec((tm, d), lambda i, j: (i, 0), pipeline_mode=pl.Buffered(1)),
            pl.BlockSpec((halo, d), lambda i, j: (jnp.maximum(i * per - 1, 0), 0)),
            pl.BlockSpec((1, d), lambda i, j: (0, 0)),
            pl.BlockSpec((d, tn), lambda i, j: (0, j)),
            pl.BlockSpec((d, tn), lambda i, j: (0, nt + j)),
            pl.BlockSpec((CONV_WIDTH, tn), lambda i, j: (0, j)),
            pl.BlockSpec((1, tn), lambda i, j: (0, j)),
            pl.BlockSpec((tn, d), lambda i, j: (j, 0)),
            pl.BlockSpec((1, d), lambda i, j: (0, 0)),
        ],
        out_specs=pl.BlockSpec((tm, d), lambda i, j: (i, 0), pipeline_mode=pl.Buffered(1)),
        out_shape=jax.ShapeDtypeStruct((t, d), jnp.float32),
        scratch_shapes=[
            pltpu.VMEM((tm + halo, d), jnp.bfloat16),
            pltpu.VMEM((tm + halo, tn), jnp.float32),
            pltpu.VMEM((tm, d), jnp.float32),
        ],
        compiler_params=_params(("parallel", "arbitrary")),
        name="ffn_final_norm",
    )(x2d, x2d, g.reshape(1, d), w_up, w_up, conv_w, conv_b.reshape(1, d_ff), w_down, g_final.reshape(1, d))


def _t5_bucket_np(dist):
    n = np.maximum(dist, 0)
    max_exact = REL_BUCKETS // 2
    nf = np.maximum(n, 1).astype(np.float32)
    large = max_exact + (np.log(nf / np.float32(max_exact)) / np.float32(math.log(REL_MAX_DIST / max_exact))
                         * np.float32(REL_BUCKETS - max_exact)).astype(np.int32)
    return np.where(n < max_exact, n, np.minimum(large, REL_BUCKETS - 1)).astype(np.int32)


def _bias_by_distance(rel_table, far_shift):
    vals = rel_table[jnp.asarray(_t5_bucket_np(np.arange(REL_MAX_DIST + 1)))]
    if far_shift:
        vals = vals - rel_table[REL_BUCKETS - 1]
    return vals * LOG2E


def _near_bias(rel_table):
    fd = _bias_by_distance(rel_table, True)
    h = fd.shape[1]
    lo = NEAR_KEYS - 1
    vec = jnp.concatenate([jnp.full((lo, h), NEG, fd.dtype), fd[:REL_MAX_DIST],
                           jnp.zeros((NEAR_KEYS - REL_MAX_DIST, h), fd.dtype)], axis=0)
    def toeplitz(start):
        period = Q_TILE + NEAR_KEYS - 1
        window = jnp.concatenate([vec[start:start + Q_TILE], vec[start - (NEAR_KEYS - 1):start]], axis=0)
        flat = jnp.tile(window, (NEAR_KEYS, 1))[:NEAR_KEYS * (period - 1)]
        return flat.reshape(NEAR_KEYS, period - 1, h)[:, :Q_TILE]

    tiles = jnp.stack([toeplitz(lo + a) for a in (0, Q_TILE)])
    tiles = tiles.reshape(2, NEAR_KEYS, Q_TILE, NSA_GROUPS, NSA_HPG).transpose(0, 3, 1, 4, 2)
    return tiles.reshape(2, NSA_GROUPS, NEAR_KEYS, ROWS)


def _band_bias(rel_table):
    fd = _bias_by_distance(rel_table, True)
    tl = np.arange(Q_TILE)[None, :]
    r = np.arange(BAND_ROWS)[:, None]
    dist = np.stack([tl - CMP_STRIDE * (r - off) - (CMP_BLOCK - 1) for off in (0, 8, 16)])
    vals = fd[jnp.asarray(np.clip(dist, 0, REL_MAX_DIST))]
    vals = jnp.where(jnp.asarray(dist >= 0)[..., None], vals, NEG)
    v, rr, q, _ = vals.shape
    return vals.transpose(0, 1, 3, 2).reshape(v, rr, NSA_GROUPS, NSA_HPG * q).transpose(0, 2, 1, 3)


def _overlap_t(n_cmp_pad, n_slc):
    i = np.arange(n_cmp_pad)[None, :]
    jj = np.arange(n_slc)[:, None]
    c_start = i * CMP_STRIDE
    ov = (c_start < jj * SLC_BLOCK + SLC_BLOCK) & (c_start + CMP_BLOCK - 1 >= jj * SLC_BLOCK)
    ov = ov & (i < n_cmp_pad - 1)
    return jnp.asarray(ov.astype(np.float32), jnp.bfloat16)


def _block_onehot(seq):
    blk = (np.arange(seq) // SLC_BLOCK) % LANES
    return jnp.asarray((blk[:, None] == np.arange(LANES)[None, :]).astype(np.float32), jnp.bfloat16)


def _pick_tile(n, pref):
    return pref if n % pref == 0 else n


def kernel(x, attn_norm_g, w_in, cmp_pos_k, cmp_w1_k, cmp_w2_k, cmp_pos_v, cmp_w1_v, cmp_w2_v,
           rel_bias_table, fox_forget_bias, w_branch_nsa, w_branch_fox, w_out,
           ffn_norm_g, w_up, conv_w, conv_b, w_down, final_norm_g):
    assert w_in.shape[0] == 1, "the final norm is fused into the single layer's FFN kernel"
    bsz, seq, d = x.shape
    t = bsz * seq
    bf = jnp.bfloat16
    scale = HEAD_DIM ** -0.5 * LOG2E
    x2d = x.reshape(t, d)
    w_in = w_in[0]

    o = np.cumsum([0, NSA_Q_DIM] + [NSA_KV_DIM] * 6 + [3 * NSA_HEADS, FOX_DIM, FOX_DIM, FOX_DIM, FOX_HEADS, d, d])
    w_qkv = jnp.concatenate([w_in[:, o[0]:o[1]] * scale, w_in[:, o[1]:o[7]],
                             w_in[:, o[8]:o[9]] * scale, w_in[:, o[9]:o[11]]], axis=1).astype(bf)
    gate_cols = []
    for grp in range(NSA_GROUPS):
        gate_cols += [w_in[:, o[7] + grp * GATES_PER_GROUP:o[7] + (grp + 1) * GATES_PER_GROUP],
                      jnp.zeros((d, LANES - GATES_PER_GROUP), w_in.dtype)]
    gate_cols += [w_in[:, o[11]:o[12]], jnp.zeros((d, IN_PROJ_TILE - N_GATE + LANES - FOX_HEADS), w_in.dtype)]
    w_gate = jnp.concatenate(gate_cols, axis=1).astype(bf)
    w_ma = w_in[:, o[12]:o[13]].astype(bf)
    w_mb = w_in[:, o[13]:o[14]].astype(bf)

    qkv, gates = _in_proj(x2d, attn_norm_g[0], jnp.concatenate([w_qkv, w_gate], axis=1),
                          _pick_tile(t, 1024), IN_PROJ_TILE)
    qkv = qkv.reshape(bsz, seq, N_QKV)
    gates = gates.reshape(bsz, seq, IN_PROJ_TILE)

    q_all = jnp.concatenate([qkv[:, :, :NSA_Q_DIM], qkv[:, :, CB_QF * HEAD_DIM:(CB_QF + FOX_HEADS) * HEAD_DIM]],
                            axis=-1)
    qt = q_all.reshape(bsz, seq, NSA_HEADS + FOX_HEADS, HEAD_DIM).transpose(0, 2, 3, 1)

    def heads_major(cb, heads):
        return qkv[:, :, cb * HEAD_DIM:(cb + heads) * HEAD_DIM].reshape(
            bsz, seq, heads, HEAD_DIM).transpose(0, 2, 1, 3)

    def values_t(cb, heads, piece):
        v = qkv[:, :, cb * HEAD_DIM:(cb + heads) * HEAD_DIM]
        v = v.reshape(bsz, seq // piece, piece, heads, HEAD_DIM).transpose(0, 3, 1, 4, 2)
        ones_row = (jnp.arange(ACC_ROWS - HEAD_DIM) == 0).astype(bf)[:, None]
        extra = jnp.broadcast_to(ones_row, v.shape[:3] + (ACC_ROWS - HEAD_DIM, piece))
        return jnp.concatenate([v, extra], axis=3)

    n_ch = seq // CMP_STRIDE

    def chunked(cb):
        sl = qkv[:, :, cb * HEAD_DIM:(cb + NSA_GROUPS) * HEAD_DIM]
        sl = sl.reshape(bsz, n_ch, CMP_STRIDE, NSA_GROUPS, HEAD_DIM).transpose(0, 3, 1, 2, 4)
        return sl.reshape(bsz, NSA_GROUPS, n_ch, CMP_STRIDE * HEAD_DIM)

    def posflat(pos):
        return jnp.broadcast_to(pos.reshape(1, CMP_BLOCK * HEAD_DIM), (8, CMP_BLOCK * HEAD_DIM)).astype(bf)

    kc = _compress(chunked(CB_KC), cmp_w1_k[0].astype(bf), posflat(cmp_pos_k[0]), cmp_w2_k[0].astype(bf))
    vc = _compress(chunked(CB_VC), cmp_w1_v[0].astype(bf), posflat(cmp_pos_v[0]), cmp_w2_v[0].astype(bf))
    vct = vc.transpose(0, 1, 3, 2)

    f_t = gates[:, :, NSA_GROUPS * LANES:NSA_GROUPS * LANES + FOX_HEADS]
    f_t = f_t.transpose(0, 2, 1).reshape(bsz * FOX_HEADS, seq)
    bias_col = jnp.tile(fox_forget_bias[0].astype(jnp.float32), bsz).reshape(bsz * FOX_HEADS, 1)
    terms = _decay_cumsum(f_t, bias_col, _pick_tile(seq, 2048))
    dec = jnp.pad(terms.transpose(1, 2, 0), ((0, 0), (0, 0), (0, LANES - DECAY_TERMS)))
    dec = dec.reshape(bsz, FOX_HEADS, seq, LANES)

    n_slc = seq // SLC_BLOCK
    near = _near_bias(rel_bias_table)
    gates_t = gates[:, :, :NSA_GROUPS * LANES].reshape(bsz, seq, NSA_GROUPS, LANES)[..., :GATE_ROWS]
    gates_t = gates_t.transpose(0, 2, 3, 1)
    ycw, pen = _cmp_win(qt, kc, vct, _band_bias(rel_bias_table), _overlap_t(n_ch, n_slc), qkv,
                        values_t(CB_VW, NSA_GROUPS, LANES), near, gates_t)
    ks = heads_major(CB_KS, NSA_GROUPS)
    kaug_sel = jnp.concatenate([ks, jnp.broadcast_to(_block_onehot(seq), ks.shape)], axis=-1)
    y_nsa = _sel(qt, pen, kaug_sel, values_t(CB_VS, NSA_GROUPS, LANES), near, ycw, gates_t)

    fox_tq = _pick_tile(seq, FOX_Q_TILE)
    neg_cum_end = -jnp.sum(terms[:, :, fox_tq - 1::fox_tq].astype(jnp.float32), axis=0)
    y_fox = _fox(neg_cum_end, qkv, qt, dec, values_t(CB_VF, FOX_HEADS, fox_tq), fox_tq)

    tm2 = _pick_tile(t, 512)
    x_mid = _merge(x2d, attn_norm_g[0], w_ma, w_mb, w_branch_nsa[0].astype(bf), w_branch_fox[0].astype(bf),
                   w_out[0].astype(bf), y_nsa.reshape(t, NSA_Q_DIM), y_fox.reshape(t, FOX_DIM),
                   tm2, _pick_tile(d, 512))
    d_ff = w_down.shape[1]
    out = _ffn(x_mid, ffn_norm_g[0], w_up[0].astype(bf), conv_w[0], conv_b[0], w_down[0].astype(bf),
               final_norm_g, seq, _pick_tile(t, 1024), _pick_tile(d_ff, 512))
    return out.reshape(bsz, seq, d)
```

```python
import functools
import math

import jax
import jax.numpy as jnp
import numpy as np
from jax import lax
from jax.experimental import pallas as pl
from jax.experimental.pallas import tpu as pltpu

HEAD_DIM = 128
NSA_HEADS = 8
NSA_GROUPS = 2
NSA_HPG = NSA_HEADS // NSA_GROUPS
FOX_HEADS = 8
CMP_BLOCK = 32
CMP_STRIDE = 16
SLC_BLOCK = 64
SLC_TOPK = 16
WINDOW = 512
REL_BUCKETS = 32
REL_MAX_DIST = 128
CONV_WIDTH = 3
EPS = 1e-6
NEG = -1e30
FORCED_SCORE = 1e4
LOG2E = math.log2(math.e)

LANES = 128
BF16_SUBLANES = 16
VMEM_LIMIT = 56 * 1024 * 1024

NSA_Q_DIM = NSA_HEADS * HEAD_DIM
NSA_KV_DIM = NSA_GROUPS * HEAD_DIM
FOX_DIM = FOX_HEADS * HEAD_DIM
N_QKV = NSA_Q_DIM + 6 * NSA_KV_DIM + 3 * FOX_DIM
GATES_PER_GROUP = 3 * NSA_HPG
N_GATE = (NSA_GROUPS + 1) * LANES
IN_PROJ_TILE = 512

CB_KC = 8
CB_VC = 10
CB_KS = 12
CB_VS = 14
CB_KW = 16
CB_VW = 18
CB_QF = 20
CB_KF = 28
CB_VF = 36

Q_TILE = 128
ROWS = NSA_HPG * Q_TILE
KEY_CHUNK = 512
FOX_Q_TILE = 512
N_SPLIT = 2
SKIP_MARGIN_LOG2 = 64.0
ROW_BLOCK = 32
DECAY_TERMS = 3
BAND_ROWS = 24
CMP_TIERS = 4
WIN_KEYS = WINDOW + Q_TILE
NEAR_KEYS = 2 * Q_TILE
GATE_ROWS = 16
SUM_ROW = HEAD_DIM
ACC_ROWS = HEAD_DIM + BF16_SUBLANES


def _params(sem):
    return pltpu.CompilerParams(dimension_semantics=sem, vmem_limit_bytes=VMEM_LIMIT)


def _norm_rows(x, g):
    return (x * lax.rsqrt(jnp.mean(x * x, axis=-1, keepdims=True) + EPS)) * g


def _in_proj_kernel(x_ref, g_ref, w_ref, qkv_ref, gate_ref, h_ref):
    j = pl.program_id(1)
    last = pl.num_programs(1) - 1

    @pl.when(j == 0)
    def _():
        h_ref[...] = _norm_rows(x_ref[...], g_ref[...]).astype(h_ref.dtype)

    y = jnp.dot(h_ref[...], w_ref[...], preferred_element_type=jnp.float32)

    @pl.when(j < last)
    def _():
        qkv_ref[...] = y.astype(qkv_ref.dtype)

    @pl.when(j == last)
    def _():
        gate_ref[...] = y


def _in_proj(x2d, g, w, tm, tn):
    t, d = x2d.shape
    n_tiles = w.shape[1] // tn
    return pl.pallas_call(
        _in_proj_kernel,
        grid=(t // tm, n_tiles),
        in_specs=[
            pl.BlockSpec((tm, d), lambda i, j: (i, 0)),
            pl.BlockSpec((1, d), lambda i, j: (0, 0)),
            pl.BlockSpec((d, tn), lambda i, j: (0, j)),
        ],
        out_specs=[
            pl.BlockSpec((tm, tn), lambda i, j: (i, jnp.minimum(j, n_tiles - 2))),
            pl.BlockSpec((tm, tn), lambda i, j: (i, 0)),
        ],
        out_shape=[
            jax.ShapeDtypeStruct((t, (n_tiles - 1) * tn), jnp.bfloat16),
            jax.ShapeDtypeStruct((t, tn), jnp.float32),
        ],
        scratch_shapes=[pltpu.VMEM((tm, d), jnp.bfloat16)],
        compiler_params=_params(("parallel", "arbitrary")),
        name="norm_in_proj",
    )(x2d, g.reshape(1, d), w)


def _compress_kernel(ch_ref, w1_ref, posf_ref, w2_ref, o_ref):
    half = ch_ref.shape[1]
    ch = ch_ref[...]
    pa = jnp.dot(ch, w1_ref[:half, :], preferred_element_type=jnp.float32)
    pb = jnp.dot(ch, w1_ref[half:, :], preferred_element_type=jnp.float32)
    pos = jnp.dot(posf_ref[...], w1_ref[...], preferred_element_type=jnp.float32)[0:1, :]
    n = pa.shape[0]
    pre = pa + pltpu.roll(pb, n - 1, 0) + pos
    act = jax.nn.gelu(pre)
    o_ref[...] = jnp.dot(act.astype(jnp.bfloat16), w2_ref[...],
                         preferred_element_type=jnp.float32).astype(o_ref.dtype)


def _compress(chunks, w1, posf, w2):
    b, g, n, k = chunks.shape
    return pl.pallas_call(
        _compress_kernel,
        grid=(b, g),
        in_specs=[
            pl.BlockSpec((None, None, n, k), lambda i, j: (i, j, 0, 0)),
            pl.BlockSpec(w1.shape, lambda i, j: (0, 0)),
            pl.BlockSpec(posf.shape, lambda i, j: (0, 0)),
            pl.BlockSpec(w2.shape, lambda i, j: (0, 0)),
        ],
        out_specs=pl.BlockSpec((None, None, n, HEAD_DIM), lambda i, j: (i, j, 0, 0)),
        out_shape=jax.ShapeDtypeStruct((b, g, n, HEAD_DIM), jnp.bfloat16),
        compiler_params=_params(("parallel", "parallel")),
        name="compress_tokens",
    )(chunks, w1, posf, w2)


def _decay_kernel(f_ref, b_ref, tri_ref, o_ref, carry_ref):
    @pl.when(pl.program_id(0) == 0)
    def _():
        carry_ref[...] = jnp.zeros_like(carry_ref)

    x = f_ref[...] + b_ref[...]
    logf = (jnp.minimum(x, 0.0) - jnp.log1p(jnp.exp(-jnp.abs(x)))) * LOG2E
    carry = carry_ref[...]
    for seg in range(f_ref.shape[1] // LANES):
        part = jnp.dot(logf[:, seg * LANES:(seg + 1) * LANES], tri_ref[...],
                       preferred_element_type=jnp.float32, precision=lax.Precision.HIGHEST) + carry
        carry = part[:, LANES - 1:LANES]
        rest = part
        for term in range(DECAY_TERMS):
            piece = rest.astype(o_ref.dtype)
            o_ref[term, :, seg * LANES:(seg + 1) * LANES] = piece
            rest = rest - piece.astype(jnp.float32)
    carry_ref[...] = carry


def _decay_cumsum(f_t, bias_col, width):
    rows, s = f_t.shape
    tri = jnp.asarray(np.triu(np.ones((LANES, LANES), np.float32)))
    return pl.pallas_call(
        _decay_kernel,
        grid=(s // width,),
        in_specs=[
            pl.BlockSpec((rows, width), lambda i: (0, i)),
            pl.BlockSpec((rows, 1), lambda i: (0, 0)),
            pl.BlockSpec((LANES, LANES), lambda i: (0, 0)),
        ],
        out_specs=pl.BlockSpec((DECAY_TERMS, rows, width), lambda i: (0, 0, i)),
        out_shape=jax.ShapeDtypeStruct((DECAY_TERMS, rows, s), jnp.bfloat16),
        scratch_shapes=[pltpu.VMEM((rows, 1), jnp.float32)],
        compiler_params=_params(("arbitrary",)),
        name="decay_cumsum",
    )(f_t, bias_col, tri)


def _col_reduce(op, x):
    reduce = {jnp.maximum: jnp.max, jnp.minimum: jnp.min, jnp.add: jnp.sum}[op]
    return reduce(x, axis=0, keepdims=True)


def _flash_init(m_ref, acc_ref):
    m_ref[...] = jnp.full(m_ref.shape, NEG, jnp.float32)
    acc_ref[...] = jnp.zeros(acc_ref.shape, jnp.float32)


def _probabilities(s, m_new):
    return jnp.exp2((s - m_new).astype(jnp.bfloat16))


def _flash_result(acc_ref):
    return acc_ref[0:HEAD_DIM, :] / acc_ref[SUM_ROW:SUM_ROW + 1, :]


def _flash_step(kaug, vt, get_q, width, m_ref, acc_ref, extra=None):
    w = width // N_SPLIT
    strips = [(i * w, (i + 1) * w) for i in range(N_SPLIT)]
    scores = [jnp.dot(kaug, get_q(lo, hi), preferred_element_type=jnp.float32) for lo, hi in strips]
    for (lo, hi), s in zip(strips, scores):
        if extra is not None:
            s = extra(s, lo, hi)
        m_prev = m_ref[:, lo:hi]
        m_new = jnp.maximum(m_prev, _col_reduce(jnp.maximum, s))
        alpha = jnp.exp2(m_prev - m_new)
        acc_ref[:, lo:hi] = alpha * acc_ref[:, lo:hi] + jnp.dot(
            vt, _probabilities(s, m_new), preferred_element_type=jnp.float32)
        m_ref[:, lo:hi] = m_new


def _flash_far_chunks(n, load_keys, load_values, get_q_for, width, m_ref, acc_ref, s_ref, mx_ref, p_ref):
    w = width // N_SPLIT
    strips = [(i * w, (i + 1) * w) for i in range(N_SPLIT)]
    base = n % 2
    pairs = n // 2

    def scores_into(c, slot):
        kaug = load_keys(c)
        get_q = get_q_for(c)
        for lo, hi in strips:
            s = jnp.dot(kaug, get_q(lo, hi), preferred_element_type=jnp.float32)
            s_ref[slot, :, lo:hi] = s
            mx_ref[slot, :, lo:hi] = _col_reduce(jnp.maximum, s)

    def value_product(c, slot):
        vt = load_values(c)
        return jnp.concatenate([jnp.dot(vt, p_ref[slot, :, lo:hi], preferred_element_type=jnp.float32)
                                for lo, hi in strips], axis=1)

    @pl.when(base == 1)
    def _():
        _flash_step(load_keys(0), load_values(0), get_q_for(0), width, m_ref, acc_ref)

    @pl.when(pairs > 0)
    def _():
        scores_into(base, 0)
        p_ref[1] = jnp.zeros(p_ref.shape[1:], p_ref.dtype)

        def pair(j, carry):
            for cur in (0, 1):
                c = base + 2 * j + cur
                nxt = 1 - cur
                pv = value_product(jnp.maximum(c - 1, base), nxt)
                scores_into(jnp.minimum(c + 1, n - 1), nxt)
                alphas = []
                for lo, hi in strips:
                    m_prev = m_ref[:, lo:hi]
                    m_new = jnp.maximum(m_prev, mx_ref[cur, :, lo:hi])
                    m_ref[:, lo:hi] = m_new
                    for r in range(0, s_ref.shape[1], ROW_BLOCK):
                        p_ref[cur, r:r + ROW_BLOCK, lo:hi] = _probabilities(
                            s_ref[cur, r:r + ROW_BLOCK, lo:hi], m_new)
                    alphas.append(jnp.exp2(m_prev - m_new))
                acc_ref[...] = jnp.concatenate(alphas, axis=1) * (acc_ref[...] + pv)
            return carry

        lax.fori_loop(0, pairs, pair, 0)
        acc_ref[...] = acc_ref[...] + value_product(n - 1, 1)


def _group_queries_t(qt_ref):
    return jnp.concatenate([qt_ref[h] for h in range(NSA_HPG)], axis=1)


def _branch_gate(gates_t, head, branch):
    row = head * 3 + branch
    return gates_t[row:row + 1, :]


def _values_t(vt_ref, k0, n_keys):
    p0 = k0 // LANES
    return jnp.concatenate([vt_ref[p0 + j] for j in range(n_keys // LANES)], axis=1)


def _compressed_branch(n_rows, i0, q4t, kc_ref, vct_ref, band_ref, ovt_ref, sc_ref, oct_ref, imp_ref):
    sc_ref[0:n_rows, :] = jnp.dot(kc_ref[0:n_rows, :], q4t, preferred_element_type=jnp.float32)
    sc_ref[pl.ds(i0, BAND_ROWS), :] = sc_ref[pl.ds(i0, BAND_ROWS), :] + band_ref[...]
    row = lax.broadcasted_iota(jnp.int32, (n_rows, ROWS), 0)
    sc = jnp.where(row < i0 + BAND_ROWS, sc_ref[0:n_rows, :], NEG)
    m = _col_reduce(jnp.maximum, sc)
    p = jnp.exp2(sc - m)
    l = _col_reduce(jnp.add, p)
    pn = p * jnp.where(m > 0.5 * NEG, 1.0 / l, 0.0)
    oct_ref[...] = jnp.dot(vct_ref[:, 0:n_rows], pn.astype(jnp.bfloat16), preferred_element_type=jnp.float32)
    psum = pn[:, 0:Q_TILE]
    for h in range(1, NSA_HPG):
        psum = psum + pn[:, h * Q_TILE:(h + 1) * Q_TILE]
    imp_ref[...] = jnp.dot(ovt_ref[:, 0:n_rows], psum.astype(jnp.bfloat16), preferred_element_type=jnp.float32)


def _cmp_win_kernel(qt_ref, kc_ref, vct_ref, band_ref, ovt_ref, kw_ref, vwt_ref, near_ref, gt_ref,
                    ycw_ref, pen_ref, sc_ref, sw_ref, oct_ref, imp_ref):
    qb = pl.program_id(2)
    n_cmp = kc_ref.shape[0]
    n_slc = ovt_ref.shape[0]
    q4t = _group_queries_t(qt_ref)

    i0 = pl.multiple_of(jnp.maximum(8 * qb - 16, 0), 8)
    n_tiers = CMP_TIERS if n_cmp % (CMP_TIERS * LANES) == 0 else 1
    step = n_cmp // n_tiers
    for tier in range(1, n_tiers + 1):
        in_tier = (i0 + BAND_ROWS <= tier * step) & (i0 + BAND_ROWS > (tier - 1) * step)
        pl.when(in_tier)(functools.partial(_compressed_branch, tier * step, i0, q4t, kc_ref, vct_ref,
                                           band_ref, ovt_ref, sc_ref, oct_ref, imp_ref))
    oct_ = oct_ref[...]
    imp = imp_ref[...]

    ji = lax.broadcasted_iota(jnp.int32, (n_slc, Q_TILE), 0)
    jf = ji.astype(jnp.float32)
    t = qb * Q_TILE + lax.broadcasted_iota(jnp.int32, (n_slc, Q_TILE), 1)
    cur = t // SLC_BLOCK
    forced = (ji == 0) | (ji == cur) | (ji == cur - 1)
    score = jnp.where(ji <= cur, jnp.where(forced, FORCED_SCORE, imp), -1.0)
    pen_t = jnp.full((n_slc, Q_TILE), NEG, jnp.float32)
    for _ in range(min(SLC_TOPK, n_slc)):
        mx = _col_reduce(jnp.maximum, score)
        idx = _col_reduce(jnp.minimum, jnp.where(score == mx, jf, float(n_slc)))
        pick = jf == idx
        pen_t = jnp.where(pick, 0.0, pen_t)
        score = jnp.where(pick, -2.0, score)
    pad = pen_ref.shape[0] - n_slc
    if pad:
        pen_t = jnp.concatenate([pen_t, jnp.full((pad, Q_TILE), NEG, jnp.float32)], axis=0)
    pen_ref[...] = pen_t.astype(pen_ref.dtype)

    t0 = qb * Q_TILE
    ws = pl.multiple_of(jnp.maximum(t0 - WINDOW, 0), Q_TILE)
    ns = pl.multiple_of(jnp.maximum(t0 - Q_TILE, 0), Q_TILE)
    sw_ref[...] = jnp.dot(kw_ref[pl.ds(ws, WIN_KEYS), :], q4t, preferred_element_type=jnp.float32)
    off = pl.multiple_of(ns - ws, Q_TILE)
    sw_ref[pl.ds(off, NEAR_KEYS), :] = sw_ref[pl.ds(off, NEAR_KEYS), :] + near_ref[...]
    key = ws + lax.broadcasted_iota(jnp.int32, (WIN_KEYS, ROWS), 0)
    qry = t0 + (lax.broadcasted_iota(jnp.int32, (WIN_KEYS, ROWS), 1) & (Q_TILE - 1))
    sw = jnp.where((key <= qry) & (key > qry - WINDOW), sw_ref[...], NEG)
    mw = _col_reduce(jnp.maximum, sw)
    ow_sum = jnp.dot(_values_t(vwt_ref, ws, WIN_KEYS), _probabilities(sw, mw),
                     preferred_element_type=jnp.float32)
    owt = ow_sum[0:HEAD_DIM, :] / ow_sum[SUM_ROW:SUM_ROW + 1, :]

    gates_t = jax.nn.sigmoid(gt_ref[...])
    for h in range(NSA_HPG):
        cols = slice(h * Q_TILE, (h + 1) * Q_TILE)
        mixed = _branch_gate(gates_t, h, 0) * oct_[:, cols] + _branch_gate(gates_t, h, 2) * owt[:, cols]
        ycw_ref[:, h * HEAD_DIM:(h + 1) * HEAD_DIM] = mixed.T


def _cmp_win(qt, kc, vct, band, ovt, qkv, vwt, near, gates_t):
    b, s, _ = qkv.shape
    n_cmp = kc.shape[2]
    n_slc = ovt.shape[0]
    nq = s // Q_TILE
    pen_w = -(-n_slc // LANES) * LANES
    n_band = band.shape[0] - 1
    return pl.pallas_call(
        _cmp_win_kernel,
        grid=(b, NSA_GROUPS, nq),
        in_specs=[
            pl.BlockSpec((None, NSA_HPG, HEAD_DIM, Q_TILE), lambda i, g, q: (i, g, 0, q)),
            pl.BlockSpec((None, None, n_cmp, HEAD_DIM), lambda i, g, q: (i, g, 0, 0)),
            pl.BlockSpec((None, None, HEAD_DIM, n_cmp), lambda i, g, q: (i, g, 0, 0)),
            pl.BlockSpec((None, None, BAND_ROWS, ROWS), lambda i, g, q: (jnp.minimum(q, n_band), g, 0, 0)),
            pl.BlockSpec((n_slc, n_cmp), lambda i, g, q: (0, 0)),
            pl.BlockSpec((None, s, HEAD_DIM), lambda i, g, q: (i, 0, CB_KW + g)),
            pl.BlockSpec((None, None, s // LANES, ACC_ROWS, LANES), lambda i, g, q: (i, g, 0, 0, 0)),
            pl.BlockSpec((None, None, NEAR_KEYS, ROWS), lambda i, g, q: (jnp.minimum(q, 1), g, 0, 0)),
            pl.BlockSpec((None, None, GATE_ROWS, Q_TILE), lambda i, g, q: (i, g, 0, q)),
        ],
        out_specs=[
            pl.BlockSpec((None, Q_TILE, NSA_HPG * HEAD_DIM), lambda i, g, q: (i, q, g)),
            pl.BlockSpec((None, None, pen_w, Q_TILE), lambda i, g, q: (i, g, 0, q)),
        ],
        out_shape=[
            jax.ShapeDtypeStruct((b, s, NSA_Q_DIM), jnp.float32),
            jax.ShapeDtypeStruct((b, NSA_GROUPS, pen_w, s), jnp.bfloat16),
        ],
        scratch_shapes=[pltpu.VMEM((n_cmp, ROWS), jnp.float32), pltpu.VMEM((WIN_KEYS, ROWS), jnp.float32),
                        pltpu.VMEM((HEAD_DIM, ROWS), jnp.float32), pltpu.VMEM((n_slc, Q_TILE), jnp.float32)],
        compiler_params=_params(("parallel", "parallel", "arbitrary")),
        name="nsa_cmp_topk_win",
    )(qt, kc, vct, band, ovt, qkv, vwt, near, gates_t)


def _range_penalty(pen, first_blk, lo_ok, hi_ok):
    n_half = pen.shape[0] // LANES
    c = lax.broadcasted_iota(jnp.int32, (LANES, pen.shape[1]), 0)
    out = pen[0:LANES, :]
    blk = c
    for hf in range(1, n_half):
        in_lower = (c + (hf - 1) * LANES >= first_blk) & (first_blk < hf * LANES)
        out = jnp.where(in_lower, out, pen[hf * LANES:(hf + 1) * LANES, :])
        blk = jnp.where(in_lower, blk, c + hf * LANES)
    return jnp.where((blk >= lo_ok) & (blk < hi_ok), out, jnp.asarray(NEG, out.dtype))


def _sel_kernel(qt_ref, pen_ref, kaug_ref, vt_ref, near_ref, ycw_ref, gt_ref, y_ref,
                qaug_ref, m_ref, acc_ref, s_ref, mx_ref, p_ref):
    qb = pl.program_id(2)
    n_half = pen_ref.shape[0] // LANES
    n_blocks = pen_ref.shape[0]
    chunks_per_half = LANES * SLC_BLOCK // KEY_CHUNK
    blocks_per_chunk = KEY_CHUNK // SLC_BLOCK

    t0 = qb * Q_TILE
    near_end = t0 - Q_TILE
    nf = jnp.maximum(near_end // KEY_CHUNK, 0)
    ms = pl.multiple_of(jnp.maximum(near_end - KEY_CHUNK, 0), Q_TILE)
    ns = pl.multiple_of(jnp.maximum(near_end, 0), Q_TILE)

    q4t = _group_queries_t(qt_ref)
    pen = pen_ref[...]
    operands = [pen[hf * LANES:(hf + 1) * LANES, :] for hf in range(n_half)]
    operands.append(_range_penalty(pen, ms // SLC_BLOCK, nf * blocks_per_chunk, near_end // SLC_BLOCK))
    operands.append(_range_penalty(pen, ns // SLC_BLOCK, 0, n_blocks))
    for idx, channels in enumerate(operands):
        qaug_ref[idx, 0:HEAD_DIM, :] = q4t
        qaug_ref[idx, HEAD_DIM:, :] = jnp.concatenate([channels] * NSA_HPG, axis=1)

    _flash_init(m_ref, acc_ref)

    def key_start(c):
        return pl.multiple_of(jnp.where(c < nf, c * KEY_CHUNK, ms), Q_TILE)

    def get_q_for(c):
        operand = jnp.where(c < nf, c // chunks_per_half, n_half)
        return lambda lo, hi: qaug_ref[operand, :, lo:hi]

    _flash_far_chunks(nf + 1, lambda c: kaug_ref[pl.ds(key_start(c), KEY_CHUNK), :],
                      lambda c: _values_t(vt_ref, key_start(c), KEY_CHUNK), get_q_for,
                      ROWS, m_ref, acc_ref, s_ref, mx_ref, p_ref)

    _flash_step(kaug_ref[pl.ds(ns, NEAR_KEYS), :], _values_t(vt_ref, ns, NEAR_KEYS),
                lambda lo, hi: qaug_ref[n_half + 1, :, lo:hi], ROWS, m_ref, acc_ref,
                extra=lambda s, lo, hi: s + near_ref[:, lo:hi])

    o = _flash_result(acc_ref)
    gates_t = jax.nn.sigmoid(gt_ref[...])
    for h in range(NSA_HPG):
        sel_h = (_branch_gate(gates_t, h, 1) * o[:, h * Q_TILE:(h + 1) * Q_TILE]).T
        y_ref[:, h * HEAD_DIM:(h + 1) * HEAD_DIM] = (
            ycw_ref[:, h * HEAD_DIM:(h + 1) * HEAD_DIM] + sel_h).astype(y_ref.dtype)


def _sel(qt, pen, kaug, vt_pieces, near, ycw, gates_t):
    b, _, s, _ = kaug.shape
    nq = s // Q_TILE
    pen_w = pen.shape[2]
    return pl.pallas_call(
        _sel_kernel,
        grid=(b, NSA_GROUPS, nq),
        in_specs=[
            pl.BlockSpec((None, NSA_HPG, HEAD_DIM, Q_TILE), lambda i, g, q: (i, g, 0, q)),
            pl.BlockSpec((None, None, pen_w, Q_TILE), lambda i, g, q: (i, g, 0, q)),
            pl.BlockSpec((None, None, s, 2 * HEAD_DIM), lambda i, g, q: (i, g, 0, 0)),
            pl.BlockSpec((None, None, s // LANES, ACC_ROWS, LANES), lambda i, g, q: (i, g, 0, 0, 0)),
            pl.BlockSpec((None, None, NEAR_KEYS, ROWS), lambda i, g, q: (jnp.minimum(q, 1), g, 0, 0)),
            pl.BlockSpec((None, Q_TILE, NSA_HPG * HEAD_DIM), lambda i, g, q: (i, q, g)),
            pl.BlockSpec((None, None, GATE_ROWS, Q_TILE), lambda i, g, q: (i, g, 0, q)),
        ],
        out_specs=pl.BlockSpec((None, Q_TILE, NSA_HPG * HEAD_DIM), lambda i, g, q: (i, q, g)),
        out_shape=jax.ShapeDtypeStruct((b, s, NSA_Q_DIM), jnp.bfloat16),
        scratch_shapes=[
            pltpu.VMEM((pen_w // LANES + 2, 2 * HEAD_DIM, ROWS), jnp.bfloat16),
            pltpu.VMEM((1, ROWS), jnp.float32),
            pltpu.VMEM((ACC_ROWS, ROWS), jnp.float32),
            pltpu.VMEM((2, KEY_CHUNK, ROWS), jnp.float32),
            pltpu.VMEM((2, 1, ROWS), jnp.float32),
            pltpu.VMEM((2, KEY_CHUNK, ROWS), jnp.bfloat16),
        ],
        compiler_params=_params(("parallel", "parallel", "arbitrary")),
        name="nsa_selected",
    )(qt, pen, kaug, vt_pieces, near, ycw, gates_t)


def _fox_kernel(nce_ref, q_ref, k_ref, dec_ref, v_ref, o_ref,
                qaug_ref, m_ref, acc_ref, s_ref, mx_ref, p_ref, knorm_ref, vt_ref):
    qi = pl.program_id(2)
    bh = pl.program_id(0) * pl.num_programs(1) + pl.program_id(1)
    tq = q_ref.shape[0]
    row = lax.broadcasted_iota(jnp.int32, (HEAD_DIM, tq), 0)
    q_t = q_ref[...].T
    qaug_ref[0:HEAD_DIM, :] = q_t
    qaug_ref[HEAD_DIM:, :] = jnp.where(row < DECAY_TERMS, -1.0, 0.0).astype(qaug_ref.dtype)
    _flash_init(m_ref, acc_ref)

    @pl.when(qi == 0)
    def _():
        pad = lax.broadcasted_iota(jnp.int32, (ACC_ROWS - HEAD_DIM, tq), 0)
        ones_rows = jnp.where(pad == 0, 1.0, 0.0).astype(vt_ref.dtype)

        def chunk_setup(c, best):
            rows = pl.ds(pl.multiple_of(c * tq, tq), tq)
            vt_ref[c, 0:HEAD_DIM, :] = v_ref[rows, :].T
            vt_ref[c, HEAD_DIM:, :] = ones_rows
            kf = k_ref[rows, :].astype(jnp.float32)
            return jnp.maximum(best, jnp.max(jnp.sum(kf * kf, axis=1, keepdims=True)))

        knorm_ref[0] = jnp.sqrt(lax.fori_loop(0, k_ref.shape[0] // tq, chunk_setup, jnp.float32(0.0)))

    qf = q_t.astype(jnp.float32)
    q_norm = jnp.sqrt(jnp.max(jnp.sum(qf * qf, axis=0, keepdims=True)))

    def get_q(lo, hi):
        return qaug_ref[:, lo:hi]

    def load_keys(c):
        k0 = pl.multiple_of(c * tq, tq)
        return jnp.concatenate([k_ref[pl.ds(k0, tq), :], dec_ref[pl.ds(k0, tq), :]], axis=1)

    def causal(s, lo, hi):
        key = lax.broadcasted_iota(jnp.int32, s.shape, 0)
        qry = lo + lax.broadcasted_iota(jnp.int32, s.shape, 1)
        return jnp.where(key <= qry, s, NEG)

    _flash_step(load_keys(qi), vt_ref[qi], get_q, tq, m_ref, acc_ref, extra=causal)

    m_low = jnp.min(m_ref[...])
    reach = q_norm * knorm_ref[0] + SKIP_MARGIN_LOG2

    def count(c, n_skip):
        return n_skip + jnp.where(reach + nce_ref[bh, c] < m_low, 1, 0)

    first = lax.fori_loop(0, qi, count, jnp.int32(0))
    _flash_far_chunks(qi - first, lambda c: load_keys(c + first), lambda c: vt_ref[c + first],
                      lambda c: get_q, tq, m_ref, acc_ref, s_ref, mx_ref, p_ref)
    o_ref[...] = _flash_result(acc_ref).T.astype(o_ref.dtype)


def _fox(neg_cum_end, qkv, dec, tq):
    b, s, _ = qkv.shape
    return pl.pallas_call(
        _fox_kernel,
        grid=(b, FOX_HEADS, s // tq),
        in_specs=[
            pl.BlockSpec(memory_space=pltpu.SMEM),
            pl.BlockSpec((None, tq, HEAD_DIM), lambda i, h, q: (i, q, CB_QF + h)),
            pl.BlockSpec((None, s, HEAD_DIM), lambda i, h, q: (i, 0, CB_KF + h)),
            pl.BlockSpec((None, None, s, LANES), lambda i, h, q: (i, h, 0, 0)),
            pl.BlockSpec((None, s, HEAD_DIM), lambda i, h, q: (i, 0, CB_VF + h)),
        ],
        out_specs=pl.BlockSpec((None, tq, HEAD_DIM), lambda i, h, q: (i, q, h)),
        out_shape=jax.ShapeDtypeStruct((b, s, FOX_DIM), jnp.bfloat16),
        scratch_shapes=[
            pltpu.VMEM((2 * HEAD_DIM, tq), jnp.bfloat16),
            pltpu.VMEM((1, tq), jnp.float32),
            pltpu.VMEM((ACC_ROWS, tq), jnp.float32),
            pltpu.VMEM((2, tq, tq), jnp.float32),
            pltpu.VMEM((2, 1, tq), jnp.float32),
            pltpu.VMEM((2, tq, tq), jnp.bfloat16),
            pltpu.SMEM((1,), jnp.float32),
            pltpu.VMEM((s // tq, ACC_ROWS, tq), jnp.bfloat16),
        ],
        compiler_params=_params(("parallel", "parallel", "arbitrary")),
        name="fox_attention",
    )(neg_cum_end, qkv, qkv, dec, qkv)


def _merge_kernel(x_ref, g_ref, wa_ref, wb_ref, pa_ref, pb_ref, wo_ref, ya_ref, yb_ref, o_ref,
                  h_ref, acc_ref):
    j = pl.program_id(1)

    @pl.when(j == 0)
    def _():
        h_ref[...] = _norm_rows(x_ref[...], g_ref[...]).astype(h_ref.dtype)
        acc_ref[...] = jnp.zeros_like(acc_ref)

    h = h_ref[...]
    ga = jax.nn.sigmoid(jnp.dot(h, wa_ref[...], preferred_element_type=jnp.float32))
    gb = jax.nn.sigmoid(jnp.dot(h, wb_ref[...], preferred_element_type=jnp.float32))
    a = jnp.dot(ya_ref[...], pa_ref[...], preferred_element_type=jnp.float32)
    bb = jnp.dot(yb_ref[...], pb_ref[...], preferred_element_type=jnp.float32)
    merged = (ga * a + gb * bb).astype(jnp.bfloat16)
    acc_ref[...] += jnp.dot(merged, wo_ref[...], preferred_element_type=jnp.float32)

    @pl.when(j == pl.num_programs(1) - 1)
    def _():
        o_ref[...] = x_ref[...] + acc_ref[...]


def _merge(x2d, g, w_ma, w_mb, p_a, p_b, w_out, y_a, y_b, tm, tn):
    t, d = x2d.shape
    ka = p_a.shape[0]
    kb = p_b.shape[0]
    return pl.pallas_call(
        _merge_kernel,
        grid=(t // tm, d // tn),
        in_specs=[
            pl.BlockSpec((tm, d), lambda i, j: (i, 0)),
            pl.BlockSpec((1, d), lambda i, j: (0, 0)),
            pl.BlockSpec((d, tn), lambda i, j: (0, j)),
            pl.BlockSpec((d, tn), lambda i, j: (0, j)),
            pl.BlockSpec((ka, tn), lambda i, j: (0, j)),
            pl.BlockSpec((kb, tn), lambda i, j: (0, j)),
            pl.BlockSpec((tn, d), lambda i, j: (j, 0)),
            pl.BlockSpec((tm, ka), lambda i, j: (i, 0)),
            pl.BlockSpec((tm, kb), lambda i, j: (i, 0)),
        ],
        out_specs=pl.BlockSpec((tm, d), lambda i, j: (i, 0)),
        out_shape=jax.ShapeDtypeStruct((t, d), jnp.float32),
        scratch_shapes=[pltpu.VMEM((tm, d), jnp.bfloat16), pltpu.VMEM((tm, d), jnp.float32)],
        compiler_params=_params(("parallel", "arbitrary")),
        name="merge_out_proj",
    )(x2d, g.reshape(1, d), w_ma, w_mb, p_a, p_b, w_out, y_a, y_b)


def _ffn_kernel(x_ref, xh_ref, g_ref, wu_ref, wv_ref, cw_ref, cb_ref, wd_ref, gf_ref, o_ref,
                h_ref, u_ref, acc_ref, *, seq, tm):
    i = pl.program_id(0)
    j = pl.program_id(1)
    halo = BF16_SUBLANES

    @pl.when(j == 0)
    def _():
        g = g_ref[...]
        keep = jnp.where((i * tm) % seq == 0, 0.0, 1.0)
        h_ref[0:halo, :] = (_norm_rows(xh_ref[...], g) * keep).astype(h_ref.dtype)
        h_ref[halo:, :] = _norm_rows(x_ref[...], g).astype(h_ref.dtype)
        acc_ref[...] = jnp.zeros_like(acc_ref)

    u_ref[...] = jnp.dot(h_ref[...], wu_ref[...], preferred_element_type=jnp.float32)
    v = jnp.dot(h_ref[halo:, :], wv_ref[...], preferred_element_type=jnp.float32)
    conv = cb_ref[...]
    for k in range(CONV_WIDTH):
        conv = conv + cw_ref[k:k + 1, :] * u_ref[pl.ds(halo - (CONV_WIDTH - 1) + k, tm), :]
    act = (jax.nn.gelu(conv) * v).astype(jnp.bfloat16)
    acc_ref[...] += jnp.dot(act, wd_ref[...], preferred_element_type=jnp.float32)

    @pl.when(j == pl.num_programs(1) - 1)
    def _():
        o_ref[...] = _norm_rows(x_ref[...] + acc_ref[...], gf_ref[...])


def _ffn(x2d, g, w_up, conv_w, conv_b, w_down, g_final, seq, tm, tn):
    t, d = x2d.shape
    d_ff = w_down.shape[0]
    nt = d_ff // tn
    halo = BF16_SUBLANES
    per = tm // halo
    return pl.pallas_call(
        functools.partial(_ffn_kernel, seq=seq, tm=tm),
        grid=(t // tm, nt),
        in_specs=[
            pl.BlockSpec((tm, d), lambda i, j: (i, 0)),
            pl.BlockSpec((halo, d), lambda i, j: (jnp.maximum(i * per - 1, 0), 0)),
            pl.BlockSpec((1, d), lambda i, j: (0, 0)),
            pl.BlockSpec((d, tn), lambda i, j: (0, j)),
            pl.BlockSpec((d, tn), lambda i, j: (0, nt + j)),
            pl.BlockSpec((CONV_WIDTH, tn), lambda i, j: (0, j)),
            pl.BlockSpec((1, tn), lambda i, j: (0, j)),
            pl.BlockSpec((tn, d), lambda i, j: (j, 0)),
            pl.BlockSpec((1, d), lambda i, j: (0, 0)),
        ],
        out_specs=pl.BlockSpec((tm, d), lambda i, j: (i, 0)),
        out_shape=jax.ShapeDtypeStruct((t, d), jnp.float32),
        scratch_shapes=[
            pltpu.VMEM((tm + halo, d), jnp.bfloat16),
            pltpu.VMEM((tm + halo, tn), jnp.float32),
            pltpu.VMEM((tm, d), jnp.float32),
        ],
        compiler_params=_params(("parallel", "arbitrary")),
        name="ffn_final_norm",
    )(x2d, x2d, g.reshape(1, d), w_up, w_up, conv_w, conv_b.reshape(1, d_ff), w_down, g_final.reshape(1, d))


def _t5_bucket_np(dist):
    n = np.maximum(dist, 0)
    max_exact = REL_BUCKETS // 2
    nf = np.maximum(n, 1).astype(np.float32)
    large = max_exact + (np.log(nf / np.float32(max_exact)) / np.float32(math.log(REL_MAX_DIST / max_exact))
                         * np.float32(REL_BUCKETS - max_exact)).astype(np.int32)
    return np.where(n < max_exact, n, np.minimum(large, REL_BUCKETS - 1)).astype(np.int32)


def _bias_by_distance(rel_table, far_shift):
    vals = rel_table[jnp.asarray(_t5_bucket_np(np.arange(REL_MAX_DIST + 1)))]
    if far_shift:
        vals = vals - rel_table[REL_BUCKETS - 1]
    return vals * LOG2E


def _near_bias(rel_table):
    fd = _bias_by_distance(rel_table, True)
    h = fd.shape[1]
    lo = NEAR_KEYS - 1
    vec = jnp.concatenate([jnp.full((lo, h), NEG, fd.dtype), fd[:REL_MAX_DIST],
                           jnp.zeros((NEAR_KEYS - REL_MAX_DIST, h), fd.dtype)], axis=0)
    def toeplitz(start):
        period = Q_TILE + NEAR_KEYS - 1
        window = jnp.concatenate([vec[start:start + Q_TILE], vec[start - (NEAR_KEYS - 1):start]], axis=0)
        flat = jnp.tile(window, (NEAR_KEYS, 1))[:NEAR_KEYS * (period - 1)]
        return flat.reshape(NEAR_KEYS, period - 1, h)[:, :Q_TILE]

    tiles = jnp.stack([toeplitz(lo + a) for a in (0, Q_TILE)])
    tiles = tiles.reshape(2, NEAR_KEYS, Q_TILE, NSA_GROUPS, NSA_HPG).transpose(0, 3, 1, 4, 2)
    return tiles.reshape(2, NSA_GROUPS, NEAR_KEYS, ROWS)


def _band_bias(rel_table):
    fd = _bias_by_distance(rel_table, True)
    tl = np.arange(Q_TILE)[None, :]
    r = np.arange(BAND_ROWS)[:, None]
    dist = np.stack([tl - CMP_STRIDE * (r - off) - (CMP_BLOCK - 1) for off in (0, 8, 16)])
    vals = fd[jnp.asarray(np.clip(dist, 0, REL_MAX_DIST))]
    vals = jnp.where(jnp.asarray(dist >= 0)[..., None], vals, NEG)
    v, rr, q, _ = vals.shape
    return vals.transpose(0, 1, 3, 2).reshape(v, rr, NSA_GROUPS, NSA_HPG * q).transpose(0, 2, 1, 3)


def _overlap_t(n_cmp_pad, n_slc):
    i = np.arange(n_cmp_pad)[None, :]
    jj = np.arange(n_slc)[:, None]
    c_start = i * CMP_STRIDE
    ov = (c_start < jj * SLC_BLOCK + SLC_BLOCK) & (c_start + CMP_BLOCK - 1 >= jj * SLC_BLOCK)
    ov = ov & (i < n_cmp_pad - 1)
    return jnp.asarray(ov.astype(np.float32), jnp.bfloat16)


def _block_onehot(seq):
    blk = (np.arange(seq) // SLC_BLOCK) % LANES
    return jnp.asarray((blk[:, None] == np.arange(LANES)[None, :]).astype(np.float32), jnp.bfloat16)


def _pick_tile(n, pref):
    return pref if n % pref == 0 else n


def kernel(x, attn_norm_g, w_in, cmp_pos_k, cmp_w1_k, cmp_w2_k, cmp_pos_v, cmp_w1_v, cmp_w2_v,
           rel_bias_table, fox_forget_bias, w_branch_nsa, w_branch_fox, w_out,
           ffn_norm_g, w_up, conv_w, conv_b, w_down, final_norm_g):
    assert w_in.shape[0] == 1, "the final norm is fused into the single layer's FFN kernel"
    bsz, seq, d = x.shape
    t = bsz * seq
    bf = jnp.bfloat16
    scale = HEAD_DIM ** -0.5 * LOG2E
    x2d = x.reshape(t, d)
    w_in = w_in[0]

    o = np.cumsum([0, NSA_Q_DIM] + [NSA_KV_DIM] * 6 + [3 * NSA_HEADS, FOX_DIM, FOX_DIM, FOX_DIM, FOX_HEADS, d, d])
    w_qkv = jnp.concatenate([w_in[:, o[0]:o[1]] * scale, w_in[:, o[1]:o[7]],
                             w_in[:, o[8]:o[9]] * scale, w_in[:, o[9]:o[11]]], axis=1).astype(bf)
    gate_cols = []
    for grp in range(NSA_GROUPS):
        gate_cols += [w_in[:, o[7] + grp * GATES_PER_GROUP:o[7] + (grp + 1) * GATES_PER_GROUP],
                      jnp.zeros((d, LANES - GATES_PER_GROUP), w_in.dtype)]
    gate_cols += [w_in[:, o[11]:o[12]], jnp.zeros((d, IN_PROJ_TILE - N_GATE + LANES - FOX_HEADS), w_in.dtype)]
    w_gate = jnp.concatenate(gate_cols, axis=1).astype(bf)
    w_ma = w_in[:, o[12]:o[13]].astype(bf)
    w_mb = w_in[:, o[13]:o[14]].astype(bf)

    qkv, gates = _in_proj(x2d, attn_norm_g[0], jnp.concatenate([w_qkv, w_gate], axis=1),
                          _pick_tile(t, 1024), IN_PROJ_TILE)
    qkv = qkv.reshape(bsz, seq, N_QKV)
    gates = gates.reshape(bsz, seq, IN_PROJ_TILE)

    qt = qkv[:, :, :NSA_Q_DIM].reshape(bsz, seq, NSA_HEADS, HEAD_DIM).transpose(0, 2, 3, 1)

    def heads_major(cb, heads):
        return qkv[:, :, cb * HEAD_DIM:(cb + heads) * HEAD_DIM].reshape(
            bsz, seq, heads, HEAD_DIM).transpose(0, 2, 1, 3)

    def values_t(cb, heads, piece):
        v = qkv[:, :, cb * HEAD_DIM:(cb + heads) * HEAD_DIM]
        v = v.reshape(bsz, seq // piece, piece, heads, HEAD_DIM).transpose(0, 3, 1, 4, 2)
        ones_row = (jnp.arange(ACC_ROWS - HEAD_DIM) == 0).astype(bf)[:, None]
        extra = jnp.broadcast_to(ones_row, v.shape[:3] + (ACC_ROWS - HEAD_DIM, piece))
        return jnp.concatenate([v, extra], axis=3)

    n_ch = seq // CMP_STRIDE

    def chunked(cb):
        sl = qkv[:, :, cb * HEAD_DIM:(cb + NSA_GROUPS) * HEAD_DIM]
        sl = sl.reshape(bsz, n_ch, CMP_STRIDE, NSA_GROUPS, HEAD_DIM).transpose(0, 3, 1, 2, 4)
        return sl.reshape(bsz, NSA_GROUPS, n_ch, CMP_STRIDE * HEAD_DIM)

    def posflat(pos):
        return jnp.broadcast_to(pos.reshape(1, CMP_BLOCK * HEAD_DIM), (8, CMP_BLOCK * HEAD_DIM)).astype(bf)

    kc = _compress(chunked(CB_KC), cmp_w1_k[0].astype(bf), posflat(cmp_pos_k[0]), cmp_w2_k[0].astype(bf))
    vc = _compress(chunked(CB_VC), cmp_w1_v[0].astype(bf), posflat(cmp_pos_v[0]), cmp_w2_v[0].astype(bf))
    vct = vc.transpose(0, 1, 3, 2)

    f_t = gates[:, :, NSA_GROUPS * LANES:NSA_GROUPS * LANES + FOX_HEADS]
    f_t = f_t.transpose(0, 2, 1).reshape(bsz * FOX_HEADS, seq)
    bias_col = jnp.tile(fox_forget_bias[0].astype(jnp.float32), bsz).reshape(bsz * FOX_HEADS, 1)
    terms = _decay_cumsum(f_t, bias_col, _pick_tile(seq, 2048))
    dec = jnp.pad(terms.transpose(1, 2, 0), ((0, 0), (0, 0), (0, LANES - DECAY_TERMS)))
    dec = dec.reshape(bsz, FOX_HEADS, seq, LANES)

    n_slc = seq // SLC_BLOCK
    near = _near_bias(rel_bias_table)
    gates_t = gates[:, :, :NSA_GROUPS * LANES].reshape(bsz, seq, NSA_GROUPS, LANES)[..., :GATE_ROWS]
    gates_t = gates_t.transpose(0, 2, 3, 1)
    ycw, pen = _cmp_win(qt, kc, vct, _band_bias(rel_bias_table), _overlap_t(n_ch, n_slc), qkv,
                        values_t(CB_VW, NSA_GROUPS, LANES), near, gates_t)
    ks = heads_major(CB_KS, NSA_GROUPS)
    kaug_sel = jnp.concatenate([ks, jnp.broadcast_to(_block_onehot(seq), ks.shape)], axis=-1)
    y_nsa = _sel(qt, pen, kaug_sel, values_t(CB_VS, NSA_GROUPS, LANES), near, ycw, gates_t)

    fox_tq = _pick_tile(seq, FOX_Q_TILE)
    neg_cum_end = -jnp.sum(terms[:, :, fox_tq - 1::fox_tq].astype(jnp.float32), axis=0)
    y_fox = _fox(neg_cum_end, qkv, dec, fox_tq)

    tm2 = _pick_tile(t, 512)
    x_mid = _merge(x2d, attn_norm_g[0], w_ma, w_mb, w_branch_nsa[0].astype(bf), w_branch_fox[0].astype(bf),
                   w_out[0].astype(bf), y_nsa.reshape(t, NSA_Q_DIM), y_fox.reshape(t, FOX_DIM),
                   tm2, _pick_tile(d, 512))
    d_ff = w_down.shape[1]
    out = _ffn(x_mid, ffn_norm_g[0], w_up[0].astype(bf), conv_w[0], conv_b[0], w_down[0].astype(bf),
               final_norm_g, seq, tm2, _pick_tile(d_ff, 512))
    return out.reshape(bsz, seq, d)
```

```python
import functools
import math

import jax
import jax.numpy as jnp
import numpy as np
from jax import lax
from jax.experimental import pallas as pl
from jax.experimental.pallas import tpu as pltpu

HEAD_DIM = 128
NSA_HEADS = 8
NSA_GROUPS = 2
NSA_HPG = NSA_HEADS // NSA_GROUPS
FOX_HEADS = 8
CMP_BLOCK = 32
CMP_STRIDE = 16
SLC_BLOCK = 64
SLC_TOPK = 16
WINDOW = 512
REL_BUCKETS = 32
REL_MAX_DIST = 128
CONV_WIDTH = 3
EPS = 1e-6
NEG = -1e30
FORCED_SCORE = 1e4
LOG2E = math.log2(math.e)

LANES = 128
BF16_SUBLANES = 16
VMEM_LIMIT = 56 * 1024 * 1024

NSA_Q_DIM = NSA_HEADS * HEAD_DIM
NSA_KV_DIM = NSA_GROUPS * HEAD_DIM
FOX_DIM = FOX_HEADS * HEAD_DIM
N_QKV = NSA_Q_DIM + 6 * NSA_KV_DIM + 3 * FOX_DIM
GATES_PER_GROUP = 3 * NSA_HPG
N_GATE = (NSA_GROUPS + 1) * LANES
IN_PROJ_TILE = 512

CB_KC = 8
CB_VC = 10
CB_KS = 12
CB_VS = 14
CB_KW = 16
CB_VW = 18
CB_QF = 20
CB_KF = 28
CB_VF = 36

Q_TILE = 128
ROWS = NSA_HPG * Q_TILE
KEY_CHUNK = 512
FOX_Q_TILE = 512
N_SPLIT = 2
SKIP_MARGIN_LOG2 = 64.0
ROW_BLOCK = 32
DECAY_TERMS = 3
BAND_ROWS = 24
CMP_TIERS = 4
WIN_KEYS = WINDOW + Q_TILE
NEAR_KEYS = 2 * Q_TILE
GATE_ROWS = 16
SUM_ROW = HEAD_DIM
ACC_ROWS = HEAD_DIM + BF16_SUBLANES


def _params(sem):
    return pltpu.CompilerParams(dimension_semantics=sem, vmem_limit_bytes=VMEM_LIMIT)


def _norm_rows(x, g):
    return (x * lax.rsqrt(jnp.mean(x * x, axis=-1, keepdims=True) + EPS)) * g


def _in_proj_kernel(x_ref, g_ref, w_ref, qkv_ref, gate_ref, h_ref):
    j = pl.program_id(1)
    last = pl.num_programs(1) - 1

    @pl.when(j == 0)
    def _():
        h_ref[...] = _norm_rows(x_ref[...], g_ref[...]).astype(h_ref.dtype)

    y = jnp.dot(h_ref[...], w_ref[...], preferred_element_type=jnp.float32)

    @pl.when(j < last)
    def _():
        qkv_ref[...] = y.astype(qkv_ref.dtype)

    @pl.when(j == last)
    def _():
        gate_ref[...] = y


def _in_proj(x2d, g, w, tm, tn):
    t, d = x2d.shape
    n_tiles = w.shape[1] // tn
    return pl.pallas_call(
        _in_proj_kernel,
        grid=(t // tm, n_tiles),
        in_specs=[
            pl.BlockSpec((tm, d), lambda i, j: (i, 0)),
            pl.BlockSpec((1, d), lambda i, j: (0, 0)),
            pl.BlockSpec((d, tn), lambda i, j: (0, j)),
        ],
        out_specs=[
            pl.BlockSpec((tm, tn), lambda i, j: (i, jnp.minimum(j, n_tiles - 2))),
            pl.BlockSpec((tm, tn), lambda i, j: (i, 0)),
        ],
        out_shape=[
            jax.ShapeDtypeStruct((t, (n_tiles - 1) * tn), jnp.bfloat16),
            jax.ShapeDtypeStruct((t, tn), jnp.float32),
        ],
        scratch_shapes=[pltpu.VMEM((tm, d), jnp.bfloat16)],
        compiler_params=_params(("parallel", "arbitrary")),
        name="norm_in_proj",
    )(x2d, g.reshape(1, d), w)


def _compress_kernel(ch_ref, w1_ref, posf_ref, w2_ref, o_ref):
    half = ch_ref.shape[1]
    ch = ch_ref[...]
    pa = jnp.dot(ch, w1_ref[:half, :], preferred_element_type=jnp.float32)
    pb = jnp.dot(ch, w1_ref[half:, :], preferred_element_type=jnp.float32)
    pos = jnp.dot(posf_ref[...], w1_ref[...], preferred_element_type=jnp.float32)[0:1, :]
    n = pa.shape[0]
    pre = pa + pltpu.roll(pb, n - 1, 0) + pos
    act = jax.nn.gelu(pre)
    o_ref[...] = jnp.dot(act.astype(jnp.bfloat16), w2_ref[...],
                         preferred_element_type=jnp.float32).astype(o_ref.dtype)


def _compress(chunks, w1, posf, w2):
    b, g, n, k = chunks.shape
    return pl.pallas_call(
        _compress_kernel,
        grid=(b, g),
        in_specs=[
            pl.BlockSpec((None, None, n, k), lambda i, j: (i, j, 0, 0)),
            pl.BlockSpec(w1.shape, lambda i, j: (0, 0)),
            pl.BlockSpec(posf.shape, lambda i, j: (0, 0)),
            pl.BlockSpec(w2.shape, lambda i, j: (0, 0)),
        ],
        out_specs=pl.BlockSpec((None, None, n, HEAD_DIM), lambda i, j: (i, j, 0, 0)),
        out_shape=jax.ShapeDtypeStruct((b, g, n, HEAD_DIM), jnp.bfloat16),
        compiler_params=_params(("parallel", "parallel")),
        name="compress_tokens",
    )(chunks, w1, posf, w2)


def _decay_kernel(f_ref, b_ref, tri_ref, o_ref, carry_ref):
    @pl.when(pl.program_id(0) == 0)
    def _():
        carry_ref[...] = jnp.zeros_like(carry_ref)

    x = f_ref[...] + b_ref[...]
    logf = (jnp.minimum(x, 0.0) - jnp.log1p(jnp.exp(-jnp.abs(x)))) * LOG2E
    carry = carry_ref[...]
    for seg in range(f_ref.shape[1] // LANES):
        part = jnp.dot(logf[:, seg * LANES:(seg + 1) * LANES], tri_ref[...],
                       preferred_element_type=jnp.float32, precision=lax.Precision.HIGHEST) + carry
        carry = part[:, LANES - 1:LANES]
        rest = part
        for term in range(DECAY_TERMS):
            piece = rest.astype(o_ref.dtype)
            o_ref[term, :, seg * LANES:(seg + 1) * LANES] = piece
            rest = rest - piece.astype(jnp.float32)
    carry_ref[...] = carry


def _decay_cumsum(f_t, bias_col, width):
    rows, s = f_t.shape
    tri = jnp.asarray(np.triu(np.ones((LANES, LANES), np.float32)))
    return pl.pallas_call(
        _decay_kernel,
        grid=(s // width,),
        in_specs=[
            pl.BlockSpec((rows, width), lambda i: (0, i)),
            pl.BlockSpec((rows, 1), lambda i: (0, 0)),
            pl.BlockSpec((LANES, LANES), lambda i: (0, 0)),
        ],
        out_specs=pl.BlockSpec((DECAY_TERMS, rows, width), lambda i: (0, 0, i)),
        out_shape=jax.ShapeDtypeStruct((DECAY_TERMS, rows, s), jnp.bfloat16),
        scratch_shapes=[pltpu.VMEM((rows, 1), jnp.float32)],
        compiler_params=_params(("arbitrary",)),
        name="decay_cumsum",
    )(f_t, bias_col, tri)


def _col_reduce(op, x):
    reduce = {jnp.maximum: jnp.max, jnp.minimum: jnp.min, jnp.add: jnp.sum}[op]
    return reduce(x, axis=0, keepdims=True)


def _flash_init(m_ref, acc_ref):
    m_ref[...] = jnp.full(m_ref.shape, NEG, jnp.float32)
    acc_ref[...] = jnp.zeros(acc_ref.shape, jnp.float32)


def _probabilities(s, m_new):
    return jnp.exp2((s - m_new).astype(jnp.bfloat16))


def _flash_result(acc_ref):
    return acc_ref[0:HEAD_DIM, :] / acc_ref[SUM_ROW:SUM_ROW + 1, :]


def _flash_step(kaug, vt, get_q, width, m_ref, acc_ref, extra=None):
    w = width // N_SPLIT
    strips = [(i * w, (i + 1) * w) for i in range(N_SPLIT)]
    scores = [jnp.dot(kaug, get_q(lo, hi), preferred_element_type=jnp.float32) for lo, hi in strips]
    for (lo, hi), s in zip(strips, scores):
        if extra is not None:
            s = extra(s, lo, hi)
        m_prev = m_ref[:, lo:hi]
        m_new = jnp.maximum(m_prev, _col_reduce(jnp.maximum, s))
        alpha = jnp.exp2(m_prev - m_new)
        acc_ref[:, lo:hi] = alpha * acc_ref[:, lo:hi] + jnp.dot(
            vt, _probabilities(s, m_new), preferred_element_type=jnp.float32)
        m_ref[:, lo:hi] = m_new


def _flash_far_chunks(n, load_keys, load_values, get_q_for, width, m_ref, acc_ref, s_ref, mx_ref, p_ref):
    w = width // N_SPLIT
    strips = [(i * w, (i + 1) * w) for i in range(N_SPLIT)]
    base = n % 2
    pairs = n // 2

    def scores_into(c, slot):
        kaug = load_keys(c)
        get_q = get_q_for(c)
        for lo, hi in strips:
            s = jnp.dot(kaug, get_q(lo, hi), preferred_element_type=jnp.float32)
            s_ref[slot, :, lo:hi] = s
            mx_ref[slot, :, lo:hi] = _col_reduce(jnp.maximum, s)

    def value_product(c, slot):
        vt = load_values(c)
        return jnp.concatenate([jnp.dot(vt, p_ref[slot, :, lo:hi], preferred_element_type=jnp.float32)
                                for lo, hi in strips], axis=1)

    @pl.when(base == 1)
    def _():
        _flash_step(load_keys(0), load_values(0), get_q_for(0), width, m_ref, acc_ref)

    @pl.when(pairs > 0)
    def _():
        scores_into(base, 0)
        p_ref[1] = jnp.zeros(p_ref.shape[1:], p_ref.dtype)

        def pair(j, carry):
            for cur in (0, 1):
                c = base + 2 * j + cur
                nxt = 1 - cur
                pv = value_product(jnp.maximum(c - 1, base), nxt)
                scores_into(jnp.minimum(c + 1, n - 1), nxt)
                alphas = []
                for lo, hi in strips:
                    m_prev = m_ref[:, lo:hi]
                    m_new = jnp.maximum(m_prev, mx_ref[cur, :, lo:hi])
                    m_ref[:, lo:hi] = m_new
                    for r in range(0, s_ref.shape[1], ROW_BLOCK):
                        p_ref[cur, r:r + ROW_BLOCK, lo:hi] = _probabilities(
                            s_ref[cur, r:r + ROW_BLOCK, lo:hi], m_new)
                    alphas.append(jnp.exp2(m_prev - m_new))
                acc_ref[...] = jnp.concatenate(alphas, axis=1) * (acc_ref[...] + pv)
            return carry

        lax.fori_loop(0, pairs, pair, 0)
        acc_ref[...] = acc_ref[...] + value_product(n - 1, 1)


def _group_queries_t(q_ref):
    return jnp.concatenate([q_ref[:, h * HEAD_DIM:(h + 1) * HEAD_DIM].T for h in range(NSA_HPG)], axis=1)


def _fill_values_t(v_ref, vt_ref):
    n_pieces, _, piece = vt_ref.shape
    pad = lax.broadcasted_iota(jnp.int32, (ACC_ROWS - HEAD_DIM, piece), 0)
    ones_rows = jnp.where(pad == 0, 1.0, 0.0).astype(vt_ref.dtype)

    def fill(p, carry):
        vt_ref[p, 0:HEAD_DIM, :] = v_ref[pl.ds(pl.multiple_of(p * piece, piece), piece), :].T
        vt_ref[p, HEAD_DIM:, :] = ones_rows
        return carry

    lax.fori_loop(0, n_pieces, fill, 0)


def _branch_gate(gates_t, head, branch):
    row = head * 3 + branch
    return gates_t[row:row + 1, :]


def _values_t(vt_ref, k0, n_keys):
    p0 = k0 // LANES
    return jnp.concatenate([vt_ref[p0 + j] for j in range(n_keys // LANES)], axis=1)


def _compressed_branch(n_rows, i0, q4t, kc_ref, vct_ref, band_ref, ovt_ref, sc_ref, oct_ref, imp_ref):
    sc_ref[0:n_rows, :] = jnp.dot(kc_ref[0:n_rows, :], q4t, preferred_element_type=jnp.float32)
    sc_ref[pl.ds(i0, BAND_ROWS), :] = sc_ref[pl.ds(i0, BAND_ROWS), :] + band_ref[...]
    row = lax.broadcasted_iota(jnp.int32, (n_rows, ROWS), 0)
    sc = jnp.where(row < i0 + BAND_ROWS, sc_ref[0:n_rows, :], NEG)
    m = _col_reduce(jnp.maximum, sc)
    p = jnp.exp2(sc - m)
    l = _col_reduce(jnp.add, p)
    pn = p * jnp.where(m > 0.5 * NEG, 1.0 / l, 0.0)
    oct_ref[...] = jnp.dot(vct_ref[:, 0:n_rows], pn.astype(jnp.bfloat16), preferred_element_type=jnp.float32)
    psum = pn[:, 0:Q_TILE]
    for h in range(1, NSA_HPG):
        psum = psum + pn[:, h * Q_TILE:(h + 1) * Q_TILE]
    imp_ref[...] = jnp.dot(ovt_ref[:, 0:n_rows], psum.astype(jnp.bfloat16), preferred_element_type=jnp.float32)


def _cmp_win_kernel(q_ref, kc_ref, vct_ref, band_ref, ovt_ref, kw_ref, vw_ref, near_ref, gt_ref,
                    ycw_ref, pen_ref, sc_ref, sw_ref, oct_ref, imp_ref, vwt_ref):
    qb = pl.program_id(2)
    n_cmp = kc_ref.shape[0]
    n_slc = ovt_ref.shape[0]
    q4t = _group_queries_t(q_ref)
    pl.when(qb == 0)(functools.partial(_fill_values_t, vw_ref, vwt_ref))

    i0 = pl.multiple_of(jnp.maximum(8 * qb - 16, 0), 8)
    n_tiers = CMP_TIERS if n_cmp % (CMP_TIERS * LANES) == 0 else 1
    step = n_cmp // n_tiers
    for tier in range(1, n_tiers + 1):
        in_tier = (i0 + BAND_ROWS <= tier * step) & (i0 + BAND_ROWS > (tier - 1) * step)
        pl.when(in_tier)(functools.partial(_compressed_branch, tier * step, i0, q4t, kc_ref, vct_ref,
                                           band_ref, ovt_ref, sc_ref, oct_ref, imp_ref))
    oct_ = oct_ref[...]
    imp = imp_ref[...]

    ji = lax.broadcasted_iota(jnp.int32, (n_slc, Q_TILE), 0)
    jf = ji.astype(jnp.float32)
    t = qb * Q_TILE + lax.broadcasted_iota(jnp.int32, (n_slc, Q_TILE), 1)
    cur = t // SLC_BLOCK
    forced = (ji == 0) | (ji == cur) | (ji == cur - 1)
    score = jnp.where(ji <= cur, jnp.where(forced, FORCED_SCORE, imp), -1.0)
    pen_t = jnp.full((n_slc, Q_TILE), NEG, jnp.float32)
    for _ in range(min(SLC_TOPK, n_slc)):
        mx = _col_reduce(jnp.maximum, score)
        idx = _col_reduce(jnp.minimum, jnp.where(score == mx, jf, float(n_slc)))
        pick = jf == idx
        pen_t = jnp.where(pick, 0.0, pen_t)
        score = jnp.where(pick, -2.0, score)
    pad = pen_ref.shape[0] - n_slc
    if pad:
        pen_t = jnp.concatenate([pen_t, jnp.full((pad, Q_TILE), NEG, jnp.float32)], axis=0)
    pen_ref[...] = pen_t.astype(pen_ref.dtype)

    t0 = qb * Q_TILE
    ws = pl.multiple_of(jnp.maximum(t0 - WINDOW, 0), Q_TILE)
    ns = pl.multiple_of(jnp.maximum(t0 - Q_TILE, 0), Q_TILE)
    sw_ref[...] = jnp.dot(kw_ref[pl.ds(ws, WIN_KEYS), :], q4t, preferred_element_type=jnp.float32)
    off = pl.multiple_of(ns - ws, Q_TILE)
    sw_ref[pl.ds(off, NEAR_KEYS), :] = sw_ref[pl.ds(off, NEAR_KEYS), :] + near_ref[...]
    key = ws + lax.broadcasted_iota(jnp.int32, (WIN_KEYS, ROWS), 0)
    qry = t0 + (lax.broadcasted_iota(jnp.int32, (WIN_KEYS, ROWS), 1) & (Q_TILE - 1))
    sw = jnp.where((key <= qry) & (key > qry - WINDOW), sw_ref[...], NEG)
    mw = _col_reduce(jnp.maximum, sw)
    ow_sum = jnp.dot(_values_t(vwt_ref, ws, WIN_KEYS), _probabilities(sw, mw),
                     preferred_element_type=jnp.float32)
    owt = ow_sum[0:HEAD_DIM, :] / ow_sum[SUM_ROW:SUM_ROW + 1, :]

    gates_t = jax.nn.sigmoid(gt_ref[...])
    for h in range(NSA_HPG):
        cols = slice(h * Q_TILE, (h + 1) * Q_TILE)
        mixed = _branch_gate(gates_t, h, 0) * oct_[:, cols] + _branch_gate(gates_t, h, 2) * owt[:, cols]
        ycw_ref[:, h * HEAD_DIM:(h + 1) * HEAD_DIM] = mixed.T


def _cmp_win(kc, vct, band, ovt, qkv, near, gates_t):
    b, s, _ = qkv.shape
    n_cmp = kc.shape[2]
    n_slc = ovt.shape[0]
    nq = s // Q_TILE
    pen_w = -(-n_slc // LANES) * LANES
    n_band = band.shape[0] - 1
    return pl.pallas_call(
        _cmp_win_kernel,
        grid=(b, NSA_GROUPS, nq),
        in_specs=[
            pl.BlockSpec((None, Q_TILE, NSA_HPG * HEAD_DIM), lambda i, g, q: (i, q, g)),
            pl.BlockSpec((None, None, n_cmp, HEAD_DIM), lambda i, g, q: (i, g, 0, 0)),
            pl.BlockSpec((None, None, HEAD_DIM, n_cmp), lambda i, g, q: (i, g, 0, 0)),
            pl.BlockSpec((None, None, BAND_ROWS, ROWS), lambda i, g, q: (jnp.minimum(q, n_band), g, 0, 0)),
            pl.BlockSpec((n_slc, n_cmp), lambda i, g, q: (0, 0)),
            pl.BlockSpec((None, s, HEAD_DIM), lambda i, g, q: (i, 0, CB_KW + g)),
            pl.BlockSpec((None, s, HEAD_DIM), lambda i, g, q: (i, 0, CB_VW + g)),
            pl.BlockSpec((None, None, NEAR_KEYS, ROWS), lambda i, g, q: (jnp.minimum(q, 1), g, 0, 0)),
            pl.BlockSpec((None, None, GATE_ROWS, Q_TILE), lambda i, g, q: (i, g, 0, q)),
        ],
        out_specs=[
            pl.BlockSpec((None, Q_TILE, NSA_HPG * HEAD_DIM), lambda i, g, q: (i, q, g)),
            pl.BlockSpec((None, None, pen_w, Q_TILE), lambda i, g, q: (i, g, 0, q)),
        ],
        out_shape=[
            jax.ShapeDtypeStruct((b, s, NSA_Q_DIM), jnp.float32),
            jax.ShapeDtypeStruct((b, NSA_GROUPS, pen_w, s), jnp.bfloat16),
        ],
        scratch_shapes=[pltpu.VMEM((n_cmp, ROWS), jnp.float32), pltpu.VMEM((WIN_KEYS, ROWS), jnp.float32),
                        pltpu.VMEM((HEAD_DIM, ROWS), jnp.float32), pltpu.VMEM((n_slc, Q_TILE), jnp.float32),
                        pltpu.VMEM((s // LANES, ACC_ROWS, LANES), jnp.bfloat16)],
        compiler_params=_params(("parallel", "parallel", "arbitrary")),
        name="nsa_cmp_topk_win",
    )(qkv, kc, vct, band, ovt, qkv, qkv, near, gates_t)


def _range_penalty(pen, first_blk, lo_ok, hi_ok):
    n_half = pen.shape[0] // LANES
    c = lax.broadcasted_iota(jnp.int32, (LANES, pen.shape[1]), 0)
    out = pen[0:LANES, :]
    blk = c
    for hf in range(1, n_half):
        in_lower = (c + (hf - 1) * LANES >= first_blk) & (first_blk < hf * LANES)
        out = jnp.where(in_lower, out, pen[hf * LANES:(hf + 1) * LANES, :])
        blk = jnp.where(in_lower, blk, c + hf * LANES)
    return jnp.where((blk >= lo_ok) & (blk < hi_ok), out, jnp.asarray(NEG, out.dtype))


def _sel_kernel(q_ref, pen_ref, kaug_ref, v_ref, near_ref, ycw_ref, gt_ref, y_ref,
                qaug_ref, m_ref, acc_ref, s_ref, mx_ref, p_ref, vt_ref):
    qb = pl.program_id(2)
    n_half = pen_ref.shape[0] // LANES
    n_blocks = pen_ref.shape[0]
    chunks_per_half = LANES * SLC_BLOCK // KEY_CHUNK
    blocks_per_chunk = KEY_CHUNK // SLC_BLOCK

    t0 = qb * Q_TILE
    near_end = t0 - Q_TILE
    nf = jnp.maximum(near_end // KEY_CHUNK, 0)
    ms = pl.multiple_of(jnp.maximum(near_end - KEY_CHUNK, 0), Q_TILE)
    ns = pl.multiple_of(jnp.maximum(near_end, 0), Q_TILE)

    pl.when(qb == 0)(functools.partial(_fill_values_t, v_ref, vt_ref))
    q4t = _group_queries_t(q_ref)
    pen = pen_ref[...]
    operands = [pen[hf * LANES:(hf + 1) * LANES, :] for hf in range(n_half)]
    operands.append(_range_penalty(pen, ms // SLC_BLOCK, nf * blocks_per_chunk, near_end // SLC_BLOCK))
    operands.append(_range_penalty(pen, ns // SLC_BLOCK, 0, n_blocks))
    for idx, channels in enumerate(operands):
        qaug_ref[idx, 0:HEAD_DIM, :] = q4t
        qaug_ref[idx, HEAD_DIM:, :] = jnp.concatenate([channels] * NSA_HPG, axis=1)

    _flash_init(m_ref, acc_ref)

    def key_start(c):
        return pl.multiple_of(jnp.where(c < nf, c * KEY_CHUNK, ms), Q_TILE)

    def get_q_for(c):
        operand = jnp.where(c < nf, c // chunks_per_half, n_half)
        return lambda lo, hi: qaug_ref[operand, :, lo:hi]

    _flash_far_chunks(nf + 1, lambda c: kaug_ref[pl.ds(key_start(c), KEY_CHUNK), :],
                      lambda c: _values_t(vt_ref, key_start(c), KEY_CHUNK), get_q_for,
                      ROWS, m_ref, acc_ref, s_ref, mx_ref, p_ref)

    _flash_step(kaug_ref[pl.ds(ns, NEAR_KEYS), :], _values_t(vt_ref, ns, NEAR_KEYS),
                lambda lo, hi: qaug_ref[n_half + 1, :, lo:hi], ROWS, m_ref, acc_ref,
                extra=lambda s, lo, hi: s + near_ref[:, lo:hi])

    o = _flash_result(acc_ref)
    gates_t = jax.nn.sigmoid(gt_ref[...])
    for h in range(NSA_HPG):
        sel_h = (_branch_gate(gates_t, h, 1) * o[:, h * Q_TILE:(h + 1) * Q_TILE]).T
        y_ref[:, h * HEAD_DIM:(h + 1) * HEAD_DIM] = (
            ycw_ref[:, h * HEAD_DIM:(h + 1) * HEAD_DIM] + sel_h).astype(y_ref.dtype)


def _sel(qkv, pen, kaug, near, ycw, gates_t):
    b, s, _ = qkv.shape
    nq = s // Q_TILE
    pen_w = pen.shape[2]
    return pl.pallas_call(
        _sel_kernel,
        grid=(b, NSA_GROUPS, nq),
        in_specs=[
            pl.BlockSpec((None, Q_TILE, NSA_HPG * HEAD_DIM), lambda i, g, q: (i, q, g)),
            pl.BlockSpec((None, None, pen_w, Q_TILE), lambda i, g, q: (i, g, 0, q)),
            pl.BlockSpec((None, None, s, 2 * HEAD_DIM), lambda i, g, q: (i, g, 0, 0)),
            pl.BlockSpec((None, s, HEAD_DIM), lambda i, g, q: (i, 0, CB_VS + g)),
            pl.BlockSpec((None, None, NEAR_KEYS, ROWS), lambda i, g, q: (jnp.minimum(q, 1), g, 0, 0)),
            pl.BlockSpec((None, Q_TILE, NSA_HPG * HEAD_DIM), lambda i, g, q: (i, q, g)),
            pl.BlockSpec((None, None, GATE_ROWS, Q_TILE), lambda i, g, q: (i, g, 0, q)),
        ],
        out_specs=pl.BlockSpec((None, Q_TILE, NSA_HPG * HEAD_DIM), lambda i, g, q: (i, q, g)),
        out_shape=jax.ShapeDtypeStruct((b, s, NSA_Q_DIM), jnp.bfloat16),
        scratch_shapes=[
            pltpu.VMEM((pen_w // LANES + 2, 2 * HEAD_DIM, ROWS), jnp.bfloat16),
            pltpu.VMEM((1, ROWS), jnp.float32),
            pltpu.VMEM((ACC_ROWS, ROWS), jnp.float32),
            pltpu.VMEM((2, KEY_CHUNK, ROWS), jnp.float32),
            pltpu.VMEM((2, 1, ROWS), jnp.float32),
            pltpu.VMEM((2, KEY_CHUNK, ROWS), jnp.bfloat16),
            pltpu.VMEM((s // LANES, ACC_ROWS, LANES), jnp.bfloat16),
        ],
        compiler_params=_params(("parallel", "parallel", "arbitrary")),
        name="nsa_selected",
    )(qkv, pen, kaug, qkv, near, ycw, gates_t)


def _fox_kernel(nce_ref, q_ref, k_ref, dec_ref, v_ref, o_ref,
                qaug_ref, m_ref, acc_ref, s_ref, mx_ref, p_ref, knorm_ref, vt_ref):
    qi = pl.program_id(2)
    bh = pl.program_id(0) * pl.num_programs(1) + pl.program_id(1)
    tq = q_ref.shape[0]
    row = lax.broadcasted_iota(jnp.int32, (HEAD_DIM, tq), 0)
    q_t = q_ref[...].T
    qaug_ref[0:HEAD_DIM, :] = q_t
    qaug_ref[HEAD_DIM:, :] = jnp.where(row < DECAY_TERMS, -1.0, 0.0).astype(qaug_ref.dtype)
    _flash_init(m_ref, acc_ref)

    @pl.when(qi == 0)
    def _():
        pad = lax.broadcasted_iota(jnp.int32, (ACC_ROWS - HEAD_DIM, tq), 0)
        ones_rows = jnp.where(pad == 0, 1.0, 0.0).astype(vt_ref.dtype)

        def chunk_setup(c, best):
            rows = pl.ds(pl.multiple_of(c * tq, tq), tq)
            vt_ref[c, 0:HEAD_DIM, :] = v_ref[rows, :].T
            vt_ref[c, HEAD_DIM:, :] = ones_rows
            kf = k_ref[rows, :].astype(jnp.float32)
            return jnp.maximum(best, jnp.max(jnp.sum(kf * kf, axis=1, keepdims=True)))

        knorm_ref[0] = jnp.sqrt(lax.fori_loop(0, k_ref.shape[0] // tq, chunk_setup, jnp.float32(0.0)))

    qf = q_t.astype(jnp.float32)
    q_norm = jnp.sqrt(jnp.max(jnp.sum(qf * qf, axis=0, keepdims=True)))

    def get_q(lo, hi):
        return qaug_ref[:, lo:hi]

    def load_keys(c):
        k0 = pl.multiple_of(c * tq, tq)
        return jnp.concatenate([k_ref[pl.ds(k0, tq), :], dec_ref[pl.ds(k0, tq), :]], axis=1)

    def causal(s, lo, hi):
        key = lax.broadcasted_iota(jnp.int32, s.shape, 0)
        qry = lo + lax.broadcasted_iota(jnp.int32, s.shape, 1)
        return jnp.where(key <= qry, s, NEG)

    _flash_step(load_keys(qi), vt_ref[qi], get_q, tq, m_ref, acc_ref, extra=causal)

    m_low = jnp.min(m_ref[...])
    reach = q_norm * knorm_ref[0] + SKIP_MARGIN_LOG2

    def count(c, n_skip):
        return n_skip + jnp.where(reach + nce_ref[bh, c] < m_low, 1, 0)

    first = lax.fori_loop(0, qi, count, jnp.int32(0))
    _flash_far_chunks(qi - first, lambda c: load_keys(c + first), lambda c: vt_ref[c + first],
                      lambda c: get_q, tq, m_ref, acc_ref, s_ref, mx_ref, p_ref)
    o_ref[...] = _flash_result(acc_ref).T.astype(o_ref.dtype)


def _fox(neg_cum_end, qkv, dec, tq):
    b, s, _ = qkv.shape
    return pl.pallas_call(
        _fox_kernel,
        grid=(b, FOX_HEADS, s // tq),
        in_specs=[
            pl.BlockSpec(memory_space=pltpu.SMEM),
            pl.BlockSpec((None, tq, HEAD_DIM), lambda i, h, q: (i, q, CB_QF + h)),
            pl.BlockSpec((None, s, HEAD_DIM), lambda i, h, q: (i, 0, CB_KF + h)),
            pl.BlockSpec((None, None, s, LANES), lambda i, h, q: (i, h, 0, 0)),
            pl.BlockSpec((None, s, HEAD_DIM), lambda i, h, q: (i, 0, CB_VF + h)),
        ],
        out_specs=pl.BlockSpec((None, tq, HEAD_DIM), lambda i, h, q: (i, q, h)),
        out_shape=jax.ShapeDtypeStruct((b, s, FOX_DIM), jnp.bfloat16),
        scratch_shapes=[
            pltpu.VMEM((2 * HEAD_DIM, tq), jnp.bfloat16),
            pltpu.VMEM((1, tq), jnp.float32),
            pltpu.VMEM((ACC_ROWS, tq), jnp.float32),
            pltpu.VMEM((2, tq, tq), jnp.float32),
            pltpu.VMEM((2, 1, tq), jnp.float32),
            pltpu.VMEM((2, tq, tq), jnp.bfloat16),
            pltpu.SMEM((1,), jnp.float32),
            pltpu.VMEM((s // tq, ACC_ROWS, tq), jnp.bfloat16),
        ],
        compiler_params=_params(("parallel", "parallel", "arbitrary")),
        name="fox_attention",
    )(neg_cum_end, qkv, qkv, dec, qkv)


def _merge_kernel(x_ref, g_ref, wa_ref, wb_ref, pa_ref, pb_ref, wo_ref, ya_ref, yb_ref, o_ref,
                  h_ref, acc_ref):
    j = pl.program_id(1)

    @pl.when(j == 0)
    def _():
        h_ref[...] = _norm_rows(x_ref[...], g_ref[...]).astype(h_ref.dtype)
        acc_ref[...] = jnp.zeros_like(acc_ref)

    h = h_ref[...]
    ga = jax.nn.sigmoid(jnp.dot(h, wa_ref[...], preferred_element_type=jnp.float32))
    gb = jax.nn.sigmoid(jnp.dot(h, wb_ref[...], preferred_element_type=jnp.float32))
    a = jnp.dot(ya_ref[...], pa_ref[...], preferred_element_type=jnp.float32)
    bb = jnp.dot(yb_ref[...], pb_ref[...], preferred_element_type=jnp.float32)
    merged = (ga * a + gb * bb).astype(jnp.bfloat16)
    acc_ref[...] += jnp.dot(merged, wo_ref[...], preferred_element_type=jnp.float32)

    @pl.when(j == pl.num_programs(1) - 1)
    def _():
        o_ref[...] = x_ref[...] + acc_ref[...]


def _merge(x2d, g, w_ma, w_mb, p_a, p_b, w_out, y_a, y_b, tm, tn):
    t, d = x2d.shape
    ka = p_a.shape[0]
    kb = p_b.shape[0]
    return pl.pallas_call(
        _merge_kernel,
        grid=(t // tm, d // tn),
        in_specs=[
            pl.BlockSpec((tm, d), lambda i, j: (i, 0)),
            pl.BlockSpec((1, d), lambda i, j: (0, 0)),
            pl.BlockSpec((d, tn), lambda i, j: (0, j)),
            pl.BlockSpec((d, tn), lambda i, j: (0, j)),
            pl.BlockSpec((ka, tn), lambda i, j: (0, j)),
            pl.BlockSpec((kb, tn), lambda i, j: (0, j)),
            pl.BlockSpec((tn, d), lambda i, j: (j, 0)),
            pl.BlockSpec((tm, ka), lambda i, j: (i, 0)),
            pl.BlockSpec((tm, kb), lambda i, j: (i, 0)),
        ],
        out_specs=pl.BlockSpec((tm, d), lambda i, j: (i, 0)),
        out_shape=jax.ShapeDtypeStruct((t, d), jnp.float32),
        scratch_shapes=[pltpu.VMEM((tm, d), jnp.bfloat16), pltpu.VMEM((tm, d), jnp.float32)],
        compiler_params=_params(("parallel", "arbitrary")),
        name="merge_out_proj",
    )(x2d, g.reshape(1, d), w_ma, w_mb, p_a, p_b, w_out, y_a, y_b)


def _ffn_kernel(x_ref, xh_ref, g_ref, wu_ref, wv_ref, cw_ref, cb_ref, wd_ref, gf_ref, o_ref,
                h_ref, u_ref, acc_ref, *, seq, tm):
    i = pl.program_id(0)
    j = pl.program_id(1)
    halo = BF16_SUBLANES

    @pl.when(j == 0)
    def _():
        g = g_ref[...]
        keep = jnp.where((i * tm) % seq == 0, 0.0, 1.0)
        h_ref[0:halo, :] = (_norm_rows(xh_ref[...], g) * keep).astype(h_ref.dtype)
        h_ref[halo:, :] = _norm_rows(x_ref[...], g).astype(h_ref.dtype)
        acc_ref[...] = jnp.zeros_like(acc_ref)

    u_ref[...] = jnp.dot(h_ref[...], wu_ref[...], preferred_element_type=jnp.float32)
    v = jnp.dot(h_ref[halo:, :], wv_ref[...], preferred_element_type=jnp.float32)
    conv = cb_ref[...]
    for k in range(CONV_WIDTH):
        conv = conv + cw_ref[k:k + 1, :] * u_ref[pl.ds(halo - (CONV_WIDTH - 1) + k, tm), :]
    act = (jax.nn.gelu(conv) * v).astype(jnp.bfloat16)
    acc_ref[...] += jnp.dot(act, wd_ref[...], preferred_element_type=jnp.float32)

    @pl.when(j == pl.num_programs(1) - 1)
    def _():
        o_ref[...] = _norm_rows(x_ref[...] + acc_ref[...], gf_ref[...])


def _ffn(x2d, g, w_up, conv_w, conv_b, w_down, g_final, seq, tm, tn):
    t, d = x2d.shape
    d_ff = w_down.shape[0]
    nt = d_ff // tn
    halo = BF16_SUBLANES
    per = tm // halo
    return pl.pallas_call(
        functools.partial(_ffn_kernel, seq=seq, tm=tm),
        grid=(t // tm, nt),
        in_specs=[
            pl.BlockSpec((tm, d), lambda i, j: (i, 0)),
            pl.BlockSpec((halo, d), lambda i, j: (jnp.maximum(i * per - 1, 0), 0)),
            pl.BlockSpec((1, d), lambda i, j: (0, 0)),
            pl.BlockSpec((d, tn), lambda i, j: (0, j)),
            pl.BlockSpec((d, tn), lambda i, j: (0, nt + j)),
            pl.BlockSpec((CONV_WIDTH, tn), lambda i, j: (0, j)),
            pl.BlockSpec((1, tn), lambda i, j: (0, j)),
            pl.BlockSpec((tn, d), lambda i, j: (j, 0)),
            pl.BlockSpec((1, d), lambda i, j: (0, 0)),
        ],
        out_specs=pl.BlockSpec((tm, d), lambda i, j: (i, 0)),
        out_shape=jax.ShapeDtypeStruct((t, d), jnp.float32),
        scratch_shapes=[
            pltpu.VMEM((tm + halo, d), jnp.bfloat16),
            pltpu.VMEM((tm + halo, tn), jnp.float32),
            pltpu.VMEM((tm, d), jnp.float32),
        ],
        compiler_params=_params(("parallel", "arbitrary")),
        name="ffn_final_norm",
    )(x2d, x2d, g.reshape(1, d), w_up, w_up, conv_w, conv_b.reshape(1, d_ff), w_down, g_final.reshape(1, d))


def _t5_bucket_np(dist):
    n = np.maximum(dist, 0)
    max_exact = REL_BUCKETS // 2
    nf = np.maximum(n, 1).astype(np.float32)
    large = max_exact + (np.log(nf / np.float32(max_exact)) / np.float32(math.log(REL_MAX_DIST / max_exact))
                         * np.float32(REL_BUCKETS - max_exact)).astype(np.int32)
    return np.where(n < max_exact, n, np.minimum(large, REL_BUCKETS - 1)).astype(np.int32)


def _bias_by_distance(rel_table, far_shift):
    vals = rel_table[jnp.asarray(_t5_bucket_np(np.arange(REL_MAX_DIST + 1)))]
    if far_shift:
        vals = vals - rel_table[REL_BUCKETS - 1]
    return vals * LOG2E


def _near_bias(rel_table):
    fd = _bias_by_distance(rel_table, True)
    h = fd.shape[1]
    lo = NEAR_KEYS - 1
    vec = jnp.concatenate([jnp.full((lo, h), NEG, fd.dtype), fd[:REL_MAX_DIST],
                           jnp.zeros((NEAR_KEYS - REL_MAX_DIST, h), fd.dtype)], axis=0)
    def toeplitz(start):
        period = Q_TILE + NEAR_KEYS - 1
        window = jnp.concatenate([vec[start:start + Q_TILE], vec[start - (NEAR_KEYS - 1):start]], axis=0)
        flat = jnp.tile(window, (NEAR_KEYS, 1))[:NEAR_KEYS * (period - 1)]
        return flat.reshape(NEAR_KEYS, period - 1, h)[:, :Q_TILE]

    tiles = jnp.stack([toeplitz(lo + a) for a in (0, Q_TILE)])
    tiles = tiles.reshape(2, NEAR_KEYS, Q_TILE, NSA_GROUPS, NSA_HPG).transpose(0, 3, 1, 4, 2)
    return tiles.reshape(2, NSA_GROUPS, NEAR_KEYS, ROWS)


def _band_bias(rel_table):
    fd = _bias_by_distance(rel_table, True)
    tl = np.arange(Q_TILE)[None, :]
    r = np.arange(BAND_ROWS)[:, None]
    dist = np.stack([tl - CMP_STRIDE * (r - off) - (CMP_BLOCK - 1) for off in (0, 8, 16)])
    vals = fd[jnp.asarray(np.clip(dist, 0, REL_MAX_DIST))]
    vals = jnp.where(jnp.asarray(dist >= 0)[..., None], vals, NEG)
    v, rr, q, _ = vals.shape
    return vals.transpose(0, 1, 3, 2).reshape(v, rr, NSA_GROUPS, NSA_HPG * q).transpose(0, 2, 1, 3)


def _overlap_t(n_cmp_pad, n_slc):
    i = np.arange(n_cmp_pad)[None, :]
    jj = np.arange(n_slc)[:, None]
    c_start = i * CMP_STRIDE
    ov = (c_start < jj * SLC_BLOCK + SLC_BLOCK) & (c_start + CMP_BLOCK - 1 >= jj * SLC_BLOCK)
    ov = ov & (i < n_cmp_pad - 1)
    return jnp.asarray(ov.astype(np.float32), jnp.bfloat16)


def _block_onehot(seq):
    blk = (np.arange(seq) // SLC_BLOCK) % LANES
    return jnp.asarray((blk[:, None] == np.arange(LANES)[None, :]).astype(np.float32), jnp.bfloat16)


def _pick_tile(n, pref):
    return pref if n % pref == 0 else n


def kernel(x, attn_norm_g, w_in, cmp_pos_k, cmp_w1_k, cmp_w2_k, cmp_pos_v, cmp_w1_v, cmp_w2_v,
           rel_bias_table, fox_forget_bias, w_branch_nsa, w_branch_fox, w_out,
           ffn_norm_g, w_up, conv_w, conv_b, w_down, final_norm_g):
    assert w_in.shape[0] == 1, "the final norm is fused into the single layer's FFN kernel"
    bsz, seq, d = x.shape
    t = bsz * seq
    bf = jnp.bfloat16
    scale = HEAD_DIM ** -0.5 * LOG2E
    x2d = x.reshape(t, d)
    w_in = w_in[0]

    o = np.cumsum([0, NSA_Q_DIM] + [NSA_KV_DIM] * 6 + [3 * NSA_HEADS, FOX_DIM, FOX_DIM, FOX_DIM, FOX_HEADS, d, d])
    w_qkv = jnp.concatenate([w_in[:, o[0]:o[1]] * scale, w_in[:, o[1]:o[7]],
                             w_in[:, o[8]:o[9]] * scale, w_in[:, o[9]:o[11]]], axis=1).astype(bf)
    gate_cols = []
    for grp in range(NSA_GROUPS):
        gate_cols += [w_in[:, o[7] + grp * GATES_PER_GROUP:o[7] + (grp + 1) * GATES_PER_GROUP],
                      jnp.zeros((d, LANES - GATES_PER_GROUP), w_in.dtype)]
    gate_cols += [w_in[:, o[11]:o[12]], jnp.zeros((d, IN_PROJ_TILE - N_GATE + LANES - FOX_HEADS), w_in.dtype)]
    w_gate = jnp.concatenate(gate_cols, axis=1).astype(bf)
    w_ma = w_in[:, o[12]:o[13]].astype(bf)
    w_mb = w_in[:, o[13]:o[14]].astype(bf)

    qkv, gates = _in_proj(x2d, attn_norm_g[0], jnp.concatenate([w_qkv, w_gate], axis=1),
                          _pick_tile(t, 1024), IN_PROJ_TILE)
    qkv = qkv.reshape(bsz, seq, N_QKV)
    gates = gates.reshape(bsz, seq, IN_PROJ_TILE)

    def heads_major(cb, heads):
        return qkv[:, :, cb * HEAD_DIM:(cb + heads) * HEAD_DIM].reshape(
            bsz, seq, heads, HEAD_DIM).transpose(0, 2, 1, 3)

    n_ch = seq // CMP_STRIDE

    def chunked(cb):
        sl = qkv[:, :, cb * HEAD_DIM:(cb + NSA_GROUPS) * HEAD_DIM]
        sl = sl.reshape(bsz, n_ch, CMP_STRIDE, NSA_GROUPS, HEAD_DIM).transpose(0, 3, 1, 2, 4)
        return sl.reshape(bsz, NSA_GROUPS, n_ch, CMP_STRIDE * HEAD_DIM)

    def posflat(pos):
        return jnp.broadcast_to(pos.reshape(1, CMP_BLOCK * HEAD_DIM), (8, CMP_BLOCK * HEAD_DIM)).astype(bf)

    kc = _compress(chunked(CB_KC), cmp_w1_k[0].astype(bf), posflat(cmp_pos_k[0]), cmp_w2_k[0].astype(bf))
    vc = _compress(chunked(CB_VC), cmp_w1_v[0].astype(bf), posflat(cmp_pos_v[0]), cmp_w2_v[0].astype(bf))
    vct = vc.transpose(0, 1, 3, 2)

    f_t = gates[:, :, NSA_GROUPS * LANES:NSA_GROUPS * LANES + FOX_HEADS]
    f_t = f_t.transpose(0, 2, 1).reshape(bsz * FOX_HEADS, seq)
    bias_col = jnp.tile(fox_forget_bias[0].astype(jnp.float32), bsz).reshape(bsz * FOX_HEADS, 1)
    terms = _decay_cumsum(f_t, bias_col, _pick_tile(seq, 2048))
    dec = jnp.pad(terms.transpose(1, 2, 0), ((0, 0), (0, 0), (0, LANES - DECAY_TERMS)))
    dec = dec.reshape(bsz, FOX_HEADS, seq, LANES)

    n_slc = seq // SLC_BLOCK
    near = _near_bias(rel_bias_table)
    gates_t = gates[:, :, :NSA_GROUPS * LANES].reshape(bsz, seq, NSA_GROUPS, LANES)[..., :GATE_ROWS]
    gates_t = gates_t.transpose(0, 2, 3, 1)
    ycw, pen = _cmp_win(kc, vct, _band_bias(rel_bias_table), _overlap_t(n_ch, n_slc), qkv, near, gates_t)
    ks = heads_major(CB_KS, NSA_GROUPS)
    kaug_sel = jnp.concatenate([ks, jnp.broadcast_to(_block_onehot(seq), ks.shape)], axis=-1)
    y_nsa = _sel(qkv, pen, kaug_sel, near, ycw, gates_t)

    fox_tq = _pick_tile(seq, FOX_Q_TILE)
    neg_cum_end = -jnp.sum(terms[:, :, fox_tq - 1::fox_tq].astype(jnp.float32), axis=0)
    y_fox = _fox(neg_cum_end, qkv, dec, fox_tq)

    tm2 = _pick_tile(t, 512)
    x_mid = _merge(x2d, attn_norm_g[0], w_ma, w_mb, w_branch_nsa[0].astype(bf), w_branch_fox[0].astype(bf),
                   w_out[0].astype(bf), y_nsa.reshape(t, NSA_Q_DIM), y_fox.reshape(t, FOX_DIM),
                   tm2, _pick_tile(d, 512))
    d_ff = w_down.shape[1]
    out = _ffn(x_mid, ffn_norm_g[0], w_up[0].astype(bf), conv_w[0], conv_b[0], w_down[0].astype(bf),
               final_norm_g, seq, tm2, _pick_tile(d_ff, 512))
    return out.reshape(bsz, seq, d)
```

```python
import functools
import math

import jax
import jax.numpy as jnp
import numpy as np
from jax import lax
from jax.experimental import pallas as pl
from jax.experimental.pallas import tpu as pltpu

HEAD_DIM = 128
NSA_HEADS = 8
NSA_GROUPS = 2
NSA_HPG = NSA_HEADS // NSA_GROUPS
FOX_HEADS = 8
CMP_BLOCK = 32
CMP_STRIDE = 16
SLC_BLOCK = 64
SLC_TOPK = 16
WINDOW = 512
REL_BUCKETS = 32
REL_MAX_DIST = 128
CONV_WIDTH = 3
EPS = 1e-6
NEG = -1e30
FORCED_SCORE = 1e4
LOG2E = math.log2(math.e)

LANES = 128
BF16_SUBLANES = 16
VMEM_LIMIT = 56 * 1024 * 1024

NSA_Q_DIM = NSA_HEADS * HEAD_DIM
NSA_KV_DIM = NSA_GROUPS * HEAD_DIM
FOX_DIM = FOX_HEADS * HEAD_DIM
N_QKV = NSA_Q_DIM + 6 * NSA_KV_DIM + 3 * FOX_DIM
GATES_PER_GROUP = 3 * NSA_HPG
N_GATE = (NSA_GROUPS + 1) * LANES
IN_PROJ_TILE = 512

CB_KC = 8
CB_VC = 10
CB_KS = 12
CB_VS = 14
CB_KW = 16
CB_VW = 18
CB_QF = 20
CB_KF = 28
CB_VF = 36

Q_TILE = 128
ROWS = NSA_HPG * Q_TILE
KEY_CHUNK = 512
FOX_Q_TILE = 512
N_SPLIT = 2
SKIP_MARGIN_LOG2 = 64.0
ROW_BLOCK = 32
DECAY_TERMS = 3
BAND_ROWS = 24
CMP_TIERS = 4
WIN_KEYS = WINDOW + Q_TILE
NEAR_KEYS = 2 * Q_TILE
GATE_ROWS = 16
SUM_ROW = HEAD_DIM
ACC_ROWS = HEAD_DIM + BF16_SUBLANES


def _params(sem):
    return pltpu.CompilerParams(dimension_semantics=sem, vmem_limit_bytes=VMEM_LIMIT)


def _norm_rows(x, g):
    return (x * lax.rsqrt(jnp.mean(x * x, axis=-1, keepdims=True) + EPS)) * g


def _in_proj_kernel(x_ref, g_ref, w_ref, qkv_ref, gate_ref, h_ref):
    j = pl.program_id(1)
    last = pl.num_programs(1) - 1

    @pl.when(j == 0)
    def _():
        h_ref[...] = _norm_rows(x_ref[...], g_ref[...]).astype(h_ref.dtype)

    y = jnp.dot(h_ref[...], w_ref[...], preferred_element_type=jnp.float32)

    @pl.when(j < last)
    def _():
        qkv_ref[...] = y.astype(qkv_ref.dtype)

    @pl.when(j == last)
    def _():
        gate_ref[...] = y


def _in_proj(x2d, g, w, tm, tn):
    t, d = x2d.shape
    n_tiles = w.shape[1] // tn
    return pl.pallas_call(
        _in_proj_kernel,
        grid=(t // tm, n_tiles),
        in_specs=[
            pl.BlockSpec((tm, d), lambda i, j: (i, 0)),
            pl.BlockSpec((1, d), lambda i, j: (0, 0)),
            pl.BlockSpec((d, tn), lambda i, j: (0, j)),
        ],
        out_specs=[
            pl.BlockSpec((tm, tn), lambda i, j: (i, jnp.minimum(j, n_tiles - 2))),
            pl.BlockSpec((tm, tn), lambda i, j: (i, 0)),
        ],
        out_shape=[
            jax.ShapeDtypeStruct((t, (n_tiles - 1) * tn), jnp.bfloat16),
            jax.ShapeDtypeStruct((t, tn), jnp.float32),
        ],
        scratch_shapes=[pltpu.VMEM((tm, d), jnp.bfloat16)],
        compiler_params=_params(("parallel", "arbitrary")),
        name="norm_in_proj",
    )(x2d, g.reshape(1, d), w)


def _compress_kernel(ch_ref, w1_ref, posf_ref, w2_ref, o_ref):
    half = ch_ref.shape[1]
    ch = ch_ref[...]
    pa = jnp.dot(ch, w1_ref[:half, :], preferred_element_type=jnp.float32)
    pb = jnp.dot(ch, w1_ref[half:, :], preferred_element_type=jnp.float32)
    pos = jnp.dot(posf_ref[...], w1_ref[...], preferred_element_type=jnp.float32)[0:1, :]
    n = pa.shape[0]
    pre = pa + pltpu.roll(pb, n - 1, 0) + pos
    act = jax.nn.gelu(pre)
    o_ref[...] = jnp.dot(act.astype(jnp.bfloat16), w2_ref[...],
                         preferred_element_type=jnp.float32).astype(o_ref.dtype)


def _compress(chunks, w1, posf, w2):
    b, g, n, k = chunks.shape
    return pl.pallas_call(
        _compress_kernel,
        grid=(b, g),
        in_specs=[
            pl.BlockSpec((None, None, n, k), lambda i, j: (i, j, 0, 0)),
            pl.BlockSpec(w1.shape, lambda i, j: (0, 0)),
            pl.BlockSpec(posf.shape, lambda i, j: (0, 0)),
            pl.BlockSpec(w2.shape, lambda i, j: (0, 0)),
        ],
        out_specs=pl.BlockSpec((None, None, n, HEAD_DIM), lambda i, j: (i, j, 0, 0)),
        out_shape=jax.ShapeDtypeStruct((b, g, n, HEAD_DIM), jnp.bfloat16),
        compiler_params=_params(("parallel", "parallel")),
        name="compress_tokens",
    )(chunks, w1, posf, w2)


def _decay_kernel(f_ref, b_ref, tri_ref, o_ref, carry_ref):
    @pl.when(pl.program_id(0) == 0)
    def _():
        carry_ref[...] = jnp.zeros_like(carry_ref)

    x = f_ref[...] + b_ref[...]
    logf = (jnp.minimum(x, 0.0) - jnp.log1p(jnp.exp(-jnp.abs(x)))) * LOG2E
    carry = carry_ref[...]
    for seg in range(f_ref.shape[1] // LANES):
        part = jnp.dot(logf[:, seg * LANES:(seg + 1) * LANES], tri_ref[...],
                       preferred_element_type=jnp.float32, precision=lax.Precision.HIGHEST) + carry
        carry = part[:, LANES - 1:LANES]
        rest = part
        for term in range(DECAY_TERMS):
            piece = rest.astype(o_ref.dtype)
            o_ref[term, :, seg * LANES:(seg + 1) * LANES] = piece
            rest = rest - piece.astype(jnp.float32)
    carry_ref[...] = carry


def _decay_cumsum(f_t, bias_col, width):
    rows, s = f_t.shape
    tri = jnp.asarray(np.triu(np.ones((LANES, LANES), np.float32)))
    return pl.pallas_call(
        _decay_kernel,
        grid=(s // width,),
        in_specs=[
            pl.BlockSpec((rows, width), lambda i: (0, i)),
            pl.BlockSpec((rows, 1), lambda i: (0, 0)),
            pl.BlockSpec((LANES, LANES), lambda i: (0, 0)),
        ],
        out_specs=pl.BlockSpec((DECAY_TERMS, rows, width), lambda i: (0, 0, i)),
        out_shape=jax.ShapeDtypeStruct((DECAY_TERMS, rows, s), jnp.bfloat16),
        scratch_shapes=[pltpu.VMEM((rows, 1), jnp.float32)],
        compiler_params=_params(("arbitrary",)),
        name="decay_cumsum",
    )(f_t, bias_col, tri)


def _col_reduce(op, x):
    reduce = {jnp.maximum: jnp.max, jnp.minimum: jnp.min, jnp.add: jnp.sum}[op]
    return reduce(x, axis=0, keepdims=True)


def _flash_init(m_ref, acc_ref):
    m_ref[...] = jnp.full(m_ref.shape, NEG, jnp.float32)
    acc_ref[...] = jnp.zeros(acc_ref.shape, jnp.float32)


def _probabilities(s, m_new):
    return jnp.exp2((s - m_new).astype(jnp.bfloat16))


def _flash_result(acc_ref):
    return acc_ref[0:HEAD_DIM, :] / acc_ref[SUM_ROW:SUM_ROW + 1, :]


def _flash_step(kaug, vt, get_q, width, m_ref, acc_ref, extra=None):
    w = width // N_SPLIT
    strips = [(i * w, (i + 1) * w) for i in range(N_SPLIT)]
    scores = [jnp.dot(kaug, get_q(lo, hi), preferred_element_type=jnp.float32) for lo, hi in strips]
    for (lo, hi), s in zip(strips, scores):
        if extra is not None:
            s = extra(s, lo, hi)
        m_prev = m_ref[:, lo:hi]
        m_new = jnp.maximum(m_prev, _col_reduce(jnp.maximum, s))
        alpha = jnp.exp2(m_prev - m_new)
        acc_ref[:, lo:hi] = alpha * acc_ref[:, lo:hi] + jnp.dot(
            vt, _probabilities(s, m_new), preferred_element_type=jnp.float32)
        m_ref[:, lo:hi] = m_new


def _flash_far_chunks(n, load_keys, load_values, get_q_for, width, m_ref, acc_ref, s_ref, mx_ref, p_ref):
    w = width // N_SPLIT
    strips = [(i * w, (i + 1) * w) for i in range(N_SPLIT)]
    base = n % 2
    pairs = n // 2

    def scores_into(c, slot):
        kaug = load_keys(c)
        get_q = get_q_for(c)
        for lo, hi in strips:
            s = jnp.dot(kaug, get_q(lo, hi), preferred_element_type=jnp.float32)
            s_ref[slot, :, lo:hi] = s
            mx_ref[slot, :, lo:hi] = _col_reduce(jnp.maximum, s)

    def value_product(c, slot):
        vt = load_values(c)
        return jnp.concatenate([jnp.dot(vt, p_ref[slot, :, lo:hi], preferred_element_type=jnp.float32)
                                for lo, hi in strips], axis=1)

    @pl.when(base == 1)
    def _():
        _flash_step(load_keys(0), load_values(0), get_q_for(0), width, m_ref, acc_ref)

    @pl.when(pairs > 0)
    def _():
        scores_into(base, 0)
        p_ref[1] = jnp.zeros(p_ref.shape[1:], p_ref.dtype)

        def pair(j, carry):
            for cur in (0, 1):
                c = base + 2 * j + cur
                nxt = 1 - cur
                pv = value_product(jnp.maximum(c - 1, base), nxt)
                scores_into(jnp.minimum(c + 1, n - 1), nxt)
                alphas = []
                for lo, hi in strips:
                    m_prev = m_ref[:, lo:hi]
                    m_new = jnp.maximum(m_prev, mx_ref[cur, :, lo:hi])
                    m_ref[:, lo:hi] = m_new
                    for r in range(0, s_ref.shape[1], ROW_BLOCK):
                        p_ref[cur, r:r + ROW_BLOCK, lo:hi] = _probabilities(
                            s_ref[cur, r:r + ROW_BLOCK, lo:hi], m_new)
                    alphas.append(jnp.exp2(m_prev - m_new))
                acc_ref[...] = jnp.concatenate(alphas, axis=1) * (acc_ref[...] + pv)
            return carry

        lax.fori_loop(0, pairs, pair, 0)
        acc_ref[...] = acc_ref[...] + value_product(n - 1, 1)


def _group_queries_t(qt_ref):
    return jnp.concatenate([qt_ref[h] for h in range(NSA_HPG)], axis=1)


def _fill_values_t(v_ref, vt_ref):
    n_pieces, _, piece = vt_ref.shape
    pad = lax.broadcasted_iota(jnp.int32, (ACC_ROWS - HEAD_DIM, piece), 0)
    ones_rows = jnp.where(pad == 0, 1.0, 0.0).astype(vt_ref.dtype)

    def fill(p, carry):
        vt_ref[p, 0:HEAD_DIM, :] = v_ref[pl.ds(pl.multiple_of(p * piece, piece), piece), :].T
        vt_ref[p, HEAD_DIM:, :] = ones_rows
        return carry

    lax.fori_loop(0, n_pieces, fill, 0)


def _branch_gate(gates_t, head, branch):
    row = head * 3 + branch
    return gates_t[row:row + 1, :]


def _values_t(vt_ref, k0, n_keys):
    p0 = k0 // LANES
    return jnp.concatenate([vt_ref[p0 + j] for j in range(n_keys // LANES)], axis=1)


def _compressed_branch(n_rows, i0, q4t, kc_ref, vct_ref, band_ref, ovt_ref, sc_ref, oct_ref, imp_ref):
    sc_ref[0:n_rows, :] = jnp.dot(kc_ref[0:n_rows, :], q4t, preferred_element_type=jnp.float32)
    sc_ref[pl.ds(i0, BAND_ROWS), :] = sc_ref[pl.ds(i0, BAND_ROWS), :] + band_ref[...]
    row = lax.broadcasted_iota(jnp.int32, (n_rows, ROWS), 0)
    sc = jnp.where(row < i0 + BAND_ROWS, sc_ref[0:n_rows, :], NEG)
    m = _col_reduce(jnp.maximum, sc)
    p = jnp.exp2(sc - m)
    l = _col_reduce(jnp.add, p)
    pn = p * jnp.where(m > 0.5 * NEG, 1.0 / l, 0.0)
    oct_ref[...] = jnp.dot(vct_ref[:, 0:n_rows], pn.astype(jnp.bfloat16), preferred_element_type=jnp.float32)
    psum = pn[:, 0:Q_TILE]
    for h in range(1, NSA_HPG):
        psum = psum + pn[:, h * Q_TILE:(h + 1) * Q_TILE]
    imp_ref[...] = jnp.dot(ovt_ref[:, 0:n_rows], psum.astype(jnp.bfloat16), preferred_element_type=jnp.float32)


def _cmp_win_kernel(qt_ref, kc_ref, vct_ref, band_ref, ovt_ref, kw_ref, vw_ref, near_ref, gt_ref,
                    ycw_ref, pen_ref, sc_ref, sw_ref, oct_ref, imp_ref, vwt_ref):
    qb = pl.program_id(2)
    n_cmp = kc_ref.shape[0]
    n_slc = ovt_ref.shape[0]
    q4t = _group_queries_t(qt_ref)
    pl.when(qb == 0)(functools.partial(_fill_values_t, vw_ref, vwt_ref))

    i0 = pl.multiple_of(jnp.maximum(8 * qb - 16, 0), 8)
    n_tiers = CMP_TIERS if n_cmp % (CMP_TIERS * LANES) == 0 else 1
    step = n_cmp // n_tiers
    for tier in range(1, n_tiers + 1):
        in_tier = (i0 + BAND_ROWS <= tier * step) & (i0 + BAND_ROWS > (tier - 1) * step)
        pl.when(in_tier)(functools.partial(_compressed_branch, tier * step, i0, q4t, kc_ref, vct_ref,
                                           band_ref, ovt_ref, sc_ref, oct_ref, imp_ref))
    oct_ = oct_ref[...]
    imp = imp_ref[...]

    ji = lax.broadcasted_iota(jnp.int32, (n_slc, Q_TILE), 0)
    jf = ji.astype(jnp.float32)
    t = qb * Q_TILE + lax.broadcasted_iota(jnp.int32, (n_slc, Q_TILE), 1)
    cur = t // SLC_BLOCK
    forced = (ji == 0) | (ji == cur) | (ji == cur - 1)
    score = jnp.where(ji <= cur, jnp.where(forced, FORCED_SCORE, imp), -1.0)
    pen_t = jnp.full((n_slc, Q_TILE), NEG, jnp.float32)
    for _ in range(min(SLC_TOPK, n_slc)):
        mx = _col_reduce(jnp.maximum, score)
        idx = _col_reduce(jnp.minimum, jnp.where(score == mx, jf, float(n_slc)))
        pick = jf == idx
        pen_t = jnp.where(pick, 0.0, pen_t)
        score = jnp.where(pick, -2.0, score)
    pad = pen_ref.shape[0] - n_slc
    if pad:
        pen_t = jnp.concatenate([pen_t, jnp.full((pad, Q_TILE), NEG, jnp.float32)], axis=0)
    pen_ref[...] = pen_t.astype(pen_ref.dtype)

    t0 = qb * Q_TILE
    ws = pl.multiple_of(jnp.maximum(t0 - WINDOW, 0), Q_TILE)
    ns = pl.multiple_of(jnp.maximum(t0 - Q_TILE, 0), Q_TILE)
    sw_ref[...] = jnp.dot(kw_ref[pl.ds(ws, WIN_KEYS), :], q4t, preferred_element_type=jnp.float32)
    off = pl.multiple_of(ns - ws, Q_TILE)
    sw_ref[pl.ds(off, NEAR_KEYS), :] = sw_ref[pl.ds(off, NEAR_KEYS), :] + near_ref[...]
    key = ws + lax.broadcasted_iota(jnp.int32, (WIN_KEYS, ROWS), 0)
    qry = t0 + (lax.broadcasted_iota(jnp.int32, (WIN_KEYS, ROWS), 1) & (Q_TILE - 1))
    sw = jnp.where((key <= qry) & (key > qry - WINDOW), sw_ref[...], NEG)
    mw = _col_reduce(jnp.maximum, sw)
    ow_sum = jnp.dot(_values_t(vwt_ref, ws, WIN_KEYS), _probabilities(sw, mw),
                     preferred_element_type=jnp.float32)
    owt = ow_sum[0:HEAD_DIM, :] / ow_sum[SUM_ROW:SUM_ROW + 1, :]

    gates_t = jax.nn.sigmoid(gt_ref[...])
    for h in range(NSA_HPG):
        cols = slice(h * Q_TILE, (h + 1) * Q_TILE)
        mixed = _branch_gate(gates_t, h, 0) * oct_[:, cols] + _branch_gate(gates_t, h, 2) * owt[:, cols]
        ycw_ref[:, h * HEAD_DIM:(h + 1) * HEAD_DIM] = mixed.T


def _cmp_win(qt, kc, vct, band, ovt, qkv, near, gates_t):
    b, s, _ = qkv.shape
    n_cmp = kc.shape[2]
    n_slc = ovt.shape[0]
    nq = s // Q_TILE
    pen_w = -(-n_slc // LANES) * LANES
    n_band = band.shape[0] - 1
    return pl.pallas_call(
        _cmp_win_kernel,
        grid=(b, NSA_GROUPS, nq),
        in_specs=[
            pl.BlockSpec((None, NSA_HPG, HEAD_DIM, Q_TILE), lambda i, g, q: (i, g, 0, q)),
            pl.BlockSpec((None, None, n_cmp, HEAD_DIM), lambda i, g, q: (i, g, 0, 0)),
            pl.BlockSpec((None, None, HEAD_DIM, n_cmp), lambda i, g, q: (i, g, 0, 0)),
            pl.BlockSpec((None, None, BAND_ROWS, ROWS), lambda i, g, q: (jnp.minimum(q, n_band), g, 0, 0)),
            pl.BlockSpec((n_slc, n_cmp), lambda i, g, q: (0, 0)),
            pl.BlockSpec((None, s, HEAD_DIM), lambda i, g, q: (i, 0, CB_KW + g)),
            pl.BlockSpec((None, s, HEAD_DIM), lambda i, g, q: (i, 0, CB_VW + g)),
            pl.BlockSpec((None, None, NEAR_KEYS, ROWS), lambda i, g, q: (jnp.minimum(q, 1), g, 0, 0)),
            pl.BlockSpec((None, None, GATE_ROWS, Q_TILE), lambda i, g, q: (i, g, 0, q)),
        ],
        out_specs=[
            pl.BlockSpec((None, Q_TILE, NSA_HPG * HEAD_DIM), lambda i, g, q: (i, q, g)),
            pl.BlockSpec((None, None, pen_w, Q_TILE), lambda i, g, q: (i, g, 0, q)),
        ],
        out_shape=[
            jax.ShapeDtypeStruct((b, s, NSA_Q_DIM), jnp.float32),
            jax.ShapeDtypeStruct((b, NSA_GROUPS, pen_w, s), jnp.bfloat16),
        ],
        scratch_shapes=[pltpu.VMEM((n_cmp, ROWS), jnp.float32), pltpu.VMEM((WIN_KEYS, ROWS), jnp.float32),
                        pltpu.VMEM((HEAD_DIM, ROWS), jnp.float32), pltpu.VMEM((n_slc, Q_TILE), jnp.float32),
                        pltpu.VMEM((s // LANES, ACC_ROWS, LANES), jnp.bfloat16)],
        compiler_params=_params(("parallel", "parallel", "arbitrary")),
        name="nsa_cmp_topk_win",
    )(qt, kc, vct, band, ovt, qkv, qkv, near, gates_t)


def _range_penalty(pen, first_blk, lo_ok, hi_ok):
    n_half = pen.shape[0] // LANES
    c = lax.broadcasted_iota(jnp.int32, (LANES, pen.shape[1]), 0)
    out = pen[0:LANES, :]
    blk = c
    for hf in range(1, n_half):
        in_lower = (c + (hf - 1) * LANES >= first_blk) & (first_blk < hf * LANES)
        out = jnp.where(in_lower, out, pen[hf * LANES:(hf + 1) * LANES, :])
        blk = jnp.where(in_lower, blk, c + hf * LANES)
    return jnp.where((blk >= lo_ok) & (blk < hi_ok), out, jnp.asarray(NEG, out.dtype))


def _sel_kernel(qt_ref, pen_ref, kaug_ref, v_ref, near_ref, ycw_ref, gt_ref, y_ref,
                qaug_ref, m_ref, acc_ref, s_ref, mx_ref, p_ref, vt_ref):
    qb = pl.program_id(2)
    n_half = pen_ref.shape[0] // LANES
    n_blocks = pen_ref.shape[0]
    chunks_per_half = LANES * SLC_BLOCK // KEY_CHUNK
    blocks_per_chunk = KEY_CHUNK // SLC_BLOCK

    t0 = qb * Q_TILE
    near_end = t0 - Q_TILE
    nf = jnp.maximum(near_end // KEY_CHUNK, 0)
    ms = pl.multiple_of(jnp.maximum(near_end - KEY_CHUNK, 0), Q_TILE)
    ns = pl.multiple_of(jnp.maximum(near_end, 0), Q_TILE)

    pl.when(qb == 0)(functools.partial(_fill_values_t, v_ref, vt_ref))
    q4t = _group_queries_t(qt_ref)
    pen = pen_ref[...]
    operands = [pen[hf * LANES:(hf + 1) * LANES, :] for hf in range(n_half)]
    operands.append(_range_penalty(pen, ms // SLC_BLOCK, nf * blocks_per_chunk, near_end // SLC_BLOCK))
    operands.append(_range_penalty(pen, ns // SLC_BLOCK, 0, n_blocks))
    for idx, channels in enumerate(operands):
        qaug_ref[idx, 0:HEAD_DIM, :] = q4t
        qaug_ref[idx, HEAD_DIM:, :] = jnp.concatenate([channels] * NSA_HPG, axis=1)

    _flash_init(m_ref, acc_ref)

    def key_start(c):
        return pl.multiple_of(jnp.where(c < nf, c * KEY_CHUNK, ms), Q_TILE)

    def get_q_for(c):
        operand = jnp.where(c < nf, c // chunks_per_half, n_half)
        return lambda lo, hi: qaug_ref[operand, :, lo:hi]

    _flash_far_chunks(nf + 1, lambda c: kaug_ref[pl.ds(key_start(c), KEY_CHUNK), :],
                      lambda c: _values_t(vt_ref, key_start(c), KEY_CHUNK), get_q_for,
                      ROWS, m_ref, acc_ref, s_ref, mx_ref, p_ref)

    _flash_step(kaug_ref[pl.ds(ns, NEAR_KEYS), :], _values_t(vt_ref, ns, NEAR_KEYS),
                lambda lo, hi: qaug_ref[n_half + 1, :, lo:hi], ROWS, m_ref, acc_ref,
                extra=lambda s, lo, hi: s + near_ref[:, lo:hi])

    o = _flash_result(acc_ref)
    gates_t = jax.nn.sigmoid(gt_ref[...])
    for h in range(NSA_HPG):
        sel_h = (_branch_gate(gates_t, h, 1) * o[:, h * Q_TILE:(h + 1) * Q_TILE]).T
        y_ref[:, h * HEAD_DIM:(h + 1) * HEAD_DIM] = (
            ycw_ref[:, h * HEAD_DIM:(h + 1) * HEAD_DIM] + sel_h).astype(y_ref.dtype)


def _sel(qkv, qt, pen, kaug, near, ycw, gates_t):
    b, s, _ = qkv.shape
    nq = s // Q_TILE
    pen_w = pen.shape[2]
    return pl.pallas_call(
        _sel_kernel,
        grid=(b, NSA_GROUPS, nq),
        in_specs=[
            pl.BlockSpec((None, NSA_HPG, HEAD_DIM, Q_TILE), lambda i, g, q: (i, g, 0, q)),
            pl.BlockSpec((None, None, pen_w, Q_TILE), lambda i, g, q: (i, g, 0, q)),
            pl.BlockSpec((None, None, s, 2 * HEAD_DIM), lambda i, g, q: (i, g, 0, 0)),
            pl.BlockSpec((None, s, HEAD_DIM), lambda i, g, q: (i, 0, CB_VS + g)),
            pl.BlockSpec((None, None, NEAR_KEYS, ROWS), lambda i, g, q: (jnp.minimum(q, 1), g, 0, 0)),
            pl.BlockSpec((None, Q_TILE, NSA_HPG * HEAD_DIM), lambda i, g, q: (i, q, g)),
            pl.BlockSpec((None, None, GATE_ROWS, Q_TILE), lambda i, g, q: (i, g, 0, q)),
        ],
        out_specs=pl.BlockSpec((None, Q_TILE, NSA_HPG * HEAD_DIM), lambda i, g, q: (i, q, g)),
        out_shape=jax.ShapeDtypeStruct((b, s, NSA_Q_DIM), jnp.bfloat16),
        scratch_shapes=[
            pltpu.VMEM((pen_w // LANES + 2, 2 * HEAD_DIM, ROWS), jnp.bfloat16),
            pltpu.VMEM((1, ROWS), jnp.float32),
            pltpu.VMEM((ACC_ROWS, ROWS), jnp.float32),
            pltpu.VMEM((2, KEY_CHUNK, ROWS), jnp.float32),
            pltpu.VMEM((2, 1, ROWS), jnp.float32),
            pltpu.VMEM((2, KEY_CHUNK, ROWS), jnp.bfloat16),
            pltpu.VMEM((s // LANES, ACC_ROWS, LANES), jnp.bfloat16),
        ],
        compiler_params=_params(("parallel", "parallel", "arbitrary")),
        name="nsa_selected",
    )(qt, pen, kaug, qkv, near, ycw, gates_t)


def _fox_kernel(nce_ref, q_ref, k_ref, dec_ref, v_ref, o_ref,
                qaug_ref, m_ref, acc_ref, s_ref, mx_ref, p_ref, knorm_ref, vt_ref):
    qi = pl.program_id(2)
    bh = pl.program_id(0) * pl.num_programs(1) + pl.program_id(1)
    tq = q_ref.shape[0]
    row = lax.broadcasted_iota(jnp.int32, (HEAD_DIM, tq), 0)
    q_t = q_ref[...].T
    qaug_ref[0:HEAD_DIM, :] = q_t
    qaug_ref[HEAD_DIM:, :] = jnp.where(row < DECAY_TERMS, -1.0, 0.0).astype(qaug_ref.dtype)
    _flash_init(m_ref, acc_ref)

    @pl.when(qi == 0)
    def _():
        pad = lax.broadcasted_iota(jnp.int32, (ACC_ROWS - HEAD_DIM, tq), 0)
        ones_rows = jnp.where(pad == 0, 1.0, 0.0).astype(vt_ref.dtype)

        def chunk_setup(c, best):
            rows = pl.ds(pl.multiple_of(c * tq, tq), tq)
            vt_ref[c, 0:HEAD_DIM, :] = v_ref[rows, :].T
            vt_ref[c, HEAD_DIM:, :] = ones_rows
            kf = k_ref[rows, :].astype(jnp.float32)
            return jnp.maximum(best, jnp.max(jnp.sum(kf * kf, axis=1, keepdims=True)))

        knorm_ref[0] = jnp.sqrt(lax.fori_loop(0, k_ref.shape[0] // tq, chunk_setup, jnp.float32(0.0)))

    qf = q_t.astype(jnp.float32)
    q_norm = jnp.sqrt(jnp.max(jnp.sum(qf * qf, axis=0, keepdims=True)))

    def get_q(lo, hi):
        return qaug_ref[:, lo:hi]

    def load_keys(c):
        k0 = pl.multiple_of(c * tq, tq)
        return jnp.concatenate([k_ref[pl.ds(k0, tq), :], dec_ref[pl.ds(k0, tq), :]], axis=1)

    def causal(s, lo, hi):
        key = lax.broadcasted_iota(jnp.int32, s.shape, 0)
        qry = lo + lax.broadcasted_iota(jnp.int32, s.shape, 1)
        return jnp.where(key <= qry, s, NEG)

    _flash_step(load_keys(qi), vt_ref[qi], get_q, tq, m_ref, acc_ref, extra=causal)

    m_low = jnp.min(m_ref[...])
    reach = q_norm * knorm_ref[0] + SKIP_MARGIN_LOG2

    def count(c, n_skip):
        return n_skip + jnp.where(reach + nce_ref[bh, c] < m_low, 1, 0)

    first = lax.fori_loop(0, qi, count, jnp.int32(0))
    _flash_far_chunks(qi - first, lambda c: load_keys(c + first), lambda c: vt_ref[c + first],
                      lambda c: get_q, tq, m_ref, acc_ref, s_ref, mx_ref, p_ref)
    o_ref[...] = _flash_result(acc_ref).T.astype(o_ref.dtype)


def _fox(neg_cum_end, qkv, dec, tq):
    b, s, _ = qkv.shape
    return pl.pallas_call(
        _fox_kernel,
        grid=(b, FOX_HEADS, s // tq),
        in_specs=[
            pl.BlockSpec(memory_space=pltpu.SMEM),
            pl.BlockSpec((None, tq, HEAD_DIM), lambda i, h, q: (i, q, CB_QF + h)),
            pl.BlockSpec((None, s, HEAD_DIM), lambda i, h, q: (i, 0, CB_KF + h)),
            pl.BlockSpec((None, None, s, LANES), lambda i, h, q: (i, h, 0, 0)),
            pl.BlockSpec((None, s, HEAD_DIM), lambda i, h, q: (i, 0, CB_VF + h)),
        ],
        out_specs=pl.BlockSpec((None, tq, HEAD_DIM), lambda i, h, q: (i, q, h)),
        out_shape=jax.ShapeDtypeStruct((b, s, FOX_DIM), jnp.bfloat16),
        scratch_shapes=[
            pltpu.VMEM((2 * HEAD_DIM, tq), jnp.bfloat16),
            pltpu.VMEM((1, tq), jnp.float32),
            pltpu.VMEM((ACC_ROWS, tq), jnp.float32),
            pltpu.VMEM((2, tq, tq), jnp.float32),
            pltpu.VMEM((2, 1, tq), jnp.float32),
            pltpu.VMEM((2, tq, tq), jnp.bfloat16),
            pltpu.SMEM((1,), jnp.float32),
            pltpu.VMEM((s // tq, ACC_ROWS, tq), jnp.bfloat16),
        ],
        compiler_params=_params(("parallel", "parallel", "arbitrary")),
        name="fox_attention",
    )(neg_cum_end, qkv, qkv, dec, qkv)


def _merge_kernel(x_ref, g_ref, wa_ref, wb_ref, pa_ref, pb_ref, wo_ref, ya_ref, yb_ref, o_ref,
                  h_ref, acc_ref):
    j = pl.program_id(1)

    @pl.when(j == 0)
    def _():
        h_ref[...] = _norm_rows(x_ref[...], g_ref[...]).astype(h_ref.dtype)
        acc_ref[...] = jnp.zeros_like(acc_ref)

    h = h_ref[...]
    ga = jax.nn.sigmoid(jnp.dot(h, wa_ref[...], preferred_element_type=jnp.float32))
    gb = jax.nn.sigmoid(jnp.dot(h, wb_ref[...], preferred_element_type=jnp.float32))
    a = jnp.dot(ya_ref[...], pa_ref[...], preferred_element_type=jnp.float32)
    bb = jnp.dot(yb_ref[...], pb_ref[...], preferred_element_type=jnp.float32)
    merged = (ga * a + gb * bb).astype(jnp.bfloat16)
    acc_ref[...] += jnp.dot(merged, wo_ref[...], preferred_element_type=jnp.float32)

    @pl.when(j == pl.num_programs(1) - 1)
    def _():
        o_ref[...] = x_ref[...] + acc_ref[...]


def _merge(x2d, g, w_ma, w_mb, p_a, p_b, w_out, y_a, y_b, tm, tn):
    t, d = x2d.shape
    ka = p_a.shape[0]
    kb = p_b.shape[0]
    return pl.pallas_call(
        _merge_kernel,
        grid=(t // tm, d // tn),
        in_specs=[
            pl.BlockSpec((tm, d), lambda i, j: (i, 0)),
            pl.BlockSpec((1, d), lambda i, j: (0, 0)),
            pl.BlockSpec((d, tn), lambda i, j: (0, j)),
            pl.BlockSpec((d, tn), lambda i, j: (0, j)),
            pl.BlockSpec((ka, tn), lambda i, j: (0, j)),
            pl.BlockSpec((kb, tn), lambda i, j: (0, j)),
            pl.BlockSpec((tn, d), lambda i, j: (j, 0)),
            pl.BlockSpec((tm, ka), lambda i, j: (i, 0)),
            pl.BlockSpec((tm, kb), lambda i, j: (i, 0)),
        ],
        out_specs=pl.BlockSpec((tm, d), lambda i, j: (i, 0)),
        out_shape=jax.ShapeDtypeStruct((t, d), jnp.float32),
        scratch_shapes=[pltpu.VMEM((tm, d), jnp.bfloat16), pltpu.VMEM((tm, d), jnp.float32)],
        compiler_params=_params(("parallel", "arbitrary")),
        name="merge_out_proj",
    )(x2d, g.reshape(1, d), w_ma, w_mb, p_a, p_b, w_out, y_a, y_b)


def _ffn_kernel(x_ref, xh_ref, g_ref, wu_ref, wv_ref, cw_ref, cb_ref, wd_ref, gf_ref, o_ref,
                h_ref, u_ref, acc_ref, *, seq, tm):
    i = pl.program_id(0)
    j = pl.program_id(1)
    halo = BF16_SUBLANES

    @pl.when(j == 0)
    def _():
        g = g_ref[...]
        keep = jnp.where((i * tm) % seq == 0, 0.0, 1.0)
        h_ref[0:halo, :] = (_norm_rows(xh_ref[...], g) * keep).astype(h_ref.dtype)
        h_ref[halo:, :] = _norm_rows(x_ref[...], g).astype(h_ref.dtype)
        acc_ref[...] = jnp.zeros_like(acc_ref)

    u_ref[...] = jnp.dot(h_ref[...], wu_ref[...], preferred_element_type=jnp.float32)
    v = jnp.dot(h_ref[halo:, :], wv_ref[...], preferred_element_type=jnp.float32)
    conv = cb_ref[...]
    for k in range(CONV_WIDTH):
        conv = conv + cw_ref[k:k + 1, :] * u_ref[pl.ds(halo - (CONV_WIDTH - 1) + k, tm), :]
    act = (jax.nn.gelu(conv) * v).astype(jnp.bfloat16)
    acc_ref[...] += jnp.dot(act, wd_ref[...], preferred_element_type=jnp.float32)

    @pl.when(j == pl.num_programs(1) - 1)
    def _():
        o_ref[...] = _norm_rows(x_ref[...] + acc_ref[...], gf_ref[...])


def _ffn(x2d, g, w_up, conv_w, conv_b, w_down, g_final, seq, tm, tn):
    t, d = x2d.shape
    d_ff = w_down.shape[0]
    nt = d_ff // tn
    halo = BF16_SUBLANES
    per = tm // halo
    return pl.pallas_call(
        functools.partial(_ffn_kernel, seq=seq, tm=tm),
        grid=(t // tm, nt),
        in_specs=[
            pl.BlockSpec((tm, d), lambda i, j: (i, 0)),
            pl.BlockSpec((halo, d), lambda i, j: (jnp.maximum(i * per - 1, 0), 0)),
            pl.BlockSpec((1, d), lambda i, j: (0, 0)),
            pl.BlockSpec((d, tn), lambda i, j: (0, j)),
            pl.BlockSpec((d, tn), lambda i, j: (0, nt + j)),
            pl.BlockSpec((CONV_WIDTH, tn), lambda i, j: (0, j)),
            pl.BlockSpec((1, tn), lambda i, j: (0, j)),
            pl.BlockSpec((tn, d), lambda i, j: (j, 0)),
            pl.BlockSpec((1, d), lambda i, j: (0, 0)),
        ],
        out_specs=pl.BlockSpec((tm, d), lambda i, j: (i, 0)),
        out_shape=jax.ShapeDtypeStruct((t, d), jnp.float32),
        scratch_shapes=[
            pltpu.VMEM((tm + halo, d), jnp.bfloat16),
            pltpu.VMEM((tm + halo, tn), jnp.float32),
            pltpu.VMEM((tm, d), jnp.float32),
        ],
        compiler_params=_params(("parallel", "arbitrary")),
        name="ffn_final_norm",
    )(x2d, x2d, g.reshape(1, d), w_up, w_up, conv_w, conv_b.reshape(1, d_ff), w_down, g_final.reshape(1, d))


def _t5_bucket_np(dist):
    n = np.maximum(dist, 0)
    max_exact = REL_BUCKETS // 2
    nf = np.maximum(n, 1).astype(np.float32)
    large = max_exact + (np.log(nf / np.float32(max_exact)) / np.float32(math.log(REL_MAX_DIST / max_exact))
                         * np.float32(REL_BUCKETS - max_exact)).astype(np.int32)
    return np.where(n < max_exact, n, np.minimum(large, REL_BUCKETS - 1)).astype(np.int32)


def _bias_by_distance(rel_table, far_shift):
    vals = rel_table[jnp.asarray(_t5_bucket_np(np.arange(REL_MAX_DIST + 1)))]
    if far_shift:
        vals = vals - rel_table[REL_BUCKETS - 1]
    return vals * LOG2E


def _near_bias(rel_table):
    fd = _bias_by_distance(rel_table, True)
    h = fd.shape[1]
    lo = NEAR_KEYS - 1
    vec = jnp.concatenate([jnp.full((lo, h), NEG, fd.dtype), fd[:REL_MAX_DIST],
                           jnp.zeros((NEAR_KEYS - REL_MAX_DIST, h), fd.dtype)], axis=0)
    def toeplitz(start):
        period = Q_TILE + NEAR_KEYS - 1
        window = jnp.concatenate([vec[start:start + Q_TILE], vec[start - (NEAR_KEYS - 1):start]], axis=0)
        flat = jnp.tile(window, (NEAR_KEYS, 1))[:NEAR_KEYS * (period - 1)]
        return flat.reshape(NEAR_KEYS, period - 1, h)[:, :Q_TILE]

    tiles = jnp.stack([toeplitz(lo + a) for a in (0, Q_TILE)])
    tiles = tiles.reshape(2, NEAR_KEYS, Q_TILE, NSA_GROUPS, NSA_HPG).transpose(0, 3, 1, 4, 2)
    return tiles.reshape(2, NSA_GROUPS, NEAR_KEYS, ROWS)


def _band_bias(rel_table):
    fd = _bias_by_distance(rel_table, True)
    tl = np.arange(Q_TILE)[None, :]
    r = np.arange(BAND_ROWS)[:, None]
    dist = np.stack([tl - CMP_STRIDE * (r - off) - (CMP_BLOCK - 1) for off in (0, 8, 16)])
    vals = fd[jnp.asarray(np.clip(dist, 0, REL_MAX_DIST))]
    vals = jnp.where(jnp.asarray(dist >= 0)[..., None], vals, NEG)
    v, rr, q, _ = vals.shape
    return vals.transpose(0, 1, 3, 2).reshape(v, rr, NSA_GROUPS, NSA_HPG * q).transpose(0, 2, 1, 3)


def _overlap_t(n_cmp_pad, n_slc):
    i = np.arange(n_cmp_pad)[None, :]
    jj = np.arange(n_slc)[:, None]
    c_start = i * CMP_STRIDE
    ov = (c_start < jj * SLC_BLOCK + SLC_BLOCK) & (c_start + CMP_BLOCK - 1 >= jj * SLC_BLOCK)
    ov = ov & (i < n_cmp_pad - 1)
    return jnp.asarray(ov.astype(np.float32), jnp.bfloat16)


def _block_onehot(seq):
    blk = (np.arange(seq) // SLC_BLOCK) % LANES
    return jnp.asarray((blk[:, None] == np.arange(LANES)[None, :]).astype(np.float32), jnp.bfloat16)


def _pick_tile(n, pref):
    return pref if n % pref == 0 else n


def kernel(x, attn_norm_g, w_in, cmp_pos_k, cmp_w1_k, cmp_w2_k, cmp_pos_v, cmp_w1_v, cmp_w2_v,
           rel_bias_table, fox_forget_bias, w_branch_nsa, w_branch_fox, w_out,
           ffn_norm_g, w_up, conv_w, conv_b, w_down, final_norm_g):
    assert w_in.shape[0] == 1, "the final norm is fused into the single layer's FFN kernel"
    bsz, seq, d = x.shape
    t = bsz * seq
    bf = jnp.bfloat16
    scale = HEAD_DIM ** -0.5 * LOG2E
    x2d = x.reshape(t, d)
    w_in = w_in[0]

    o = np.cumsum([0, NSA_Q_DIM] + [NSA_KV_DIM] * 6 + [3 * NSA_HEADS, FOX_DIM, FOX_DIM, FOX_DIM, FOX_HEADS, d, d])
    w_qkv = jnp.concatenate([w_in[:, o[0]:o[1]] * scale, w_in[:, o[1]:o[7]],
                             w_in[:, o[8]:o[9]] * scale, w_in[:, o[9]:o[11]]], axis=1).astype(bf)
    gate_cols = []
    for grp in range(NSA_GROUPS):
        gate_cols += [w_in[:, o[7] + grp * GATES_PER_GROUP:o[7] + (grp + 1) * GATES_PER_GROUP],
                      jnp.zeros((d, LANES - GATES_PER_GROUP), w_in.dtype)]
    gate_cols += [w_in[:, o[11]:o[12]], jnp.zeros((d, IN_PROJ_TILE - N_GATE + LANES - FOX_HEADS), w_in.dtype)]
    w_gate = jnp.concatenate(gate_cols, axis=1).astype(bf)
    w_ma = w_in[:, o[12]:o[13]].astype(bf)
    w_mb = w_in[:, o[13]:o[14]].astype(bf)

    qkv, gates = _in_proj(x2d, attn_norm_g[0], jnp.concatenate([w_qkv, w_gate], axis=1),
                          _pick_tile(t, 1024), IN_PROJ_TILE)
    qkv = qkv.reshape(bsz, seq, N_QKV)
    gates = gates.reshape(bsz, seq, IN_PROJ_TILE)

    qt = qkv[:, :, :NSA_Q_DIM].reshape(bsz, seq, NSA_HEADS, HEAD_DIM).transpose(0, 2, 3, 1)

    def heads_major(cb, heads):
        return qkv[:, :, cb * HEAD_DIM:(cb + heads) * HEAD_DIM].reshape(
            bsz, seq, heads, HEAD_DIM).transpose(0, 2, 1, 3)

    n_ch = seq // CMP_STRIDE

    def chunked(cb):
        sl = qkv[:, :, cb * HEAD_DIM:(cb + NSA_GROUPS) * HEAD_DIM]
        sl = sl.reshape(bsz, n_ch, CMP_STRIDE, NSA_GROUPS, HEAD_DIM).transpose(0, 3, 1, 2, 4)
        return sl.reshape(bsz, NSA_GROUPS, n_ch, CMP_STRIDE * HEAD_DIM)

    def posflat(pos):
        return jnp.broadcast_to(pos.reshape(1, CMP_BLOCK * HEAD_DIM), (8, CMP_BLOCK * HEAD_DIM)).astype(bf)

    kc = _compress(chunked(CB_KC), cmp_w1_k[0].astype(bf), posflat(cmp_pos_k[0]), cmp_w2_k[0].astype(bf))
    vc = _compress(chunked(CB_VC), cmp_w1_v[0].astype(bf), posflat(cmp_pos_v[0]), cmp_w2_v[0].astype(bf))
    vct = vc.transpose(0, 1, 3, 2)

    f_t = gates[:, :, NSA_GROUPS * LANES:NSA_GROUPS * LANES + FOX_HEADS]
    f_t = f_t.transpose(0, 2, 1).reshape(bsz * FOX_HEADS, seq)
    bias_col = jnp.tile(fox_forget_bias[0].astype(jnp.float32), bsz).reshape(bsz * FOX_HEADS, 1)
    terms = _decay_cumsum(f_t, bias_col, _pick_tile(seq, 2048))
    dec = jnp.pad(terms.transpose(1, 2, 0), ((0, 0), (0, 0), (0, LANES - DECAY_TERMS)))
    dec = dec.reshape(bsz, FOX_HEADS, seq, LANES)

    n_slc = seq // SLC_BLOCK
    near = _near_bias(rel_bias_table)
    gates_t = gates[:, :, :NSA_GROUPS * LANES].reshape(bsz, seq, NSA_GROUPS, LANES)[..., :GATE_ROWS]
    gates_t = gates_t.transpose(0, 2, 3, 1)
    ycw, pen = _cmp_win(qt, kc, vct, _band_bias(rel_bias_table), _overlap_t(n_ch, n_slc), qkv, near, gates_t)
    ks = heads_major(CB_KS, NSA_GROUPS)
    kaug_sel = jnp.concatenate([ks, jnp.broadcast_to(_block_onehot(seq), ks.shape)], axis=-1)
    y_nsa = _sel(qkv, qt, pen, kaug_sel, near, ycw, gates_t)

    fox_tq = _pick_tile(seq, FOX_Q_TILE)
    neg_cum_end = -jnp.sum(terms[:, :, fox_tq - 1::fox_tq].astype(jnp.float32), axis=0)
    y_fox = _fox(neg_cum_end, qkv, dec, fox_tq)

    tm2 = _pick_tile(t, 512)
    x_mid = _merge(x2d, attn_norm_g[0], w_ma, w_mb, w_branch_nsa[0].astype(bf), w_branch_fox[0].astype(bf),
                   w_out[0].astype(bf), y_nsa.reshape(t, NSA_Q_DIM), y_fox.reshape(t, FOX_DIM),
                   tm2, _pick_tile(d, 512))
    d_ff = w_down.shape[1]
    out = _ffn(x_mid, ffn_norm_g[0], w_up[0].astype(bf), conv_w[0], conv_b[0], w_down[0].astype(bf),
               final_norm_g, seq, tm2, _pick_tile(d_ff, 512))
    return out.reshape(bsz, seq, d)
```

```python
import functools
import math

import jax
import jax.numpy as jnp
import numpy as np
from jax import lax
from jax.experimental import pallas as pl
from jax.experimental.pallas import tpu as pltpu

HEAD_DIM = 128
NSA_HEADS = 8
NSA_GROUPS = 2
NSA_HPG = NSA_HEADS // NSA_GROUPS
FOX_HEADS = 8
CMP_BLOCK = 32
CMP_STRIDE = 16
SLC_BLOCK = 64
SLC_TOPK = 16
WINDOW = 512
REL_BUCKETS = 32
REL_MAX_DIST = 128
CONV_WIDTH = 3
EPS = 1e-6
NEG = -1e30
FORCED_SCORE = 1e4
LOG2E = math.log2(math.e)

LANES = 128
BF16_SUBLANES = 16
VMEM_LIMIT = 56 * 1024 * 1024

NSA_Q_DIM = NSA_HEADS * HEAD_DIM
NSA_KV_DIM = NSA_GROUPS * HEAD_DIM
FOX_DIM = FOX_HEADS * HEAD_DIM
N_QKV = NSA_Q_DIM + 6 * NSA_KV_DIM + 3 * FOX_DIM
GATES_PER_GROUP = 3 * NSA_HPG
N_GATE = (NSA_GROUPS + 1) * LANES
IN_PROJ_TILE = 512

CB_KC = 8
CB_VC = 10
CB_KS = 12
CB_VS = 14
CB_KW = 16
CB_VW = 18
CB_QF = 20
CB_KF = 28
CB_VF = 36

Q_TILE = 128
ROWS = NSA_HPG * Q_TILE
KEY_CHUNK = 512
FOX_Q_TILE = 512
N_SPLIT = 2
SKIP_MARGIN_LOG2 = 40.0
ROW_BLOCK = 32
DECAY_TERMS = 3
BAND_ROWS = 24
CMP_TIERS = 4
WIN_KEYS = WINDOW + Q_TILE
NEAR_KEYS = 2 * Q_TILE
GATE_ROWS = 16
SUM_ROW = HEAD_DIM
ACC_ROWS = HEAD_DIM + BF16_SUBLANES


def _params(sem):
    return pltpu.CompilerParams(dimension_semantics=sem, vmem_limit_bytes=VMEM_LIMIT)


def _norm_rows(x, g):
    return (x * lax.rsqrt(jnp.mean(x * x, axis=-1, keepdims=True) + EPS)) * g


def _in_proj_kernel(x_ref, g_ref, w_ref, qkv_ref, gate_ref, h_ref):
    j = pl.program_id(1)
    last = pl.num_programs(1) - 1

    @pl.when(j == 0)
    def _():
        h_ref[...] = _norm_rows(x_ref[...], g_ref[...]).astype(h_ref.dtype)

    y = jnp.dot(h_ref[...], w_ref[...], preferred_element_type=jnp.float32)

    @pl.when(j < last)
    def _():
        qkv_ref[...] = y.astype(qkv_ref.dtype)

    @pl.when(j == last)
    def _():
        gate_ref[...] = y


def _in_proj(x2d, g, w, tm, tn):
    t, d = x2d.shape
    n_tiles = w.shape[1] // tn
    return pl.pallas_call(
        _in_proj_kernel,
        grid=(t // tm, n_tiles),
        in_specs=[
            pl.BlockSpec((tm, d), lambda i, j: (i, 0)),
            pl.BlockSpec((1, d), lambda i, j: (0, 0)),
            pl.BlockSpec((d, tn), lambda i, j: (0, j)),
        ],
        out_specs=[
            pl.BlockSpec((tm, tn), lambda i, j: (i, jnp.minimum(j, n_tiles - 2))),
            pl.BlockSpec((tm, tn), lambda i, j: (i, 0)),
        ],
        out_shape=[
            jax.ShapeDtypeStruct((t, (n_tiles - 1) * tn), jnp.bfloat16),
            jax.ShapeDtypeStruct((t, tn), jnp.float32),
        ],
        scratch_shapes=[pltpu.VMEM((tm, d), jnp.bfloat16)],
        compiler_params=_params(("parallel", "arbitrary")),
        name="norm_in_proj",
    )(x2d, g.reshape(1, d), w)


def _compress_kernel(ch_ref, w1_ref, posf_ref, w2_ref, o_ref):
    half = ch_ref.shape[1]
    ch = ch_ref[...]
    pa = jnp.dot(ch, w1_ref[:half, :], preferred_element_type=jnp.float32)
    pb = jnp.dot(ch, w1_ref[half:, :], preferred_element_type=jnp.float32)
    pos = jnp.dot(posf_ref[...], w1_ref[...], preferred_element_type=jnp.float32)[0:1, :]
    n = pa.shape[0]
    pre = pa + pltpu.roll(pb, n - 1, 0) + pos
    act = jax.nn.gelu(pre)
    o_ref[...] = jnp.dot(act.astype(jnp.bfloat16), w2_ref[...],
                         preferred_element_type=jnp.float32).astype(o_ref.dtype)


def _compress(chunks, w1, posf, w2):
    b, g, n, k = chunks.shape
    return pl.pallas_call(
        _compress_kernel,
        grid=(b, g),
        in_specs=[
            pl.BlockSpec((None, None, n, k), lambda i, j: (i, j, 0, 0)),
            pl.BlockSpec(w1.shape, lambda i, j: (0, 0)),
            pl.BlockSpec(posf.shape, lambda i, j: (0, 0)),
            pl.BlockSpec(w2.shape, lambda i, j: (0, 0)),
        ],
        out_specs=pl.BlockSpec((None, None, n, HEAD_DIM), lambda i, j: (i, j, 0, 0)),
        out_shape=jax.ShapeDtypeStruct((b, g, n, HEAD_DIM), jnp.bfloat16),
        compiler_params=_params(("parallel", "parallel")),
        name="compress_tokens",
    )(chunks, w1, posf, w2)


def _decay_kernel(f_ref, b_ref, tri_ref, o_ref, carry_ref):
    @pl.when(pl.program_id(0) == 0)
    def _():
        carry_ref[...] = jnp.zeros_like(carry_ref)

    x = f_ref[...] + b_ref[...]
    logf = (jnp.minimum(x, 0.0) - jnp.log1p(jnp.exp(-jnp.abs(x)))) * LOG2E
    carry = carry_ref[...]
    for seg in range(f_ref.shape[1] // LANES):
        part = jnp.dot(logf[:, seg * LANES:(seg + 1) * LANES], tri_ref[...],
                       preferred_element_type=jnp.float32, precision=lax.Precision.HIGHEST) + carry
        carry = part[:, LANES - 1:LANES]
        rest = part
        for term in range(DECAY_TERMS):
            piece = rest.astype(o_ref.dtype)
            o_ref[term, :, seg * LANES:(seg + 1) * LANES] = piece
            rest = rest - piece.astype(jnp.float32)
    carry_ref[...] = carry


def _decay_cumsum(f_t, bias_col, width):
    rows, s = f_t.shape
    tri = jnp.asarray(np.triu(np.ones((LANES, LANES), np.float32)))
    return pl.pallas_call(
        _decay_kernel,
        grid=(s // width,),
        in_specs=[
            pl.BlockSpec((rows, width), lambda i: (0, i)),
            pl.BlockSpec((rows, 1), lambda i: (0, 0)),
            pl.BlockSpec((LANES, LANES), lambda i: (0, 0)),
        ],
        out_specs=pl.BlockSpec((DECAY_TERMS, rows, width), lambda i: (0, 0, i)),
        out_shape=jax.ShapeDtypeStruct((DECAY_TERMS, rows, s), jnp.bfloat16),
        scratch_shapes=[pltpu.VMEM((rows, 1), jnp.float32)],
        compiler_params=_params(("arbitrary",)),
        name="decay_cumsum",
    )(f_t, bias_col, tri)


def _col_reduce(op, x):
    reduce = {jnp.maximum: jnp.max, jnp.minimum: jnp.min, jnp.add: jnp.sum}[op]
    return reduce(x, axis=0, keepdims=True)


def _flash_init(m_ref, acc_ref):
    m_ref[...] = jnp.full(m_ref.shape, NEG, jnp.float32)
    acc_ref[...] = jnp.zeros(acc_ref.shape, jnp.float32)


def _probabilities(s, m_new):
    return jnp.exp2((s - m_new).astype(jnp.bfloat16))


def _flash_result(acc_ref):
    return acc_ref[0:HEAD_DIM, :] / acc_ref[SUM_ROW:SUM_ROW + 1, :]


def _flash_step(kaug, vt, get_q, width, m_ref, acc_ref, extra=None):
    w = width // N_SPLIT
    strips = [(i * w, (i + 1) * w) for i in range(N_SPLIT)]
    scores = [jnp.dot(kaug, get_q(lo, hi), preferred_element_type=jnp.float32) for lo, hi in strips]
    for (lo, hi), s in zip(strips, scores):
        if extra is not None:
            s = extra(s, lo, hi)
        m_prev = m_ref[:, lo:hi]
        m_new = jnp.maximum(m_prev, _col_reduce(jnp.maximum, s))
        alpha = jnp.exp2(m_prev - m_new)
        acc_ref[:, lo:hi] = alpha * acc_ref[:, lo:hi] + jnp.dot(
            vt, _probabilities(s, m_new), preferred_element_type=jnp.float32)
        m_ref[:, lo:hi] = m_new


def _flash_far_chunks(n, load_keys, load_values, get_q_for, width, m_ref, acc_ref, s_ref, mx_ref, p_ref):
    w = width // N_SPLIT
    strips = [(i * w, (i + 1) * w) for i in range(N_SPLIT)]
    base = n % 2
    pairs = n // 2

    def scores_into(c, slot):
        kaug = load_keys(c)
        get_q = get_q_for(c)
        for lo, hi in strips:
            s = jnp.dot(kaug, get_q(lo, hi), preferred_element_type=jnp.float32)
            s_ref[slot, :, lo:hi] = s
            mx_ref[slot, :, lo:hi] = _col_reduce(jnp.maximum, s)

    def value_product(c, slot):
        vt = load_values(c)
        return jnp.concatenate([jnp.dot(vt, p_ref[slot, :, lo:hi], preferred_element_type=jnp.float32)
                                for lo, hi in strips], axis=1)

    @pl.when(base == 1)
    def _():
        _flash_step(load_keys(0), load_values(0), get_q_for(0), width, m_ref, acc_ref)

    @pl.when(pairs > 0)
    def _():
        scores_into(base, 0)
        p_ref[1] = jnp.zeros(p_ref.shape[1:], p_ref.dtype)

        def pair(j, carry):
            for cur in (0, 1):
                c = base + 2 * j + cur
                nxt = 1 - cur
                pv = value_product(jnp.maximum(c - 1, base), nxt)
                scores_into(jnp.minimum(c + 1, n - 1), nxt)
                alphas = []
                for lo, hi in strips:
                    m_prev = m_ref[:, lo:hi]
                    m_new = jnp.maximum(m_prev, mx_ref[cur, :, lo:hi])
                    m_ref[:, lo:hi] = m_new
                    for r in range(0, s_ref.shape[1], ROW_BLOCK):
                        p_ref[cur, r:r + ROW_BLOCK, lo:hi] = _probabilities(
                            s_ref[cur, r:r + ROW_BLOCK, lo:hi], m_new)
                    alphas.append(jnp.exp2(m_prev - m_new))
                acc_ref[...] = jnp.concatenate(alphas, axis=1) * (acc_ref[...] + pv)
            return carry

        lax.fori_loop(0, pairs, pair, 0)
        acc_ref[...] = acc_ref[...] + value_product(n - 1, 1)


def _group_queries_t(qt_ref):
    return jnp.concatenate([qt_ref[h] for h in range(NSA_HPG)], axis=1)


def _fill_values_t(v_ref, vt_ref):
    n_pieces, _, piece = vt_ref.shape
    pad = lax.broadcasted_iota(jnp.int32, (ACC_ROWS - HEAD_DIM, piece), 0)
    ones_rows = jnp.where(pad == 0, 1.0, 0.0).astype(vt_ref.dtype)

    def fill(p, carry):
        vt_ref[p, 0:HEAD_DIM, :] = v_ref[pl.ds(pl.multiple_of(p * piece, piece), piece), :].T
        vt_ref[p, HEAD_DIM:, :] = ones_rows
        return carry

    lax.fori_loop(0, n_pieces, fill, 0)


def _branch_gate(gates_t, head, branch):
    row = head * 3 + branch
    return gates_t[row:row + 1, :]


def _values_t(vt_ref, k0, n_keys):
    p0 = k0 // LANES
    return jnp.concatenate([vt_ref[p0 + j] for j in range(n_keys // LANES)], axis=1)


def _compressed_branch(n_rows, i0, q4t, kc_ref, vct_ref, band_ref, ovt_ref, sc_ref, oct_ref, imp_ref):
    sc_ref[0:n_rows, :] = jnp.dot(kc_ref[0:n_rows, :], q4t, preferred_element_type=jnp.float32)
    sc_ref[pl.ds(i0, BAND_ROWS), :] = sc_ref[pl.ds(i0, BAND_ROWS), :] + band_ref[...]
    row = lax.broadcasted_iota(jnp.int32, (n_rows, ROWS), 0)
    sc = jnp.where(row < i0 + BAND_ROWS, sc_ref[0:n_rows, :], NEG)
    m = _col_reduce(jnp.maximum, sc)
    p = jnp.exp2(sc - m)
    l = _col_reduce(jnp.add, p)
    pn = p * jnp.where(m > 0.5 * NEG, 1.0 / l, 0.0)
    oct_ref[...] = jnp.dot(vct_ref[:, 0:n_rows], pn.astype(jnp.bfloat16), preferred_element_type=jnp.float32)
    psum = pn[:, 0:Q_TILE]
    for h in range(1, NSA_HPG):
        psum = psum + pn[:, h * Q_TILE:(h + 1) * Q_TILE]
    imp_ref[...] = jnp.dot(ovt_ref[:, 0:n_rows], psum.astype(jnp.bfloat16), preferred_element_type=jnp.float32)


def _cmp_win_kernel(qt_ref, kc_ref, vct_ref, band_ref, ovt_ref, kw_ref, vw_ref, near_ref, gt_ref,
                    ycw_ref, pen_ref, sc_ref, sw_ref, oct_ref, imp_ref, vwt_ref):
    qb = pl.program_id(2)
    n_cmp = kc_ref.shape[0]
    n_slc = ovt_ref.shape[0]
    q4t = _group_queries_t(qt_ref)
    pl.when(qb == 0)(functools.partial(_fill_values_t, vw_ref, vwt_ref))

    i0 = pl.multiple_of(jnp.maximum(8 * qb - 16, 0), 8)
    n_tiers = CMP_TIERS if n_cmp % (CMP_TIERS * LANES) == 0 else 1
    step = n_cmp // n_tiers
    for tier in range(1, n_tiers + 1):
        in_tier = (i0 + BAND_ROWS <= tier * step) & (i0 + BAND_ROWS > (tier - 1) * step)
        pl.when(in_tier)(functools.partial(_compressed_branch, tier * step, i0, q4t, kc_ref, vct_ref,
                                           band_ref, ovt_ref, sc_ref, oct_ref, imp_ref))
    oct_ = oct_ref[...]
    imp = imp_ref[...]

    ji = lax.broadcasted_iota(jnp.int32, (n_slc, Q_TILE), 0)
    jf = ji.astype(jnp.float32)
    t = qb * Q_TILE + lax.broadcasted_iota(jnp.int32, (n_slc, Q_TILE), 1)
    cur = t // SLC_BLOCK
    forced = (ji == 0) | (ji == cur) | (ji == cur - 1)
    score = jnp.where(ji <= cur, jnp.where(forced, FORCED_SCORE, imp), -1.0)
    pen_t = jnp.full((n_slc, Q_TILE), NEG, jnp.float32)
    for _ in range(min(SLC_TOPK, n_slc)):
        mx = _col_reduce(jnp.maximum, score)
        idx = _col_reduce(jnp.minimum, jnp.where(score == mx, jf, float(n_slc)))
        pick = jf == idx
        pen_t = jnp.where(pick, 0.0, pen_t)
        score = jnp.where(pick, -2.0, score)
    pad = pen_ref.shape[0] - n_slc
    if pad:
        pen_t = jnp.concatenate([pen_t, jnp.full((pad, Q_TILE), NEG, jnp.float32)], axis=0)
    pen_ref[...] = pen_t.astype(pen_ref.dtype)

    t0 = qb * Q_TILE
    ws = pl.multiple_of(jnp.maximum(t0 - WINDOW, 0), Q_TILE)
    ns = pl.multiple_of(jnp.maximum(t0 - Q_TILE, 0), Q_TILE)
    sw_ref[...] = jnp.dot(kw_ref[pl.ds(ws, WIN_KEYS), :], q4t, preferred_element_type=jnp.float32)
    off = pl.multiple_of(ns - ws, Q_TILE)
    sw_ref[pl.ds(off, NEAR_KEYS), :] = sw_ref[pl.ds(off, NEAR_KEYS), :] + near_ref[...]
    key = ws + lax.broadcasted_iota(jnp.int32, (WIN_KEYS, ROWS), 0)
    qry = t0 + (lax.broadcasted_iota(jnp.int32, (WIN_KEYS, ROWS), 1) & (Q_TILE - 1))
    sw = jnp.where((key <= qry) & (key > qry - WINDOW), sw_ref[...], NEG)
    mw = _col_reduce(jnp.maximum, sw)
    ow_sum = jnp.dot(_values_t(vwt_ref, ws, WIN_KEYS), _probabilities(sw, mw),
                     preferred_element_type=jnp.float32)
    owt = ow_sum[0:HEAD_DIM, :] / ow_sum[SUM_ROW:SUM_ROW + 1, :]

    gates_t = jax.nn.sigmoid(gt_ref[...])
    for h in range(NSA_HPG):
        cols = slice(h * Q_TILE, (h + 1) * Q_TILE)
        mixed = _branch_gate(gates_t, h, 0) * oct_[:, cols] + _branch_gate(gates_t, h, 2) * owt[:, cols]
        ycw_ref[:, h * HEAD_DIM:(h + 1) * HEAD_DIM] = mixed.T


def _cmp_win(qt, kc, vct, band, ovt, qkv, near, gates_t):
    b, s, _ = qkv.shape
    n_cmp = kc.shape[2]
    n_slc = ovt.shape[0]
    nq = s // Q_TILE
    pen_w = -(-n_slc // LANES) * LANES
    n_band = band.shape[0] - 1
    return pl.pallas_call(
        _cmp_win_kernel,
        grid=(b, NSA_GROUPS, nq),
        in_specs=[
            pl.BlockSpec((None, NSA_HPG, HEAD_DIM, Q_TILE), lambda i, g, q: (i, g, 0, q)),
            pl.BlockSpec((None, None, n_cmp, HEAD_DIM), lambda i, g, q: (i, g, 0, 0)),
            pl.BlockSpec((None, None, HEAD_DIM, n_cmp), lambda i, g, q: (i, g, 0, 0)),
            pl.BlockSpec((None, None, BAND_ROWS, ROWS), lambda i, g, q: (jnp.minimum(q, n_band), g, 0, 0)),
            pl.BlockSpec((n_slc, n_cmp), lambda i, g, q: (0, 0)),
            pl.BlockSpec((None, s, HEAD_DIM), lambda i, g, q: (i, 0, CB_KW + g)),
            pl.BlockSpec((None, s, HEAD_DIM), lambda i, g, q: (i, 0, CB_VW + g)),
            pl.BlockSpec((None, None, NEAR_KEYS, ROWS), lambda i, g, q: (jnp.minimum(q, 1), g, 0, 0)),
            pl.BlockSpec((None, None, GATE_ROWS, Q_TILE), lambda i, g, q: (i, g, 0, q)),
        ],
        out_specs=[
            pl.BlockSpec((None, Q_TILE, NSA_HPG * HEAD_DIM), lambda i, g, q: (i, q, g)),
            pl.BlockSpec((None, None, pen_w, Q_TILE), lambda i, g, q: (i, g, 0, q)),
        ],
        out_shape=[
            jax.ShapeDtypeStruct((b, s, NSA_Q_DIM), jnp.float32),
            jax.ShapeDtypeStruct((b, NSA_GROUPS, pen_w, s), jnp.bfloat16),
        ],
        scratch_shapes=[pltpu.VMEM((n_cmp, ROWS), jnp.float32), pltpu.VMEM((WIN_KEYS, ROWS), jnp.float32),
                        pltpu.VMEM((HEAD_DIM, ROWS), jnp.float32), pltpu.VMEM((n_slc, Q_TILE), jnp.float32),
                        pltpu.VMEM((s // LANES, ACC_ROWS, LANES), jnp.bfloat16)],
        compiler_params=_params(("parallel", "parallel", "arbitrary")),
        name="nsa_cmp_topk_win",
    )(qt, kc, vct, band, ovt, qkv, qkv, near, gates_t)


def _range_penalty(pen, first_blk, lo_ok, hi_ok):
    n_half = pen.shape[0] // LANES
    c = lax.broadcasted_iota(jnp.int32, (LANES, pen.shape[1]), 0)
    out = pen[0:LANES, :]
    blk = c
    for hf in range(1, n_half):
        in_lower = (c + (hf - 1) * LANES >= first_blk) & (first_blk < hf * LANES)
        out = jnp.where(in_lower, out, pen[hf * LANES:(hf + 1) * LANES, :])
        blk = jnp.where(in_lower, blk, c + hf * LANES)
    return jnp.where((blk >= lo_ok) & (blk < hi_ok), out, jnp.asarray(NEG, out.dtype))


def _sel_kernel(qt_ref, pen_ref, kaug_ref, v_ref, near_ref, ycw_ref, gt_ref, y_ref,
                qaug_ref, m_ref, acc_ref, s_ref, mx_ref, p_ref, vt_ref):
    qb = pl.program_id(2)
    n_half = pen_ref.shape[0] // LANES
    n_blocks = pen_ref.shape[0]
    chunks_per_half = LANES * SLC_BLOCK // KEY_CHUNK
    blocks_per_chunk = KEY_CHUNK // SLC_BLOCK

    t0 = qb * Q_TILE
    near_end = t0 - Q_TILE
    nf = jnp.maximum(near_end // KEY_CHUNK, 0)
    ms = pl.multiple_of(jnp.maximum(near_end - KEY_CHUNK, 0), Q_TILE)
    ns = pl.multiple_of(jnp.maximum(near_end, 0), Q_TILE)

    pl.when(qb == 0)(functools.partial(_fill_values_t, v_ref, vt_ref))
    q4t = _group_queries_t(qt_ref)
    pen = pen_ref[...]
    operands = [pen[hf * LANES:(hf + 1) * LANES, :] for hf in range(n_half)]
    operands.append(_range_penalty(pen, ms // SLC_BLOCK, nf * blocks_per_chunk, near_end // SLC_BLOCK))
    operands.append(_range_penalty(pen, ns // SLC_BLOCK, 0, n_blocks))
    for idx, channels in enumerate(operands):
        qaug_ref[idx, 0:HEAD_DIM, :] = q4t
        qaug_ref[idx, HEAD_DIM:, :] = jnp.concatenate([channels] * NSA_HPG, axis=1)

    _flash_init(m_ref, acc_ref)

    def key_start(c):
        return pl.multiple_of(jnp.where(c < nf, c * KEY_CHUNK, ms), Q_TILE)

    def get_q_for(c):
        operand = jnp.where(c < nf, c // chunks_per_half, n_half)
        return lambda lo, hi: qaug_ref[operand, :, lo:hi]

    _flash_far_chunks(nf + 1, lambda c: kaug_ref[pl.ds(key_start(c), KEY_CHUNK), :],
                      lambda c: _values_t(vt_ref, key_start(c), KEY_CHUNK), get_q_for,
                      ROWS, m_ref, acc_ref, s_ref, mx_ref, p_ref)

    _flash_step(kaug_ref[pl.ds(ns, NEAR_KEYS), :], _values_t(vt_ref, ns, NEAR_KEYS),
                lambda lo, hi: qaug_ref[n_half + 1, :, lo:hi], ROWS, m_ref, acc_ref,
                extra=lambda s, lo, hi: s + near_ref[:, lo:hi])

    o = _flash_result(acc_ref)
    gates_t = jax.nn.sigmoid(gt_ref[...])
    for h in range(NSA_HPG):
        sel_h = (_branch_gate(gates_t, h, 1) * o[:, h * Q_TILE:(h + 1) * Q_TILE]).T
        y_ref[:, h * HEAD_DIM:(h + 1) * HEAD_DIM] = (
            ycw_ref[:, h * HEAD_DIM:(h + 1) * HEAD_DIM] + sel_h).astype(y_ref.dtype)


def _sel(qkv, qt, pen, kaug, near, ycw, gates_t):
    b, s, _ = qkv.shape
    nq = s // Q_TILE
    pen_w = pen.shape[2]
    return pl.pallas_call(
        _sel_kernel,
        grid=(b, NSA_GROUPS, nq),
        in_specs=[
            pl.BlockSpec((None, NSA_HPG, HEAD_DIM, Q_TILE), lambda i, g, q: (i, g, 0, q)),
            pl.BlockSpec((None, None, pen_w, Q_TILE), lambda i, g, q: (i, g, 0, q)),
            pl.BlockSpec((None, None, s, 2 * HEAD_DIM), lambda i, g, q: (i, g, 0, 0)),
            pl.BlockSpec((None, s, HEAD_DIM), lambda i, g, q: (i, 0, CB_VS + g)),
            pl.BlockSpec((None, None, NEAR_KEYS, ROWS), lambda i, g, q: (jnp.minimum(q, 1), g, 0, 0)),
            pl.BlockSpec((None, Q_TILE, NSA_HPG * HEAD_DIM), lambda i, g, q: (i, q, g)),
            pl.BlockSpec((None, None, GATE_ROWS, Q_TILE), lambda i, g, q: (i, g, 0, q)),
        ],
        out_specs=pl.BlockSpec((None, Q_TILE, NSA_HPG * HEAD_DIM), lambda i, g, q: (i, q, g)),
        out_shape=jax.ShapeDtypeStruct((b, s, NSA_Q_DIM), jnp.bfloat16),
        scratch_shapes=[
            pltpu.VMEM((pen_w // LANES + 2, 2 * HEAD_DIM, ROWS), jnp.bfloat16),
            pltpu.VMEM((1, ROWS), jnp.float32),
            pltpu.VMEM((ACC_ROWS, ROWS), jnp.float32),
            pltpu.VMEM((2, KEY_CHUNK, ROWS), jnp.float32),
            pltpu.VMEM((2, 1, ROWS), jnp.float32),
            pltpu.VMEM((2, KEY_CHUNK, ROWS), jnp.bfloat16),
            pltpu.VMEM((s // LANES, ACC_ROWS, LANES), jnp.bfloat16),
        ],
        compiler_params=_params(("parallel", "parallel", "arbitrary")),
        name="nsa_selected",
    )(qt, pen, kaug, qkv, near, ycw, gates_t)


def _fox_kernel(nce_ref, q_ref, k_ref, dec_ref, v_ref, o_ref,
                qaug_ref, m_ref, acc_ref, s_ref, mx_ref, p_ref, knorm_ref, vt_ref):
    qi = pl.program_id(2)
    bh = pl.program_id(0) * pl.num_programs(1) + pl.program_id(1)
    tq = q_ref.shape[0]
    row = lax.broadcasted_iota(jnp.int32, (HEAD_DIM, tq), 0)
    q_t = q_ref[...].T
    qaug_ref[0:HEAD_DIM, :] = q_t
    qaug_ref[HEAD_DIM:, :] = jnp.where(row < DECAY_TERMS, -1.0, 0.0).astype(qaug_ref.dtype)
    _flash_init(m_ref, acc_ref)

    @pl.when(qi == 0)
    def _():
        pad = lax.broadcasted_iota(jnp.int32, (ACC_ROWS - HEAD_DIM, tq), 0)
        ones_rows = jnp.where(pad == 0, 1.0, 0.0).astype(vt_ref.dtype)

        def chunk_setup(c, best):
            rows = pl.ds(pl.multiple_of(c * tq, tq), tq)
            vt_ref[c, 0:HEAD_DIM, :] = v_ref[rows, :].T
            vt_ref[c, HEAD_DIM:, :] = ones_rows
            kf = k_ref[rows, :].astype(jnp.float32)
            return jnp.maximum(best, jnp.max(jnp.sum(kf * kf, axis=1, keepdims=True)))

        knorm_ref[0] = jnp.sqrt(lax.fori_loop(0, k_ref.shape[0] // tq, chunk_setup, jnp.float32(0.0)))

    qf = q_t.astype(jnp.float32)
    q_norm = jnp.sqrt(jnp.max(jnp.sum(qf * qf, axis=0, keepdims=True)))

    def get_q(lo, hi):
        return qaug_ref[:, lo:hi]

    def load_keys(c):
        k0 = pl.multiple_of(c * tq, tq)
        return jnp.concatenate([k_ref[pl.ds(k0, tq), :], dec_ref[pl.ds(k0, tq), :]], axis=1)

    def causal(s, lo, hi):
        key = lax.broadcasted_iota(jnp.int32, s.shape, 0)
        qry = lo + lax.broadcasted_iota(jnp.int32, s.shape, 1)
        return jnp.where(key <= qry, s, NEG)

    _flash_step(load_keys(qi), vt_ref[qi], get_q, tq, m_ref, acc_ref, extra=causal)

    m_low = jnp.min(m_ref[...])
    reach = q_norm * knorm_ref[0] + SKIP_MARGIN_LOG2

    def count(c, n_skip):
        return n_skip + jnp.where(reach + nce_ref[bh, c] < m_low, 1, 0)

    first = lax.fori_loop(0, qi, count, jnp.int32(0))
    _flash_far_chunks(qi - first, lambda c: load_keys(c + first), lambda c: vt_ref[c + first],
                      lambda c: get_q, tq, m_ref, acc_ref, s_ref, mx_ref, p_ref)
    o_ref[...] = _flash_result(acc_ref).T.astype(o_ref.dtype)


def _fox(neg_cum_end, qkv, dec, tq):
    b, s, _ = qkv.shape
    return pl.pallas_call(
        _fox_kernel,
        grid=(b, FOX_HEADS, s // tq),
        in_specs=[
            pl.BlockSpec(memory_space=pltpu.SMEM),
            pl.BlockSpec((None, tq, HEAD_DIM), lambda i, h, q: (i, q, CB_QF + h)),
            pl.BlockSpec((None, s, HEAD_DIM), lambda i, h, q: (i, 0, CB_KF + h)),
            pl.BlockSpec((None, None, s, LANES), lambda i, h, q: (i, h, 0, 0)),
            pl.BlockSpec((None, s, HEAD_DIM), lambda i, h, q: (i, 0, CB_VF + h)),
        ],
        out_specs=pl.BlockSpec((None, tq, HEAD_DIM), lambda i, h, q: (i, q, h)),
        out_shape=jax.ShapeDtypeStruct((b, s, FOX_DIM), jnp.bfloat16),
        scratch_shapes=[
            pltpu.VMEM((2 * HEAD_DIM, tq), jnp.bfloat16),
            pltpu.VMEM((1, tq), jnp.float32),
            pltpu.VMEM((ACC_ROWS, tq), jnp.float32),
            pltpu.VMEM((2, tq, tq), jnp.float32),
            pltpu.VMEM((2, 1, tq), jnp.float32),
            pltpu.VMEM((2, tq, tq), jnp.bfloat16),
            pltpu.SMEM((1,), jnp.float32),
            pltpu.VMEM((s // tq, ACC_ROWS, tq), jnp.bfloat16),
        ],
        compiler_params=_params(("parallel", "parallel", "arbitrary")),
        name="fox_attention",
    )(neg_cum_end, qkv, qkv, dec, qkv)


def _merge_kernel(x_ref, g_ref, wa_ref, wb_ref, pa_ref, pb_ref, wo_ref, ya_ref, yb_ref, o_ref,
                  h_ref, acc_ref):
    j = pl.program_id(1)

    @pl.when(j == 0)
    def _():
        h_ref[...] = _norm_rows(x_ref[...], g_ref[...]).astype(h_ref.dtype)
        acc_ref[...] = jnp.zeros_like(acc_ref)

    h = h_ref[...]
    ga = jax.nn.sigmoid(jnp.dot(h, wa_ref[...], preferred_element_type=jnp.float32))
    gb = jax.nn.sigmoid(jnp.dot(h, wb_ref[...], preferred_element_type=jnp.float32))
    a = jnp.dot(ya_ref[...], pa_ref[...], preferred_element_type=jnp.float32)
    bb = jnp.dot(yb_ref[...], pb_ref[...], preferred_element_type=jnp.float32)
    merged = (ga * a + gb * bb).astype(jnp.bfloat16)
    acc_ref[...] += jnp.dot(merged, wo_ref[...], preferred_element_type=jnp.float32)

    @pl.when(j == pl.num_programs(1) - 1)
    def _():
        o_ref[...] = x_ref[...] + acc_ref[...]


def _merge(x2d, g, w_ma, w_mb, p_a, p_b, w_out, y_a, y_b, tm, tn):
    t, d = x2d.shape
    ka = p_a.shape[0]
    kb = p_b.shape[0]
    return pl.pallas_call(
        _merge_kernel,
        grid=(t // tm, d // tn),
        in_specs=[
            pl.BlockSpec((tm, d), lambda i, j: (i, 0)),
            pl.BlockSpec((1, d), lambda i, j: (0, 0)),
            pl.BlockSpec((d, tn), lambda i, j: (0, j)),
            pl.BlockSpec((d, tn), lambda i, j: (0, j)),
            pl.BlockSpec((ka, tn), lambda i, j: (0, j)),
            pl.BlockSpec((kb, tn), lambda i, j: (0, j)),
            pl.BlockSpec((tn, d), lambda i, j: (j, 0)),
            pl.BlockSpec((tm, ka), lambda i, j: (i, 0)),
            pl.BlockSpec((tm, kb), lambda i, j: (i, 0)),
        ],
        out_specs=pl.BlockSpec((tm, d), lambda i, j: (i, 0)),
        out_shape=jax.ShapeDtypeStruct((t, d), jnp.float32),
        scratch_shapes=[pltpu.VMEM((tm, d), jnp.bfloat16), pltpu.VMEM((tm, d), jnp.float32)],
        compiler_params=_params(("parallel", "arbitrary")),
        name="merge_out_proj",
    )(x2d, g.reshape(1, d), w_ma, w_mb, p_a, p_b, w_out, y_a, y_b)


def _ffn_kernel(x_ref, xh_ref, g_ref, wu_ref, wv_ref, cw_ref, cb_ref, wd_ref, gf_ref, o_ref,
                h_ref, u_ref, acc_ref, *, seq, tm):
    i = pl.program_id(0)
    j = pl.program_id(1)
    halo = BF16_SUBLANES

    @pl.when(j == 0)
    def _():
        g = g_ref[...]
        keep = jnp.where((i * tm) % seq == 0, 0.0, 1.0)
        h_ref[0:halo, :] = (_norm_rows(xh_ref[...], g) * keep).astype(h_ref.dtype)
        h_ref[halo:, :] = _norm_rows(x_ref[...], g).astype(h_ref.dtype)
        acc_ref[...] = jnp.zeros_like(acc_ref)

    u_ref[...] = jnp.dot(h_ref[...], wu_ref[...], preferred_element_type=jnp.float32)
    v = jnp.dot(h_ref[halo:, :], wv_ref[...], preferred_element_type=jnp.float32)
    conv = cb_ref[...]
    for k in range(CONV_WIDTH):
        conv = conv + cw_ref[k:k + 1, :] * u_ref[pl.ds(halo - (CONV_WIDTH - 1) + k, tm), :]
    act = (jax.nn.gelu(conv) * v).astype(jnp.bfloat16)
    acc_ref[...] += jnp.dot(act, wd_ref[...], preferred_element_type=jnp.float32)

    @pl.when(j == pl.num_programs(1) - 1)
    def _():
        o_ref[...] = _norm_rows(x_ref[...] + acc_ref[...], gf_ref[...])


def _ffn(x2d, g, w_up, conv_w, conv_b, w_down, g_final, seq, tm, tn):
    t, d = x2d.shape
    d_ff = w_down.shape[0]
    nt = d_ff // tn
    halo = BF16_SUBLANES
    per = tm // halo
    return pl.pallas_call(
        functools.partial(_ffn_kernel, seq=seq, tm=tm),
        grid=(t // tm, nt),
        in_specs=[
            pl.BlockSpec((tm, d), lambda i, j: (i, 0)),
            pl.BlockSpec((halo, d), lambda i, j: (jnp.maximum(i * per - 1, 0), 0)),
            pl.BlockSpec((1, d), lambda i, j: (0, 0)),
            pl.BlockSpec((d, tn), lambda i, j: (0, j)),
            pl.BlockSpec((d, tn), lambda i, j: (0, nt + j)),
            pl.BlockSpec((CONV_WIDTH, tn), lambda i, j: (0, j)),
            pl.BlockSpec((1, tn), lambda i, j: (0, j)),
            pl.BlockSpec((tn, d), lambda i, j: (j, 0)),
            pl.BlockSpec((1, d), lambda i, j: (0, 0)),
        ],
        out_specs=pl.BlockSpec((tm, d), lambda i, j: (i, 0)),
        out_shape=jax.ShapeDtypeStruct((t, d), jnp.float32),
        scratch_shapes=[
            pltpu.VMEM((tm + halo, d), jnp.bfloat16),
            pltpu.VMEM((tm + halo, tn), jnp.float32),
            pltpu.VMEM((tm, d), jnp.float32),
        ],
        compiler_params=_params(("parallel", "arbitrary")),
        name="ffn_final_norm",
    )(x2d, x2d, g.reshape(1, d), w_up, w_up, conv_w, conv_b.reshape(1, d_ff), w_down, g_final.reshape(1, d))


def _t5_bucket_np(dist):
    n = np.maximum(dist, 0)
    max_exact = REL_BUCKETS // 2
    nf = np.maximum(n, 1).astype(np.float32)
    large = max_exact + (np.log(nf / np.float32(max_exact)) / np.float32(math.log(REL_MAX_DIST / max_exact))
                         * np.float32(REL_BUCKETS - max_exact)).astype(np.int32)
    return np.where(n < max_exact, n, np.minimum(large, REL_BUCKETS - 1)).astype(np.int32)


def _bias_by_distance(rel_table, far_shift):
    vals = rel_table[jnp.asarray(_t5_bucket_np(np.arange(REL_MAX_DIST + 1)))]
    if far_shift:
        vals = vals - rel_table[REL_BUCKETS - 1]
    return vals * LOG2E


def _near_bias(rel_table):
    fd = _bias_by_distance(rel_table, True)
    h = fd.shape[1]
    lo = NEAR_KEYS - 1
    vec = jnp.concatenate([jnp.full((lo, h), NEG, fd.dtype), fd[:REL_MAX_DIST],
                           jnp.zeros((NEAR_KEYS - REL_MAX_DIST, h), fd.dtype)], axis=0)
    def toeplitz(start):
        period = Q_TILE + NEAR_KEYS - 1
        window = jnp.concatenate([vec[start:start + Q_TILE], vec[start - (NEAR_KEYS - 1):start]], axis=0)
        flat = jnp.tile(window, (NEAR_KEYS, 1))[:NEAR_KEYS * (period - 1)]
        return flat.reshape(NEAR_KEYS, period - 1, h)[:, :Q_TILE]

    tiles = jnp.stack([toeplitz(lo + a) for a in (0, Q_TILE)])
    tiles = tiles.reshape(2, NEAR_KEYS, Q_TILE, NSA_GROUPS, NSA_HPG).transpose(0, 3, 1, 4, 2)
    return tiles.reshape(2, NSA_GROUPS, NEAR_KEYS, ROWS)


def _band_bias(rel_table):
    fd = _bias_by_distance(rel_table, True)
    tl = np.arange(Q_TILE)[None, :]
    r = np.arange(BAND_ROWS)[:, None]
    dist = np.stack([tl - CMP_STRIDE * (r - off) - (CMP_BLOCK - 1) for off in (0, 8, 16)])
    vals = fd[jnp.asarray(np.clip(dist, 0, REL_MAX_DIST))]
    vals = jnp.where(jnp.asarray(dist >= 0)[..., None], vals, NEG)
    v, rr, q, _ = vals.shape
    return vals.transpose(0, 1, 3, 2).reshape(v, rr, NSA_GROUPS, NSA_HPG * q).transpose(0, 2, 1, 3)


def _overlap_t(n_cmp_pad, n_slc):
    i = np.arange(n_cmp_pad)[None, :]
    jj = np.arange(n_slc)[:, None]
    c_start = i * CMP_STRIDE
    ov = (c_start < jj * SLC_BLOCK + SLC_BLOCK) & (c_start + CMP_BLOCK - 1 >= jj * SLC_BLOCK)
    ov = ov & (i < n_cmp_pad - 1)
    return jnp.asarray(ov.astype(np.float32), jnp.bfloat16)


def _block_onehot(seq):
    blk = (np.arange(seq) // SLC_BLOCK) % LANES
    return jnp.asarray((blk[:, None] == np.arange(LANES)[None, :]).astype(np.float32), jnp.bfloat16)


def _pick_tile(n, pref):
    return pref if n % pref == 0 else n


def kernel(x, attn_norm_g, w_in, cmp_pos_k, cmp_w1_k, cmp_w2_k, cmp_pos_v, cmp_w1_v, cmp_w2_v,
           rel_bias_table, fox_forget_bias, w_branch_nsa, w_branch_fox, w_out,
           ffn_norm_g, w_up, conv_w, conv_b, w_down, final_norm_g):
    assert w_in.shape[0] == 1, "the final norm is fused into the single layer's FFN kernel"
    bsz, seq, d = x.shape
    t = bsz * seq
    bf = jnp.bfloat16
    scale = HEAD_DIM ** -0.5 * LOG2E
    x2d = x.reshape(t, d)
    w_in = w_in[0]

    o = np.cumsum([0, NSA_Q_DIM] + [NSA_KV_DIM] * 6 + [3 * NSA_HEADS, FOX_DIM, FOX_DIM, FOX_DIM, FOX_HEADS, d, d])
    w_qkv = jnp.concatenate([w_in[:, o[0]:o[1]] * scale, w_in[:, o[1]:o[7]],
                             w_in[:, o[8]:o[9]] * scale, w_in[:, o[9]:o[11]]], axis=1).astype(bf)
    gate_cols = []
    for grp in range(NSA_GROUPS):
        gate_cols += [w_in[:, o[7] + grp * GATES_PER_GROUP:o[7] + (grp + 1) * GATES_PER_GROUP],
                      jnp.zeros((d, LANES - GATES_PER_GROUP), w_in.dtype)]
    gate_cols += [w_in[:, o[11]:o[12]], jnp.zeros((d, IN_PROJ_TILE - N_GATE + LANES - FOX_HEADS), w_in.dtype)]
    w_gate = jnp.concatenate(gate_cols, axis=1).astype(bf)
    w_ma = w_in[:, o[12]:o[13]].astype(bf)
    w_mb = w_in[:, o[13]:o[14]].astype(bf)

    qkv, gates = _in_proj(x2d, attn_norm_g[0], jnp.concatenate([w_qkv, w_gate], axis=1),
                          _pick_tile(t, 1024), IN_PROJ_TILE)
    qkv = qkv.reshape(bsz, seq, N_QKV)
    gates = gates.reshape(bsz, seq, IN_PROJ_TILE)

    qt = qkv[:, :, :NSA_Q_DIM].reshape(bsz, seq, NSA_HEADS, HEAD_DIM).transpose(0, 2, 3, 1)

    def heads_major(cb, heads):
        return qkv[:, :, cb * HEAD_DIM:(cb + heads) * HEAD_DIM].reshape(
            bsz, seq, heads, HEAD_DIM).transpose(0, 2, 1, 3)

    n_ch = seq // CMP_STRIDE

    def chunked(cb):
        sl = qkv[:, :, cb * HEAD_DIM:(cb + NSA_GROUPS) * HEAD_DIM]
        sl = sl.reshape(bsz, n_ch, CMP_STRIDE, NSA_GROUPS, HEAD_DIM).transpose(0, 3, 1, 2, 4)
        return sl.reshape(bsz, NSA_GROUPS, n_ch, CMP_STRIDE * HEAD_DIM)

    def posflat(pos):
        return jnp.broadcast_to(pos.reshape(1, CMP_BLOCK * HEAD_DIM), (8, CMP_BLOCK * HEAD_DIM)).astype(bf)

    kc = _compress(chunked(CB_KC), cmp_w1_k[0].astype(bf), posflat(cmp_pos_k[0]), cmp_w2_k[0].astype(bf))
    vc = _compress(chunked(CB_VC), cmp_w1_v[0].astype(bf), posflat(cmp_pos_v[0]), cmp_w2_v[0].astype(bf))
    vct = vc.transpose(0, 1, 3, 2)

    f_t = gates[:, :, NSA_GROUPS * LANES:NSA_GROUPS * LANES + FOX_HEADS]
    f_t = f_t.transpose(0, 2, 1).reshape(bsz * FOX_HEADS, seq)
    bias_col = jnp.tile(fox_forget_bias[0].astype(jnp.float32), bsz).reshape(bsz * FOX_HEADS, 1)
    terms = _decay_cumsum(f_t, bias_col, _pick_tile(seq, 2048))
    dec = jnp.pad(terms.transpose(1, 2, 0), ((0, 0), (0, 0), (0, LANES - DECAY_TERMS)))
    dec = dec.reshape(bsz, FOX_HEADS, seq, LANES)

    n_slc = seq // SLC_BLOCK
    near = _near_bias(rel_bias_table)
    gates_t = gates[:, :, :NSA_GROUPS * LANES].reshape(bsz, seq, NSA_GROUPS, LANES)[..., :GATE_ROWS]
    gates_t = gates_t.transpose(0, 2, 3, 1)
    ycw, pen = _cmp_win(qt, kc, vct, _band_bias(rel_bias_table), _overlap_t(n_ch, n_slc), qkv, near, gates_t)
    ks = heads_major(CB_KS, NSA_GROUPS)
    kaug_sel = jnp.concatenate([ks, jnp.broadcast_to(_block_onehot(seq), ks.shape)], axis=-1)
    y_nsa = _sel(qkv, qt, pen, kaug_sel, near, ycw, gates_t)

    fox_tq = _pick_tile(seq, FOX_Q_TILE)
    neg_cum_end = -jnp.sum(terms[:, :, fox_tq - 1::fox_tq].astype(jnp.float32), axis=0)
    y_fox = _fox(neg_cum_end, qkv, dec, fox_tq)

    tm2 = _pick_tile(t, 512)
    x_mid = _merge(x2d, attn_norm_g[0], w_ma, w_mb, w_branch_nsa[0].astype(bf), w_branch_fox[0].astype(bf),
                   w_out[0].astype(bf), y_nsa.reshape(t, NSA_Q_DIM), y_fox.reshape(t, FOX_DIM),
                   tm2, _pick_tile(d, 512))
    d_ff = w_down.shape[1]
    out = _ffn(x_mid, ffn_norm_g[0], w_up[0].astype(bf), conv_w[0], conv_b[0], w_down[0].astype(bf),
               final_norm_g, seq, tm2, _pick_tile(d_ff, 512))
    return out.reshape(bsz, seq, d)
```
